```python
import math
import jax, jax.numpy as jnp
from jax import lax
import numpy as np


D_MODEL = 2048
BATCH = 4
SEQ = 2048
DEPTH = 2
DEC_BATCH = 128
DEC_SEQ = 4
PAST_LEN = 16384
PAGE_SIZE = 128

LRU_WIDTH = D_MODEL // 2
LRU_BLOCKS = 8
LRU_BS = LRU_WIDTH // LRU_BLOCKS
LRU_C = 8.0
CONV_W = 4
GLA_HEADS = 4
GLA_DK = D_MODEL // 4 // GLA_HEADS
GLA_DV = D_MODEL // 2 // GLA_HEADS
GLA_RANK = 16
GLA_TAU = 16.0
GLA_CHUNK = 64
MLSTM_HEADS = 4
MLSTM_WIDTH = D_MODEL // 2
MLSTM_DH = MLSTM_WIDTH // MLSTM_HEADS
MLSTM_QKV_BS = 4
MLSTM_NB = MLSTM_WIDTH // MLSTM_QKV_BS
MLSTM_CHUNK = 64
D_FF = 4 * D_MODEL
N_BRANCH = 3
EPS = 1e-6
SPLIT_SIZES = (LRU_WIDTH, GLA_HEADS * GLA_DK, GLA_HEADS * GLA_DK, GLA_HEADS * GLA_DV, GLA_RANK,
               GLA_HEADS * GLA_DV, MLSTM_WIDTH, MLSTM_WIDTH, 2 * MLSTM_HEADS, N_BRANCH * D_MODEL)
D_IN = (LRU_WIDTH + 2 * GLA_HEADS * GLA_DK + 2 * GLA_HEADS * GLA_DV + GLA_RANK
        + 2 * MLSTM_WIDTH + 2 * MLSTM_HEADS + N_BRANCH * D_MODEL)

kernel_name = 'hybrid_rglru_gla_mlstm_decoder_step'


def _split_points():
    return [int(v) for v in np.cumsum(SPLIT_SIZES)[:-1]]


def rmsnorm(x, g):
    xf = x.astype(jnp.float32)
    y = xf * lax.rsqrt(jnp.mean(xf * xf, axis=-1, keepdims=True) + EPS)
    return (y * g.astype(jnp.float32)).astype(x.dtype)


def blockdiag(x, w):
    nb, bi, bo = w.shape
    y = jnp.einsum('btnc,ncd->btnd', x.reshape(*x.shape[:-1], nb, bi), w)
    return y.reshape(*x.shape[:-1], nb * bo)


def causal_conv(x, buf, w, b):
    T = x.shape[1]
    xp = jnp.concatenate([buf.astype(x.dtype), x], axis=1)
    y = b + xp[:, 0:T] * w[0]
    for j in range(1, CONV_W):
        y = y + xp[:, j:j + T] * w[j]
    return y, xp[:, -(CONV_W - 1):]


def rglru(x, h0, w_a, b_a, w_x, b_x, lam):
    f32 = jnp.float32
    r = jax.nn.sigmoid((blockdiag(x, w_a) + b_a).astype(f32))
    i = jax.nn.sigmoid((blockdiag(x, w_x) + b_x).astype(f32))
    log_a = LRU_C * r * jax.nn.log_sigmoid(lam.astype(f32))
    a = jnp.exp(log_a)
    u = jnp.sqrt(-jnp.expm1(2.0 * log_a)) * (i * x.astype(f32))

    def step(h, au):
        a_t, u_t = au
        h = a_t * h + u_t
        return h, h

    hT, hs = lax.scan(step, h0.astype(f32), (a.swapaxes(0, 1), u.swapaxes(0, 1)))
    return hs.swapaxes(0, 1), hT


def gla_chunked(q, k, v, g, S0):
    f32 = jnp.float32
    B, T, H, DK = q.shape
    DV = v.shape[-1]
    L = math.gcd(T, GLA_CHUNK)
    nc = T // L

    def chunks(a):
        return a.astype(f32).reshape(B, nc, L, H, a.shape[-1]).transpose(1, 0, 3, 2, 4)

    qc, kc, vc, gc = chunks(q * (DK ** -0.5)), chunks(k), chunks(v), chunks(g)
    mask = jnp.tril(jnp.ones((L, L), dtype=bool))

    def step(S, inp):
        q_, k_, v_, g_ = inp
        b = jnp.cumsum(g_, axis=-2)
        qe = q_ * jnp.exp(b)
        ke = k_ * jnp.exp(-b)
        A = jnp.where(mask, jnp.einsum('bhid,bhjd->bhij', qe, ke), 0.0)
        o = jnp.einsum('bhid,bhde->bhie', qe, S) + jnp.einsum('bhij,bhje->bhie', A, v_)
        bL = b[..., -1:, :]
        S = jnp.exp(bL[..., 0, :])[..., None] * S + jnp.einsum('bhjd,bhje->bhde', k_ * jnp.exp(bL - b), v_)
        return S, o

    ST, o = lax.scan(step, S0.astype(f32), (qc, kc, vc, gc))
    return o.transpose(1, 0, 3, 2, 4).reshape(B, T, H, DV), ST


def mlstm_chunked(q, k, v, ig, fg, C0, n0, m0):
    f32 = jnp.float32
    B, T, H, DH = q.shape
    L = math.gcd(T, MLSTM_CHUNK)
    nc = T // L

    def chunks(a):
        return a.astype(f32).reshape(B, nc, L, H, a.shape[-1]).transpose(1, 0, 3, 2, 4)

    def gchunks(a):
        return a.astype(f32).reshape(B, nc, L, H).transpose(1, 0, 3, 2)

    qc, kc, vc = chunks(q), chunks(k), chunks(v)
    ic, lfc = gchunks(ig), gchunks(jax.nn.log_sigmoid(fg.astype(f32)))
    mask = jnp.tril(jnp.ones((L, L), dtype=bool))

    def step(carry, inp):
        C, n, m = carry
        q_, k_, v_, i_, lf_ = inp
        F = jnp.cumsum(lf_, axis=-1)
        Dm = jnp.where(mask, F[..., :, None] - F[..., None, :] + i_[..., None, :], -jnp.inf)
        inter = F + m[..., None]
        mt = jnp.maximum(inter, jnp.max(Dm, axis=-1))
        W = jnp.exp(Dm - mt[..., None])
        ci = jnp.exp(inter - mt)
        S = jnp.einsum('bhid,bhjd->bhij', q_, k_) * W
        num = ci[..., None] * jnp.einsum('bhid,bhde->bhie', q_, C) + jnp.einsum('bhij,bhje->bhie', S, v_)
        den = ci * jnp.einsum('bhid,bhd->bhi', q_, n) + jnp.sum(S, axis=-1)
        h = num / jnp.maximum(jnp.abs(den), jnp.exp(-mt))[..., None]
        FL = F[..., -1]
        dj = FL[..., None] - F + i_
        m_new = jnp.maximum(FL + m, jnp.max(dj, axis=-1))
        cs = jnp.exp(FL + m - m_new)
        wj = jnp.exp(dj - m_new[..., None])
        C = cs[..., None, None] * C + jnp.einsum('bhj,bhjd,bhje->bhde', wj, k_, v_)
        n = cs[..., None] * n + jnp.einsum('bhj,bhjd->bhd', wj, k_)
        return (C, n, m_new), h

    (CT, nT, mT), h = lax.scan(step, (C0.astype(f32), n0.astype(f32), m0.astype(f32)), (qc, kc, vc, ic, lfc))
    return h.transpose(1, 0, 3, 2, 4).reshape(B, T, H, DH), CT, nT, mT


def mixer(xn, st, P):
    lru_h, lru_conv, gla_S, ml_C, ml_n, ml_m, ml_conv = st
    B, T, _ = xn.shape
    dt = xn.dtype
    proj = xn @ P['w_in']
    (lru_x, g_q, g_k, g_v, g_lr, g_gate, m_x, m_o, m_if, mg) = jnp.split(proj, _split_points(), axis=-1)
    u, lru_conv_new = causal_conv(lru_x, lru_conv, P['lru_conv_w'], P['lru_conv_b'])
    y_lru, lru_h_new = rglru(u, lru_h, P['lru_w_a'], P['lru_b_a'], P['lru_w_x'], P['lru_b_x'], P['lru_lam'])
    y_lru = y_lru.astype(dt)
    q = g_q.reshape(B, T, GLA_HEADS, GLA_DK)
    k = g_k.reshape(B, T, GLA_HEADS, GLA_DK)
    v = g_v.reshape(B, T, GLA_HEADS, GLA_DV)
    logdec = jax.nn.log_sigmoid((g_lr @ P['gla_w_g2'] + P['gla_b_g']).astype(jnp.float32)) / GLA_TAU
    o, gla_S_new = gla_chunked(q, k, v, logdec.reshape(B, T, GLA_HEADS, GLA_DK), gla_S)
    y_gla = rmsnorm(o.astype(dt), P['gla_g_norm']).reshape(B, T, GLA_HEADS * GLA_DV) * jax.nn.silu(g_gate)
    mc, ml_conv_new = causal_conv(m_x, ml_conv, P['ml_conv_w'], P['ml_conv_b'])
    mc = jax.nn.silu(mc)
    mq = blockdiag(mc, P['ml_w_q']).reshape(B, T, MLSTM_HEADS, MLSTM_DH)
    mk = blockdiag(mc, P['ml_w_k']).reshape(B, T, MLSTM_HEADS, MLSTM_DH) * (MLSTM_DH ** -0.5)
    mv = blockdiag(m_x, P['ml_w_v']).reshape(B, T, MLSTM_HEADS, MLSTM_DH)
    gates = (m_if + P['ml_b_if']).astype(jnp.float32)
    h, C_new, n_new, m_new = mlstm_chunked(mq, mk, mv, gates[..., :MLSTM_HEADS], gates[..., MLSTM_HEADS:],
                                           ml_C, ml_n, ml_m)
    y_ml = jax.nn.sigmoid(m_o) * rmsnorm(h.astype(dt), P['ml_g_norm']).reshape(B, T, MLSTM_WIDTH)
    gate = jax.nn.sigmoid(mg).reshape(B, T, N_BRANCH, D_MODEL)
    merged = (gate[:, :, 0] * (y_lru @ P['w_br_lru'])
              + gate[:, :, 1] * (y_gla @ P['w_br_gla'])
              + gate[:, :, 2] * (y_ml @ P['w_br_ml']))
    new_st = (lru_h_new.astype(dt), lru_conv_new.astype(dt), gla_S_new.astype(dt), C_new.astype(dt),
              n_new.astype(dt), m_new.astype(dt), ml_conv_new.astype(dt))
    return merged @ P['w_out'], new_st


def run_trunk(x, c, states, layer_params, g_final):
    new_states = []
    for l in range(DEPTH):
        P = layer_params[l]
        mod = jax.nn.silu(c) @ P['w_ada'] + P['b_ada']
        sh1, sc1, gt1, sh2, sc2, gt2 = [m[:, None, :] for m in jnp.split(mod, 6, axis=-1)]
        xn = rmsnorm(x, P['g_norm1']) * (1.0 + sc1) + sh1
        mix, st = mixer(xn, states[l], P)
        x = x + gt1 * mix
        xn = rmsnorm(x, P['g_norm2']) * (1.0 + sc2) + sh2
        x = x + gt2 * (jnp.square(jax.nn.relu(xn @ P['w_ff1'])) @ P['w_ff2'])
        new_states.append(st)
    y = rmsnorm(x, g_final)
    stacked = [jnp.stack([s[i] for s in new_states]) for i in range(7)]
    return y, stacked


def setup_inputs(seed: int = 0) -> dict:
    key = jax.random.key(seed)
    ks = jax.random.split(key, 48)
    f32 = jnp.float32

    def nrm(i, shape, s):
        return jax.random.normal(ks[i], shape, f32) * s

    a0 = jax.random.uniform(ks[47], (DEPTH, LRU_WIDTH), f32, minval=0.9, maxval=0.999)
    a1 = a0 ** (1.0 / LRU_C)
    b_if = jnp.concatenate([nrm(44, (DEPTH, MLSTM_HEADS), 0.1) - 1.0,
                            3.0 + nrm(45, (DEPTH, MLSTM_HEADS), 0.5)], axis=-1)
    return {
        'x_prompt': nrm(0, (BATCH, SEQ, D_MODEL), 1.0),
        'x_sample': nrm(1, (DEC_BATCH, DEC_SEQ, D_MODEL), 1.0),
        'c_prompt': nrm(2, (BATCH, D_MODEL), 1.0),
        'c_sample': nrm(3, (DEC_BATCH, D_MODEL), 1.0),
        'state_lru_h': nrm(4, (DEPTH, DEC_BATCH, LRU_WIDTH), 0.5),
        'state_lru_conv': nrm(5, (DEPTH, DEC_BATCH, CONV_W - 1, LRU_WIDTH), 1.0),
        'state_gla': nrm(6, (DEPTH, DEC_BATCH, GLA_HEADS, GLA_DK, GLA_DV), 1.0),
        'state_mlstm_C': nrm(7, (DEPTH, DEC_BATCH, MLSTM_HEADS, MLSTM_DH, MLSTM_DH), 0.5),
        'state_mlstm_n': nrm(8, (DEPTH, DEC_BATCH, MLSTM_HEADS, MLSTM_DH), 0.5),
        'state_mlstm_m': nrm(9, (DEPTH, DEC_BATCH, MLSTM_HEADS), 0.5),
        'state_mlstm_conv': nrm(10, (DEPTH, DEC_BATCH, CONV_W - 1, MLSTM_WIDTH), 1.0),
        'w_ada': nrm(11, (DEPTH, D_MODEL, 6 * D_MODEL), 0.2 * D_MODEL ** -0.5),
        'b_ada': nrm(12, (DEPTH, 6 * D_MODEL), 0.02),
        'g_norm1': 1.0 + nrm(13, (DEPTH, D_MODEL), 0.02),
        'g_norm2': 1.0 + nrm(14, (DEPTH, D_MODEL), 0.02),
        'w_in': nrm(15, (DEPTH, D_MODEL, D_IN), D_MODEL ** -0.5),
        'lru_conv_w': nrm(16, (DEPTH, CONV_W, LRU_WIDTH), CONV_W ** -0.5),
        'lru_conv_b': nrm(17, (DEPTH, LRU_WIDTH), 0.02),
        'lru_w_a': nrm(18, (DEPTH, LRU_BLOCKS, LRU_BS, LRU_BS), LRU_BS ** -0.5),
        'lru_b_a': nrm(19, (DEPTH, LRU_WIDTH), 0.02),
        'lru_w_x': nrm(20, (DEPTH, LRU_BLOCKS, LRU_BS, LRU_BS), LRU_BS ** -0.5),
        'lru_b_x': nrm(21, (DEPTH, LRU_WIDTH), 0.02),
        'lru_lam': jnp.log(a1) - jnp.log1p(-a1),
        'gla_w_g2': nrm(22, (DEPTH, GLA_RANK, GLA_HEADS * GLA_DK), GLA_RANK ** -0.5),
        'gla_b_g': nrm(23, (DEPTH, GLA_HEADS * GLA_DK), 0.5),
        'gla_g_norm': 1.0 + nrm(24, (DEPTH, GLA_DV), 0.02),
        'ml_conv_w': nrm(25, (DEPTH, CONV_W, MLSTM_WIDTH), CONV_W ** -0.5),
        'ml_conv_b': nrm(26, (DEPTH, MLSTM_WIDTH), 0.02),
        'ml_w_q': nrm(27, (DEPTH, MLSTM_NB, MLSTM_QKV_BS, MLSTM_QKV_BS), MLSTM_QKV_BS ** -0.5),
        'ml_w_k': nrm(28, (DEPTH, MLSTM_NB, MLSTM_QKV_BS, MLSTM_QKV_BS), MLSTM_QKV_BS ** -0.5),
        'ml_w_v': nrm(29, (DEPTH, MLSTM_NB, MLSTM_QKV_BS, MLSTM_QKV_BS), MLSTM_QKV_BS ** -0.5),
        'ml_b_if': b_if,
        'ml_g_norm': 1.0 + nrm(30, (DEPTH, MLSTM_DH), 0.02),
        'w_br_lru': nrm(31, (DEPTH, LRU_WIDTH, D_MODEL), LRU_WIDTH ** -0.5),
        'w_br_gla': nrm(32, (DEPTH, GLA_HEADS * GLA_DV, D_MODEL), (GLA_HEADS * GLA_DV) ** -0.5),
        'w_br_ml': nrm(33, (DEPTH, MLSTM_WIDTH, D_MODEL), MLSTM_WIDTH ** -0.5),
        'w_out': nrm(34, (DEPTH, D_MODEL, D_MODEL), D_MODEL ** -0.5),
        'w_ff1': nrm(35, (DEPTH, D_MODEL, D_FF), D_MODEL ** -0.5),
        'w_ff2': nrm(36, (DEPTH, D_FF, D_MODEL), D_FF ** -0.5),
        'g_final': 1.0 + nrm(37, (D_MODEL,), 0.02),
    }


def reference(x_prompt, x_sample, c_prompt, c_sample, state_lru_h, state_lru_conv, state_gla,
              state_mlstm_C, state_mlstm_n, state_mlstm_m, state_mlstm_conv,
              w_ada, b_ada, g_norm1, g_norm2, w_in, lru_conv_w, lru_conv_b, lru_w_a, lru_b_a,
              lru_w_x, lru_b_x, lru_lam, gla_w_g2, gla_b_g, gla_g_norm, ml_conv_w, ml_conv_b,
              ml_w_q, ml_w_k, ml_w_v, ml_b_if, ml_g_norm, w_br_lru, w_br_gla, w_br_ml, w_out,
              w_ff1, w_ff2, g_final):
    layer_params = [dict(w_ada=w_ada[l], b_ada=b_ada[l], g_norm1=g_norm1[l], g_norm2=g_norm2[l],
                         w_in=w_in[l], lru_conv_w=lru_conv_w[l], lru_conv_b=lru_conv_b[l],
                         lru_w_a=lru_w_a[l], lru_b_a=lru_b_a[l], lru_w_x=lru_w_x[l], lru_b_x=lru_b_x[l],
                         lru_lam=lru_lam[l], gla_w_g2=gla_w_g2[l], gla_b_g=gla_b_g[l],
                         gla_g_norm=gla_g_norm[l], ml_conv_w=ml_conv_w[l], ml_conv_b=ml_conv_b[l],
                         ml_w_q=ml_w_q[l], ml_w_k=ml_w_k[l], ml_w_v=ml_w_v[l], ml_b_if=ml_b_if[l],
                         ml_g_norm=ml_g_norm[l], w_br_lru=w_br_lru[l], w_br_gla=w_br_gla[l],
                         w_br_ml=w_br_ml[l], w_out=w_out[l], w_ff1=w_ff1[l], w_ff2=w_ff2[l])
                    for l in range(DEPTH)]
    dt = x_prompt.dtype
    zero = (jnp.zeros((BATCH, LRU_WIDTH), dt), jnp.zeros((BATCH, CONV_W - 1, LRU_WIDTH), dt),
            jnp.zeros((BATCH, GLA_HEADS, GLA_DK, GLA_DV), dt),
            jnp.zeros((BATCH, MLSTM_HEADS, MLSTM_DH, MLSTM_DH), dt),
            jnp.zeros((BATCH, MLSTM_HEADS, MLSTM_DH), dt), jnp.zeros((BATCH, MLSTM_HEADS), dt),
            jnp.zeros((BATCH, CONV_W - 1, MLSTM_WIDTH), dt))
    prompt_states = [zero for _ in range(DEPTH)]
    sample_states = [(state_lru_h[l], state_lru_conv[l], state_gla[l], state_mlstm_C[l], state_mlstm_n[l],
                      state_mlstm_m[l], state_mlstm_conv[l]) for l in range(DEPTH)]
    y_prompt, ps = run_trunk(x_prompt, c_prompt, prompt_states, layer_params, g_final)
    y_sample, ss = run_trunk(x_sample, c_sample, sample_states, layer_params, g_final)
    p_lru_h, p_lru_conv, p_gla, p_C, p_n, p_m, p_ml_conv = ps
    s_lru_h, s_lru_conv, s_gla, s_C, s_n, s_m, s_ml_conv = ss
    return (y_prompt, y_sample, p_lru_h, p_lru_conv, p_gla, p_C, p_n, p_m, p_ml_conv,
            s_lru_h, s_lru_conv, s_gla, s_C, s_n, s_m, s_ml_conv)
```

```python
import functools

import jax
import jax.numpy as jnp
from jax import lax
from jax.experimental import pallas as pl
from jax.experimental.pallas import tpu as pltpu

F32, BF16 = jnp.float32, jnp.bfloat16
HIGHEST = lax.Precision.HIGHEST
SDS = jax.ShapeDtypeStruct

D = 2048
DEPTH = 2
LRU_W = 1024
LRU_BLOCKS = 8
LRU_C = 8.0
CONV_W = 4
GLA_H = 4
GLA_DK = 128
GLA_DV = 256
GLA_RANK = 16
GLA_TAU = 16.0
GLA_CHUNK = 64
ML_H = 4
ML_W = 1024
ML_DH = 256
ML_BS = 4
ML_CHUNK = 64
D_FF = 4 * D
EPS = 1e-6
N_MOD = 6

LANE = 128
SUBLANE = 8
VMEM_LIMIT_CAP = 56 * 1024 * 1024

C_LRU, C_GQ, C_GK, C_GV, C_GGATE, C_MX, C_MO, C_MG = 0, 1024, 1536, 2048, 3072, 4096, 5120, 6144
N_MAIN = 12288
S_GLR = 0
S_MIF = 16
NEG_BIG = -1e30


def _params(sem, vmem_bytes):
    return pltpu.CompilerParams(dimension_semantics=sem, vmem_limit_bytes=int(min(vmem_bytes, VMEM_LIMIT_CAP)))


def _dot(a, b):
    return jnp.dot(a.astype(BF16), b.astype(BF16), preferred_element_type=F32)


def _dot_nt(a, b):
    return lax.dot_general(a.astype(BF16), b.astype(BF16), (((1,), (1,)), ((), ())), preferred_element_type=F32)


def _dot_hi(a, b):
    return jnp.dot(a, b, precision=HIGHEST, preferred_element_type=F32)


def _log_sigmoid(z):
    return jnp.minimum(z, 0.0) - jnp.log1p(jnp.exp(-jnp.abs(z)))


def _silu(z):
    return z * jax.nn.sigmoid(z)


def _rms(x, g):
    return x * lax.rsqrt(jnp.mean(x * x, axis=-1, keepdims=True) + EPS) * g


def _pad_rows(x, rows):
    if x.shape[0] == rows:
        return x
    return jnp.concatenate([x, jnp.zeros((rows - x.shape[0], x.shape[1]), x.dtype)], axis=0)


def _tri(n):
    r = lax.broadcasted_iota(jnp.int32, (n, n), 0)
    c = lax.broadcasted_iota(jnp.int32, (n, n), 1)
    return r >= c


def _ada_kernel(c_ref, w_ref, b_ref, o_ref):
    o_ref[...] = _dot(_silu(c_ref[...]), w_ref[...]) + b_ref[...]


def _ada(c_all, w_ada, b_ada):
    m = c_all.shape[0]
    tn = 512
    n_out = N_MOD * D
    return pl.pallas_call(
        _ada_kernel,
        grid=(DEPTH, n_out // tn),
        in_specs=[
            pl.BlockSpec((m, D), lambda l, n: (0, 0)),
            pl.BlockSpec((None, D, tn), lambda l, n: (l, 0, n)),
            pl.BlockSpec((None, 1, tn), lambda l, n: (l, 0, n)),
        ],
        out_specs=pl.BlockSpec((None, m, tn), lambda l, n: (l, 0, n)),
        out_shape=SDS((DEPTH, m, n_out), F32),
        compiler_params=_params(("parallel", "arbitrary"), 2 * (m * D + D * tn + m * tn) * 4 + (8 << 20)),
        name="ada",
    )(c_all, w_ada, b_ada.reshape(DEPTH, 1, n_out))


def _modspec(layer, comp, r, width, tm, rpg, jmap=None):
    nb = D // width
    jm = (lambda j: 0) if jmap is None else jmap
    if r == 1:
        return pl.BlockSpec((None, None, 1, width), lambda i, j: (layer, (i * tm) // rpg, 0, comp * nb + jm(j)))
    return pl.BlockSpec((None, None, tm, width), lambda i, j: (layer, 0, i, comp * nb + jm(j)))


def _norm_mod_store(xn_s, x_ref, g_ref, sc_ref, sh_ref):
    tm = x_ref.shape[0]
    rc = min(tm, 256)
    for c in range(tm // rc):
        rows = slice(c * rc, (c + 1) * rc)
        mrows = rows if sc_ref.shape[0] == tm else slice(None)
        xn = _rms(x_ref[rows, :], g_ref[...]) * (1.0 + sc_ref[mrows, :]) + sh_ref[mrows, :]
        xn_s[rows, :] = xn.astype(BF16)


def _in_proj_kernel(x_ref, sc_ref, sh_ref, g_ref, wm_ref, ws_ref, om_ref, os_ref, xn_s):
    @pl.when(pl.program_id(1) == 0)
    def _():
        _norm_mod_store(xn_s, x_ref, g_ref, sc_ref, sh_ref)
        os_ref[...] = jnp.dot(xn_s[...], ws_ref[...], preferred_element_type=F32)

    om_ref[...] = jnp.dot(xn_s[...], wm_ref[...], preferred_element_type=F32)


def _in_proj(x, mod, layer, g1, w_main, w_small, *, tm, tn, rpg, r):
    n = x.shape[0]
    vmem = 2 * (tm * D * 4 + 2 * r * D * 4 + D * tn * 2 + D * LANE * 2 + tm * tn * 4 + tm * LANE * 4) + tm * D * 2
    vmem += 8 << 20
    return pl.pallas_call(
        _in_proj_kernel,
        grid=(n // tm, N_MAIN // tn),
        in_specs=[
            pl.BlockSpec((tm, D), lambda i, j: (i, 0)),
            _modspec(layer, 1, r, D, tm, rpg),
            _modspec(layer, 0, r, D, tm, rpg),
            pl.BlockSpec((None, 1, D), lambda i, j: (layer, 0, 0)),
            pl.BlockSpec((None, D, tn), lambda i, j: (layer, 0, j)),
            pl.BlockSpec((None, D, LANE), lambda i, j: (layer, 0, 0)),
        ],
        out_specs=[pl.BlockSpec((tm, tn), lambda i, j: (i, j)), pl.BlockSpec((tm, LANE), lambda i, j: (i, 0))],
        out_shape=[SDS((n, N_MAIN), F32), SDS((n, LANE), F32)],
        scratch_shapes=[pltpu.VMEM((tm, D), BF16)],
        compiler_params=_params(("parallel", "arbitrary"), vmem),
        name="in_proj",
    )(x, mod, mod, g1, w_main, w_small)


def _merge_kernel(yl_ref, yg_ref, ym_ref, g0_ref, g1_ref, g2_ref, w0_ref, w1_ref, w2_ref, o_ref, y_s):
    @pl.when(pl.program_id(1) == 0)
    def _():
        y_s[0] = yl_ref[...].astype(BF16)
        y_s[1] = yg_ref[...].astype(BF16)
        y_s[2] = ym_ref[...].astype(BF16)

    acc = jax.nn.sigmoid(g0_ref[...]) * jnp.dot(y_s[0], w0_ref[...], preferred_element_type=F32)
    acc += jax.nn.sigmoid(g1_ref[...]) * jnp.dot(y_s[1], w1_ref[...], preferred_element_type=F32)
    acc += jax.nn.sigmoid(g2_ref[...]) * jnp.dot(y_s[2], w2_ref[...], preferred_element_type=F32)
    o_ref[...] = acc.astype(BF16)


def _merge(y_lru, y_gla, y_ml, proj, layer, w_lru, w_gla, w_ml, *, tm, tn):
    n = y_lru.shape[0]
    w = LRU_W
    gb = C_MG // tn
    nb = D // tn
    yspec = pl.BlockSpec((tm, w), lambda i, j: (i, 0))
    wspec = pl.BlockSpec((None, w, tn), lambda i, j: (layer, 0, j))
    vmem = 2 * (3 * tm * w * 4 + 3 * tm * tn * 4 + 3 * w * tn * 2 + tm * tn * 2) + 3 * tm * w * 2 + 4 * tm * tn * 4
    return pl.pallas_call(
        _merge_kernel,
        grid=(n // tm, nb),
        in_specs=[
            yspec, yspec, yspec,
            pl.BlockSpec((tm, tn), lambda i, j: (i, gb + j)),
            pl.BlockSpec((tm, tn), lambda i, j: (i, gb + nb + j)),
            pl.BlockSpec((tm, tn), lambda i, j: (i, gb + 2 * nb + j)),
            wspec, wspec, wspec,
        ],
        out_specs=pl.BlockSpec((tm, tn), lambda i, j: (i, j)),
        out_shape=SDS((n, D), BF16),
        scratch_shapes=[pltpu.VMEM((3, tm, w), BF16)],
        compiler_params=_params(("parallel", "arbitrary"), vmem),
        name="merge",
    )(y_lru, y_gla, y_ml, proj, proj, proj, w_lru, w_gla, w_ml)


def _out_proj_kernel(m_ref, w_ref, x_ref, gt_ref, o_ref):
    o_ref[...] = x_ref[...] + gt_ref[...] * jnp.dot(m_ref[...], w_ref[...], preferred_element_type=F32)


def _out_proj(merged, x, mod, layer, w_out, *, tm, tn, rpg, r):
    n = x.shape[0]
    vmem = 2 * (tm * D * 2 + D * tn * 2 + 2 * tm * tn * 4 + r * tn * 4) + 2 * tm * tn * 4
    return pl.pallas_call(
        _out_proj_kernel,
        grid=(n // tm, D // tn),
        in_specs=[
            pl.BlockSpec((tm, D), lambda i, j: (i, 0)),
            pl.BlockSpec((None, D, tn), lambda i, j: (layer, 0, j)),
            pl.BlockSpec((tm, tn), lambda i, j: (i, j)),
            _modspec(layer, 2, r, tn, tm, rpg, jmap=lambda j: j),
        ],
        out_specs=pl.BlockSpec((tm, tn), lambda i, j: (i, j)),
        out_shape=SDS((n, D), F32),
        compiler_params=_params(("parallel", "arbitrary"), vmem),
        name="out_proj",
    )(merged, w_out, x, mod)


def _ffn_kernel(x_ref, sc_ref, sh_ref, gt_ref, g_ref, gf_ref, w1_ref, w2_ref, o_ref, xn_s, *, final_norm):
    f = pl.program_id(1)

    @pl.when(f == 0)
    def _():
        _norm_mod_store(xn_s, x_ref, g_ref, sc_ref, sh_ref)
        o_ref[...] = jnp.zeros_like(o_ref)

    h = jnp.square(jnp.maximum(jnp.dot(xn_s[...], w1_ref[...], preferred_element_type=F32), 0.0))
    o_ref[...] += jnp.dot(h.astype(BF16), w2_ref[...], preferred_element_type=F32)

    @pl.when(f == pl.num_programs(1) - 1)
    def _():
        y = x_ref[...] + gt_ref[...] * o_ref[...]
        if final_norm:
            y = _rms(y, gf_ref[...])
        o_ref[...] = y


def _ffn(x, mod, layer, g2, g_final, w1, w2, *, tm, tf, rpg, r, final_norm):
    n = x.shape[0]
    vmem = 2 * (2 * tm * D * 4 + 3 * r * D * 4 + 2 * D * tf * 2) + tm * D * 2 + 2 * tm * tf * 4 + 2 * tm * D * 4
    return pl.pallas_call(
        functools.partial(_ffn_kernel, final_norm=final_norm),
        grid=(n // tm, D_FF // tf),
        in_specs=[
            pl.BlockSpec((tm, D), lambda i, f: (i, 0)),
            _modspec(layer, 4, r, D, tm, rpg),
            _modspec(layer, 3, r, D, tm, rpg),
            _modspec(layer, 5, r, D, tm, rpg),
            pl.BlockSpec((None, 1, D), lambda i, f: (layer, 0, 0)),
            pl.BlockSpec((1, D), lambda i, f: (0, 0)),
            pl.BlockSpec((None, D, tf), lambda i, f: (layer, 0, f)),
            pl.BlockSpec((None, tf, D), lambda i, f: (layer, f, 0)),
        ],
        out_specs=pl.BlockSpec((tm, D), lambda i, f: (i, 0)),
        out_shape=SDS((n, D), F32),
        scratch_shapes=[pltpu.VMEM((tm, D), BF16)],
        compiler_params=_params(("parallel", "arbitrary"), vmem),
        name="ffn",
    )(x, mod, mod, mod, g2, g_final, w1, w2)


def _lru_kernel(x_ref, cbuf_ref, h0_ref, cw_ref, cb_ref, wa_ref, ba_ref, wx_ref, bx_ref, lam_ref,
                y_ref, ht_ref, ct_ref, xp_s, a_s, u_s, h_s, *, tc, tcp):
    @pl.when(pl.program_id(1) == 0)
    def _():
        xp_s[...] = jnp.zeros_like(xp_s)
        xp_s[SUBLANE - 3:SUBLANE, :] = cbuf_ref[...]
        h_s[...] = h0_ref[...]

    xp_s[SUBLANE:SUBLANE + tc, :] = x_ref[...]
    w = cw_ref[...]
    u = cb_ref[...] + xp_s[5:5 + tcp, :] * w[0:1]
    for j in range(1, CONV_W):
        u = u + xp_s[5 + j:5 + j + tcp, :] * w[j:j + 1]
    tail = xp_s[5 + tc:8 + tc, :]
    xp_s[5:8, :] = tail
    ct_ref[...] = tail

    for blk in range(LRU_BLOCKS):
        sl = slice(blk * LANE, (blk + 1) * LANE)
        ub = u[:, sl]
        r = jax.nn.sigmoid(_dot(ub, wa_ref[blk]) + ba_ref[:, sl])
        i = jax.nn.sigmoid(_dot(ub, wx_ref[blk]) + bx_ref[:, sl])
        log_a = LRU_C * r * _log_sigmoid(lam_ref[:, sl])
        t = jnp.tanh(log_a)
        a_s[:, sl] = jnp.exp(log_a)
        u_s[:, sl] = jnp.sqrt(-2.0 * t / (1.0 - t)) * (i * ub)

    def step(t, h):
        h = a_s[pl.ds(t, 1), :] * h + u_s[pl.ds(t, 1), :]
        a_s[pl.ds(t, 1), :] = h
        return h

    h = lax.fori_loop(0, tc, step, h_s[...], unroll=min(tc, SUBLANE))
    h_s[...] = h
    ht_ref[...] = h
    y_ref[...] = a_s[0:tc, :]


def _lru(proj3, cbuf, h0, layer, cw, cb, wa, ba, wx, bx, lam, *, tc):
    b, t, _ = proj3.shape
    w = LRU_W
    tcp = max(tc, SUBLANE)
    lspec3 = lambda s: pl.BlockSpec((None,) + s, lambda bi, c: (layer, 0, 0))
    lspec4 = lambda s: pl.BlockSpec((None,) + s, lambda bi, c: (layer, 0, 0, 0))
    vmem = 2 * (2 * tc * w * 4 + 2 * LRU_BLOCKS * LANE * LANE * 4) + (3 * tcp + 16) * w * 4 + 8 * tcp * w * 4 + (4 << 20)
    return pl.pallas_call(
        functools.partial(_lru_kernel, tc=tc, tcp=tcp),
        grid=(b, t // tc),
        in_specs=[
            pl.BlockSpec((None, tc, w), lambda bi, c: (bi, c, C_LRU // w)),
            pl.BlockSpec((None, CONV_W - 1, w), lambda bi, c: (bi, 0, 0)),
            pl.BlockSpec((None, 1, w), lambda bi, c: (bi, 0, 0)),
            lspec3((CONV_W, w)), lspec3((1, w)),
            lspec4((LRU_BLOCKS, LANE, LANE)), lspec3((1, w)),
            lspec4((LRU_BLOCKS, LANE, LANE)), lspec3((1, w)),
            lspec3((1, w)),
        ],
        out_specs=[
            pl.BlockSpec((None, tc, w), lambda bi, c: (bi, c, 0)),
            pl.BlockSpec((None, 1, w), lambda bi, c: (bi, 0, 0)),
            pl.BlockSpec((None, CONV_W - 1, w), lambda bi, c: (bi, 0, 0)),
        ],
        out_shape=[SDS((b, t, w), F32), SDS((b, 1, w), F32), SDS((b, CONV_W - 1, w), F32)],
        scratch_shapes=[pltpu.VMEM((SUBLANE + tcp, w), F32), pltpu.VMEM((tcp, w), F32),
                        pltpu.VMEM((tcp, w), F32), pltpu.VMEM((1, w), F32)],
        compiler_params=_params(("parallel", "arbitrary"), vmem),
        name="lru",
    )(proj3, cbuf, h0, cw, cb, wa, ba, wx, bx, lam)


def _gla_kernel(q_ref, k_ref, v_ref, gate_ref, small_ref, wg2_ref, bg_ref, gn_ref, s0_ref,
                y_ref, st_ref, s_s, *, L, Lp):
    @pl.when(pl.program_id(1) == 0)
    def _():
        s_s[...] = s0_ref[...]

    q = _pad_rows(q_ref[...], Lp)
    k = _pad_rows(k_ref[...], Lp)
    v = _pad_rows(v_ref[...], Lp)
    gate = _pad_rows(gate_ref[...], Lp)
    small = _pad_rows(small_ref[...], Lp)
    logdec = _log_sigmoid(_dot(small, wg2_ref[...]) + bg_ref[...]) * (1.0 / GLA_TAU)
    if L < Lp:
        row = lax.broadcasted_iota(jnp.int32, logdec.shape, 0)
        logdec = jnp.where(row < L, logdec, 0.0)
    tri = _tri(Lp)
    bcum = _dot_hi(tri.astype(F32), logdec)
    scale = GLA_DK ** -0.5
    for h in range(GLA_H):
        ks = slice(h * GLA_DK, (h + 1) * GLA_DK)
        vs = slice(h * GLA_DV, (h + 1) * GLA_DV)
        bh = bcum[:, ks]
        qe = (q[:, ks] * scale) * jnp.exp(bh)
        ke = k[:, ks] * jnp.exp(-bh)
        a = jnp.where(tri, _dot_nt(qe, ke), 0.0)
        s_h = s_s[h]
        o = _dot(qe, s_h) + _dot(a, v[:, vs])
        bl = bh[L - 1:L, :]
        kd = k[:, ks] * jnp.exp(bl - bh)
        dec_col = jnp.exp(jnp.broadcast_to(bl, (SUBLANE, GLA_DK))).T[:, 0:1]
        s_new = dec_col * s_h + _dot(kd.T, v[:, vs])
        s_s[h] = s_new
        st_ref[h] = s_new
        y = _rms(o, gn_ref[...]) * _silu(gate[:, vs])
        y_ref[:, vs] = y[0:L, :]


def _gla(proj3, small3, s0, layer, s0_layer, wg2, bg, gn, *, L):
    b, t, _ = proj3.shape
    Lp = max(L, SUBLANE)
    kw, vw = GLA_H * GLA_DK, GLA_H * GLA_DV
    if s0_layer is None:
        s0spec = pl.BlockSpec((None, GLA_H, GLA_DK, GLA_DV), lambda bi, c: (bi, 0, 0, 0))
    else:
        s0spec = pl.BlockSpec((None, None, GLA_H, GLA_DK, GLA_DV), lambda bi, c: (s0_layer, bi, 0, 0, 0))
    vmem = 2 * (2 * L * kw * 4 + 3 * L * vw * 4 + L * LANE * 4 + LANE * kw * 4 + 2 * GLA_H * GLA_DK * GLA_DV * 4)
    vmem += GLA_H * GLA_DK * GLA_DV * 4 + 40 * Lp * vw * 4 + (8 << 20)
    return pl.pallas_call(
        functools.partial(_gla_kernel, L=L, Lp=Lp),
        grid=(b, t // L),
        in_specs=[
            pl.BlockSpec((None, L, kw), lambda bi, c: (bi, c, C_GQ // kw)),
            pl.BlockSpec((None, L, kw), lambda bi, c: (bi, c, C_GK // kw)),
            pl.BlockSpec((None, L, vw), lambda bi, c: (bi, c, C_GV // vw)),
            pl.BlockSpec((None, L, vw), lambda bi, c: (bi, c, C_GGATE // vw)),
            pl.BlockSpec((None, L, LANE), lambda bi, c: (bi, c, 0)),
            pl.BlockSpec((None, LANE, kw), lambda bi, c: (layer, 0, 0)),
            pl.BlockSpec((None, 1, kw), lambda bi, c: (layer, 0, 0)),
            pl.BlockSpec((None, 1, GLA_DV), lambda bi, c: (layer, 0, 0)),
            s0spec,
        ],
        out_specs=[
            pl.BlockSpec((None, L, vw), lambda bi, c: (bi, c, 0)),
            pl.BlockSpec((None, GLA_H, GLA_DK, GLA_DV), lambda bi, c: (bi, 0, 0, 0)),
        ],
        out_shape=[SDS((b, t, vw), F32), SDS((b, GLA_H, GLA_DK, GLA_DV), F32)],
        scratch_shapes=[pltpu.VMEM((GLA_H, GLA_DK, GLA_DV), F32)],
        compiler_params=_params(("parallel", "arbitrary"), vmem),
        name="gla",
    )(proj3, proj3, proj3, proj3, small3, wg2, bg, gn, s0)


def _mlstm_kernel(mx_ref, mo_ref, small_ref, cbuf_ref, cw_ref, cb_ref, wq_ref, wk_ref, wv_ref, bif_ref, gn_ref,
                  c0_ref, n0_ref, m0_ref,
                  y_ref, ct_ref, nt_ref, mt_ref, convt_ref, xp_s, c_s, n_s, m_s, *, L, Lp):
    @pl.when(pl.program_id(1) == 0)
    def _():
        xp_s[...] = jnp.zeros_like(xp_s)
        xp_s[SUBLANE - 3:SUBLANE, :] = cbuf_ref[...]
        c_s[...] = c0_ref[...]
        n_s[...] = n0_ref[...]
        m_s[...] = m0_ref[...]

    xp_s[SUBLANE:SUBLANE + L, :] = mx_ref[...]
    w = cw_ref[...]
    conv = cb_ref[...] + xp_s[5:5 + Lp, :] * w[0:1]
    for j in range(1, CONV_W):
        conv = conv + xp_s[5 + j:5 + j + Lp, :] * w[j:j + 1]
    mx = xp_s[SUBLANE:SUBLANE + Lp, :]
    tail = xp_s[5 + L:8 + L, :]
    xp_s[5:8, :] = tail
    convt_ref[...] = tail
    mc = _silu(conv)

    qs, ks, vs = [], [], []
    for blk in range(ML_W // LANE):
        sl = slice(blk * LANE, (blk + 1) * LANE)
        qs.append(_dot_hi(mc[:, sl], wq_ref[blk]))
        ks.append(_dot_hi(mc[:, sl], wk_ref[blk]) * (ML_DH ** -0.5))
        vs.append(_dot_hi(mx[:, sl], wv_ref[blk]))

    gates = _pad_rows(small_ref[...], Lp) + bif_ref[...]
    lf = _log_sigmoid(gates)
    gates_t = gates.T
    lf_t = _log_sigmoid(gates_t)
    if L < Lp:
        row = lax.broadcasted_iota(jnp.int32, gates.shape, 0)
        col = lax.broadcasted_iota(jnp.int32, gates_t.shape, 1)
        lf = jnp.where(row < L, lf, 0.0)
        lf_t = jnp.where(col < L, lf_t, 0.0)
        gates_t = jnp.where(col < L, gates_t, NEG_BIG)
    tri = _tri(Lp)
    f_col = _dot_hi(tri.astype(F32), lf)
    r_i = lax.broadcasted_iota(jnp.int32, (Lp, Lp), 0)
    c_i = lax.broadcasted_iota(jnp.int32, (Lp, Lp), 1)
    f_row = _dot_hi(lf_t, (r_i <= c_i).astype(F32))

    mo = _pad_rows(mo_ref[...], Lp)
    for h in range(ML_H):
        q = jnp.concatenate(qs[2 * h:2 * h + 2], axis=1)
        k = jnp.concatenate(ks[2 * h:2 * h + 2], axis=1)
        v = jnp.concatenate(vs[2 * h:2 * h + 2], axis=1)
        fc = f_col[:, S_MIF + ML_H + h:S_MIF + ML_H + h + 1]
        fr = f_row[S_MIF + ML_H + h:S_MIF + ML_H + h + 1, :]
        igr = gates_t[S_MIF + h:S_MIF + h + 1, :]
        m_h = m_s[h:h + 1, 0:1]
        dm = jnp.where(tri, fc - fr + igr, NEG_BIG)
        inter = fc + m_h
        mt = jnp.maximum(inter, jnp.max(dm, axis=-1, keepdims=True))
        wgt = jnp.exp(dm - mt)
        ci = jnp.exp(inter - mt)
        s = _dot_nt(q, k) * wgt
        c_h = c_s[h]
        n_h = n_s[h:h + 1, :]
        num = ci * _dot(q, c_h) + _dot(s, v)
        den = ci * jnp.sum(q * n_h, axis=-1, keepdims=True) + jnp.sum(s, axis=-1, keepdims=True)
        hh = num / jnp.maximum(jnp.abs(den), jnp.exp(-mt))
        fl = fr[:, L - 1:L]
        dj = fl - fr + igr
        m_new = jnp.maximum(fl + m_h, jnp.max(dj, axis=-1, keepdims=True))
        cs = jnp.exp(fl + m_h - m_new)
        wj = jnp.exp(dj - m_new)
        c_new = cs * c_h + _dot(k.T * wj, v)
        n_new = cs * n_h + _dot(jnp.broadcast_to(wj, (SUBLANE, Lp)), k)[0:1, :]
        c_s[h] = c_new
        ct_ref[h] = c_new
        n_s[h:h + 1, :] = n_new
        nt_ref[h:h + 1, :] = n_new
        m_row = jnp.broadcast_to(m_new, (1, LANE))
        m_s[h:h + 1, :] = m_row
        mt_ref[h:h + 1, :] = m_row
        sl = slice(h * ML_DH, (h + 1) * ML_DH)
        y = jax.nn.sigmoid(mo[:, sl]) * _rms(hh, gn_ref[...])
        y_ref[:, sl] = y[0:L, :]


def _mlstm(proj3, small3, cbuf, c0, n0, m0, layer, st_layer, cw, cb, wq, wk, wv, bif, gn, *, L):
    b, t, _ = proj3.shape
    Lp = max(L, SUBLANE)
    w = ML_W
    nblk = w // LANE
    lspec3 = lambda s: pl.BlockSpec((None,) + s, lambda bi, c: (layer, 0, 0))
    lspec4 = lambda s: pl.BlockSpec((None,) + s, lambda bi, c: (layer, 0, 0, 0))
    if st_layer is None:
        c0spec = pl.BlockSpec((None, ML_H, ML_DH, ML_DH), lambda bi, c: (bi, 0, 0, 0))
    else:
        c0spec = pl.BlockSpec((None, None, ML_H, ML_DH, ML_DH), lambda bi, c: (st_layer, bi, 0, 0, 0))
    cbytes = ML_H * ML_DH * ML_DH * 4
    vmem = 2 * (3 * L * w * 4 + L * LANE * 4 + 3 * nblk * LANE * LANE * 4 + 2 * cbytes) + cbytes
    vmem += (SUBLANE + Lp) * w * 4 + 60 * Lp * w * 4 + (8 << 20)
    return pl.pallas_call(
        functools.partial(_mlstm_kernel, L=L, Lp=Lp),
        grid=(b, t // L),
        in_specs=[
            pl.BlockSpec((None, L, w), lambda bi, c: (bi, c, C_MX // w)),
            pl.BlockSpec((None, L, w), lambda bi, c: (bi, c, C_MO // w)),
            pl.BlockSpec((None, L, LANE), lambda bi, c: (bi, c, 0)),
            pl.BlockSpec((None, CONV_W - 1, w), lambda bi, c: (bi, 0, 0)),
            lspec3((CONV_W, w)), lspec3((1, w)),
            lspec4((nblk, LANE, LANE)), lspec4((nblk, LANE, LANE)), lspec4((nblk, LANE, LANE)),
            lspec3((1, LANE)), lspec3((1, ML_DH)),
            c0spec,
            pl.BlockSpec((None, ML_H, ML_DH), lambda bi, c: (bi, 0, 0)),
            pl.BlockSpec((None, ML_H, LANE), lambda bi, c: (bi, 0, 0)),
        ],
        out_specs=[
            pl.BlockSpec((None, L, w), lambda bi, c: (bi, c, 0)),
            pl.BlockSpec((None, ML_H, ML_DH, ML_DH), lambda bi, c: (bi, 0, 0, 0)),
            pl.BlockSpec((None, ML_H, ML_DH), lambda bi, c: (bi, 0, 0)),
            pl.BlockSpec((None, ML_H, LANE), lambda bi, c: (bi, 0, 0)),
            pl.BlockSpec((None, CONV_W - 1, w), lambda bi, c: (bi, 0, 0)),
        ],
        out_shape=[SDS((b, t, w), F32), SDS((b, ML_H, ML_DH, ML_DH), F32), SDS((b, ML_H, ML_DH), F32),
                   SDS((b, ML_H, LANE), F32), SDS((b, CONV_W - 1, w), F32)],
        scratch_shapes=[pltpu.VMEM((SUBLANE + Lp, w), F32), pltpu.VMEM((ML_H, ML_DH, ML_DH), F32),
                        pltpu.VMEM((ML_H, ML_DH), F32), pltpu.VMEM((ML_H, LANE), F32)],
        compiler_params=_params(("parallel", "arbitrary"), vmem),
        name="mlstm",
    )(proj3, proj3, small3, cbuf, cw, cb, wq, wk, wv, bif, gn, c0, n0, m0)


def _prep_weights(p):
    w_in = p["w_in"]
    w_main = jnp.concatenate([w_in[:, :, :3072], w_in[:, :, 3088:6160], w_in[:, :, 6168:]], axis=-1).astype(BF16)
    w_small = jnp.concatenate(
        [w_in[:, :, 3072:3088], w_in[:, :, 6160:6168], jnp.zeros((DEPTH, D, LANE - GLA_RANK - 2 * ML_H), F32)],
        axis=-1).astype(BF16)
    wg2 = jnp.concatenate(
        [p["gla_w_g2"], jnp.zeros((DEPTH, LANE - GLA_RANK, GLA_H * GLA_DK), F32)], axis=1)
    eye = jnp.eye(LANE // ML_BS, dtype=F32)

    def dense_bd(wb):
        wb = wb.reshape(DEPTH, ML_W // LANE, LANE // ML_BS, ML_BS, ML_BS)
        return jnp.einsum("dgncx,nm->dgncmx", wb, eye).reshape(DEPTH, ML_W // LANE, LANE, LANE)

    bif = jnp.zeros((DEPTH, 1, LANE), F32).at[:, 0, S_MIF:S_MIF + 2 * ML_H].set(p["ml_b_if"])
    r3 = lambda a: a.reshape(DEPTH, 1, a.shape[-1])
    return dict(
        w_main=w_main, w_small=w_small, wg2=wg2, bg=r3(p["gla_b_g"]), gla_gn=r3(p["gla_g_norm"]),
        g1=r3(p["g_norm1"]), g2=r3(p["g_norm2"]), gf=p["g_final"].reshape(1, D),
        lru_cw=p["lru_conv_w"], lru_cb=r3(p["lru_conv_b"]), lru_wa=p["lru_w_a"], lru_ba=r3(p["lru_b_a"]),
        lru_wx=p["lru_w_x"], lru_bx=r3(p["lru_b_x"]), lru_lam=r3(p["lru_lam"]),
        ml_cw=p["ml_conv_w"], ml_cb=r3(p["ml_conv_b"]), ml_wq=dense_bd(p["ml_w_q"]), ml_wk=dense_bd(p["ml_w_k"]),
        ml_wv=dense_bd(p["ml_w_v"]), ml_bif=bif, ml_gn=r3(p["ml_g_norm"]),
        w_br_lru=p["w_br_lru"].astype(BF16), w_br_gla=p["w_br_gla"].astype(BF16), w_br_ml=p["w_br_ml"].astype(BF16),
        w_out=p["w_out"].astype(BF16), w_ff1=p["w_ff1"].astype(BF16), w_ff2=p["w_ff2"].astype(BF16),
    )


def _tiles(b, t):
    n = b * t
    tm = min(n, 1024)
    if t >= tm:
        rpg, r = t, 1
    else:
        rpg, r = n, tm
    return dict(tm=tm, rpg=rpg, r=r, tc=min(t, 256), lg=min(t, GLA_CHUNK), lm=min(t, ML_CHUNK))


def _run_group(x3, mod, states, w):
    b, t, _ = x3.shape
    n = b * t
    cfg = _tiles(b, t)
    tm, rpg, r = cfg["tm"], cfg["rpg"], cfg["r"]
    tm_ffn = min(tm, 512)
    r_ffn = r if r == 1 else tm_ffn
    x = x3.reshape(n, D)
    new_states = []
    for l in range(DEPTH):
        if states is None:
            lru_h = jnp.zeros((b, 1, LRU_W), F32)
            lru_cv = jnp.zeros((b, CONV_W - 1, LRU_W), F32)
            gla_s = jnp.zeros((b, GLA_H, GLA_DK, GLA_DV), F32)
            ml_c = jnp.zeros((b, ML_H, ML_DH, ML_DH), F32)
            ml_n = jnp.zeros((b, ML_H, ML_DH), F32)
            ml_m = jnp.zeros((b, ML_H, LANE), F32)
            ml_cv = jnp.zeros((b, CONV_W - 1, ML_W), F32)
            st_layer = None
        else:
            s_h, s_cv, s_gla, s_c, s_n, s_m, s_mcv = states
            lru_h = s_h[l].reshape(b, 1, LRU_W)
            lru_cv = s_cv[l]
            gla_s, ml_c = s_gla, s_c
            ml_n = s_n[l]
            ml_m = jnp.broadcast_to(s_m[l][:, :, None], (b, ML_H, LANE))
            ml_cv = s_mcv[l]
            st_layer = l
        proj, small = _in_proj(x, mod, l, w["g1"], w["w_main"], w["w_small"], tm=tm, tn=512, rpg=rpg, r=r)
        proj3 = proj.reshape(b, t, N_MAIN)
        small3 = small.reshape(b, t, LANE)
        y_lru, h_t, cv_t = _lru(proj3, lru_cv, lru_h, l, w["lru_cw"], w["lru_cb"], w["lru_wa"], w["lru_ba"],
                                w["lru_wx"], w["lru_bx"], w["lru_lam"], tc=cfg["tc"])
        y_gla, s_t = _gla(proj3, small3, gla_s, l, st_layer, w["wg2"], w["bg"], w["gla_gn"], L=cfg["lg"])
        y_ml, c_t, n_t, m_t, mcv_t = _mlstm(proj3, small3, ml_cv, ml_c, ml_n, ml_m, l, st_layer,
                                            w["ml_cw"], w["ml_cb"], w["ml_wq"], w["ml_wk"], w["ml_wv"],
                                            w["ml_bif"], w["ml_gn"], L=cfg["lm"])
        merged = _merge(y_lru.reshape(n, LRU_W), y_gla.reshape(n, GLA_H * GLA_DV), y_ml.reshape(n, ML_W), proj, l,
                        w["w_br_lru"], w["w_br_gla"], w["w_br_ml"], tm=tm, tn=512)
        x = _out_proj(merged, x, mod, l, w["w_out"], tm=tm, tn=512, rpg=rpg, r=r)
        mod_ffn = mod if r_ffn == r else mod[:, :, :r_ffn]
        x = _ffn(x, mod_ffn, l, w["g2"], w["gf"], w["w_ff1"], w["w_ff2"], tm=tm_ffn, tf=1024,
                 rpg=rpg if r == 1 else n, r=r_ffn, final_norm=(l == DEPTH - 1))
        new_states.append((h_t.reshape(b, LRU_W), cv_t, s_t, c_t, n_t, m_t[:, :, 0], mcv_t))
    stacked = [jnp.stack([s[i] for s in new_states]) for i in range(7)]
    return x.reshape(b, t, D), stacked


def kernel(x_prompt, x_sample, c_prompt, c_sample, state_lru_h, state_lru_conv, state_gla, state_mlstm_C, state_mlstm_n, state_mlstm_m, state_mlstm_conv, w_ada, b_ada, g_norm1, g_norm2, w_in, lru_conv_w, lru_conv_b, lru_w_a, lru_b_a, lru_w_x, lru_b_x, lru_lam, gla_w_g2, gla_b_g, gla_g_norm, ml_conv_w, ml_conv_b, ml_w_q, ml_w_k, ml_w_v, ml_b_if, ml_g_norm, w_br_lru, w_br_gla, w_br_ml, w_out, w_ff1, w_ff2, g_final):
    p = dict(g_norm1=g_norm1, g_norm2=g_norm2, w_in=w_in, lru_conv_w=lru_conv_w, lru_conv_b=lru_conv_b,
             lru_w_a=lru_w_a, lru_b_a=lru_b_a, lru_w_x=lru_w_x, lru_b_x=lru_b_x, lru_lam=lru_lam,
             gla_w_g2=gla_w_g2, gla_b_g=gla_b_g, gla_g_norm=gla_g_norm, ml_conv_w=ml_conv_w, ml_conv_b=ml_conv_b,
             ml_w_q=ml_w_q, ml_w_k=ml_w_k, ml_w_v=ml_w_v, ml_b_if=ml_b_if, ml_g_norm=ml_g_norm,
             w_br_lru=w_br_lru, w_br_gla=w_br_gla, w_br_ml=w_br_ml, w_out=w_out, w_ff1=w_ff1, w_ff2=w_ff2,
             g_final=g_final)
    w = _prep_weights(p)
    bp, tp, _ = x_prompt.shape
    bs, ts, _ = x_sample.shape
    pad = (-bp) % SUBLANE
    c_all = jnp.concatenate([c_prompt, jnp.zeros((pad, D), F32), c_sample], axis=0)
    mod_all = _ada(c_all, w_ada, b_ada)
    mod_p = mod_all[:, :bp].reshape(DEPTH, bp, 1, N_MOD * D)
    mod_s = jnp.repeat(mod_all[:, bp + pad:], ts, axis=1).reshape(DEPTH, 1, bs * ts, N_MOD * D)
    y_p, ps = _run_group(x_prompt, mod_p, None, w)
    y_s, ss = _run_group(x_sample, mod_s,
                         (state_lru_h, state_lru_conv, state_gla, state_mlstm_C, state_mlstm_n, state_mlstm_m,
                          state_mlstm_conv), w)
    return (y_p, y_s, *ps, *ss)
```

```python
import functools

import jax
import jax.numpy as jnp
from jax import lax
from jax.experimental import pallas as pl
from jax.experimental.pallas import tpu as pltpu

F32, BF16 = jnp.float32, jnp.bfloat16
HIGHEST = lax.Precision.HIGHEST
SDS = jax.ShapeDtypeStruct

D = 2048
DEPTH = 2
LRU_W = 1024
LRU_BLOCKS = 8
LRU_C = 8.0
CONV_W = 4
GLA_H = 4
GLA_DK = 128
GLA_DV = 256
GLA_RANK = 16
GLA_TAU = 16.0
GLA_CHUNK = 64
ML_H = 4
ML_W = 1024
ML_DH = 256
ML_BS = 4
ML_CHUNK = 64
D_FF = 4 * D
EPS = 1e-6
N_MOD = 6
T_STEP = 4
STEP_BB = 4

LANE = 128
SUBLANE = 8
VMEM_LIMIT_CAP = 56 * 1024 * 1024

C_LRU, C_GQ, C_GK, C_GV, C_GGATE, C_MX, C_MO, C_MG = 0, 1024, 1536, 2048, 3072, 4096, 5120, 6144
N_MAIN = 12288
S_GLR = 0
S_IG = 16
S_FG = 20
NEG_BIG = -1e30


def _params(sem, vmem_bytes):
    return pltpu.CompilerParams(dimension_semantics=sem, vmem_limit_bytes=int(min(vmem_bytes, VMEM_LIMIT_CAP)))


def _dot(a, b):
    return jnp.dot(a.astype(BF16), b.astype(BF16), preferred_element_type=F32)


def _dot_nt(a, b):
    return lax.dot_general(a.astype(BF16), b.astype(BF16), (((1,), (1,)), ((), ())), preferred_element_type=F32)


def _dot_hi(a, b):
    return jnp.dot(a, b, precision=HIGHEST, preferred_element_type=F32)


def _split(x):
    hi = x.astype(BF16)
    return hi, (x - hi.astype(F32)).astype(BF16)


def _dot3(x_hi, x_lo, w_hi, w_lo):
    d = lambda a, b: jnp.dot(a, b, preferred_element_type=F32)
    return d(x_hi, w_hi) + (d(x_lo, w_hi) + d(x_hi, w_lo))


def _log_sigmoid(z):
    return jnp.minimum(z, 0.0) - jnp.log1p(jnp.exp(-jnp.abs(z)))


def _silu(z):
    return z * jax.nn.sigmoid(z)


def _rms(x, g):
    return x * lax.rsqrt(jnp.mean(x * x, axis=-1, keepdims=True) + EPS) * g


def _tri(n):
    r = lax.broadcasted_iota(jnp.int32, (n, n), 0)
    c = lax.broadcasted_iota(jnp.int32, (n, n), 1)
    return r >= c


def _lru_gates(ub, wa, wx, ba, bx, lam):
    r = jax.nn.sigmoid(_dot(ub, wa) + ba)
    i = jax.nn.sigmoid(_dot(ub, wx) + bx)
    log_a = LRU_C * r * _log_sigmoid(lam)
    t = jnp.tanh(log_a)
    return jnp.exp(log_a), jnp.sqrt(-2.0 * t / (1.0 - t)) * (i * ub)


def _blockdiag3(x, wh_ref, wl_ref):
    xh, xl = _split(x)
    outs = []
    for blk in range(ML_W // LANE):
        sl = slice(blk * LANE, (blk + 1) * LANE)
        outs.append(_dot3(xh[:, sl], xl[:, sl], wh_ref[blk], wl_ref[blk]))
    return jnp.concatenate(outs, axis=1)


def _ada_kernel(c_ref, w_ref, b_ref, o_ref):
    o_ref[...] = _dot(_silu(c_ref[...]), w_ref[...]) + b_ref[...]


def _ada(c_all, w_ada, b_ada):
    m = c_all.shape[0]
    tn = 512
    n_out = N_MOD * D
    return pl.pallas_call(
        _ada_kernel,
        grid=(DEPTH, n_out // tn),
        in_specs=[
            pl.BlockSpec((m, D), lambda l, n: (0, 0)),
            pl.BlockSpec((None, D, tn), lambda l, n: (l, 0, n)),
            pl.BlockSpec((None, 1, tn), lambda l, n: (l, 0, n)),
        ],
        out_specs=pl.BlockSpec((None, m, tn), lambda l, n: (l, 0, n)),
        out_shape=SDS((DEPTH, m, n_out), F32),
        compiler_params=_params(("parallel", "arbitrary"), 2 * (m * D + D * tn + m * tn) * 4 + (8 << 20)),
        name="ada",
    )(c_all, w_ada, b_ada.reshape(DEPTH, 1, n_out))


def _modspec(layer, comp, r, width, tm, rpg, jmap=None):
    nb = D // width
    jm = (lambda j: 0) if jmap is None else jmap
    if r == 1:
        return pl.BlockSpec((None, None, 1, width), lambda i, j: (layer, (i * tm) // rpg, 0, comp * nb + jm(j)))
    return pl.BlockSpec((None, None, tm, width), lambda i, j: (layer, 0, i, comp * nb + jm(j)))


def _norm_mod_store(xn_s, x_ref, g_ref, sc_ref, sh_ref):
    tm = x_ref.shape[0]
    rc = min(tm, 256)
    for c in range(tm // rc):
        rows = slice(c * rc, (c + 1) * rc)
        mrows = rows if sc_ref.shape[0] == tm else slice(None)
        xn = _rms(x_ref[rows, :], g_ref[...]) * (1.0 + sc_ref[mrows, :]) + sh_ref[mrows, :]
        xn_s[rows, :] = xn.astype(BF16)


def _in_proj_kernel(x_ref, sc_ref, sh_ref, g_ref, wm_ref, ws_ref, om_ref, os_ref, xn_s):
    @pl.when(pl.program_id(1) == 0)
    def _():
        _norm_mod_store(xn_s, x_ref, g_ref, sc_ref, sh_ref)
        os_ref[...] = jnp.dot(xn_s[...], ws_ref[...], preferred_element_type=F32)

    om_ref[...] = jnp.dot(xn_s[...], wm_ref[...], preferred_element_type=F32)


def _in_proj(x, mod, layer, g1, w_main, w_small, *, tm, tn, rpg, r):
    n = x.shape[0]
    vmem = 2 * (tm * D * 4 + 2 * r * D * 4 + D * tn * 2 + D * LANE * 2 + tm * tn * 4 + tm * LANE * 4) + tm * D * 2
    vmem += 8 << 20
    return pl.pallas_call(
        _in_proj_kernel,
        grid=(n // tm, N_MAIN // tn),
        in_specs=[
            pl.BlockSpec((tm, D), lambda i, j: (i, 0)),
            _modspec(layer, 1, r, D, tm, rpg),
            _modspec(layer, 0, r, D, tm, rpg),
            pl.BlockSpec((None, 1, D), lambda i, j: (layer, 0, 0)),
            pl.BlockSpec((None, D, tn), lambda i, j: (layer, 0, j)),
            pl.BlockSpec((None, D, LANE), lambda i, j: (layer, 0, 0)),
        ],
        out_specs=[pl.BlockSpec((tm, tn), lambda i, j: (i, j)), pl.BlockSpec((tm, LANE), lambda i, j: (i, 0))],
        out_shape=[SDS((n, N_MAIN), F32), SDS((n, LANE), F32)],
        scratch_shapes=[pltpu.VMEM((tm, D), BF16)],
        compiler_params=_params(("parallel", "arbitrary"), vmem),
        name="in_proj",
    )(x, mod, mod, g1, w_main, w_small)


def _merge_kernel(yl_ref, yg_ref, ym_ref, g0_ref, g1_ref, g2_ref, w0_ref, w1_ref, w2_ref, o_ref):
    acc = jax.nn.sigmoid(g0_ref[...]) * jnp.dot(yl_ref[...], w0_ref[...], preferred_element_type=F32)
    acc += jax.nn.sigmoid(g1_ref[...]) * jnp.dot(yg_ref[...], w1_ref[...], preferred_element_type=F32)
    acc += jax.nn.sigmoid(g2_ref[...]) * jnp.dot(ym_ref[...], w2_ref[...], preferred_element_type=F32)
    o_ref[...] = acc.astype(BF16)


def _merge(y_lru, y_gla, y_ml, proj, layer, w_lru, w_gla, w_ml, *, tm, tn):
    n = y_lru.shape[0]
    w = LRU_W
    gb = C_MG // tn
    nb = D // tn
    yspec = pl.BlockSpec((tm, w), lambda i, j: (i, 0))
    wspec = pl.BlockSpec((None, w, tn), lambda i, j: (layer, 0, j))
    vmem = 2 * (3 * tm * w * 2 + 3 * tm * tn * 4 + 3 * w * tn * 2 + tm * tn * 2) + 6 * tm * tn * 4 + (4 << 20)
    return pl.pallas_call(
        _merge_kernel,
        grid=(n // tm, nb),
        in_specs=[
            yspec, yspec, yspec,
            pl.BlockSpec((tm, tn), lambda i, j: (i, gb + j)),
            pl.BlockSpec((tm, tn), lambda i, j: (i, gb + nb + j)),
            pl.BlockSpec((tm, tn), lambda i, j: (i, gb + 2 * nb + j)),
            wspec, wspec, wspec,
        ],
        out_specs=pl.BlockSpec((tm, tn), lambda i, j: (i, j)),
        out_shape=SDS((n, D), BF16),
        compiler_params=_params(("parallel", "arbitrary"), vmem),
        name="merge",
    )(y_lru, y_gla, y_ml, proj, proj, proj, w_lru, w_gla, w_ml)


def _out_proj_kernel(m_ref, w_ref, x_ref, gt_ref, o_ref):
    o_ref[...] = x_ref[...] + gt_ref[...] * jnp.dot(m_ref[...], w_ref[...], preferred_element_type=F32)


def _out_proj(merged, x, mod, layer, w_out, *, tm, tn, rpg, r):
    n = x.shape[0]
    vmem = 2 * (tm * D * 2 + D * tn * 2 + 2 * tm * tn * 4 + r * tn * 4) + 2 * tm * tn * 4 + (4 << 20)
    return pl.pallas_call(
        _out_proj_kernel,
        grid=(n // tm, D // tn),
        in_specs=[
            pl.BlockSpec((tm, D), lambda i, j: (i, 0)),
            pl.BlockSpec((None, D, tn), lambda i, j: (layer, 0, j)),
            pl.BlockSpec((tm, tn), lambda i, j: (i, j)),
            _modspec(layer, 2, r, tn, tm, rpg, jmap=lambda j: j),
        ],
        out_specs=pl.BlockSpec((tm, tn), lambda i, j: (i, j)),
        out_shape=SDS((n, D), F32),
        compiler_params=_params(("parallel", "arbitrary"), vmem),
        name="out_proj",
    )(merged, w_out, x, mod)


def _ffn_kernel(x_ref, sc_ref, sh_ref, gt_ref, g_ref, gf_ref, w1_ref, w2_ref, o_ref, xn_s, *, final_norm):
    f = pl.program_id(1)

    @pl.when(f == 0)
    def _():
        _norm_mod_store(xn_s, x_ref, g_ref, sc_ref, sh_ref)
        o_ref[...] = jnp.zeros_like(o_ref)

    h = jnp.square(jnp.maximum(jnp.dot(xn_s[...], w1_ref[...], preferred_element_type=F32), 0.0))
    o_ref[...] += jnp.dot(h.astype(BF16), w2_ref[...], preferred_element_type=F32)

    @pl.when(f == pl.num_programs(1) - 1)
    def _():
        y = x_ref[...] + gt_ref[...] * o_ref[...]
        if final_norm:
            y = _rms(y, gf_ref[...])
        o_ref[...] = y


def _ffn(x, mod, layer, g2, g_final, w1, w2, *, tm, tf, rpg, r, final_norm):
    n = x.shape[0]
    vmem = 2 * (2 * tm * D * 4 + 3 * r * D * 4 + 2 * D * tf * 2) + tm * D * 2 + 2 * tm * tf * 4 + 2 * tm * D * 4
    return pl.pallas_call(
        functools.partial(_ffn_kernel, final_norm=final_norm),
        grid=(n // tm, D_FF // tf),
        in_specs=[
            pl.BlockSpec((tm, D), lambda i, f: (i, 0)),
            _modspec(layer, 4, r, D, tm, rpg),
            _modspec(layer, 3, r, D, tm, rpg),
            _modspec(layer, 5, r, D, tm, rpg),
            pl.BlockSpec((None, 1, D), lambda i, f: (layer, 0, 0)),
            pl.BlockSpec((1, D), lambda i, f: (0, 0)),
            pl.BlockSpec((None, D, tf), lambda i, f: (layer, 0, f)),
            pl.BlockSpec((None, tf, D), lambda i, f: (layer, f, 0)),
        ],
        out_specs=pl.BlockSpec((tm, D), lambda i, f: (i, 0)),
        out_shape=SDS((n, D), F32),
        scratch_shapes=[pltpu.VMEM((tm, D), BF16)],
        compiler_params=_params(("parallel", "arbitrary"), vmem),
        name="ffn",
    )(x, mod, mod, mod, g2, g_final, w1, w2)


def _lru_kernel(x_ref, cw_ref, cb_ref, wa_ref, ba_ref, wx_ref, bx_ref, lam_ref,
                y_ref, ht_ref, ct_ref, xp_s, a_s, u_s, h_s, *, tc):
    @pl.when(pl.program_id(1) == 0)
    def _():
        xp_s[0:SUBLANE, :] = jnp.zeros((SUBLANE, LRU_W), F32)
        h_s[...] = jnp.zeros_like(h_s)

    xp_s[SUBLANE:SUBLANE + tc, :] = x_ref[...]
    w = cw_ref[...]
    u = cb_ref[...] + xp_s[5:5 + tc, :] * w[0:1]
    for j in range(1, CONV_W):
        u = u + xp_s[5 + j:5 + j + tc, :] * w[j:j + 1]
    tail = xp_s[5 + tc:8 + tc, :]
    xp_s[5:8, :] = tail

    for blk in range(LRU_BLOCKS):
        sl = slice(blk * LANE, (blk + 1) * LANE)
        a_s[:, sl], u_s[:, sl] = _lru_gates(u[:, sl], wa_ref[blk], wx_ref[blk], ba_ref[:, sl], bx_ref[:, sl],
                                            lam_ref[:, sl])

    def step(t, h):
        h = a_s[pl.ds(t, 1), :] * h + u_s[pl.ds(t, 1), :]
        a_s[pl.ds(t, 1), :] = h
        return h

    h = lax.fori_loop(0, tc, step, h_s[...], unroll=SUBLANE)
    h_s[...] = h
    y_ref[...] = a_s[...].astype(BF16)

    @pl.when(pl.program_id(1) == pl.num_programs(1) - 1)
    def _():
        ht_ref[...] = h
        ct_ref[...] = tail


def _lru(proj3, layer, cw, cb, wa, ba, wx, bx, lam, *, tc):
    b, t, _ = proj3.shape
    w = LRU_W
    lspec3 = lambda s: pl.BlockSpec((None,) + s, lambda bi, c: (layer, 0, 0))
    lspec4 = lambda s: pl.BlockSpec((None,) + s, lambda bi, c: (layer, 0, 0, 0))
    vmem = 2 * (tc * w * 4 + tc * w * 2 + 2 * LRU_BLOCKS * LANE * LANE * 4) + (3 * tc + 16) * w * 4 + 8 * tc * w * 4
    return pl.pallas_call(
        functools.partial(_lru_kernel, tc=tc),
        grid=(b, t // tc),
        in_specs=[
            pl.BlockSpec((None, tc, w), lambda bi, c: (bi, c, C_LRU // w)),
            lspec3((CONV_W, w)), lspec3((1, w)),
            lspec4((LRU_BLOCKS, LANE, LANE)), lspec3((1, w)),
            lspec4((LRU_BLOCKS, LANE, LANE)), lspec3((1, w)),
            lspec3((1, w)),
        ],
        out_specs=[
            pl.BlockSpec((None, tc, w), lambda bi, c: (bi, c, 0)),
            pl.BlockSpec((None, 1, w), lambda bi, c: (bi, 0, 0)),
            pl.BlockSpec((None, CONV_W - 1, w), lambda bi, c: (bi, 0, 0)),
        ],
        out_shape=[SDS((b, t, w), BF16), SDS((b, 1, w), F32), SDS((b, CONV_W - 1, w), F32)],
        scratch_shapes=[pltpu.VMEM((SUBLANE + tc, w), F32), pltpu.VMEM((tc, w), F32),
                        pltpu.VMEM((tc, w), F32), pltpu.VMEM((1, w), F32)],
        compiler_params=_params(("parallel", "arbitrary"), vmem + (4 << 20)),
        name="lru",
    )(proj3, cw, cb, wa, ba, wx, bx, lam)


def _gla_kernel(q_ref, k_ref, v_ref, gate_ref, small_ref, wg2_ref, bg_ref, gn_ref, y_ref, st_ref, s_s, *, L):
    @pl.when(pl.program_id(1) == 0)
    def _():
        s_s[...] = jnp.zeros_like(s_s)

    q, k, v, gate = q_ref[...], k_ref[...], v_ref[...], gate_ref[...]
    logdec = _log_sigmoid(_dot(small_ref[...], wg2_ref[...]) + bg_ref[...]) * (1.0 / GLA_TAU)
    tri = _tri(L)
    bcum = _dot_hi(tri.astype(F32), logdec)
    scale = GLA_DK ** -0.5
    for h in range(GLA_H):
        ks = slice(h * GLA_DK, (h + 1) * GLA_DK)
        vs = slice(h * GLA_DV, (h + 1) * GLA_DV)
        bh = bcum[:, ks]
        qe = (q[:, ks] * scale) * jnp.exp(bh)
        ke = k[:, ks] * jnp.exp(-bh)
        a = jnp.where(tri, _dot_nt(qe, ke), 0.0)
        s_h = s_s[h]
        o = _dot(qe, s_h) + _dot(a, v[:, vs])
        bl = bh[L - 1:L, :]
        kd = k[:, ks] * jnp.exp(bl - bh)
        dec_col = jnp.exp(jnp.broadcast_to(bl, (SUBLANE, GLA_DK))).T[:, 0:1]
        s_s[h] = dec_col * s_h + _dot(kd.T, v[:, vs])
        y_ref[:, vs] = (_rms(o, gn_ref[...]) * _silu(gate[:, vs])).astype(BF16)

    @pl.when(pl.program_id(1) == pl.num_programs(1) - 1)
    def _():
        st_ref[...] = s_s[...]


def _gla(proj3, small3, layer, wg2, bg, gn, *, L):
    b, t, _ = proj3.shape
    kw, vw = GLA_H * GLA_DK, GLA_H * GLA_DV
    sbytes = GLA_H * GLA_DK * GLA_DV * 4
    vmem = 2 * (2 * L * kw * 4 + 2 * L * vw * 4 + L * vw * 2 + L * LANE * 4 + LANE * kw * 4 + sbytes)
    vmem += sbytes + 40 * L * vw * 4 + (8 << 20)
    return pl.pallas_call(
        functools.partial(_gla_kernel, L=L),
        grid=(b, t // L),
        in_specs=[
            pl.BlockSpec((None, L, kw), lambda bi, c: (bi, c, C_GQ // kw)),
            pl.BlockSpec((None, L, kw), lambda bi, c: (bi, c, C_GK // kw)),
            pl.BlockSpec((None, L, vw), lambda bi, c: (bi, c, C_GV // vw)),
            pl.BlockSpec((None, L, vw), lambda bi, c: (bi, c, C_GGATE // vw)),
            pl.BlockSpec((None, L, LANE), lambda bi, c: (bi, c, 0)),
            pl.BlockSpec((None, LANE, kw), lambda bi, c: (layer, 0, 0)),
            pl.BlockSpec((None, 1, kw), lambda bi, c: (layer, 0, 0)),
            pl.BlockSpec((None, 1, GLA_DV), lambda bi, c: (layer, 0, 0)),
        ],
        out_specs=[
            pl.BlockSpec((None, L, vw), lambda bi, c: (bi, c, 0)),
            pl.BlockSpec((None, GLA_H, GLA_DK, GLA_DV), lambda bi, c: (bi, 0, 0, 0)),
        ],
        out_shape=[SDS((b, t, vw), BF16), SDS((b, GLA_H, GLA_DK, GLA_DV), F32)],
        scratch_shapes=[pltpu.VMEM((GLA_H, GLA_DK, GLA_DV), F32)],
        compiler_params=_params(("parallel", "arbitrary"), vmem),
        name="gla",
    )(proj3, proj3, proj3, proj3, small3, wg2, bg, gn)


def _mlstm_kernel(mx_ref, mo_ref, small_ref, cw_ref, cb_ref, wqh_ref, wql_ref, wkh_ref, wkl_ref, wvh_ref, wvl_ref,
                  bif_ref, gn_ref, y_ref, ct_ref, nt_ref, mt_ref, convt_ref, xp_s, c_s, n_s, m_s, *, L):
    @pl.when(pl.program_id(1) == 0)
    def _():
        xp_s[0:SUBLANE, :] = jnp.zeros((SUBLANE, ML_W), F32)
        c_s[...] = jnp.zeros_like(c_s)
        n_s[...] = jnp.zeros_like(n_s)
        m_s[...] = jnp.zeros_like(m_s)

    xp_s[SUBLANE:SUBLANE + L, :] = mx_ref[...]
    w = cw_ref[...]
    conv = cb_ref[...] + xp_s[5:5 + L, :] * w[0:1]
    for j in range(1, CONV_W):
        conv = conv + xp_s[5 + j:5 + j + L, :] * w[j:j + 1]
    mx = mx_ref[...]
    tail = xp_s[5 + L:8 + L, :]
    xp_s[5:8, :] = tail
    mc = _silu(conv)

    mch, mcl = _split(mc)
    mxh, mxl = _split(mx)
    q_blocks, k_blocks, v_blocks = [], [], []
    for blk in range(ML_W // LANE):
        sl = slice(blk * LANE, (blk + 1) * LANE)
        q_blocks.append(_dot3(mch[:, sl], mcl[:, sl], wqh_ref[blk], wql_ref[blk]))
        k_blocks.append(_dot3(mch[:, sl], mcl[:, sl], wkh_ref[blk], wkl_ref[blk]) * (ML_DH ** -0.5))
        v_blocks.append(_dot3(mxh[:, sl], mxl[:, sl], wvh_ref[blk], wvl_ref[blk]))

    gates = small_ref[...] + bif_ref[...]
    lf = _log_sigmoid(gates)
    gates_t = gates.T
    lf_t = _log_sigmoid(gates_t)
    tri = _tri(L)
    f_col = _dot_hi(tri.astype(F32), lf)
    r_i = lax.broadcasted_iota(jnp.int32, (L, L), 0)
    c_i = lax.broadcasted_iota(jnp.int32, (L, L), 1)
    f_row = _dot_hi(lf_t, (r_i <= c_i).astype(F32))

    mo = mo_ref[...]
    bpb = ML_DH // LANE
    for h in range(ML_H):
        q = jnp.concatenate(q_blocks[bpb * h:bpb * h + bpb], axis=1)
        k = jnp.concatenate(k_blocks[bpb * h:bpb * h + bpb], axis=1)
        v = jnp.concatenate(v_blocks[bpb * h:bpb * h + bpb], axis=1)
        fc = f_col[:, S_FG + h:S_FG + h + 1]
        fr = f_row[S_FG + h:S_FG + h + 1, :]
        igr = gates_t[S_IG + h:S_IG + h + 1, :]
        m_h = m_s[h:h + 1, 0:1]
        dm = jnp.where(tri, fc - fr + igr, NEG_BIG)
        inter = fc + m_h
        mt = jnp.maximum(inter, jnp.max(dm, axis=-1, keepdims=True))
        wgt = jnp.exp(dm - mt)
        ci = jnp.exp(inter - mt)
        s = _dot_nt(q, k) * wgt
        c_h = c_s[h]
        n_h = n_s[h:h + 1, :]
        num = ci * _dot(q, c_h) + _dot(s, v)
        den = ci * jnp.sum(q * n_h, axis=-1, keepdims=True) + jnp.sum(s, axis=-1, keepdims=True)
        hh = num / jnp.maximum(jnp.abs(den), jnp.exp(-mt))
        fl = fr[:, L - 1:L]
        dj = fl - fr + igr
        m_new = jnp.maximum(fl + m_h, jnp.max(dj, axis=-1, keepdims=True))
        cs = jnp.exp(fl + m_h - m_new)
        wj = jnp.exp(dj - m_new)
        c_s[h] = cs * c_h + _dot(k.T * wj, v)
        n_s[h:h + 1, :] = cs * n_h + _dot(jnp.broadcast_to(wj, (SUBLANE, L)), k)[0:1, :]
        m_s[h:h + 1, :] = jnp.broadcast_to(m_new, (1, LANE))
        sl = slice(h * ML_DH, (h + 1) * ML_DH)
        y_ref[:, sl] = (jax.nn.sigmoid(mo[:, sl]) * _rms(hh, gn_ref[...])).astype(BF16)

    @pl.when(pl.program_id(1) == pl.num_programs(1) - 1)
    def _():
        ct_ref[...] = c_s[...]
        nt_ref[...] = n_s[...]
        mt_ref[...] = m_s[...]
        convt_ref[...] = tail


def _mlstm(proj3, small3, layer, cw, cb, wq, wk, wv, bif, gn, *, L):
    b, t, _ = proj3.shape
    w = ML_W
    nblk = w // LANE
    lspec3 = lambda s: pl.BlockSpec((None,) + s, lambda bi, c: (layer, 0, 0))
    lspec4 = lambda s: pl.BlockSpec((None,) + s, lambda bi, c: (layer, 0, 0, 0))
    cbytes = ML_H * ML_DH * ML_DH * 4
    vmem = 2 * (2 * L * w * 4 + L * w * 2 + L * LANE * 4 + 6 * nblk * LANE * LANE * 2 + cbytes) + cbytes
    vmem += (SUBLANE + L) * w * 4 + 60 * L * w * 4 + (8 << 20)
    wspecs = [lspec4((nblk, LANE, LANE))] * 6
    return pl.pallas_call(
        functools.partial(_mlstm_kernel, L=L),
        grid=(b, t // L),
        in_specs=[
            pl.BlockSpec((None, L, w), lambda bi, c: (bi, c, C_MX // w)),
            pl.BlockSpec((None, L, w), lambda bi, c: (bi, c, C_MO // w)),
            pl.BlockSpec((None, L, LANE), lambda bi, c: (bi, c, 0)),
            lspec3((CONV_W, w)), lspec3((1, w)),
            *wspecs,
            lspec3((1, LANE)), lspec3((1, ML_DH)),
        ],
        out_specs=[
            pl.BlockSpec((None, L, w), lambda bi, c: (bi, c, 0)),
            pl.BlockSpec((None, ML_H, ML_DH, ML_DH), lambda bi, c: (bi, 0, 0, 0)),
            pl.BlockSpec((None, ML_H, ML_DH), lambda bi, c: (bi, 0, 0)),
            pl.BlockSpec((None, ML_H, LANE), lambda bi, c: (bi, 0, 0)),
            pl.BlockSpec((None, CONV_W - 1, w), lambda bi, c: (bi, 0, 0)),
        ],
        out_shape=[SDS((b, t, w), BF16), SDS((b, ML_H, ML_DH, ML_DH), F32), SDS((b, ML_H, ML_DH), F32),
                   SDS((b, ML_H, LANE), F32), SDS((b, CONV_W - 1, w), F32)],
        scratch_shapes=[pltpu.VMEM((SUBLANE + L, w), F32), pltpu.VMEM((ML_H, ML_DH, ML_DH), F32),
                        pltpu.VMEM((ML_H, ML_DH), F32), pltpu.VMEM((ML_H, LANE), F32)],
        compiler_params=_params(("parallel", "arbitrary"), vmem),
        name="mlstm",
    )(proj3, proj3, small3, cw, cb, *wq, *wk, *wv, bif, gn)


def _tmask(shape):
    return lax.broadcasted_iota(jnp.int32, shape, 0) % T_STEP


def _down(x, s):
    return x if s == 0 else pltpu.roll(x, s, 0)


def _up(x, s):
    return x if s == 0 else pltpu.roll(x, x.shape[0] - s, 0)


def _seg_cumsum(x):
    t = _tmask(x.shape)
    out = x
    for s in range(1, T_STEP):
        out = out + jnp.where(t >= s, _down(x, s), 0.0)
    return out


def _seg_last(x):
    t = _tmask(x.shape)
    out = x
    for s in range(1, T_STEP):
        out = jnp.where(t == T_STEP - 1 - s, _up(x, s), out)
    return out


def _seg_allreduce(x, op):
    t = _tmask(x.shape)
    y = op(x, jnp.where(t % 2 == 1, _down(x, 1), _up(x, 1)))
    return op(y, jnp.where(t >= 2, _down(y, 2), _up(y, 2)))


def _conv_rows(x, e, w, b):
    t = _tmask(x.shape)
    acc = b + x * w[CONV_W - 1:CONV_W]
    for k in range(1, CONV_W):
        hist = jnp.where(t >= k, _down(x, k), _up(e, CONV_W - 1 - k))
        acc = acc + hist * w[CONV_W - 1 - k:CONV_W - k]
    return acc


def _col(x, lane):
    li = lax.broadcasted_iota(jnp.int32, x.shape, 1)
    return jnp.sum(jnp.where(li == lane, x, 0.0), axis=-1, keepdims=True)


def _sample_mixer_kernel(lx_ref, q_ref, k_ref, v_ref, gate_ref, mx_ref, mo_ref, small_ref,
                         el_ref, em_ref, h0_ref, n0_ref, m0_ref, s0_ref, c0_ref,
                         lcw_ref, lcb_ref, wa_ref, ba_ref, wx_ref, bx_ref, lam_ref,
                         wg2_ref, bg_ref, ggn_ref,
                         mcw_ref, mcb_ref, wqh_ref, wql_ref, wkh_ref, wkl_ref, wvh_ref, wvl_ref, bif_ref, mgn_ref,
                         *rest):
    yl_ref, yg_ref, ym_ref, hrow_ref, nrow_ref, mrow_ref, st_ref, ct_ref = rest[-8:]
    rows = lx_ref.shape[0]
    n_pairs = rows // SUBLANE
    seq_per_tile = SUBLANE // T_STEP
    t_col = _tmask((rows, 1))
    row8 = lax.broadcasted_iota(jnp.int32, (SUBLANE, 1), 0)

    u = _conv_rows(lx_ref[...], el_ref[...], lcw_ref[...], lcb_ref[...])
    a_blocks, g_blocks = [], []
    for blk in range(LRU_BLOCKS):
        sl = slice(blk * LANE, (blk + 1) * LANE)
        a_b, g_b = _lru_gates(u[:, sl], wa_ref[blk], wx_ref[blk], ba_ref[:, sl], bx_ref[:, sl], lam_ref[:, sl])
        a_blocks.append(a_b)
        g_blocks.append(g_b)
    a = jnp.concatenate(a_blocks, axis=1)
    t_w = _tmask(a.shape)
    g = jnp.concatenate(g_blocks, axis=1) + jnp.where(t_w == 0, a * h0_ref[...], 0.0)
    a1 = jnp.where(t_w >= 1, a * _down(a, 1), a)
    g1 = jnp.where(t_w >= 1, g + a * _down(g, 1), g)
    h = jnp.where(t_w >= 2, g1 + a1 * _down(g1, 2), g1)
    yl_ref[...] = h.astype(BF16)
    hrow_ref[...] = h

    q, k, v = q_ref[...], k_ref[...], v_ref[...]
    logdec = _log_sigmoid(_dot(small_ref[...], wg2_ref[...]) + bg_ref[...]) * (1.0 / GLA_TAU)
    bc = _seg_cumsum(logdec)
    bl = _seg_last(bc)
    qs = q * (GLA_DK ** -0.5)
    o_heads = [jnp.zeros((rows, GLA_DV), F32) for _ in range(GLA_H)]
    for s in range(T_STEP):
        prod = qs * _down(k, s) * jnp.exp(bc - _down(bc, s))
        v_s = _down(v, s)
        for hh in range(GLA_H):
            a_sh = jnp.sum(prod[:, hh * GLA_DK:(hh + 1) * GLA_DK], axis=-1, keepdims=True)
            a_sh = jnp.where(t_col >= s, a_sh, 0.0)
            o_heads[hh] = o_heads[hh] + a_sh * v_s[:, hh * GLA_DV:(hh + 1) * GLA_DV]
    qe = qs * jnp.exp(bc)
    kd = k * jnp.exp(bl - bc)
    dec = jnp.exp(bl)
    gate = gate_ref[...]
    for hh in range(GLA_H):
        ks = slice(hh * GLA_DK, (hh + 1) * GLA_DK)
        vs = slice(hh * GLA_DV, (hh + 1) * GLA_DV)
        o_state = []
        for p in range(n_pairs):
            r8 = slice(p * SUBLANE, (p + 1) * SUBLANE)
            kd_t = kd[r8, ks].T
            dec_t = dec[r8, ks].T
            res = None
            for j in range(seq_per_tile):
                b = p * seq_per_tile + j
                s_b = s0_ref[b, hh]
                r_j = _dot(qe[r8, ks], s_b)
                res = r_j if res is None else jnp.where(row8 // T_STEP == j, r_j, res)
                v_j = jnp.where(row8 // T_STEP == j, v[r8, vs], 0.0)
                last = j * T_STEP + T_STEP - 1
                st_ref[b, hh] = dec_t[:, last:last + 1] * s_b + _dot(kd_t, v_j)
            o_state.append(res)
        o = o_heads[hh] + jnp.concatenate(o_state, axis=0)
        yg_ref[:, vs] = (_rms(o, ggn_ref[...]) * _silu(gate[:, vs])).astype(BF16)

    mx = mx_ref[...]
    mc = _silu(_conv_rows(mx, em_ref[...], mcw_ref[...], mcb_ref[...]))
    mq = _blockdiag3(mc, wqh_ref, wql_ref)
    mk = _blockdiag3(mc, wkh_ref, wkl_ref) * (ML_DH ** -0.5)
    mv = _blockdiag3(mx, wvh_ref, wvl_ref)
    gts = small_ref[...] + bif_ref[...]
    ig = pltpu.roll(gts, S_FG - S_IG, 1)
    fcum = _seg_cumsum(_log_sigmoid(gts))
    flast = _seg_last(fcum)
    m0 = m0_ref[...]
    t_g = _tmask(gts.shape)
    inter = fcum + m0
    dms = [jnp.where(t_g >= s, fcum - _down(fcum, s) + _down(ig, s), NEG_BIG) for s in range(T_STEP)]
    mt = inter
    for dm in dms:
        mt = jnp.maximum(mt, dm)
    ci_t = jnp.exp(inter - mt)
    emt_t = jnp.exp(-mt)
    w_t = [jnp.exp(dm - mt) for dm in dms]
    dj = flast - fcum + ig
    m_new = jnp.maximum(flast + m0, _seg_allreduce(dj, jnp.maximum))
    cs_t = jnp.exp(flast + m0 - m_new)
    wj_t = jnp.exp(dj - m_new)
    mrow_ref[...] = m_new
    mo = mo_ref[...]
    n0 = n0_ref[...]
    for hh in range(ML_H):
        sl = slice(hh * ML_DH, (hh + 1) * ML_DH)
        lane = S_FG + hh
        qh, kh, vh = mq[:, sl], mk[:, sl], mv[:, sl]
        ci, emt, cs, wj = _col(ci_t, lane), _col(emt_t, lane), _col(cs_t, lane), _col(wj_t, lane)
        num = jnp.zeros((rows, ML_DH), F32)
        den = ci * jnp.sum(qh * n0[:, sl], axis=-1, keepdims=True)
        for s in range(T_STEP):
            sc = jnp.sum(qh * _down(kh, s), axis=-1, keepdims=True) * _col(w_t[s], lane)
            num = num + sc * _down(vh, s)
            den = den + sc
        kw = kh * wj
        nrow_ref[:, sl] = cs * n0[:, sl] + _seg_allreduce(kw, jnp.add)
        qc = []
        for p in range(n_pairs):
            r8 = slice(p * SUBLANE, (p + 1) * SUBLANE)
            kw_t = kw[r8, :].T
            res = None
            for j in range(seq_per_tile):
                b = p * seq_per_tile + j
                c_b = c0_ref[b, hh]
                r_j = _dot(qh[r8, :], c_b)
                res = r_j if res is None else jnp.where(row8 // T_STEP == j, r_j, res)
                v_j = jnp.where(row8 // T_STEP == j, vh[r8, :], 0.0)
                last = p * SUBLANE + j * T_STEP + T_STEP - 1
                ct_ref[b, hh] = cs[last:last + 1, :] * c_b + _dot(kw_t, v_j)
            qc.append(res)
        num = num + ci * jnp.concatenate(qc, axis=0)
        hcell = num / jnp.maximum(jnp.abs(den), emt)
        ym_ref[:, sl] = (jax.nn.sigmoid(mo[:, sl]) * _rms(hcell, mgn_ref[...])).astype(BF16)


def _sample_mixer(proj, small, el, em, h0e, n0e, m0e, s_state, c_state, layer, w, prev):
    n = proj.shape[0]
    nseq = n // T_STEP
    bb = STEP_BB
    rows = bb * T_STEP
    row = lambda width, blk: pl.BlockSpec((rows, width), lambda i: (i, blk))
    lrow = lambda width: pl.BlockSpec((None, rows, width), lambda i: (layer, i, 0))
    lw3 = lambda s: pl.BlockSpec((None,) + s, lambda i: (layer, 0, 0))
    lw4 = lambda s: pl.BlockSpec((None,) + s, lambda i: (layer, 0, 0, 0))
    sspec = pl.BlockSpec((None, bb, GLA_H, GLA_DK, GLA_DV), lambda i: (layer, i, 0, 0, 0))
    cspec = pl.BlockSpec((None, bb, ML_H, ML_DH, ML_DH), lambda i: (layer, i, 0, 0, 0))
    nblk = ML_W // LANE
    kw, vw = GLA_H * GLA_DK, GLA_H * GLA_DV
    in_specs = [
        row(LRU_W, C_LRU // LRU_W), row(kw, C_GQ // kw), row(kw, C_GK // kw), row(vw, C_GV // vw),
        row(vw, C_GGATE // vw), row(ML_W, C_MX // ML_W), row(ML_W, C_MO // ML_W), row(LANE, 0),
        lrow(LRU_W), lrow(ML_W), lrow(LRU_W), lrow(ML_W), lrow(LANE), sspec, cspec,
        lw3((CONV_W, LRU_W)), lw3((1, LRU_W)), lw4((LRU_BLOCKS, LANE, LANE)), lw3((1, LRU_W)),
        lw4((LRU_BLOCKS, LANE, LANE)), lw3((1, LRU_W)), lw3((1, LRU_W)),
        lw3((LANE, kw)), lw3((1, kw)), lw3((1, GLA_DV)),
        lw3((CONV_W, ML_W)), lw3((1, ML_W)), *([lw4((nblk, LANE, LANE))] * 6), lw3((1, LANE)), lw3((1, ML_DH)),
    ]
    args = [proj] * 7 + [small, el, em, h0e, n0e, m0e, s_state, c_state,
                         w["lru_cw"], w["lru_cb"], w["lru_wa"], w["lru_ba"], w["lru_wx"], w["lru_bx"], w["lru_lam"],
                         w["wg2"], w["bg"], w["gla_gn"],
                         w["ml_cw"], w["ml_cb"], *w["ml_wq"], *w["ml_wk"], *w["ml_wv"], w["ml_bif"], w["ml_gn"]]
    aliases = {}
    if prev is not None:
        aliases = {len(in_specs) + j: 3 + j for j in range(len(prev))}
        in_specs += [pl.BlockSpec(memory_space=pl.ANY)] * len(prev)
        args += list(prev)
    sbytes = bb * GLA_H * GLA_DK * GLA_DV * 4
    cbytes = bb * ML_H * ML_DH * ML_DH * 4
    vmem = 4 * (sbytes + cbytes) + 2 * rows * (7 * 1024 + 8 * 1024) * 4 + 80 * rows * 1024 * 4 + (12 << 20)
    return pl.pallas_call(
        _sample_mixer_kernel,
        grid=(nseq // bb,),
        in_specs=in_specs,
        out_specs=[row(LRU_W, 0), row(vw, 0), row(ML_W, 0), lrow(LRU_W), lrow(ML_W), lrow(LANE), sspec, cspec],
        out_shape=[SDS((n, LRU_W), BF16), SDS((n, vw), BF16), SDS((n, ML_W), BF16),
                   SDS((DEPTH, n, LRU_W), F32), SDS((DEPTH, n, ML_W), F32), SDS((DEPTH, n, LANE), F32),
                   SDS((DEPTH, nseq, GLA_H, GLA_DK, GLA_DV), F32), SDS((DEPTH, nseq, ML_H, ML_DH, ML_DH), F32)],
        input_output_aliases=aliases,
        compiler_params=_params(("arbitrary",), vmem),
        name="sample_mixer",
    )(*args)


def _prep_weights(p):
    w_in = p["w_in"]
    w_main = jnp.concatenate([w_in[:, :, :3072], w_in[:, :, 3088:6160], w_in[:, :, 6168:]], axis=-1).astype(BF16)
    w_small = jnp.concatenate(
        [w_in[:, :, 3072:3088], w_in[:, :, 6160:6168], jnp.zeros((DEPTH, D, LANE - GLA_RANK - 2 * ML_H), F32)],
        axis=-1).astype(BF16)
    wg2 = jnp.concatenate(
        [p["gla_w_g2"], jnp.zeros((DEPTH, LANE - GLA_RANK, GLA_H * GLA_DK), F32)], axis=1)
    eye = jnp.eye(LANE // ML_BS, dtype=F32)

    def dense_bd(wb):
        wb = wb.reshape(DEPTH, ML_W // LANE, LANE // ML_BS, ML_BS, ML_BS)
        dense = jnp.einsum("dgncx,nm->dgncmx", wb, eye).reshape(DEPTH, ML_W // LANE, LANE, LANE)
        hi = dense.astype(BF16)
        return hi, (dense - hi.astype(F32)).astype(BF16)

    bif = jnp.zeros((DEPTH, 1, LANE), F32).at[:, 0, S_IG:S_IG + 2 * ML_H].set(p["ml_b_if"])
    r3 = lambda a: a.reshape(DEPTH, 1, a.shape[-1])
    return dict(
        w_main=w_main, w_small=w_small, wg2=wg2, bg=r3(p["gla_b_g"]), gla_gn=r3(p["gla_g_norm"]),
        g1=r3(p["g_norm1"]), g2=r3(p["g_norm2"]), gf=p["g_final"].reshape(1, D),
        lru_cw=p["lru_conv_w"], lru_cb=r3(p["lru_conv_b"]), lru_wa=p["lru_w_a"], lru_ba=r3(p["lru_b_a"]),
        lru_wx=p["lru_w_x"], lru_bx=r3(p["lru_b_x"]), lru_lam=r3(p["lru_lam"]),
        ml_cw=p["ml_conv_w"], ml_cb=r3(p["ml_conv_b"]), ml_wq=dense_bd(p["ml_w_q"]), ml_wk=dense_bd(p["ml_w_k"]),
        ml_wv=dense_bd(p["ml_w_v"]), ml_bif=bif, ml_gn=r3(p["ml_g_norm"]),
        w_br_lru=p["w_br_lru"].astype(BF16), w_br_gla=p["w_br_gla"].astype(BF16), w_br_ml=p["w_br_ml"].astype(BF16),
        w_out=p["w_out"].astype(BF16), w_ff1=p["w_ff1"].astype(BF16), w_ff2=p["w_ff2"].astype(BF16),
    )


def _dense_tail(x, y_lru, y_gla, y_ml, proj, mod, l, w, *, tm, rpg, r, tm_ffn, rpg_ffn, r_ffn):
    merged = _merge(y_lru, y_gla, y_ml, proj, l, w["w_br_lru"], w["w_br_gla"], w["w_br_ml"], tm=tm, tn=512)
    x = _out_proj(merged, x, mod, l, w["w_out"], tm=tm, tn=512, rpg=rpg, r=r)
    return _ffn(x, mod, l, w["g2"], w["gf"], w["w_ff1"], w["w_ff2"], tm=tm_ffn, tf=1024, rpg=rpg_ffn, r=r_ffn,
                final_norm=(l == DEPTH - 1))


def _run_prompt(x3, mod, w):
    b, t, _ = x3.shape
    n = b * t
    tm = min(t, 1024)
    x = x3.reshape(n, D)
    states = []
    for l in range(DEPTH):
        proj, small = _in_proj(x, mod, l, w["g1"], w["w_main"], w["w_small"], tm=tm, tn=512, rpg=t, r=1)
        proj3 = proj.reshape(b, t, N_MAIN)
        small3 = small.reshape(b, t, LANE)
        y_lru, h_t, cv_t = _lru(proj3, l, w["lru_cw"], w["lru_cb"], w["lru_wa"], w["lru_ba"], w["lru_wx"],
                                w["lru_bx"], w["lru_lam"], tc=min(t, 256))
        y_gla, s_t = _gla(proj3, small3, l, w["wg2"], w["bg"], w["gla_gn"], L=min(t, GLA_CHUNK))
        y_ml, c_t, n_t, m_t, mcv_t = _mlstm(proj3, small3, l, w["ml_cw"], w["ml_cb"], w["ml_wq"], w["ml_wk"],
                                            w["ml_wv"], w["ml_bif"], w["ml_gn"], L=min(t, ML_CHUNK))
        x = _dense_tail(x, y_lru.reshape(n, LRU_W), y_gla.reshape(n, GLA_H * GLA_DV), y_ml.reshape(n, ML_W), proj,
                        mod, l, w, tm=tm, rpg=t, r=1, tm_ffn=min(tm, 512), rpg_ffn=t, r_ffn=1)
        states.append((h_t.reshape(b, LRU_W), cv_t, s_t, c_t, n_t, m_t[:, :, 0], mcv_t))
    stacked = [jnp.stack([s[i] for s in states]) for i in range(7)]
    return x.reshape(b, t, D), stacked


def _run_sample(x3, mod, states, w):
    b, t, _ = x3.shape
    n = b * t
    tm = n
    s_h, s_cv, s_gla, s_c, s_n, s_m, s_mcv = states
    pad_t = lambda a: jnp.pad(a, ((0, 0), (0, 0), (0, t - a.shape[2]), (0, 0))).reshape(DEPTH, n, a.shape[-1])
    el = pad_t(s_cv)
    em = pad_t(s_mcv)
    h0e = pad_t(s_h[:, :, None, :])
    n0e = jnp.broadcast_to(s_n.reshape(DEPTH, b, 1, ML_W), (DEPTH, b, t, ML_W)).reshape(DEPTH, n, ML_W)
    m_l = jnp.pad(s_m, ((0, 0), (0, 0), (S_FG, LANE - S_FG - ML_H)))
    m0e = jnp.broadcast_to(m_l[:, :, None, :], (DEPTH, b, t, LANE)).reshape(DEPTH, n, LANE)
    x = x3.reshape(n, D)
    prev = None
    conv_l, conv_m = [], []
    for l in range(DEPTH):
        proj, small = _in_proj(x, mod, l, w["g1"], w["w_main"], w["w_small"], tm=tm, tn=512, rpg=n, r=tm)
        outs = _sample_mixer(proj, small, el, em, h0e, n0e, m0e, s_gla, s_c, l, w, prev)
        y_lru, y_gla, y_ml = outs[:3]
        prev = outs[3:]
        p4 = proj.reshape(b, t, N_MAIN)
        conv_l.append(p4[:, t - (CONV_W - 1):, C_LRU:C_LRU + LRU_W])
        conv_m.append(p4[:, t - (CONV_W - 1):, C_MX:C_MX + ML_W])
        x = _dense_tail(x, y_lru, y_gla, y_ml, proj, mod, l, w, tm=tm, rpg=n, r=tm, tm_ffn=tm, rpg_ffn=n, r_ffn=tm)
    hrow, nrow, mrow, s_out, c_out = prev
    last = t - 1
    new_states = [hrow.reshape(DEPTH, b, t, LRU_W)[:, :, last], jnp.stack(conv_l), s_out, c_out,
                  nrow.reshape(DEPTH, b, t, ML_H, ML_DH)[:, :, last],
                  mrow.reshape(DEPTH, b, t, LANE)[:, :, last, S_FG:S_FG + ML_H], jnp.stack(conv_m)]
    return x.reshape(b, t, D), new_states


def kernel(x_prompt, x_sample, c_prompt, c_sample, state_lru_h, state_lru_conv, state_gla, state_mlstm_C, state_mlstm_n, state_mlstm_m, state_mlstm_conv, w_ada, b_ada, g_norm1, g_norm2, w_in, lru_conv_w, lru_conv_b, lru_w_a, lru_b_a, lru_w_x, lru_b_x, lru_lam, gla_w_g2, gla_b_g, gla_g_norm, ml_conv_w, ml_conv_b, ml_w_q, ml_w_k, ml_w_v, ml_b_if, ml_g_norm, w_br_lru, w_br_gla, w_br_ml, w_out, w_ff1, w_ff2, g_final):
    p = dict(g_norm1=g_norm1, g_norm2=g_norm2, w_in=w_in, lru_conv_w=lru_conv_w, lru_conv_b=lru_conv_b,
             lru_w_a=lru_w_a, lru_b_a=lru_b_a, lru_w_x=lru_w_x, lru_b_x=lru_b_x, lru_lam=lru_lam,
             gla_w_g2=gla_w_g2, gla_b_g=gla_b_g, gla_g_norm=gla_g_norm, ml_conv_w=ml_conv_w, ml_conv_b=ml_conv_b,
             ml_w_q=ml_w_q, ml_w_k=ml_w_k, ml_w_v=ml_w_v, ml_b_if=ml_b_if, ml_g_norm=ml_g_norm,
             w_br_lru=w_br_lru, w_br_gla=w_br_gla, w_br_ml=w_br_ml, w_out=w_out, w_ff1=w_ff1, w_ff2=w_ff2,
             g_final=g_final)
    w = _prep_weights(p)
    bp = x_prompt.shape[0]
    bs, ts, _ = x_sample.shape
    assert ts == T_STEP and bs % STEP_BB == 0
    pad = (-bp) % SUBLANE
    c_all = jnp.concatenate([c_prompt, jnp.zeros((pad, D), F32), c_sample], axis=0)
    mod_all = _ada(c_all, w_ada, b_ada)
    mod_p = mod_all[:, :bp].reshape(DEPTH, bp, 1, N_MOD * D)
    mod_s = jnp.repeat(mod_all[:, bp + pad:], ts, axis=1).reshape(DEPTH, 1, bs * ts, N_MOD * D)
    y_p, ps = _run_prompt(x_prompt, mod_p, w)
    y_s, ss = _run_sample(x_sample, mod_s,
                          (state_lru_h, state_lru_conv, state_gla, state_mlstm_C, state_mlstm_n, state_mlstm_m,
                           state_mlstm_conv), w)
    return (y_p, y_s, *ps, *ss)
```

```python
import functools

import jax
import jax.numpy as jnp
from jax import lax
from jax.experimental import pallas as pl
from jax.experimental.pallas import tpu as pltpu

F32, BF16 = jnp.float32, jnp.bfloat16
SDS = jax.ShapeDtypeStruct

D = 2048
DEPTH = 2
LRU_W = 1024
LRU_BLOCKS = 8
LRU_C = 8.0
CONV_W = 4
GLA_H = 4
GLA_DK = 128
GLA_DV = 256
GLA_RANK = 16
GLA_TAU = 16.0
ML_H = 4
ML_W = 1024
ML_DH = 256
ML_BS = 4
CHUNK = 64
PROMPT_ROWS = 256
D_FF = 4 * D
EPS = 1e-6
N_MOD = 6
T_STEP = 4
STEP_BB = 4

LANE = 128
SUBLANE = 8
VMEM_LIMIT_CAP = 56 * 1024 * 1024

C_LRU, C_GQ, C_GK, C_GV, C_GGATE, C_MX, C_MO, C_MG = 0, 1024, 1536, 2048, 3072, 4096, 5120, 6144
N_MAIN = 12288
S_GLR = 0
S_IG = 16
S_FG = 20
NEG_BIG = -1e30


def _params(sem, vmem_bytes):
    return pltpu.CompilerParams(dimension_semantics=sem, vmem_limit_bytes=int(min(vmem_bytes, VMEM_LIMIT_CAP)))


def _dot(a, b):
    return jnp.dot(a.astype(BF16), b.astype(BF16), preferred_element_type=F32)


def _dot_nt(a, b):
    return lax.dot_general(a.astype(BF16), b.astype(BF16), (((1,), (1,)), ((), ())), preferred_element_type=F32)


def _split3(x):
    p0 = x.astype(BF16)
    r = x - p0.astype(F32)
    p1 = r.astype(BF16)
    return p0, p1, (r - p1.astype(F32)).astype(BF16)


def _cumsum_rows(tri_b, x):
    return sum(jnp.dot(tri_b, p, preferred_element_type=F32) for p in _split3(x))


def _cumsum_lanes(x, triu_b):
    return sum(jnp.dot(p, triu_b, preferred_element_type=F32) for p in _split3(x))


def _split(x):
    hi = x.astype(BF16)
    return hi, (x - hi.astype(F32)).astype(BF16)


def _dot3(x_hi, x_lo, w_hi, w_lo):
    d = lambda a, b: jnp.dot(a, b, preferred_element_type=F32)
    return d(x_hi, w_hi) + (d(x_lo, w_hi) + d(x_hi, w_lo))


def _log_sigmoid(z):
    return jnp.minimum(z, 0.0) - jnp.log1p(jnp.exp(-jnp.abs(z)))


def _silu(z):
    return z * jax.nn.sigmoid(z)


def _rms(x, g):
    return x * lax.rsqrt(jnp.mean(x * x, axis=-1, keepdims=True) + EPS) * g


def _tri(n):
    r = lax.broadcasted_iota(jnp.int32, (n, n), 0)
    c = lax.broadcasted_iota(jnp.int32, (n, n), 1)
    return r >= c


def _lru_gates(ub, wa, wx, ba, bx, lam):
    r = jax.nn.sigmoid(_dot(ub, wa) + ba)
    i = jax.nn.sigmoid(_dot(ub, wx) + bx)
    log_a = LRU_C * r * _log_sigmoid(lam)
    t = jnp.tanh(log_a)
    return jnp.exp(log_a), jnp.sqrt(-2.0 * t / (1.0 - t)) * (i * ub)


def _blockdiag3(x, wh_ref, wl_ref):
    xh, xl = _split(x)
    outs = []
    for blk in range(ML_W // LANE):
        sl = slice(blk * LANE, (blk + 1) * LANE)
        outs.append(_dot3(xh[:, sl], xl[:, sl], wh_ref[blk], wl_ref[blk]))
    return jnp.concatenate(outs, axis=1)


def _ada_kernel(c_ref, w_ref, b_ref, o_ref):
    o_ref[...] = _dot(_silu(c_ref[...]), w_ref[...]) + b_ref[...]


def _ada(c_all, w_ada, b_ada):
    m = c_all.shape[0]
    tn = 512
    n_out = N_MOD * D
    return pl.pallas_call(
        _ada_kernel,
        grid=(DEPTH, n_out // tn),
        in_specs=[
            pl.BlockSpec((m, D), lambda l, n: (0, 0)),
            pl.BlockSpec((None, D, tn), lambda l, n: (l, 0, n)),
            pl.BlockSpec((None, 1, tn), lambda l, n: (l, 0, n)),
        ],
        out_specs=pl.BlockSpec((None, m, tn), lambda l, n: (l, 0, n)),
        out_shape=SDS((DEPTH, m, n_out), F32),
        compiler_params=_params(("parallel", "arbitrary"), 2 * (m * D + D * tn + m * tn) * 4 + (12 << 20)),
        name="ada",
    )(c_all, w_ada, b_ada.reshape(DEPTH, 1, n_out))


def _modspec(layer, comp, r, width, tm, rpg, jmap=None):
    nb = D // width
    jm = (lambda j: 0) if jmap is None else jmap
    if r == 1:
        return pl.BlockSpec((None, None, 1, width), lambda i, j: (layer, (i * tm) // rpg, 0, comp * nb + jm(j)))
    return pl.BlockSpec((None, None, tm, width), lambda i, j: (layer, 0, i, comp * nb + jm(j)))


def _norm_mod_store(xn_s, x_ref, g_ref, sc_ref, sh_ref):
    tm = x_ref.shape[0]
    rc = min(tm, 256)
    for c in range(tm // rc):
        rows = slice(c * rc, (c + 1) * rc)
        mrows = rows if sc_ref.shape[0] == tm else slice(None)
        xn = _rms(x_ref[rows, :], g_ref[...]) * (1.0 + sc_ref[mrows, :]) + sh_ref[mrows, :]
        xn_s[rows, :] = xn.astype(BF16)


def _in_proj_kernel(x_ref, sc_ref, sh_ref, g_ref, wm_ref, ws_ref, om_ref, os_ref, xn_s):
    @pl.when(pl.program_id(1) == 0)
    def _():
        _norm_mod_store(xn_s, x_ref, g_ref, sc_ref, sh_ref)
        os_ref[...] = jnp.dot(xn_s[...], ws_ref[...], preferred_element_type=F32)

    om_ref[...] = jnp.dot(xn_s[...], wm_ref[...], preferred_element_type=F32)


def _in_proj(x, mod, layer, g1, w_main, w_small, *, tm, tn, rpg, r):
    n = x.shape[0]
    vmem = 2 * (tm * D * 4 + 2 * r * D * 4 + D * tn * 2 + D * LANE * 2 + tm * tn * 4 + tm * LANE * 4) + tm * D * 2
    vmem += 8 << 20
    return pl.pallas_call(
        _in_proj_kernel,
        grid=(n // tm, N_MAIN // tn),
        in_specs=[
            pl.BlockSpec((tm, D), lambda i, j: (i, 0)),
            _modspec(layer, 1, r, D, tm, rpg),
            _modspec(layer, 0, r, D, tm, rpg),
            pl.BlockSpec((None, 1, D), lambda i, j: (layer, 0, 0)),
            pl.BlockSpec((None, D, tn), lambda i, j: (layer, 0, j)),
            pl.BlockSpec((None, D, LANE), lambda i, j: (layer, 0, 0)),
        ],
        out_specs=[pl.BlockSpec((tm, tn), lambda i, j: (i, j)), pl.BlockSpec((tm, LANE), lambda i, j: (i, 0))],
        out_shape=[SDS((n, N_MAIN), F32), SDS((n, LANE), F32)],
        scratch_shapes=[pltpu.VMEM((tm, D), BF16)],
        compiler_params=_params(("parallel", "arbitrary"), vmem),
        name="in_proj",
    )(x, mod, mod, g1, w_main, w_small)


def _merge_kernel(yl_ref, yg_ref, ym_ref, g0_ref, g1_ref, g2_ref, w0_ref, w1_ref, w2_ref, o_ref):
    acc = jax.nn.sigmoid(g0_ref[...]) * jnp.dot(yl_ref[...], w0_ref[...], preferred_element_type=F32)
    acc += jax.nn.sigmoid(g1_ref[...]) * jnp.dot(yg_ref[...], w1_ref[...], preferred_element_type=F32)
    acc += jax.nn.sigmoid(g2_ref[...]) * jnp.dot(ym_ref[...], w2_ref[...], preferred_element_type=F32)
    o_ref[...] = acc.astype(BF16)


def _merge(y_lru, y_gla, y_ml, proj, layer, w_lru, w_gla, w_ml, *, tm, tn):
    n = y_lru.shape[0]
    w = LRU_W
    gb = C_MG // tn
    nb = D // tn
    yspec = pl.BlockSpec((tm, w), lambda i, j: (i, 0))
    wspec = pl.BlockSpec((None, w, tn), lambda i, j: (layer, 0, j))
    vmem = 2 * (3 * tm * w * 2 + 3 * tm * tn * 4 + 3 * w * tn * 2 + tm * tn * 2) + 6 * tm * tn * 4 + (4 << 20)
    return pl.pallas_call(
        _merge_kernel,
        grid=(n // tm, nb),
        in_specs=[
            yspec, yspec, yspec,
            pl.BlockSpec((tm, tn), lambda i, j: (i, gb + j)),
            pl.BlockSpec((tm, tn), lambda i, j: (i, gb + nb + j)),
            pl.BlockSpec((tm, tn), lambda i, j: (i, gb + 2 * nb + j)),
            wspec, wspec, wspec,
        ],
        out_specs=pl.BlockSpec((tm, tn), lambda i, j: (i, j)),
        out_shape=SDS((n, D), BF16),
        compiler_params=_params(("parallel", "arbitrary"), vmem),
        name="merge",
    )(y_lru, y_gla, y_ml, proj, proj, proj, w_lru, w_gla, w_ml)


def _out_proj_kernel(m_ref, w_ref, x_ref, gt_ref, o_ref):
    o_ref[...] = x_ref[...] + gt_ref[...] * jnp.dot(m_ref[...], w_ref[...], preferred_element_type=F32)


def _out_proj(merged, x, mod, layer, w_out, *, tm, tn, rpg, r):
    n = x.shape[0]
    vmem = 2 * (tm * D * 2 + D * tn * 2 + 2 * tm * tn * 4 + r * tn * 4) + 2 * tm * tn * 4 + (4 << 20)
    return pl.pallas_call(
        _out_proj_kernel,
        grid=(n // tm, D // tn),
        in_specs=[
            pl.BlockSpec((tm, D), lambda i, j: (i, 0)),
            pl.BlockSpec((None, D, tn), lambda i, j: (layer, 0, j)),
            pl.BlockSpec((tm, tn), lambda i, j: (i, j)),
            _modspec(layer, 2, r, tn, tm, rpg, jmap=lambda j: j),
        ],
        out_specs=pl.BlockSpec((tm, tn), lambda i, j: (i, j)),
        out_shape=SDS((n, D), F32),
        compiler_params=_params(("parallel", "arbitrary"), vmem),
        name="out_proj",
    )(merged, w_out, x, mod)


def _ffn_kernel(x_ref, sc_ref, sh_ref, gt_ref, g_ref, gf_ref, w1_ref, w2_ref, o_ref, xn_s, *, final_norm):
    f = pl.program_id(1)

    @pl.when(f == 0)
    def _():
        _norm_mod_store(xn_s, x_ref, g_ref, sc_ref, sh_ref)
        o_ref[...] = jnp.zeros_like(o_ref)

    h = jnp.square(jnp.maximum(jnp.dot(xn_s[...], w1_ref[...], preferred_element_type=F32), 0.0))
    o_ref[...] += jnp.dot(h.astype(BF16), w2_ref[...], preferred_element_type=F32)

    @pl.when(f == pl.num_programs(1) - 1)
    def _():
        y = x_ref[...] + gt_ref[...] * o_ref[...]
        if final_norm:
            y = _rms(y, gf_ref[...])
        o_ref[...] = y


def _ffn(x, mod, layer, g2, g_final, w1, w2, *, tm, tf, rpg, r, final_norm):
    n = x.shape[0]
    vmem = 2 * (2 * tm * D * 4 + 3 * r * D * 4 + 2 * D * tf * 2) + tm * D * 2 + 2 * tm * tf * 4 + 2 * tm * D * 4
    return pl.pallas_call(
        functools.partial(_ffn_kernel, final_norm=final_norm),
        grid=(n // tm, D_FF // tf),
        in_specs=[
            pl.BlockSpec((tm, D), lambda i, f: (i, 0)),
            _modspec(layer, 4, r, D, tm, rpg),
            _modspec(layer, 3, r, D, tm, rpg),
            _modspec(layer, 5, r, D, tm, rpg),
            pl.BlockSpec((None, 1, D), lambda i, f: (layer, 0, 0)),
            pl.BlockSpec((1, D), lambda i, f: (0, 0)),
            pl.BlockSpec((None, D, tf), lambda i, f: (layer, 0, f)),
            pl.BlockSpec((None, tf, D), lambda i, f: (layer, f, 0)),
        ],
        out_specs=pl.BlockSpec((tm, D), lambda i, f: (i, 0)),
        out_shape=SDS((n, D), F32),
        scratch_shapes=[pltpu.VMEM((tm, D), BF16)],
        compiler_params=_params(("parallel", "arbitrary"), vmem),
        name="ffn",
    )(x, mod, mod, mod, g2, g_final, w1, w2)


def _conv_chunk(xp_s, x, w, b, L):
    xp_s[SUBLANE:SUBLANE + L, :] = x
    acc = b + xp_s[5:5 + L, :] * w[0:1]
    for j in range(1, CONV_W):
        acc = acc + xp_s[5 + j:5 + j + L, :] * w[j:j + 1]
    tail = xp_s[5 + L:8 + L, :]
    xp_s[5:8, :] = tail
    return acc, tail


def _prompt_mixer_kernel(lx_ref, q_ref, k_ref, v_ref, gate_ref, mx_ref, mo_ref, small_ref,
                         lcw_ref, lcb_ref, wa_ref, ba_ref, wx_ref, bx_ref, lam_ref,
                         wg2_ref, bg_ref, ggn_ref,
                         mcw_ref, mcb_ref, wqh_ref, wql_ref, wkh_ref, wkl_ref, wvh_ref, wvl_ref, bif_ref, mgn_ref,
                         *rest, L):
    (yl_ref, yg_ref, ym_ref, ht_ref, lct_ref, st_ref, ct_ref, nt_ref, mt_ref, mct_ref,
     xpl_s, xpm_s, a_s, u_s, h_s, s_s, c_s, n_s, m_s) = rest[-19:]

    @pl.when(pl.program_id(1) == 0)
    def _():
        xpl_s[0:SUBLANE, :] = jnp.zeros((SUBLANE, LRU_W), F32)
        xpm_s[0:SUBLANE, :] = jnp.zeros((SUBLANE, ML_W), F32)
        h_s[...] = jnp.zeros_like(h_s)
        s_s[...] = jnp.zeros_like(s_s)
        c_s[...] = jnp.zeros_like(c_s)
        n_s[...] = jnp.zeros_like(n_s)
        m_s[...] = jnp.zeros_like(m_s)

    rows = lx_ref.shape[0]

    u, l_tail = _conv_chunk(xpl_s, lx_ref[...], lcw_ref[...], lcb_ref[...], rows)
    for blk in range(LRU_BLOCKS):
        sl = slice(blk * LANE, (blk + 1) * LANE)
        a_s[:, sl], u_s[:, sl] = _lru_gates(u[:, sl], wa_ref[blk], wx_ref[blk], ba_ref[:, sl], bx_ref[:, sl],
                                            lam_ref[:, sl])
    t8 = lax.broadcasted_iota(jnp.int32, (SUBLANE, LRU_W), 0)
    h = h_s[...]
    for g in range(rows // SUBLANE):
        r8 = slice(g * SUBLANE, (g + 1) * SUBLANE)
        a8, u8 = a_s[r8, :], u_s[r8, :]
        for s in (1, 2, 4):
            u8 = jnp.where(t8 >= s, u8 + a8 * pltpu.roll(u8, s, 0), u8)
            a8 = jnp.where(t8 >= s, a8 * pltpu.roll(a8, s, 0), a8)
        h8 = u8 + a8 * h
        a_s[r8, :] = h8
        h = h8[SUBLANE - 1:SUBLANE, :]
    h_s[...] = h
    yl_ref[...] = a_s[...].astype(BF16)

    q, k, v, gate = q_ref[...], k_ref[...], v_ref[...], gate_ref[...]
    small = small_ref[...]
    logdec = _log_sigmoid(_dot(small, wg2_ref[...]) + bg_ref[...]) * (1.0 / GLA_TAU)
    mx = mx_ref[...]
    conv, m_tail = _conv_chunk(xpm_s, mx, mcw_ref[...], mcb_ref[...], rows)
    mc = _silu(conv)
    mq = _blockdiag3(mc, wqh_ref, wql_ref)
    mk = _blockdiag3(mc, wkh_ref, wkl_ref) * (ML_DH ** -0.5)
    mv = _blockdiag3(mx, wvh_ref, wvl_ref)
    gates = small + bif_ref[...]
    gates_t = gates.T
    lf = _log_sigmoid(gates)
    lf_t = _log_sigmoid(gates_t)
    mo = mo_ref[...]
    tri = _tri(L)
    tri_b = tri.astype(BF16)
    r_i = lax.broadcasted_iota(jnp.int32, (L, L), 0)
    c_i = lax.broadcasted_iota(jnp.int32, (L, L), 1)
    triu_b = (r_i <= c_i).astype(BF16)

    for cc in range(rows // L):
        rs = slice(cc * L, (cc + 1) * L)
        bcum = _cumsum_rows(tri_b, logdec[rs, :])
        for hh in range(GLA_H):
            ks = slice(hh * GLA_DK, (hh + 1) * GLA_DK)
            vs = slice(hh * GLA_DV, (hh + 1) * GLA_DV)
            bh = bcum[:, ks]
            k_h, v_h = k[rs, ks], v[rs, vs]
            qe = (q[rs, ks] * (GLA_DK ** -0.5)) * jnp.exp(bh)
            ke = k_h * jnp.exp(-bh)
            a = jnp.where(tri, _dot_nt(qe, ke), 0.0)
            s_h = s_s[hh]
            o = _dot(qe, s_h) + _dot(a, v_h)
            bl = bh[L - 1:L, :]
            kd = k_h * jnp.exp(bl - bh)
            dec_col = jnp.exp(jnp.broadcast_to(bl, (SUBLANE, GLA_DK))).T[:, 0:1]
            s_s[hh] = dec_col * s_h + _dot(kd.T, v_h)
            yg_ref[rs, vs] = (_rms(o, ggn_ref[...]) * _silu(gate[rs, vs])).astype(BF16)

        f_col = _cumsum_rows(tri_b, lf[rs, :])
        f_row = _cumsum_lanes(lf_t[:, rs], triu_b)
        for hh in range(ML_H):
            sl = slice(hh * ML_DH, (hh + 1) * ML_DH)
            qh, kh, vh = mq[rs, sl], mk[rs, sl], mv[rs, sl]
            fc = f_col[:, S_FG + hh:S_FG + hh + 1]
            fr = f_row[S_FG + hh:S_FG + hh + 1, :]
            igr = gates_t[S_IG + hh:S_IG + hh + 1, rs]
            m_h = m_s[hh:hh + 1, 0:1]
            dm = jnp.where(tri, fc - fr + igr, NEG_BIG)
            inter = fc + m_h
            mt = jnp.maximum(inter, jnp.max(dm, axis=-1, keepdims=True))
            ci = jnp.exp(inter - mt)
            s = _dot_nt(qh, kh) * jnp.exp(dm - mt)
            c_h = c_s[hh]
            n_h = n_s[hh:hh + 1, :]
            num = ci * _dot(qh, c_h) + _dot(s, vh)
            den = ci * jnp.sum(qh * n_h, axis=-1, keepdims=True) + jnp.sum(s, axis=-1, keepdims=True)
            hcell = num / jnp.maximum(jnp.abs(den), jnp.exp(-mt))
            fl = fr[:, L - 1:L]
            dj = fl - fr + igr
            m_new = jnp.maximum(fl + m_h, jnp.max(dj, axis=-1, keepdims=True))
            cs = jnp.exp(fl + m_h - m_new)
            wj = jnp.exp(dj - m_new)
            c_s[hh] = cs * c_h + _dot(kh.T * wj, vh)
            n_s[hh:hh + 1, :] = cs * n_h + _dot(jnp.broadcast_to(wj, (SUBLANE, L)), kh)[0:1, :]
            m_s[hh:hh + 1, :] = jnp.broadcast_to(m_new, (1, LANE))
            ym_ref[rs, sl] = (jax.nn.sigmoid(mo[rs, sl]) * _rms(hcell, mgn_ref[...])).astype(BF16)

    @pl.when(pl.program_id(1) == pl.num_programs(1) - 1)
    def _():
        ht_ref[...] = h
        lct_ref[...] = l_tail
        st_ref[...] = s_s[...]
        ct_ref[...] = c_s[...]
        nt_ref[...] = n_s[...]
        mt_ref[...] = m_s[...]
        mct_ref[...] = m_tail


def _prompt_mixer(proj3, small3, layer, w, prev, *, L, rows):
    b, t, _ = proj3.shape
    kw, vw = GLA_H * GLA_DK, GLA_H * GLA_DV
    nblk = ML_W // LANE
    blk3 = lambda width, cb: pl.BlockSpec((None, rows, width), lambda bi, c: (bi, c, cb))
    lw3 = lambda s: pl.BlockSpec((None,) + s, lambda bi, c: (layer, 0, 0))
    lw4 = lambda s: pl.BlockSpec((None,) + s, lambda bi, c: (layer, 0, 0, 0))
    st3 = lambda s: pl.BlockSpec((None, None) + s, lambda bi, c: (layer, bi, 0, 0))
    st4 = lambda s: pl.BlockSpec((None, None) + s, lambda bi, c: (layer, bi, 0, 0, 0))
    in_specs = [
        blk3(LRU_W, C_LRU // LRU_W), blk3(kw, C_GQ // kw), blk3(kw, C_GK // kw), blk3(vw, C_GV // vw),
        blk3(vw, C_GGATE // vw), blk3(ML_W, C_MX // ML_W), blk3(ML_W, C_MO // ML_W), blk3(LANE, 0),
        lw3((CONV_W, LRU_W)), lw3((1, LRU_W)), lw4((LRU_BLOCKS, LANE, LANE)), lw3((1, LRU_W)),
        lw4((LRU_BLOCKS, LANE, LANE)), lw3((1, LRU_W)), lw3((1, LRU_W)),
        lw3((LANE, kw)), lw3((1, kw)), lw3((1, GLA_DV)),
        lw3((CONV_W, ML_W)), lw3((1, ML_W)), *([lw4((nblk, LANE, LANE))] * 6), lw3((1, LANE)), lw3((1, ML_DH)),
    ]
    args = [proj3] * 7 + [small3,
                          w["lru_cw"], w["lru_cb"], w["lru_wa"], w["lru_ba"], w["lru_wx"], w["lru_bx"], w["lru_lam"],
                          w["wg2"], w["bg"], w["gla_gn"],
                          w["ml_cw"], w["ml_cb"], *w["ml_wq"], *w["ml_wk"], *w["ml_wv"], w["ml_bif"], w["ml_gn"]]
    aliases = {}
    if prev is not None:
        aliases = {len(in_specs) + j: 3 + j for j in range(len(prev))}
        in_specs += [pl.BlockSpec(memory_space=pl.ANY)] * len(prev)
        args += list(prev)
    sbytes = GLA_H * GLA_DK * GLA_DV * 4
    cbytes = ML_H * ML_DH * ML_DH * 4
    vmem = 2 * rows * (7 * 1024 * 4 + 3 * 1024 * 2) + 3 * (sbytes + cbytes) + 40 * rows * 1024 * 4 + (12 << 20)
    return pl.pallas_call(
        functools.partial(_prompt_mixer_kernel, L=L),
        grid=(b, t // rows),
        in_specs=in_specs,
        out_specs=[blk3(LRU_W, 0), blk3(vw, 0), blk3(ML_W, 0),
                   st3((1, LRU_W)), st3((CONV_W - 1, LRU_W)), st4((GLA_H, GLA_DK, GLA_DV)),
                   st4((ML_H, ML_DH, ML_DH)), st3((ML_H, ML_DH)), st3((ML_H, LANE)), st3((CONV_W - 1, ML_W))],
        out_shape=[SDS((b, t, LRU_W), BF16), SDS((b, t, vw), BF16), SDS((b, t, ML_W), BF16),
                   SDS((DEPTH, b, 1, LRU_W), F32), SDS((DEPTH, b, CONV_W - 1, LRU_W), F32),
                   SDS((DEPTH, b, GLA_H, GLA_DK, GLA_DV), F32), SDS((DEPTH, b, ML_H, ML_DH, ML_DH), F32),
                   SDS((DEPTH, b, ML_H, ML_DH), F32), SDS((DEPTH, b, ML_H, LANE), F32),
                   SDS((DEPTH, b, CONV_W - 1, ML_W), F32)],
        scratch_shapes=[pltpu.VMEM((SUBLANE + rows, LRU_W), F32), pltpu.VMEM((SUBLANE + rows, ML_W), F32),
                        pltpu.VMEM((rows, LRU_W), F32), pltpu.VMEM((rows, LRU_W), F32), pltpu.VMEM((1, LRU_W), F32),
                        pltpu.VMEM((GLA_H, GLA_DK, GLA_DV), F32), pltpu.VMEM((ML_H, ML_DH, ML_DH), F32),
                        pltpu.VMEM((ML_H, ML_DH), F32), pltpu.VMEM((ML_H, LANE), F32)],
        input_output_aliases=aliases,
        compiler_params=_params(("parallel", "arbitrary"), vmem),
        name="prompt_mixer",
    )(*args)


def _tmask(shape):
    return lax.broadcasted_iota(jnp.int32, shape, 0) % T_STEP


def _down(x, s):
    return x if s == 0 else pltpu.roll(x, s, 0)


def _up(x, s):
    return x if s == 0 else pltpu.roll(x, x.shape[0] - s, 0)


def _seg_cumsum(x):
    t = _tmask(x.shape)
    out = x
    for s in range(1, T_STEP):
        out = out + jnp.where(t >= s, _down(x, s), 0.0)
    return out


def _seg_last(x):
    t = _tmask(x.shape)
    out = x
    for s in range(1, T_STEP):
        out = jnp.where(t == T_STEP - 1 - s, _up(x, s), out)
    return out


def _seg_allreduce(x, op):
    t = _tmask(x.shape)
    y = op(x, jnp.where(t % 2 == 1, _down(x, 1), _up(x, 1)))
    return op(y, jnp.where(t >= 2, _down(y, 2), _up(y, 2)))


def _conv_rows(x, e, w, b):
    t = _tmask(x.shape)
    acc = b + x * w[CONV_W - 1:CONV_W]
    for k in range(1, CONV_W):
        hist = jnp.where(t >= k, _down(x, k), _up(e, CONV_W - 1 - k))
        acc = acc + hist * w[CONV_W - 1 - k:CONV_W - k]
    return acc


def _col(x, lane):
    li = lax.broadcasted_iota(jnp.int32, x.shape, 1)
    return jnp.sum(jnp.where(li == lane, x, 0.0), axis=-1, keepdims=True)


def _sample_mixer_kernel(lx_ref, q_ref, k_ref, v_ref, gate_ref, mx_ref, mo_ref, small_ref,
                         el_ref, em_ref, h0_ref, n0_ref, m0_ref, s0_ref, c0_ref,
                         lcw_ref, lcb_ref, wa_ref, ba_ref, wx_ref, bx_ref, lam_ref,
                         wg2_ref, bg_ref, ggn_ref,
                         mcw_ref, mcb_ref, wqh_ref, wql_ref, wkh_ref, wkl_ref, wvh_ref, wvl_ref, bif_ref, mgn_ref,
                         *rest):
    yl_ref, yg_ref, ym_ref, hrow_ref, nrow_ref, mrow_ref, st_ref, ct_ref = rest[-8:]
    rows = lx_ref.shape[0]
    n_pairs = rows // SUBLANE
    seq_per_tile = SUBLANE // T_STEP
    t_col = _tmask((rows, 1))
    row8 = lax.broadcasted_iota(jnp.int32, (SUBLANE, 1), 0)

    u = _conv_rows(lx_ref[...], el_ref[...], lcw_ref[...], lcb_ref[...])
    a_blocks, g_blocks = [], []
    for blk in range(LRU_BLOCKS):
        sl = slice(blk * LANE, (blk + 1) * LANE)
        a_b, g_b = _lru_gates(u[:, sl], wa_ref[blk], wx_ref[blk], ba_ref[:, sl], bx_ref[:, sl], lam_ref[:, sl])
        a_blocks.append(a_b)
        g_blocks.append(g_b)
    a = jnp.concatenate(a_blocks, axis=1)
    t_w = _tmask(a.shape)
    g = jnp.concatenate(g_blocks, axis=1) + jnp.where(t_w == 0, a * h0_ref[...], 0.0)
    a1 = jnp.where(t_w >= 1, a * _down(a, 1), a)
    g1 = jnp.where(t_w >= 1, g + a * _down(g, 1), g)
    h = jnp.where(t_w >= 2, g1 + a1 * _down(g1, 2), g1)
    yl_ref[...] = h.astype(BF16)
    hrow_ref[...] = h

    q, k, v = q_ref[...], k_ref[...], v_ref[...]
    logdec = _log_sigmoid(_dot(small_ref[...], wg2_ref[...]) + bg_ref[...]) * (1.0 / GLA_TAU)
    bc = _seg_cumsum(logdec)
    bl = _seg_last(bc)
    qs = q * (GLA_DK ** -0.5)
    o_heads = [jnp.zeros((rows, GLA_DV), F32) for _ in range(GLA_H)]
    for s in range(T_STEP):
        prod = qs * _down(k, s) * jnp.exp(bc - _down(bc, s))
        v_s = _down(v, s)
        for hh in range(GLA_H):
            a_sh = jnp.sum(prod[:, hh * GLA_DK:(hh + 1) * GLA_DK], axis=-1, keepdims=True)
            a_sh = jnp.where(t_col >= s, a_sh, 0.0)
            o_heads[hh] = o_heads[hh] + a_sh * v_s[:, hh * GLA_DV:(hh + 1) * GLA_DV]
    qe = qs * jnp.exp(bc)
    kd = k * jnp.exp(bl - bc)
    dec = jnp.exp(bl)
    gate = gate_ref[...]
    for hh in range(GLA_H):
        ks = slice(hh * GLA_DK, (hh + 1) * GLA_DK)
        vs = slice(hh * GLA_DV, (hh + 1) * GLA_DV)
        o_state = []
        for p in range(n_pairs):
            r8 = slice(p * SUBLANE, (p + 1) * SUBLANE)
            kd_t = kd[r8, ks].T
            dec_t = dec[r8, ks].T
            res = None
            for j in range(seq_per_tile):
                b = p * seq_per_tile + j
                s_b = s0_ref[b, hh]
                r_j = _dot(qe[r8, ks], s_b)
                res = r_j if res is None else jnp.where(row8 // T_STEP == j, r_j, res)
                v_j = jnp.where(row8 // T_STEP == j, v[r8, vs], 0.0)
                last = j * T_STEP + T_STEP - 1
                st_ref[b, hh] = dec_t[:, last:last + 1] * s_b + _dot(kd_t, v_j)
            o_state.append(res)
        o = o_heads[hh] + jnp.concatenate(o_state, axis=0)
        yg_ref[:, vs] = (_rms(o, ggn_ref[...]) * _silu(gate[:, vs])).astype(BF16)

    mx = mx_ref[...]
    mc = _silu(_conv_rows(mx, em_ref[...], mcw_ref[...], mcb_ref[...]))
    mq = _blockdiag3(mc, wqh_ref, wql_ref)
    mk = _blockdiag3(mc, wkh_ref, wkl_ref) * (ML_DH ** -0.5)
    mv = _blockdiag3(mx, wvh_ref, wvl_ref)
    gts = small_ref[...] + bif_ref[...]
    ig = pltpu.roll(gts, S_FG - S_IG, 1)
    fcum = _seg_cumsum(_log_sigmoid(gts))
    flast = _seg_last(fcum)
    m0 = m0_ref[...]
    t_g = _tmask(gts.shape)
    inter = fcum + m0
    dms = [jnp.where(t_g >= s, fcum - _down(fcum, s) + _down(ig, s), NEG_BIG) for s in range(T_STEP)]
    mt = inter
    for dm in dms:
        mt = jnp.maximum(mt, dm)
    ci_t = jnp.exp(inter - mt)
    emt_t = jnp.exp(-mt)
    w_t = [jnp.exp(dm - mt) for dm in dms]
    dj = flast - fcum + ig
    m_new = jnp.maximum(flast + m0, _seg_allreduce(dj, jnp.maximum))
    cs_t = jnp.exp(flast + m0 - m_new)
    wj_t = jnp.exp(dj - m_new)
    mrow_ref[...] = m_new
    mo = mo_ref[...]
    n0 = n0_ref[...]
    for hh in range(ML_H):
        sl = slice(hh * ML_DH, (hh + 1) * ML_DH)
        lane = S_FG + hh
        qh, kh, vh = mq[:, sl], mk[:, sl], mv[:, sl]
        ci, emt, cs, wj = _col(ci_t, lane), _col(emt_t, lane), _col(cs_t, lane), _col(wj_t, lane)
        num = jnp.zeros((rows, ML_DH), F32)
        den = ci * jnp.sum(qh * n0[:, sl], axis=-1, keepdims=True)
        for s in range(T_STEP):
            sc = jnp.sum(qh * _down(kh, s), axis=-1, keepdims=True) * _col(w_t[s], lane)
            num = num + sc * _down(vh, s)
            den = den + sc
        kw = kh * wj
        nrow_ref[:, sl] = cs * n0[:, sl] + _seg_allreduce(kw, jnp.add)
        qc = []
        for p in range(n_pairs):
            r8 = slice(p * SUBLANE, (p + 1) * SUBLANE)
            kw_t = kw[r8, :].T
            res = None
            for j in range(seq_per_tile):
                b = p * seq_per_tile + j
                c_b = c0_ref[b, hh]
                r_j = _dot(qh[r8, :], c_b)
                res = r_j if res is None else jnp.where(row8 // T_STEP == j, r_j, res)
                v_j = jnp.where(row8 // T_STEP == j, vh[r8, :], 0.0)
                last = p * SUBLANE + j * T_STEP + T_STEP - 1
                ct_ref[b, hh] = cs[last:last + 1, :] * c_b + _dot(kw_t, v_j)
            qc.append(res)
        num = num + ci * jnp.concatenate(qc, axis=0)
        hcell = num / jnp.maximum(jnp.abs(den), emt)
        ym_ref[:, sl] = (jax.nn.sigmoid(mo[:, sl]) * _rms(hcell, mgn_ref[...])).astype(BF16)


def _sample_mixer(proj, small, el, em, h0e, n0e, m0e, s_state, c_state, layer, w, prev):
    n = proj.shape[0]
    nseq = n // T_STEP
    bb = STEP_BB
    rows = bb * T_STEP
    row = lambda width, blk: pl.BlockSpec((rows, width), lambda i: (i, blk))
    lrow = lambda width: pl.BlockSpec((None, rows, width), lambda i: (layer, i, 0))
    lw3 = lambda s: pl.BlockSpec((None,) + s, lambda i: (layer, 0, 0))
    lw4 = lambda s: pl.BlockSpec((None,) + s, lambda i: (layer, 0, 0, 0))
    sspec = pl.BlockSpec((None, bb, GLA_H, GLA_DK, GLA_DV), lambda i: (layer, i, 0, 0, 0))
    cspec = pl.BlockSpec((None, bb, ML_H, ML_DH, ML_DH), lambda i: (layer, i, 0, 0, 0))
    nblk = ML_W // LANE
    kw, vw = GLA_H * GLA_DK, GLA_H * GLA_DV
    in_specs = [
        row(LRU_W, C_LRU // LRU_W), row(kw, C_GQ // kw), row(kw, C_GK // kw), row(vw, C_GV // vw),
        row(vw, C_GGATE // vw), row(ML_W, C_MX // ML_W), row(ML_W, C_MO // ML_W), row(LANE, 0),
        lrow(LRU_W), lrow(ML_W), lrow(LRU_W), lrow(ML_W), lrow(LANE), sspec, cspec,
        lw3((CONV_W, LRU_W)), lw3((1, LRU_W)), lw4((LRU_BLOCKS, LANE, LANE)), lw3((1, LRU_W)),
        lw4((LRU_BLOCKS, LANE, LANE)), lw3((1, LRU_W)), lw3((1, LRU_W)),
        lw3((LANE, kw)), lw3((1, kw)), lw3((1, GLA_DV)),
        lw3((CONV_W, ML_W)), lw3((1, ML_W)), *([lw4((nblk, LANE, LANE))] * 6), lw3((1, LANE)), lw3((1, ML_DH)),
    ]
    args = [proj] * 7 + [small, el, em, h0e, n0e, m0e, s_state, c_state,
                         w["lru_cw"], w["lru_cb"], w["lru_wa"], w["lru_ba"], w["lru_wx"], w["lru_bx"], w["lru_lam"],
                         w["wg2"], w["bg"], w["gla_gn"],
                         w["ml_cw"], w["ml_cb"], *w["ml_wq"], *w["ml_wk"], *w["ml_wv"], w["ml_bif"], w["ml_gn"]]
    aliases = {}
    if prev is not None:
        aliases = {len(in_specs) + j: 3 + j for j in range(len(prev))}
        in_specs += [pl.BlockSpec(memory_space=pl.ANY)] * len(prev)
        args += list(prev)
    sbytes = bb * GLA_H * GLA_DK * GLA_DV * 4
    cbytes = bb * ML_H * ML_DH * ML_DH * 4
    vmem = 4 * (sbytes + cbytes) + 2 * rows * (7 * 1024 + 8 * 1024) * 4 + 80 * rows * 1024 * 4 + (12 << 20)
    return pl.pallas_call(
        _sample_mixer_kernel,
        grid=(nseq // bb,),
        in_specs=in_specs,
        out_specs=[row(LRU_W, 0), row(vw, 0), row(ML_W, 0), lrow(LRU_W), lrow(ML_W), lrow(LANE), sspec, cspec],
        out_shape=[SDS((n, LRU_W), BF16), SDS((n, vw), BF16), SDS((n, ML_W), BF16),
                   SDS((DEPTH, n, LRU_W), F32), SDS((DEPTH, n, ML_W), F32), SDS((DEPTH, n, LANE), F32),
                   SDS((DEPTH, nseq, GLA_H, GLA_DK, GLA_DV), F32), SDS((DEPTH, nseq, ML_H, ML_DH, ML_DH), F32)],
        input_output_aliases=aliases,
        compiler_params=_params(("arbitrary",), vmem),
        name="sample_mixer",
    )(*args)


def _prep_weights(p):
    w_in = p["w_in"]
    w_main = jnp.concatenate([w_in[:, :, :3072], w_in[:, :, 3088:6160], w_in[:, :, 6168:]], axis=-1).astype(BF16)
    w_small = jnp.concatenate(
        [w_in[:, :, 3072:3088], w_in[:, :, 6160:6168], jnp.zeros((DEPTH, D, LANE - GLA_RANK - 2 * ML_H), F32)],
        axis=-1).astype(BF16)
    wg2 = jnp.concatenate(
        [p["gla_w_g2"], jnp.zeros((DEPTH, LANE - GLA_RANK, GLA_H * GLA_DK), F32)], axis=1)
    eye = jnp.eye(LANE // ML_BS, dtype=F32)

    def dense_bd(wb):
        wb = wb.reshape(DEPTH, ML_W // LANE, LANE // ML_BS, ML_BS, ML_BS)
        dense = jnp.einsum("dgncx,nm->dgncmx", wb, eye).reshape(DEPTH, ML_W // LANE, LANE, LANE)
        hi = dense.astype(BF16)
        return hi, (dense - hi.astype(F32)).astype(BF16)

    bif = jnp.zeros((DEPTH, 1, LANE), F32).at[:, 0, S_IG:S_IG + 2 * ML_H].set(p["ml_b_if"])
    r3 = lambda a: a.reshape(DEPTH, 1, a.shape[-1])
    return dict(
        w_main=w_main, w_small=w_small, wg2=wg2, bg=r3(p["gla_b_g"]), gla_gn=r3(p["gla_g_norm"]),
        g1=r3(p["g_norm1"]), g2=r3(p["g_norm2"]), gf=p["g_final"].reshape(1, D),
        lru_cw=p["lru_conv_w"], lru_cb=r3(p["lru_conv_b"]), lru_wa=p["lru_w_a"], lru_ba=r3(p["lru_b_a"]),
        lru_wx=p["lru_w_x"], lru_bx=r3(p["lru_b_x"]), lru_lam=r3(p["lru_lam"]),
        ml_cw=p["ml_conv_w"], ml_cb=r3(p["ml_conv_b"]), ml_wq=dense_bd(p["ml_w_q"]), ml_wk=dense_bd(p["ml_w_k"]),
        ml_wv=dense_bd(p["ml_w_v"]), ml_bif=bif, ml_gn=r3(p["ml_g_norm"]),
        w_br_lru=p["w_br_lru"].astype(BF16), w_br_gla=p["w_br_gla"].astype(BF16), w_br_ml=p["w_br_ml"].astype(BF16),
        w_out=p["w_out"].astype(BF16), w_ff1=p["w_ff1"].astype(BF16), w_ff2=p["w_ff2"].astype(BF16),
    )


def _dense_tail(x, y_lru, y_gla, y_ml, proj, mod, l, w, *, tm, rpg, r, tm_ffn):
    merged = _merge(y_lru, y_gla, y_ml, proj, l, w["w_br_lru"], w["w_br_gla"], w["w_br_ml"], tm=tm, tn=512)
    x = _out_proj(merged, x, mod, l, w["w_out"], tm=tm, tn=512, rpg=rpg, r=r)
    return _ffn(x, mod, l, w["g2"], w["gf"], w["w_ff1"], w["w_ff2"], tm=tm_ffn, tf=1024, rpg=rpg,
                r=r if r == 1 else tm_ffn, final_norm=(l == DEPTH - 1))


def _run_prompt(x3, mod, w):
    b, t, _ = x3.shape
    n = b * t
    tm = min(t, 1024)
    x = x3.reshape(n, D)
    prev = None
    for l in range(DEPTH):
        proj, small = _in_proj(x, mod, l, w["g1"], w["w_main"], w["w_small"], tm=tm, tn=1024, rpg=t, r=1)
        outs = _prompt_mixer(proj.reshape(b, t, N_MAIN), small.reshape(b, t, LANE), l, w, prev, L=min(t, CHUNK),
                             rows=min(t, PROMPT_ROWS))
        prev = outs[3:]
        y_lru, y_gla, y_ml = (y.reshape(n, y.shape[-1]) for y in outs[:3])
        x = _dense_tail(x, y_lru, y_gla, y_ml, proj, mod, l, w, tm=tm, rpg=t, r=1, tm_ffn=min(tm, 512))
    h_t, lcv_t, s_t, c_t, n_t, m_t, mcv_t = prev
    return x.reshape(b, t, D), [h_t.reshape(DEPTH, b, LRU_W), lcv_t, s_t, c_t, n_t, m_t[..., 0], mcv_t]


def _run_sample(x3, mod, states, w):
    b, t, _ = x3.shape
    n = b * t
    tm = n
    s_h, s_cv, s_gla, s_c, s_n, s_m, s_mcv = states
    pad_t = lambda a: jnp.pad(a, ((0, 0), (0, 0), (0, t - a.shape[2]), (0, 0))).reshape(DEPTH, n, a.shape[-1])
    el = pad_t(s_cv)
    em = pad_t(s_mcv)
    h0e = pad_t(s_h[:, :, None, :])
    n0e = jnp.broadcast_to(s_n.reshape(DEPTH, b, 1, ML_W), (DEPTH, b, t, ML_W)).reshape(DEPTH, n, ML_W)
    m_l = jnp.pad(s_m, ((0, 0), (0, 0), (S_FG, LANE - S_FG - ML_H)))
    m0e = jnp.broadcast_to(m_l[:, :, None, :], (DEPTH, b, t, LANE)).reshape(DEPTH, n, LANE)
    x = x3.reshape(n, D)
    prev = None
    conv_l, conv_m = [], []
    keep = t - (CONV_W - 1)
    for l in range(DEPTH):
        proj, small = _in_proj(x, mod, l, w["g1"], w["w_main"], w["w_small"], tm=tm, tn=1024, rpg=n, r=tm)
        outs = _sample_mixer(proj, small, el, em, h0e, n0e, m0e, s_gla, s_c, l, w, prev)
        y_lru, y_gla, y_ml = outs[:3]
        prev = outs[3:]
        conv_l.append(proj[:, C_LRU:C_LRU + LRU_W].reshape(b, t, LRU_W)[:, keep:])
        conv_m.append(proj[:, C_MX:C_MX + ML_W].reshape(b, t, ML_W)[:, keep:])
        x = _dense_tail(x, y_lru, y_gla, y_ml, proj, mod, l, w, tm=tm, rpg=n, r=tm, tm_ffn=tm)
    hrow, nrow, mrow, s_out, c_out = prev
    last = t - 1
    new_states = [hrow.reshape(DEPTH, b, t, LRU_W)[:, :, last], jnp.stack(conv_l), s_out, c_out,
                  nrow.reshape(DEPTH, b, t, ML_H, ML_DH)[:, :, last],
                  mrow.reshape(DEPTH, b, t, LANE)[:, :, last, S_FG:S_FG + ML_H], jnp.stack(conv_m)]
    return x.reshape(b, t, D), new_states


def kernel(x_prompt, x_sample, c_prompt, c_sample, state_lru_h, state_lru_conv, state_gla, state_mlstm_C, state_mlstm_n, state_mlstm_m, state_mlstm_conv, w_ada, b_ada, g_norm1, g_norm2, w_in, lru_conv_w, lru_conv_b, lru_w_a, lru_b_a, lru_w_x, lru_b_x, lru_lam, gla_w_g2, gla_b_g, gla_g_norm, ml_conv_w, ml_conv_b, ml_w_q, ml_w_k, ml_w_v, ml_b_if, ml_g_norm, w_br_lru, w_br_gla, w_br_ml, w_out, w_ff1, w_ff2, g_final):
    p = dict(g_norm1=g_norm1, g_norm2=g_norm2, w_in=w_in, lru_conv_w=lru_conv_w, lru_conv_b=lru_conv_b,
             lru_w_a=lru_w_a, lru_b_a=lru_b_a, lru_w_x=lru_w_x, lru_b_x=lru_b_x, lru_lam=lru_lam,
             gla_w_g2=gla_w_g2, gla_b_g=gla_b_g, gla_g_norm=gla_g_norm, ml_conv_w=ml_conv_w, ml_conv_b=ml_conv_b,
             ml_w_q=ml_w_q, ml_w_k=ml_w_k, ml_w_v=ml_w_v, ml_b_if=ml_b_if, ml_g_norm=ml_g_norm,
             w_br_lru=w_br_lru, w_br_gla=w_br_gla, w_br_ml=w_br_ml, w_out=w_out, w_ff1=w_ff1, w_ff2=w_ff2,
             g_final=g_final)
    w = _prep_weights(p)
    bp = x_prompt.shape[0]
    bs, ts, _ = x_sample.shape
    assert ts == T_STEP and bs % STEP_BB == 0
    ns = bs * ts
    pad = (-(ns + bp)) % SUBLANE
    c_all = jnp.concatenate([jnp.repeat(c_sample, ts, axis=0), c_prompt, jnp.zeros((pad, D), F32)], axis=0)
    mod_all = _ada(c_all, w_ada, b_ada)
    mod_s = mod_all.reshape(DEPTH, 1, ns + bp + pad, N_MOD * D)
    mod_p = mod_all[:, ns:ns + bp].reshape(DEPTH, bp, 1, N_MOD * D)
    y_p, ps = _run_prompt(x_prompt, mod_p, w)
    y_s, ss = _run_sample(x_sample, mod_s,
                          (state_lru_h, state_lru_conv, state_gla, state_mlstm_C, state_mlstm_n, state_mlstm_m,
                           state_mlstm_conv), w)
    return (y_p, y_s, *ps, *ss)
```

```python
import functools

import jax
import jax.numpy as jnp
from jax import lax
from jax.experimental import pallas as pl
from jax.experimental.pallas import tpu as pltpu

F32, BF16 = jnp.float32, jnp.bfloat16
SDS = jax.ShapeDtypeStruct

D = 2048
DEPTH = 2
LRU_W = 1024
LRU_BLOCKS = 8
LRU_C = 8.0
CONV_W = 4
GLA_H = 4
GLA_DK = 128
GLA_DV = 256
GLA_RANK = 16
GLA_TAU = 16.0
ML_H = 4
ML_W = 1024
ML_DH = 256
ML_BS = 4
CHUNK = 64
PROMPT_ROWS = 256
D_FF = 4 * D
EPS = 1e-6
N_MOD = 6
T_STEP = 4
STEP_BB = 4

LANE = 128
SUBLANE = 8
VMEM_LIMIT_CAP = 56 * 1024 * 1024

C_LRU, C_GQ, C_GK, C_GV, C_GGATE, C_MX, C_MO, C_MG = 0, 1024, 1536, 2048, 3072, 4096, 5120, 6144
N_MAIN = 12288
W_SHIFT1 = GLA_RANK
W_SHIFT2 = GLA_RANK + 2 * ML_H
S_GLR = 0
S_IG = 16
S_FG = 20
NEG_BIG = -1e30


def _params(sem, vmem_bytes):
    return pltpu.CompilerParams(dimension_semantics=sem, vmem_limit_bytes=int(min(vmem_bytes, VMEM_LIMIT_CAP)))


def _dot(a, b):
    return jnp.dot(a.astype(BF16), b.astype(BF16), preferred_element_type=F32)


def _dot_nt(a, b):
    return lax.dot_general(a.astype(BF16), b.astype(BF16), (((1,), (1,)), ((), ())), preferred_element_type=F32)


def _split3(x):
    p0 = x.astype(BF16)
    r = x - p0.astype(F32)
    p1 = r.astype(BF16)
    return p0, p1, (r - p1.astype(F32)).astype(BF16)


def _cumsum_rows(tri_b, x):
    return sum(jnp.dot(tri_b, p, preferred_element_type=F32) for p in _split3(x))


def _cumsum_lanes(x, triu_b):
    return sum(jnp.dot(p, triu_b, preferred_element_type=F32) for p in _split3(x))


def _split(x):
    hi = x.astype(BF16)
    return hi, (x - hi.astype(F32)).astype(BF16)


def _dot3(x_hi, x_lo, w_hi, w_lo):
    d = lambda a, b: jnp.dot(a, b, preferred_element_type=F32)
    return d(x_hi, w_hi) + (d(x_lo, w_hi) + d(x_hi, w_lo))


def _log_sigmoid(z):
    return jnp.minimum(z, 0.0) - jnp.log1p(jnp.exp(-jnp.abs(z)))


def _silu(z):
    return z * jax.nn.sigmoid(z)


def _rms(x, g):
    return x * lax.rsqrt(jnp.mean(x * x, axis=-1, keepdims=True) + EPS) * g


def _tri(n):
    r = lax.broadcasted_iota(jnp.int32, (n, n), 0)
    c = lax.broadcasted_iota(jnp.int32, (n, n), 1)
    return r >= c


def _lru_gates(ub, wa, wx, ba, bx, lam):
    r = jax.nn.sigmoid(_dot(ub, wa) + ba)
    i = jax.nn.sigmoid(_dot(ub, wx) + bx)
    log_a = LRU_C * r * _log_sigmoid(lam)
    t = jnp.tanh(log_a)
    return jnp.exp(log_a), jnp.sqrt(-2.0 * t / (1.0 - t)) * (i * ub)


def _blockdiag3(x, wh_ref, wl_ref):
    xh, xl = _split(x)
    outs = []
    for blk in range(ML_W // LANE):
        sl = slice(blk * LANE, (blk + 1) * LANE)
        outs.append(_dot3(xh[:, sl], xl[:, sl], wh_ref[blk], wl_ref[blk]))
    return jnp.concatenate(outs, axis=1)


def _ada_kernel(c_ref, w_ref, b_ref, o_ref):
    o_ref[...] = _dot(_silu(c_ref[...]), w_ref[...]) + b_ref[...]


def _ada(c_all, w_ada, b_ada):
    m = c_all.shape[0]
    tn = 512
    n_out = N_MOD * D
    return pl.pallas_call(
        _ada_kernel,
        grid=(DEPTH, n_out // tn),
        in_specs=[
            pl.BlockSpec((m, D), lambda l, n: (0, 0)),
            pl.BlockSpec((None, D, tn), lambda l, n: (l, 0, n)),
            pl.BlockSpec((None, 1, tn), lambda l, n: (l, 0, n)),
        ],
        out_specs=pl.BlockSpec((None, m, tn), lambda l, n: (l, 0, n)),
        out_shape=SDS((DEPTH, m, n_out), F32),
        compiler_params=_params(("parallel", "arbitrary"), 2 * (m * D + D * tn + m * tn) * 4 + (12 << 20)),
        name="ada",
    )(c_all, w_ada, b_ada.reshape(DEPTH, 1, n_out))


def _modspec(layer, comp, r, width, tm, rpg, jmap=None):
    nb = D // width
    jm = (lambda j: 0) if jmap is None else jmap
    if r == 1:
        return pl.BlockSpec((None, None, 1, width), lambda i, j: (layer, (i * tm) // rpg, 0, comp * nb + jm(j)))
    mode = pl.Buffered(1) if jmap is None else None
    return pl.BlockSpec((None, None, tm, width), lambda i, j: (layer, 0, i, comp * nb + jm(j)), pipeline_mode=mode)


def _norm_mod_store(xn_s, x_ref, g_ref, sc_ref, sh_ref):
    tm = x_ref.shape[0]
    rc = min(tm, 256)
    for c in range(tm // rc):
        rows = slice(c * rc, (c + 1) * rc)
        mrows = rows if sc_ref.shape[0] == tm else slice(None)
        xn = _rms(x_ref[rows, :], g_ref[...]) * (1.0 + sc_ref[mrows, :]) + sh_ref[mrows, :]
        xn_s[rows, :] = xn.astype(BF16)


def _in_proj_kernel(x_ref, sc_ref, sh_ref, g_ref, wm_ref, ws_ref, om_ref, os_ref, xn_s):
    @pl.when(pl.program_id(1) == 0)
    def _():
        _norm_mod_store(xn_s, x_ref, g_ref, sc_ref, sh_ref)
        os_ref[...] = jnp.dot(xn_s[...], ws_ref[...], preferred_element_type=F32)

    om_ref[...] = jnp.dot(xn_s[...], wm_ref[...], preferred_element_type=F32)


def _in_proj(x, mod, layer, g1, w_main, w_small, *, tm, tn, rpg, r):
    n = x.shape[0]
    vmem = 2 * (tm * D * 4 + 2 * r * D * 4 + D * tn * 2 + D * LANE * 2 + tm * tn * 4 + tm * LANE * 4) + tm * D * 2
    vmem += 8 << 20
    return pl.pallas_call(
        _in_proj_kernel,
        grid=(n // tm, N_MAIN // tn),
        in_specs=[
            pl.BlockSpec((tm, D), lambda i, j: (i, 0)),
            _modspec(layer, 1, r, D, tm, rpg),
            _modspec(layer, 0, r, D, tm, rpg),
            pl.BlockSpec((None, 1, D), lambda i, j: (layer, 0, 0)),
            pl.BlockSpec((D, tn), lambda i, j: (0, j)),
            pl.BlockSpec((None, D, LANE), lambda i, j: (layer, 0, 0)),
        ],
        out_specs=[pl.BlockSpec((tm, tn), lambda i, j: (i, j)), pl.BlockSpec((tm, LANE), lambda i, j: (i, 0))],
        out_shape=[SDS((n, N_MAIN), F32), SDS((n, LANE), F32)],
        scratch_shapes=[pltpu.VMEM((tm, D), BF16)],
        compiler_params=_params(("parallel", "arbitrary"), vmem),
        name="in_proj",
    )(x, mod, mod, g1, w_main, w_small)


def _in_proj_cast_kernel(x_ref, sc_ref, sh_ref, g_ref, wa_ref, wb_ref, ws_ref, om_ref, os_ref, wc_ref, xn_s, *, tn):
    j = pl.program_id(1)

    @pl.when(j == 0)
    def _():
        _norm_mod_store(xn_s, x_ref, g_ref, sc_ref, sh_ref)
        os_ref[...] = jnp.dot(xn_s[...], ws_ref[...], preferred_element_type=F32)

    def emit(shift):
        def body():
            if shift == 0:
                wt = wa_ref[...]
            else:
                src = jnp.concatenate([wa_ref[...], wb_ref[...]], axis=1)
                wt = pltpu.roll(src, tn + LANE - shift, 1)[:, :tn]
            wt = wt.astype(BF16)
            wc_ref[...] = wt
            om_ref[...] = jnp.dot(xn_s[...], wt, preferred_element_type=F32)
        return body

    nb1, nb2 = C_GGATE // tn, C_MG // tn
    pl.when(j < nb1)(emit(0))
    pl.when(jnp.logical_and(j >= nb1, j < nb2))(emit(W_SHIFT1))
    pl.when(j >= nb2)(emit(W_SHIFT2))


def _in_proj_cast(x, mod, layer, g1, w_in, w_small, *, tn, rpg):
    n = x.shape[0]
    tm = n
    vmem = 2 * (tm * D * 4 + 2 * tm * D * 4 + D * (tn + LANE) * 4 + D * LANE * 2 + tm * tn * 4 + tm * LANE * 4
                + D * tn * 2) + tm * D * 2 + 3 * D * (tn + LANE) * 4 + (4 << 20)
    return pl.pallas_call(
        functools.partial(_in_proj_cast_kernel, tn=tn),
        grid=(1, N_MAIN // tn),
        in_specs=[
            pl.BlockSpec((tm, D), lambda i, j: (i, 0), pipeline_mode=pl.Buffered(1)),
            _modspec(layer, 1, tm, D, tm, rpg),
            _modspec(layer, 0, tm, D, tm, rpg),
            pl.BlockSpec((None, 1, D), lambda i, j: (layer, 0, 0)),
            pl.BlockSpec((None, D, tn), lambda i, j: (layer, 0, j)),
            pl.BlockSpec((None, D, LANE), lambda i, j: (layer, 0, (j + 1) * (tn // LANE))),
            pl.BlockSpec((None, D, LANE), lambda i, j: (layer, 0, 0)),
        ],
        out_specs=[pl.BlockSpec((tm, tn), lambda i, j: (i, j)), pl.BlockSpec((tm, LANE), lambda i, j: (i, 0)),
                   pl.BlockSpec((D, tn), lambda i, j: (0, j))],
        out_shape=[SDS((n, N_MAIN), F32), SDS((n, LANE), F32), SDS((D, N_MAIN), BF16)],
        scratch_shapes=[pltpu.VMEM((tm, D), BF16)],
        compiler_params=_params(("arbitrary", "arbitrary"), vmem),
        name="in_proj_cast",
    )(x, mod, mod, g1, w_in, w_in, w_small)


def _wspec(shape, imap, layer, cast):
    if cast:
        return pl.BlockSpec((None,) + shape, lambda i, j: (layer,) + imap(i, j))
    return pl.BlockSpec(shape, imap)


def _merge_kernel(yl_ref, yg_ref, ym_ref, g0_ref, g1_ref, g2_ref, w0_ref, w1_ref, w2_ref, o_ref, *wc_refs):
    ws = [w_ref[...].astype(BF16) for w_ref in (w0_ref, w1_ref, w2_ref)]
    for wc_ref, wb in zip(wc_refs, ws):
        wc_ref[...] = wb
    acc = jax.nn.sigmoid(g0_ref[...]) * jnp.dot(yl_ref[...], ws[0], preferred_element_type=F32)
    acc += jax.nn.sigmoid(g1_ref[...]) * jnp.dot(yg_ref[...], ws[1], preferred_element_type=F32)
    acc += jax.nn.sigmoid(g2_ref[...]) * jnp.dot(ym_ref[...], ws[2], preferred_element_type=F32)
    o_ref[...] = acc.astype(BF16)


def _merge(y_lru, y_gla, y_ml, proj, layer, w_lru, w_gla, w_ml, *, tm, tn, cast):
    n = y_lru.shape[0]
    w = LRU_W
    gb = C_MG // tn
    nb = D // tn
    yspec = pl.BlockSpec((tm, w), lambda i, j: (i, 0))
    wspec = _wspec((w, tn), lambda i, j: (0, j), layer, cast)
    wbytes = 4 if cast else 2
    vmem = 2 * (3 * tm * w * 2 + 3 * tm * tn * 4 + 3 * w * tn * wbytes + tm * tn * 2) + 6 * tm * tn * 4 + (4 << 20)
    out_specs = [pl.BlockSpec((tm, tn), lambda i, j: (i, j))]
    out_shape = [SDS((n, D), BF16)]
    if cast:
        out_specs += [pl.BlockSpec((w, tn), lambda i, j: (0, j))] * 3
        out_shape += [SDS((w, D), BF16)] * 3
        vmem += 2 * 3 * w * tn * 2 + 3 * w * tn * 4
    return pl.pallas_call(
        _merge_kernel,
        grid=(n // tm, nb),
        in_specs=[
            yspec, yspec, yspec,
            pl.BlockSpec((tm, tn), lambda i, j: (i, gb + j)),
            pl.BlockSpec((tm, tn), lambda i, j: (i, gb + nb + j)),
            pl.BlockSpec((tm, tn), lambda i, j: (i, gb + 2 * nb + j)),
            wspec, wspec, wspec,
        ],
        out_specs=out_specs,
        out_shape=out_shape,
        compiler_params=_params(("arbitrary" if cast else "parallel", "arbitrary"), vmem),
        name="merge",
    )(y_lru, y_gla, y_ml, proj, proj, proj, w_lru, w_gla, w_ml)


def _out_proj_kernel(m_ref, w_ref, x_ref, gt_ref, o_ref, *wc_refs):
    wb = w_ref[...].astype(BF16)
    for wc_ref in wc_refs:
        wc_ref[...] = wb
    o_ref[...] = x_ref[...] + gt_ref[...] * jnp.dot(m_ref[...], wb, preferred_element_type=F32)


def _out_proj(merged, x, mod, layer, w_out, *, tm, tn, rpg, r, cast):
    n = x.shape[0]
    wbytes = 4 if cast else 2
    vmem = 2 * (tm * D * 2 + D * tn * wbytes + 2 * tm * tn * 4 + r * tn * 4) + 2 * tm * tn * 4 + (4 << 20)
    out_specs = [pl.BlockSpec((tm, tn), lambda i, j: (i, j))]
    out_shape = [SDS((n, D), F32)]
    if cast:
        out_specs.append(pl.BlockSpec((D, tn), lambda i, j: (0, j)))
        out_shape.append(SDS((D, D), BF16))
        vmem += 2 * D * tn * 2 + D * tn * 4
    return pl.pallas_call(
        _out_proj_kernel,
        grid=(n // tm, D // tn),
        in_specs=[
            pl.BlockSpec((tm, D), lambda i, j: (i, 0)),
            _wspec((D, tn), lambda i, j: (0, j), layer, cast),
            pl.BlockSpec((tm, tn), lambda i, j: (i, j)),
            _modspec(layer, 2, r, tn, tm, rpg, jmap=lambda j: j),
        ],
        out_specs=out_specs,
        out_shape=out_shape,
        compiler_params=_params(("arbitrary" if cast else "parallel", "arbitrary"), vmem),
        name="out_proj",
    )(merged, w_out, x, mod)


def _ffn_kernel(x_ref, sc_ref, sh_ref, gt_ref, g_ref, gf_ref, w1_ref, w2_ref, o_ref, *rest, final_norm):
    xn_s = rest[-1]
    f = pl.program_id(1)

    @pl.when(f == 0)
    def _():
        _norm_mod_store(xn_s, x_ref, g_ref, sc_ref, sh_ref)
        o_ref[...] = jnp.zeros_like(o_ref)

    w1 = w1_ref[...].astype(BF16)
    w2 = w2_ref[...].astype(BF16)
    for wc_ref, wb in zip(rest[:-1], (w1, w2)):
        wc_ref[...] = wb
    h = jnp.square(jnp.maximum(jnp.dot(xn_s[...], w1, preferred_element_type=F32), 0.0))
    o_ref[...] += jnp.dot(h.astype(BF16), w2, preferred_element_type=F32)

    @pl.when(f == pl.num_programs(1) - 1)
    def _():
        y = x_ref[...] + gt_ref[...] * o_ref[...]
        if final_norm:
            y = _rms(y, gf_ref[...])
        o_ref[...] = y


def _ffn(x, mod, layer, g2, g_final, w1, w2, *, tm, tf, rpg, r, final_norm, cast):
    n = x.shape[0]
    wbytes = 4 if cast else 2
    vmem = 2 * (2 * tm * D * 4 + 3 * r * D * 4 + 2 * D * tf * wbytes) + tm * D * 2 + 2 * tm * tf * 4 + 2 * tm * D * 4
    single = pl.Buffered(1) if n == tm else None
    out_specs = [pl.BlockSpec((tm, D), lambda i, f: (i, 0), pipeline_mode=single)]
    out_shape = [SDS((n, D), F32)]
    if cast:
        out_specs += [pl.BlockSpec((D, tf), lambda i, f: (0, f)), pl.BlockSpec((tf, D), lambda i, f: (f, 0))]
        out_shape += [SDS((D, D_FF), BF16), SDS((D_FF, D), BF16)]
        vmem += 2 * 2 * D * tf * 2
    return pl.pallas_call(
        functools.partial(_ffn_kernel, final_norm=final_norm),
        grid=(n // tm, D_FF // tf),
        in_specs=[
            pl.BlockSpec((tm, D), lambda i, f: (i, 0), pipeline_mode=single),
            _modspec(layer, 4, r, D, tm, rpg),
            _modspec(layer, 3, r, D, tm, rpg),
            _modspec(layer, 5, r, D, tm, rpg),
            pl.BlockSpec((None, 1, D), lambda i, f: (layer, 0, 0)),
            pl.BlockSpec((1, D), lambda i, f: (0, 0)),
            _wspec((D, tf), lambda i, f: (0, f), layer, cast),
            _wspec((tf, D), lambda i, f: (f, 0), layer, cast),
        ],
        out_specs=out_specs,
        out_shape=out_shape,
        scratch_shapes=[pltpu.VMEM((tm, D), BF16)],
        compiler_params=_params(("arbitrary" if cast else "parallel", "arbitrary"), vmem),
        name="ffn",
    )(x, mod, mod, mod, g2, g_final, w1, w2)


def _conv_chunk(xp_s, x, w, b, L):
    xp_s[SUBLANE:SUBLANE + L, :] = x
    acc = b + xp_s[5:5 + L, :] * w[0:1]
    for j in range(1, CONV_W):
        acc = acc + xp_s[5 + j:5 + j + L, :] * w[j:j + 1]
    tail = xp_s[5 + L:8 + L, :]
    xp_s[5:8, :] = tail
    return acc, tail


def _prompt_mixer_kernel(lx_ref, q_ref, k_ref, v_ref, gate_ref, mx_ref, mo_ref, small_ref,
                         lcw_ref, lcb_ref, wa_ref, ba_ref, wx_ref, bx_ref, lam_ref,
                         wg2_ref, bg_ref, ggn_ref,
                         mcw_ref, mcb_ref, wqh_ref, wql_ref, wkh_ref, wkl_ref, wvh_ref, wvl_ref, bif_ref, mgn_ref,
                         *rest, L):
    (yl_ref, yg_ref, ym_ref, ht_ref, lct_ref, st_ref, ct_ref, nt_ref, mt_ref, mct_ref,
     xpl_s, xpm_s, a_s, u_s, h_s, s_s, c_s, n_s, m_s) = rest[-19:]

    @pl.when(pl.program_id(1) == 0)
    def _():
        xpl_s[0:SUBLANE, :] = jnp.zeros((SUBLANE, LRU_W), F32)
        xpm_s[0:SUBLANE, :] = jnp.zeros((SUBLANE, ML_W), F32)
        h_s[...] = jnp.zeros_like(h_s)
        s_s[...] = jnp.zeros_like(s_s)
        c_s[...] = jnp.zeros_like(c_s)
        n_s[...] = jnp.zeros_like(n_s)
        m_s[...] = jnp.zeros_like(m_s)

    rows = lx_ref.shape[0]

    u, l_tail = _conv_chunk(xpl_s, lx_ref[...], lcw_ref[...], lcb_ref[...], rows)
    for blk in range(LRU_BLOCKS):
        sl = slice(blk * LANE, (blk + 1) * LANE)
        a_s[:, sl], u_s[:, sl] = _lru_gates(u[:, sl], wa_ref[blk], wx_ref[blk], ba_ref[:, sl], bx_ref[:, sl],
                                            lam_ref[:, sl])
    t8 = lax.broadcasted_iota(jnp.int32, (SUBLANE, LRU_W), 0)
    h = h_s[...]
    for g in range(rows // SUBLANE):
        r8 = slice(g * SUBLANE, (g + 1) * SUBLANE)
        a8, u8 = a_s[r8, :], u_s[r8, :]
        for s in (1, 2, 4):
            u8 = jnp.where(t8 >= s, u8 + a8 * pltpu.roll(u8, s, 0), u8)
            a8 = jnp.where(t8 >= s, a8 * pltpu.roll(a8, s, 0), a8)
        h8 = u8 + a8 * h
        a_s[r8, :] = h8
        h = h8[SUBLANE - 1:SUBLANE, :]
    h_s[...] = h
    yl_ref[...] = a_s[...].astype(BF16)

    q, k, v, gate = q_ref[...], k_ref[...], v_ref[...], gate_ref[...]
    small = small_ref[...]
    logdec = _log_sigmoid(_dot(small, wg2_ref[...]) + bg_ref[...]) * (1.0 / GLA_TAU)
    mx = mx_ref[...]
    conv, m_tail = _conv_chunk(xpm_s, mx, mcw_ref[...], mcb_ref[...], rows)
    mc = _silu(conv)
    mq = _blockdiag3(mc, wqh_ref, wql_ref)
    mk = _blockdiag3(mc, wkh_ref, wkl_ref) * (ML_DH ** -0.5)
    mv = _blockdiag3(mx, wvh_ref, wvl_ref)
    gates = small + bif_ref[...]
    gates_t = gates.T
    lf = _log_sigmoid(gates)
    lf_t = _log_sigmoid(gates_t)
    mo = mo_ref[...]
    tri = _tri(L)
    tri_b = tri.astype(BF16)
    r_i = lax.broadcasted_iota(jnp.int32, (L, L), 0)
    c_i = lax.broadcasted_iota(jnp.int32, (L, L), 1)
    triu_b = (r_i <= c_i).astype(BF16)

    for cc in range(rows // L):
        rs = slice(cc * L, (cc + 1) * L)
        bcum = _cumsum_rows(tri_b, logdec[rs, :])
        for hh in range(GLA_H):
            ks = slice(hh * GLA_DK, (hh + 1) * GLA_DK)
            vs = slice(hh * GLA_DV, (hh + 1) * GLA_DV)
            bh = bcum[:, ks]
            k_h, v_h = k[rs, ks], v[rs, vs]
            qe = (q[rs, ks] * (GLA_DK ** -0.5)) * jnp.exp(bh)
            ke = k_h * jnp.exp(-bh)
            a = jnp.where(tri, _dot_nt(qe, ke), 0.0)
            s_h = s_s[hh]
            o = _dot(qe, s_h) + _dot(a, v_h)
            bl = bh[L - 1:L, :]
            kd = k_h * jnp.exp(bl - bh)
            dec_col = jnp.exp(jnp.broadcast_to(bl, (SUBLANE, GLA_DK))).T[:, 0:1]
            s_s[hh] = dec_col * s_h + _dot(kd.T, v_h)
            yg_ref[rs, vs] = (_rms(o, ggn_ref[...]) * _silu(gate[rs, vs])).astype(BF16)

        f_col = _cumsum_rows(tri_b, lf[rs, :])
        f_row = _cumsum_lanes(lf_t[:, rs], triu_b)
        for hh in range(ML_H):
            sl = slice(hh * ML_DH, (hh + 1) * ML_DH)
            qh, kh, vh = mq[rs, sl], mk[rs, sl], mv[rs, sl]
            fc = f_col[:, S_FG + hh:S_FG + hh + 1]
            fr = f_row[S_FG + hh:S_FG + hh + 1, :]
            igr = gates_t[S_IG + hh:S_IG + hh + 1, rs]
            m_h = m_s[hh:hh + 1, 0:1]
            dm = jnp.where(tri, fc - fr + igr, NEG_BIG)
            inter = fc + m_h
            mt = jnp.maximum(inter, jnp.max(dm, axis=-1, keepdims=True))
            ci = jnp.exp(inter - mt)
            s = _dot_nt(qh, kh) * jnp.exp(dm - mt)
            c_h = c_s[hh]
            n_h = n_s[hh:hh + 1, :]
            num = ci * _dot(qh, c_h) + _dot(s, vh)
            den = ci * jnp.sum(qh * n_h, axis=-1, keepdims=True) + jnp.sum(s, axis=-1, keepdims=True)
            hcell = num / jnp.maximum(jnp.abs(den), jnp.exp(-mt))
            fl = fr[:, L - 1:L]
            dj = fl - fr + igr
            m_new = jnp.maximum(fl + m_h, jnp.max(dj, axis=-1, keepdims=True))
            cs = jnp.exp(fl + m_h - m_new)
            wj = jnp.exp(dj - m_new)
            c_s[hh] = cs * c_h + _dot(kh.T * wj, vh)
            n_s[hh:hh + 1, :] = cs * n_h + _dot(jnp.broadcast_to(wj, (SUBLANE, L)), kh)[0:1, :]
            m_s[hh:hh + 1, :] = jnp.broadcast_to(m_new, (1, LANE))
            ym_ref[rs, sl] = (jax.nn.sigmoid(mo[rs, sl]) * _rms(hcell, mgn_ref[...])).astype(BF16)

    @pl.when(pl.program_id(1) == pl.num_programs(1) - 1)
    def _():
        ht_ref[...] = h
        lct_ref[...] = l_tail
        st_ref[...] = s_s[...]
        ct_ref[...] = c_s[...]
        nt_ref[...] = n_s[...]
        mt_ref[...] = m_s[...]
        mct_ref[...] = m_tail


def _prompt_mixer(proj3, small3, layer, w, prev, *, L, rows):
    b, t, _ = proj3.shape
    kw, vw = GLA_H * GLA_DK, GLA_H * GLA_DV
    nblk = ML_W // LANE
    blk3 = lambda width, cb: pl.BlockSpec((None, rows, width), lambda bi, c: (bi, c, cb))
    lw3 = lambda s: pl.BlockSpec((None,) + s, lambda bi, c: (layer, 0, 0))
    lw4 = lambda s: pl.BlockSpec((None,) + s, lambda bi, c: (layer, 0, 0, 0))
    st3 = lambda s: pl.BlockSpec((None, None) + s, lambda bi, c: (layer, bi, 0, 0))
    st4 = lambda s: pl.BlockSpec((None, None) + s, lambda bi, c: (layer, bi, 0, 0, 0))
    in_specs = [
        blk3(LRU_W, C_LRU // LRU_W), blk3(kw, C_GQ // kw), blk3(kw, C_GK // kw), blk3(vw, C_GV // vw),
        blk3(vw, C_GGATE // vw), blk3(ML_W, C_MX // ML_W), blk3(ML_W, C_MO // ML_W), blk3(LANE, 0),
        lw3((CONV_W, LRU_W)), lw3((1, LRU_W)), lw4((LRU_BLOCKS, LANE, LANE)), lw3((1, LRU_W)),
        lw4((LRU_BLOCKS, LANE, LANE)), lw3((1, LRU_W)), lw3((1, LRU_W)),
        lw3((LANE, kw)), lw3((1, kw)), lw3((1, GLA_DV)),
        lw3((CONV_W, ML_W)), lw3((1, ML_W)), *([lw4((nblk, LANE, LANE))] * 6), lw3((1, LANE)), lw3((1, ML_DH)),
    ]
    args = [proj3] * 7 + [small3,
                          w["lru_cw"], w["lru_cb"], w["lru_wa"], w["lru_ba"], w["lru_wx"], w["lru_bx"], w["lru_lam"],
                          w["wg2"], w["bg"], w["gla_gn"],
                          w["ml_cw"], w["ml_cb"], *w["ml_wq"], *w["ml_wk"], *w["ml_wv"], w["ml_bif"], w["ml_gn"]]
    aliases = {}
    if prev is not None:
        aliases = {len(in_specs) + j: 3 + j for j in range(len(prev))}
        in_specs += [pl.BlockSpec(memory_space=pl.ANY)] * len(prev)
        args += list(prev)
    sbytes = GLA_H * GLA_DK * GLA_DV * 4
    cbytes = ML_H * ML_DH * ML_DH * 4
    vmem = 2 * rows * (7 * 1024 * 4 + 3 * 1024 * 2) + 3 * (sbytes + cbytes) + 40 * rows * 1024 * 4 + (12 << 20)
    return pl.pallas_call(
        functools.partial(_prompt_mixer_kernel, L=L),
        grid=(b, t // rows),
        in_specs=in_specs,
        out_specs=[blk3(LRU_W, 0), blk3(vw, 0), blk3(ML_W, 0),
                   st3((1, LRU_W)), st3((CONV_W - 1, LRU_W)), st4((GLA_H, GLA_DK, GLA_DV)),
                   st4((ML_H, ML_DH, ML_DH)), st3((ML_H, ML_DH)), st3((ML_H, LANE)), st3((CONV_W - 1, ML_W))],
        out_shape=[SDS((b, t, LRU_W), BF16), SDS((b, t, vw), BF16), SDS((b, t, ML_W), BF16),
                   SDS((DEPTH, b, 1, LRU_W), F32), SDS((DEPTH, b, CONV_W - 1, LRU_W), F32),
                   SDS((DEPTH, b, GLA_H, GLA_DK, GLA_DV), F32), SDS((DEPTH, b, ML_H, ML_DH, ML_DH), F32),
                   SDS((DEPTH, b, ML_H, ML_DH), F32), SDS((DEPTH, b, ML_H, LANE), F32),
                   SDS((DEPTH, b, CONV_W - 1, ML_W), F32)],
        scratch_shapes=[pltpu.VMEM((SUBLANE + rows, LRU_W), F32), pltpu.VMEM((SUBLANE + rows, ML_W), F32),
                        pltpu.VMEM((rows, LRU_W), F32), pltpu.VMEM((rows, LRU_W), F32), pltpu.VMEM((1, LRU_W), F32),
                        pltpu.VMEM((GLA_H, GLA_DK, GLA_DV), F32), pltpu.VMEM((ML_H, ML_DH, ML_DH), F32),
                        pltpu.VMEM((ML_H, ML_DH), F32), pltpu.VMEM((ML_H, LANE), F32)],
        input_output_aliases=aliases,
        compiler_params=_params(("parallel", "arbitrary"), vmem),
        name="prompt_mixer",
    )(*args)


def _tmask(shape):
    return lax.broadcasted_iota(jnp.int32, shape, 0) % T_STEP


def _down(x, s):
    return x if s == 0 else pltpu.roll(x, s, 0)


def _up(x, s):
    return x if s == 0 else pltpu.roll(x, x.shape[0] - s, 0)


def _seg_cumsum(x):
    t = _tmask(x.shape)
    out = x
    for s in range(1, T_STEP):
        out = out + jnp.where(t >= s, _down(x, s), 0.0)
    return out


def _seg_last(x):
    t = _tmask(x.shape)
    out = x
    for s in range(1, T_STEP):
        out = jnp.where(t == T_STEP - 1 - s, _up(x, s), out)
    return out


def _seg_allreduce(x, op):
    t = _tmask(x.shape)
    y = op(x, jnp.where(t % 2 == 1, _down(x, 1), _up(x, 1)))
    return op(y, jnp.where(t >= 2, _down(y, 2), _up(y, 2)))


def _conv_rows(x, e, w, b):
    t = _tmask(x.shape)
    acc = b + x * w[CONV_W - 1:CONV_W]
    for k in range(1, CONV_W):
        hist = jnp.where(t >= k, _down(x, k), _up(e, CONV_W - 1 - k))
        acc = acc + hist * w[CONV_W - 1 - k:CONV_W - k]
    return acc


def _col(x, lane):
    li = lax.broadcasted_iota(jnp.int32, x.shape, 1)
    return jnp.sum(jnp.where(li == lane, x, 0.0), axis=-1, keepdims=True)


def _sample_mixer_kernel(lx_ref, q_ref, k_ref, v_ref, gate_ref, mx_ref, mo_ref, small_ref,
                         el_ref, em_ref, h0_ref, n0_ref, m0_ref, s0_ref, c0_ref,
                         lcw_ref, lcb_ref, wa_ref, ba_ref, wx_ref, bx_ref, lam_ref,
                         wg2_ref, bg_ref, ggn_ref,
                         mcw_ref, mcb_ref, wqh_ref, wql_ref, wkh_ref, wkl_ref, wvh_ref, wvl_ref, bif_ref, mgn_ref,
                         *rest):
    yl_ref, yg_ref, ym_ref, hrow_ref, nrow_ref, mrow_ref, st_ref, ct_ref = rest[-8:]
    rows = lx_ref.shape[0]
    n_pairs = rows // SUBLANE
    seq_per_tile = SUBLANE // T_STEP
    t_col = _tmask((rows, 1))
    row8 = lax.broadcasted_iota(jnp.int32, (SUBLANE, 1), 0)

    u = _conv_rows(lx_ref[...], el_ref[...], lcw_ref[...], lcb_ref[...])
    a_blocks, g_blocks = [], []
    for blk in range(LRU_BLOCKS):
        sl = slice(blk * LANE, (blk + 1) * LANE)
        a_b, g_b = _lru_gates(u[:, sl], wa_ref[blk], wx_ref[blk], ba_ref[:, sl], bx_ref[:, sl], lam_ref[:, sl])
        a_blocks.append(a_b)
        g_blocks.append(g_b)
    a = jnp.concatenate(a_blocks, axis=1)
    t_w = _tmask(a.shape)
    g = jnp.concatenate(g_blocks, axis=1) + jnp.where(t_w == 0, a * h0_ref[...], 0.0)
    a1 = jnp.where(t_w >= 1, a * _down(a, 1), a)
    g1 = jnp.where(t_w >= 1, g + a * _down(g, 1), g)
    h = jnp.where(t_w >= 2, g1 + a1 * _down(g1, 2), g1)
    yl_ref[...] = h.astype(BF16)
    hrow_ref[...] = h

    q, k, v = q_ref[...], k_ref[...], v_ref[...]
    logdec = _log_sigmoid(_dot(small_ref[...], wg2_ref[...]) + bg_ref[...]) * (1.0 / GLA_TAU)
    bc = _seg_cumsum(logdec)
    bl = _seg_last(bc)
    qs = q * (GLA_DK ** -0.5)
    o_heads = [jnp.zeros((rows, GLA_DV), F32) for _ in range(GLA_H)]
    for s in range(T_STEP):
        prod = qs * _down(k, s) * jnp.exp(bc - _down(bc, s))
        v_s = _down(v, s)
        for hh in range(GLA_H):
            a_sh = jnp.sum(prod[:, hh * GLA_DK:(hh + 1) * GLA_DK], axis=-1, keepdims=True)
            a_sh = jnp.where(t_col >= s, a_sh, 0.0)
            o_heads[hh] = o_heads[hh] + a_sh * v_s[:, hh * GLA_DV:(hh + 1) * GLA_DV]
    qe = qs * jnp.exp(bc)
    kd = k * jnp.exp(bl - bc)
    dec = jnp.exp(bl)
    gate = gate_ref[...]
    for hh in range(GLA_H):
        ks = slice(hh * GLA_DK, (hh + 1) * GLA_DK)
        vs = slice(hh * GLA_DV, (hh + 1) * GLA_DV)
        o_state = []
        for p in range(n_pairs):
            r8 = slice(p * SUBLANE, (p + 1) * SUBLANE)
            kd_t = kd[r8, ks].T
            dec_t = dec[r8, ks].T
            res = None
            for j in range(seq_per_tile):
                b = p * seq_per_tile + j
                s_b = s0_ref[b, hh]
                r_j = _dot(qe[r8, ks], s_b)
                res = r_j if res is None else jnp.where(row8 // T_STEP == j, r_j, res)
                v_j = jnp.where(row8 // T_STEP == j, v[r8, vs], 0.0)
                last = j * T_STEP + T_STEP - 1
                st_ref[b, hh] = dec_t[:, last:last + 1] * s_b + _dot(kd_t, v_j)
            o_state.append(res)
        o = o_heads[hh] + jnp.concatenate(o_state, axis=0)
        yg_ref[:, vs] = (_rms(o, ggn_ref[...]) * _silu(gate[:, vs])).astype(BF16)

    mx = mx_ref[...]
    mc = _silu(_conv_rows(mx, em_ref[...], mcw_ref[...], mcb_ref[...]))
    mq = _blockdiag3(mc, wqh_ref, wql_ref)
    mk = _blockdiag3(mc, wkh_ref, wkl_ref) * (ML_DH ** -0.5)
    mv = _blockdiag3(mx, wvh_ref, wvl_ref)
    gts = small_ref[...] + bif_ref[...]
    ig = pltpu.roll(gts, S_FG - S_IG, 1)
    fcum = _seg_cumsum(_log_sigmoid(gts))
    flast = _seg_last(fcum)
    m0 = m0_ref[...]
    t_g = _tmask(gts.shape)
    inter = fcum + m0
    dms = [jnp.where(t_g >= s, fcum - _down(fcum, s) + _down(ig, s), NEG_BIG) for s in range(T_STEP)]
    mt = inter
    for dm in dms:
        mt = jnp.maximum(mt, dm)
    ci_t = jnp.exp(inter - mt)
    emt_t = jnp.exp(-mt)
    w_t = [jnp.exp(dm - mt) for dm in dms]
    dj = flast - fcum + ig
    m_new = jnp.maximum(flast + m0, _seg_allreduce(dj, jnp.maximum))
    cs_t = jnp.exp(flast + m0 - m_new)
    wj_t = jnp.exp(dj - m_new)
    mrow_ref[...] = m_new
    mo = mo_ref[...]
    n0 = n0_ref[...]
    for hh in range(ML_H):
        sl = slice(hh * ML_DH, (hh + 1) * ML_DH)
        lane = S_FG + hh
        qh, kh, vh = mq[:, sl], mk[:, sl], mv[:, sl]
        ci, emt, cs, wj = _col(ci_t, lane), _col(emt_t, lane), _col(cs_t, lane), _col(wj_t, lane)
        num = jnp.zeros((rows, ML_DH), F32)
        den = ci * jnp.sum(qh * n0[:, sl], axis=-1, keepdims=True)
        for s in range(T_STEP):
            sc = jnp.sum(qh * _down(kh, s), axis=-1, keepdims=True) * _col(w_t[s], lane)
            num = num + sc * _down(vh, s)
            den = den + sc
        kw = kh * wj
        nrow_ref[:, sl] = cs * n0[:, sl] + _seg_allreduce(kw, jnp.add)
        qc = []
        for p in range(n_pairs):
            r8 = slice(p * SUBLANE, (p + 1) * SUBLANE)
            kw_t = kw[r8, :].T
            res = None
            for j in range(seq_per_tile):
                b = p * seq_per_tile + j
                c_b = c0_ref[b, hh]
                r_j = _dot(qh[r8, :], c_b)
                res = r_j if res is None else jnp.where(row8 // T_STEP == j, r_j, res)
                v_j = jnp.where(row8 // T_STEP == j, vh[r8, :], 0.0)
                last = p * SUBLANE + j * T_STEP + T_STEP - 1
                ct_ref[b, hh] = cs[last:last + 1, :] * c_b + _dot(kw_t, v_j)
            qc.append(res)
        num = num + ci * jnp.concatenate(qc, axis=0)
        hcell = num / jnp.maximum(jnp.abs(den), emt)
        ym_ref[:, sl] = (jax.nn.sigmoid(mo[:, sl]) * _rms(hcell, mgn_ref[...])).astype(BF16)


def _sample_mixer(proj, small, el, em, h0e, n0e, m0e, s_state, c_state, layer, w, prev):
    n = proj.shape[0]
    nseq = n // T_STEP
    bb = STEP_BB
    rows = bb * T_STEP
    row = lambda width, blk: pl.BlockSpec((rows, width), lambda i: (i, blk))
    lrow = lambda width: pl.BlockSpec((None, rows, width), lambda i: (layer, i, 0))
    lw3 = lambda s: pl.BlockSpec((None,) + s, lambda i: (layer, 0, 0))
    lw4 = lambda s: pl.BlockSpec((None,) + s, lambda i: (layer, 0, 0, 0))
    sspec = pl.BlockSpec((None, bb, GLA_H, GLA_DK, GLA_DV), lambda i: (layer, i, 0, 0, 0))
    cspec = pl.BlockSpec((None, bb, ML_H, ML_DH, ML_DH), lambda i: (layer, i, 0, 0, 0))
    nblk = ML_W // LANE
    kw, vw = GLA_H * GLA_DK, GLA_H * GLA_DV
    in_specs = [
        row(LRU_W, C_LRU // LRU_W), row(kw, C_GQ // kw), row(kw, C_GK // kw), row(vw, C_GV // vw),
        row(vw, C_GGATE // vw), row(ML_W, C_MX // ML_W), row(ML_W, C_MO // ML_W), row(LANE, 0),
        lrow(LRU_W), lrow(ML_W), lrow(LRU_W), lrow(ML_W), lrow(LANE), sspec, cspec,
        lw3((CONV_W, LRU_W)), lw3((1, LRU_W)), lw4((LRU_BLOCKS, LANE, LANE)), lw3((1, LRU_W)),
        lw4((LRU_BLOCKS, LANE, LANE)), lw3((1, LRU_W)), lw3((1, LRU_W)),
        lw3((LANE, kw)), lw3((1, kw)), lw3((1, GLA_DV)),
        lw3((CONV_W, ML_W)), lw3((1, ML_W)), *([lw4((nblk, LANE, LANE))] * 6), lw3((1, LANE)), lw3((1, ML_DH)),
    ]
    args = [proj] * 7 + [small, el, em, h0e, n0e, m0e, s_state, c_state,
                         w["lru_cw"], w["lru_cb"], w["lru_wa"], w["lru_ba"], w["lru_wx"], w["lru_bx"], w["lru_lam"],
                         w["wg2"], w["bg"], w["gla_gn"],
                         w["ml_cw"], w["ml_cb"], *w["ml_wq"], *w["ml_wk"], *w["ml_wv"], w["ml_bif"], w["ml_gn"]]
    aliases = {}
    if prev is not None:
        aliases = {len(in_specs) + j: 3 + j for j in range(len(prev))}
        in_specs += [pl.BlockSpec(memory_space=pl.ANY)] * len(prev)
        args += list(prev)
    sbytes = bb * GLA_H * GLA_DK * GLA_DV * 4
    cbytes = bb * ML_H * ML_DH * ML_DH * 4
    vmem = 4 * (sbytes + cbytes) + 2 * rows * (7 * 1024 + 8 * 1024) * 4 + 80 * rows * 1024 * 4 + (12 << 20)
    return pl.pallas_call(
        _sample_mixer_kernel,
        grid=(nseq // bb,),
        in_specs=in_specs,
        out_specs=[row(LRU_W, 0), row(vw, 0), row(ML_W, 0), lrow(LRU_W), lrow(ML_W), lrow(LANE), sspec, cspec],
        out_shape=[SDS((n, LRU_W), BF16), SDS((n, vw), BF16), SDS((n, ML_W), BF16),
                   SDS((DEPTH, n, LRU_W), F32), SDS((DEPTH, n, ML_W), F32), SDS((DEPTH, n, LANE), F32),
                   SDS((DEPTH, nseq, GLA_H, GLA_DK, GLA_DV), F32), SDS((DEPTH, nseq, ML_H, ML_DH, ML_DH), F32)],
        input_output_aliases=aliases,
        compiler_params=_params(("arbitrary",), vmem),
        name="sample_mixer",
    )(*args)


def _prep_weights(p):
    w_in = p["w_in"]
    c1 = C_GGATE
    c2 = C_MG + W_SHIFT1
    w_small = jnp.concatenate(
        [w_in[:, :, c1:c1 + GLA_RANK], w_in[:, :, c2:c2 + 2 * ML_H],
         jnp.zeros((DEPTH, D, LANE - GLA_RANK - 2 * ML_H), F32)], axis=-1).astype(BF16)
    wg2 = jnp.concatenate(
        [p["gla_w_g2"], jnp.zeros((DEPTH, LANE - GLA_RANK, GLA_H * GLA_DK), F32)], axis=1)
    eye = jnp.eye(LANE // ML_BS, dtype=F32)

    def dense_bd(wb):
        wb = wb.reshape(DEPTH, ML_W // LANE, LANE // ML_BS, ML_BS, ML_BS)
        dense = jnp.einsum("dgncx,nm->dgncmx", wb, eye).reshape(DEPTH, ML_W // LANE, LANE, LANE)
        hi = dense.astype(BF16)
        return hi, (dense - hi.astype(F32)).astype(BF16)

    bif = jnp.zeros((DEPTH, 1, LANE), F32).at[:, 0, S_IG:S_IG + 2 * ML_H].set(p["ml_b_if"])
    r3 = lambda a: a.reshape(DEPTH, 1, a.shape[-1])
    return dict(
        w_small=w_small, wg2=wg2, bg=r3(p["gla_b_g"]), gla_gn=r3(p["gla_g_norm"]),
        g1=r3(p["g_norm1"]), g2=r3(p["g_norm2"]), gf=p["g_final"].reshape(1, D),
        lru_cw=p["lru_conv_w"], lru_cb=r3(p["lru_conv_b"]), lru_wa=p["lru_w_a"], lru_ba=r3(p["lru_b_a"]),
        lru_wx=p["lru_w_x"], lru_bx=r3(p["lru_b_x"]), lru_lam=r3(p["lru_lam"]),
        ml_cw=p["ml_conv_w"], ml_cb=r3(p["ml_conv_b"]), ml_wq=dense_bd(p["ml_w_q"]), ml_wk=dense_bd(p["ml_w_k"]),
        ml_wv=dense_bd(p["ml_w_v"]), ml_bif=bif, ml_gn=r3(p["ml_g_norm"]),
        w_in=w_in, w_br=(p["w_br_lru"], p["w_br_gla"], p["w_br_ml"]), w_out=p["w_out"], w_ff1=p["w_ff1"],
        w_ff2=p["w_ff2"],
    )


def _trunk(xp3, xs3, mod_p, mod_s, states, w):
    bp, tp, _ = xp3.shape
    bs, ts, _ = xs3.shape
    n_p, n_s = bp * tp, bs * ts
    tm_p = min(tp, 1024)
    tm_f = min(tp, 512)
    s_h, s_cv, s_gla, s_c, s_n, s_m, s_mcv = states
    pad_t = lambda a: jnp.pad(a, ((0, 0), (0, 0), (0, ts - a.shape[2]), (0, 0))).reshape(DEPTH, n_s, a.shape[-1])
    el = pad_t(s_cv)
    em = pad_t(s_mcv)
    h0e = pad_t(s_h[:, :, None, :])
    n0e = jnp.broadcast_to(s_n.reshape(DEPTH, bs, 1, ML_W), (DEPTH, bs, ts, ML_W)).reshape(DEPTH, n_s, ML_W)
    m_l = jnp.pad(s_m, ((0, 0), (0, 0), (S_FG, LANE - S_FG - ML_H)))
    m0e = jnp.broadcast_to(m_l[:, :, None, :], (DEPTH, bs, ts, LANE)).reshape(DEPTH, n_s, LANE)
    x_p = xp3.reshape(n_p, D)
    x_s = xs3.reshape(n_s, D)
    prev_p = prev_s = None
    conv_l, conv_m = [], []
    keep = ts - (CONV_W - 1)
    for l in range(DEPTH):
        last = l == DEPTH - 1
        proj_s, small_s, w_main = _in_proj_cast(x_s, mod_s, l, w["g1"], w["w_in"], w["w_small"], tn=512, rpg=n_s)
        proj_p, small_p = _in_proj(x_p, mod_p, l, w["g1"], w_main, w["w_small"], tm=tm_p, tn=1024, rpg=tp, r=1)
        outs_s = _sample_mixer(proj_s, small_s, el, em, h0e, n0e, m0e, s_gla, s_c, l, w, prev_s)
        prev_s = outs_s[3:]
        outs_p = _prompt_mixer(proj_p.reshape(bp, tp, N_MAIN), small_p.reshape(bp, tp, LANE), l, w, prev_p,
                               L=min(tp, CHUNK), rows=min(tp, PROMPT_ROWS))
        prev_p = outs_p[3:]
        conv_l.append(proj_s[:, C_LRU:C_LRU + LRU_W].reshape(bs, ts, LRU_W)[:, keep:])
        conv_m.append(proj_s[:, C_MX:C_MX + ML_W].reshape(bs, ts, ML_W)[:, keep:])
        merged_s, *w_br = _merge(*outs_s[:3], proj_s, l, *w["w_br"], tm=n_s, tn=512, cast=True)
        (merged_p,) = _merge(*(y.reshape(n_p, y.shape[-1]) for y in outs_p[:3]), proj_p, l, *w_br, tm=tm_p, tn=512,
                             cast=False)
        x_s, w_out = _out_proj(merged_s, x_s, mod_s, l, w["w_out"], tm=n_s, tn=512, rpg=n_s, r=n_s, cast=True)
        (x_p,) = _out_proj(merged_p, x_p, mod_p, l, w_out, tm=tm_p, tn=512, rpg=tp, r=1, cast=False)
        x_s, w_ff1, w_ff2 = _ffn(x_s, mod_s, l, w["g2"], w["gf"], w["w_ff1"], w["w_ff2"], tm=n_s, tf=256, rpg=n_s,
                                 r=n_s, final_norm=last, cast=True)
        (x_p,) = _ffn(x_p, mod_p, l, w["g2"], w["gf"], w_ff1, w_ff2, tm=tm_f, tf=1024, rpg=tp, r=1, final_norm=last,
                      cast=False)
    h_t, lcv_t, s_t, c_t, n_t, m_t, mcv_t = prev_p
    p_states = [h_t.reshape(DEPTH, bp, LRU_W), lcv_t, s_t, c_t, n_t, m_t[..., 0], mcv_t]
    hrow, nrow, mrow, s_out, c_out = prev_s
    t_last = ts - 1
    s_states = [hrow.reshape(DEPTH, bs, ts, LRU_W)[:, :, t_last], jnp.stack(conv_l), s_out, c_out,
                nrow.reshape(DEPTH, bs, ts, ML_H, ML_DH)[:, :, t_last],
                mrow.reshape(DEPTH, bs, ts, LANE)[:, :, t_last, S_FG:S_FG + ML_H], jnp.stack(conv_m)]
    return x_p.reshape(bp, tp, D), x_s.reshape(bs, ts, D), p_states, s_states


def kernel(x_prompt, x_sample, c_prompt, c_sample, state_lru_h, state_lru_conv, state_gla, state_mlstm_C, state_mlstm_n, state_mlstm_m, state_mlstm_conv, w_ada, b_ada, g_norm1, g_norm2, w_in, lru_conv_w, lru_conv_b, lru_w_a, lru_b_a, lru_w_x, lru_b_x, lru_lam, gla_w_g2, gla_b_g, gla_g_norm, ml_conv_w, ml_conv_b, ml_w_q, ml_w_k, ml_w_v, ml_b_if, ml_g_norm, w_br_lru, w_br_gla, w_br_ml, w_out, w_ff1, w_ff2, g_final):
    p = dict(g_norm1=g_norm1, g_norm2=g_norm2, w_in=w_in, lru_conv_w=lru_conv_w, lru_conv_b=lru_conv_b,
             lru_w_a=lru_w_a, lru_b_a=lru_b_a, lru_w_x=lru_w_x, lru_b_x=lru_b_x, lru_lam=lru_lam,
             gla_w_g2=gla_w_g2, gla_b_g=gla_b_g, gla_g_norm=gla_g_norm, ml_conv_w=ml_conv_w, ml_conv_b=ml_conv_b,
             ml_w_q=ml_w_q, ml_w_k=ml_w_k, ml_w_v=ml_w_v, ml_b_if=ml_b_if, ml_g_norm=ml_g_norm,
             w_br_lru=w_br_lru, w_br_gla=w_br_gla, w_br_ml=w_br_ml, w_out=w_out, w_ff1=w_ff1, w_ff2=w_ff2,
             g_final=g_final)
    w = _prep_weights(p)
    bp = x_prompt.shape[0]
    bs, ts, _ = x_sample.shape
    assert ts == T_STEP and bs % STEP_BB == 0
    ns = bs * ts
    pad = (-(ns + bp)) % SUBLANE
    c_all = jnp.concatenate([jnp.repeat(c_sample, ts, axis=0), c_prompt, jnp.zeros((pad, D), F32)], axis=0)
    mod_all = _ada(c_all, w_ada, b_ada)
    mod_s = mod_all.reshape(DEPTH, 1, ns + bp + pad, N_MOD * D)
    mod_p = mod_all[:, ns:ns + bp].reshape(DEPTH, bp, 1, N_MOD * D)
    y_p, y_s, ps, ss = _trunk(x_prompt, x_sample, mod_p, mod_s,
                              (state_lru_h, state_lru_conv, state_gla, state_mlstm_C, state_mlstm_n, state_mlstm_m,
                               state_mlstm_conv), w)
    return (y_p, y_s, *ps, *ss)
```

```python
import functools

import jax
import jax.numpy as jnp
from jax import lax
from jax.experimental import pallas as pl
from jax.experimental.pallas import tpu as pltpu

F32, BF16 = jnp.float32, jnp.bfloat16
SDS = jax.ShapeDtypeStruct

D = 2048
DEPTH = 2
LRU_W = 1024
LRU_BLOCKS = 8
LRU_C = 8.0
CONV_W = 4
GLA_H = 4
GLA_DK = 128
GLA_DV = 256
GLA_RANK = 16
GLA_TAU = 16.0
ML_H = 4
ML_W = 1024
ML_DH = 256
ML_BS = 4
CHUNK = 64
PROMPT_ROWS = 256
D_FF = 4 * D
EPS = 1e-6
N_MOD = 6
T_STEP = 4
STEP_BB = 4

LANE = 128
SUBLANE = 8
VMEM_LIMIT_CAP = 56 * 1024 * 1024

C_LRU, C_GQ, C_GK, C_GV, C_GGATE, C_MX, C_MO, C_MG = 0, 1024, 1536, 2048, 3072, 4096, 5120, 6144
N_MAIN = 12288
W_SHIFT1 = GLA_RANK
W_SHIFT2 = GLA_RANK + 2 * ML_H
S_GLR = 0
S_IG = 16
S_FG = 20
NEG_BIG = -1e30


def _params(sem, vmem_bytes):
    return pltpu.CompilerParams(dimension_semantics=sem, vmem_limit_bytes=int(min(vmem_bytes, VMEM_LIMIT_CAP)))


def _dot(a, b):
    return jnp.dot(a.astype(BF16), b.astype(BF16), preferred_element_type=F32)


def _dot_nt(a, b):
    return lax.dot_general(a.astype(BF16), b.astype(BF16), (((1,), (1,)), ((), ())), preferred_element_type=F32)


def _split3(x):
    p0 = x.astype(BF16)
    r = x - p0.astype(F32)
    p1 = r.astype(BF16)
    return p0, p1, (r - p1.astype(F32)).astype(BF16)


def _cumsum_rows(tri_b, x):
    return sum(jnp.dot(tri_b, p, preferred_element_type=F32) for p in _split3(x))


def _cumsum_lanes(x, triu_b):
    return sum(jnp.dot(p, triu_b, preferred_element_type=F32) for p in _split3(x))


def _split(x):
    hi = x.astype(BF16)
    return hi, (x - hi.astype(F32)).astype(BF16)


def _dot3(x_hi, x_lo, w_hi, w_lo):
    d = lambda a, b: jnp.dot(a, b, preferred_element_type=F32)
    return d(x_hi, w_hi) + (d(x_lo, w_hi) + d(x_hi, w_lo))


def _log_sigmoid(z):
    return jnp.minimum(z, 0.0) - jnp.log1p(jnp.exp(-jnp.abs(z)))


def _silu(z):
    return z * jax.nn.sigmoid(z)


def _rms(x, g):
    return x * lax.rsqrt(jnp.mean(x * x, axis=-1, keepdims=True) + EPS) * g


def _tri(n):
    r = lax.broadcasted_iota(jnp.int32, (n, n), 0)
    c = lax.broadcasted_iota(jnp.int32, (n, n), 1)
    return r >= c


def _lru_gates(ub, wa, wx, ba, bx, lam):
    r = jax.nn.sigmoid(_dot(ub, wa) + ba)
    i = jax.nn.sigmoid(_dot(ub, wx) + bx)
    log_a = LRU_C * r * _log_sigmoid(lam)
    t = jnp.tanh(log_a)
    return jnp.exp(log_a), jnp.sqrt(-2.0 * t / (1.0 - t)) * (i * ub)


def _blockdiag3(x, wh_ref, wl_ref):
    xh, xl = _split(x)
    outs = []
    for blk in range(ML_W // LANE):
        sl = slice(blk * LANE, (blk + 1) * LANE)
        outs.append(_dot3(xh[:, sl], xl[:, sl], wh_ref[blk], wl_ref[blk]))
    return jnp.concatenate(outs, axis=1)


def _ada_kernel(c_ref, w_ref, b_ref, o_ref):
    o_ref[...] = _dot(_silu(c_ref[...]), w_ref[...]) + b_ref[...]


def _ada(c_all, w_ada, b_ada):
    m = c_all.shape[0]
    tn = 512
    n_out = N_MOD * D
    return pl.pallas_call(
        _ada_kernel,
        grid=(DEPTH, n_out // tn),
        in_specs=[
            pl.BlockSpec((m, D), lambda l, n: (0, 0)),
            pl.BlockSpec((None, D, tn), lambda l, n: (l, 0, n)),
            pl.BlockSpec((None, 1, tn), lambda l, n: (l, 0, n)),
        ],
        out_specs=pl.BlockSpec((None, m, tn), lambda l, n: (l, 0, n)),
        out_shape=SDS((DEPTH, m, n_out), F32),
        compiler_params=_params(("parallel", "arbitrary"), 2 * (m * D + D * tn + m * tn) * 4 + (12 << 20)),
        name="ada",
    )(c_all, w_ada, b_ada.reshape(DEPTH, 1, n_out))


def _modspec(layer, comp, r, width, tm, rpg, jmap=None):
    nb = D // width
    jm = (lambda j: 0) if jmap is None else jmap
    if r == 1:
        return pl.BlockSpec((None, None, 1, width), lambda i, j: (layer, (i * tm) // rpg, 0, comp * nb + jm(j)))
    mode = pl.Buffered(1) if jmap is None else None
    return pl.BlockSpec((None, None, tm, width), lambda i, j: (layer, 0, i, comp * nb + jm(j)), pipeline_mode=mode)


def _norm_mod_store(xn_s, x_ref, g_ref, sc_ref, sh_ref):
    tm = x_ref.shape[0]
    rc = min(tm, 256)
    for c in range(tm // rc):
        rows = slice(c * rc, (c + 1) * rc)
        mrows = rows if sc_ref.shape[0] == tm else slice(None)
        xn = _rms(x_ref[rows, :], g_ref[...]) * (1.0 + sc_ref[mrows, :]) + sh_ref[mrows, :]
        xn_s[rows, :] = xn.astype(BF16)


def _in_proj_kernel(x_ref, sc_ref, sh_ref, g_ref, wm_ref, ws_ref, om_ref, os_ref, xn_s):
    @pl.when(pl.program_id(1) == 0)
    def _():
        _norm_mod_store(xn_s, x_ref, g_ref, sc_ref, sh_ref)
        os_ref[...] = jnp.dot(xn_s[...], ws_ref[...].astype(BF16), preferred_element_type=F32)

    om_ref[...] = jnp.dot(xn_s[...], wm_ref[...], preferred_element_type=F32)


def _in_proj(x, mod, layer, g1, w_main, w_small, *, tm, tn, rpg, r):
    n = x.shape[0]
    vmem = 2 * (tm * D * 4 + 2 * r * D * 4 + D * tn * 2 + D * LANE * 2 + tm * tn * 4 + tm * LANE * 4) + tm * D * 2
    vmem += 8 << 20
    return pl.pallas_call(
        _in_proj_kernel,
        grid=(n // tm, N_MAIN // tn),
        in_specs=[
            pl.BlockSpec((tm, D), lambda i, j: (i, 0)),
            _modspec(layer, 1, r, D, tm, rpg),
            _modspec(layer, 0, r, D, tm, rpg),
            pl.BlockSpec((None, 1, D), lambda i, j: (layer, 0, 0)),
            pl.BlockSpec((D, tn), lambda i, j: (0, j)),
            pl.BlockSpec((None, D, LANE), lambda i, j: (layer, 0, 0)),
        ],
        out_specs=[pl.BlockSpec((tm, tn), lambda i, j: (i, j)), pl.BlockSpec((tm, LANE), lambda i, j: (i, 0))],
        out_shape=[SDS((n, N_MAIN), F32), SDS((n, LANE), F32)],
        scratch_shapes=[pltpu.VMEM((tm, D), BF16)],
        compiler_params=_params(("parallel", "arbitrary"), vmem),
        name="in_proj",
    )(x, mod, mod, g1, w_main, w_small)


def _in_proj_cast_kernel(x_ref, sc_ref, sh_ref, g_ref, wt_ref, ws_ref, om_ref, os_ref, wc_ref, xn_s):
    @pl.when(pl.program_id(1) == 0)
    def _():
        _norm_mod_store(xn_s, x_ref, g_ref, sc_ref, sh_ref)
        os_ref[...] = jnp.dot(xn_s[...], ws_ref[...].astype(BF16), preferred_element_type=F32)

    wb = wt_ref[...].T.astype(BF16)
    wc_ref[...] = wb
    om_ref[...] = jnp.dot(xn_s[...], wb, preferred_element_type=F32)


def _in_proj_cast(x, mod, layer, g1, w_in_t, w_small, *, tn, rpg):
    n = x.shape[0]
    tm = n
    nb1, nb2 = C_GGATE // tn, C_MG // tn

    def w_rows(i, j):
        shift = jnp.where(j >= nb2, W_SHIFT2 // SUBLANE, jnp.where(j >= nb1, W_SHIFT1 // SUBLANE, 0))
        return (layer, (j * (tn // SUBLANE) + shift) * SUBLANE, 0)

    vmem = tm * D * 4 + 2 * tm * D * 4 + 2 * (tn * D * 4 + D * LANE * 4 + tm * tn * 4 + tm * LANE * 4 + D * tn * 2)
    vmem += tm * D * 2 + 3 * D * tn * 4 + (4 << 20)
    return pl.pallas_call(
        _in_proj_cast_kernel,
        grid=(1, N_MAIN // tn),
        in_specs=[
            pl.BlockSpec((tm, D), lambda i, j: (i, 0), pipeline_mode=pl.Buffered(1)),
            _modspec(layer, 1, tm, D, tm, rpg),
            _modspec(layer, 0, tm, D, tm, rpg),
            pl.BlockSpec((None, 1, D), lambda i, j: (layer, 0, 0)),
            pl.BlockSpec((None, pl.Element(tn), pl.Element(D)), w_rows),
            pl.BlockSpec((None, D, LANE), lambda i, j: (layer, 0, 0)),
        ],
        out_specs=[pl.BlockSpec((tm, tn), lambda i, j: (i, j)), pl.BlockSpec((tm, LANE), lambda i, j: (i, 0)),
                   pl.BlockSpec((D, tn), lambda i, j: (0, j))],
        out_shape=[SDS((n, N_MAIN), F32), SDS((n, LANE), F32), SDS((D, N_MAIN), BF16)],
        scratch_shapes=[pltpu.VMEM((tm, D), BF16)],
        compiler_params=_params(("arbitrary", "arbitrary"), vmem),
        name="in_proj_cast",
    )(x, mod, mod, g1, w_in_t, w_small)


def _wspec(shape, imap, layer, cast):
    if cast:
        return pl.BlockSpec((None,) + shape, lambda i, j: (layer,) + imap(i, j))
    return pl.BlockSpec(shape, imap)


def _merge_kernel(yl_ref, yg_ref, ym_ref, g0_ref, g1_ref, g2_ref, w0_ref, w1_ref, w2_ref, o_ref, *wc_refs):
    ws = [w_ref[...].astype(BF16) for w_ref in (w0_ref, w1_ref, w2_ref)]
    for wc_ref, wb in zip(wc_refs, ws):
        wc_ref[...] = wb
    acc = jax.nn.sigmoid(g0_ref[...]) * jnp.dot(yl_ref[...], ws[0], preferred_element_type=F32)
    acc += jax.nn.sigmoid(g1_ref[...]) * jnp.dot(yg_ref[...], ws[1], preferred_element_type=F32)
    acc += jax.nn.sigmoid(g2_ref[...]) * jnp.dot(ym_ref[...], ws[2], preferred_element_type=F32)
    o_ref[...] = acc.astype(BF16)


def _merge(y_lru, y_gla, y_ml, proj, layer, w_lru, w_gla, w_ml, *, tm, tn, cast):
    n = y_lru.shape[0]
    w = LRU_W
    gb = C_MG // tn
    nb = D // tn
    yspec = pl.BlockSpec((tm, w), lambda i, j: (i, 0))
    wspec = _wspec((w, tn), lambda i, j: (0, j), layer, cast)
    wbytes = 4 if cast else 2
    vmem = 2 * (3 * tm * w * 2 + 3 * tm * tn * 4 + 3 * w * tn * wbytes + tm * tn * 2) + 6 * tm * tn * 4 + (4 << 20)
    out_specs = [pl.BlockSpec((tm, tn), lambda i, j: (i, j))]
    out_shape = [SDS((n, D), BF16)]
    if cast:
        out_specs += [pl.BlockSpec((w, tn), lambda i, j: (0, j))] * 3
        out_shape += [SDS((w, D), BF16)] * 3
        vmem += 2 * 3 * w * tn * 2 + 3 * w * tn * 4
    return pl.pallas_call(
        _merge_kernel,
        grid=(n // tm, nb),
        in_specs=[
            yspec, yspec, yspec,
            pl.BlockSpec((tm, tn), lambda i, j: (i, gb + j)),
            pl.BlockSpec((tm, tn), lambda i, j: (i, gb + nb + j)),
            pl.BlockSpec((tm, tn), lambda i, j: (i, gb + 2 * nb + j)),
            wspec, wspec, wspec,
        ],
        out_specs=out_specs,
        out_shape=out_shape,
        compiler_params=_params(("arbitrary" if cast else "parallel", "arbitrary"), vmem),
        name="merge",
    )(y_lru, y_gla, y_ml, proj, proj, proj, w_lru, w_gla, w_ml)


def _out_proj_kernel(m_ref, w_ref, x_ref, gt_ref, o_ref, *wc_refs):
    wb = w_ref[...].astype(BF16)
    for wc_ref in wc_refs:
        wc_ref[...] = wb
    o_ref[...] = x_ref[...] + gt_ref[...] * jnp.dot(m_ref[...], wb, preferred_element_type=F32)


def _out_proj(merged, x, mod, layer, w_out, *, tm, tn, rpg, r, cast):
    n = x.shape[0]
    wbytes = 4 if cast else 2
    vmem = 2 * (tm * D * 2 + D * tn * wbytes + 2 * tm * tn * 4 + r * tn * 4) + 2 * tm * tn * 4 + (4 << 20)
    out_specs = [pl.BlockSpec((tm, tn), lambda i, j: (i, j))]
    out_shape = [SDS((n, D), F32)]
    if cast:
        out_specs.append(pl.BlockSpec((D, tn), lambda i, j: (0, j)))
        out_shape.append(SDS((D, D), BF16))
        vmem += 2 * D * tn * 2 + D * tn * 4
    return pl.pallas_call(
        _out_proj_kernel,
        grid=(n // tm, D // tn),
        in_specs=[
            pl.BlockSpec((tm, D), lambda i, j: (i, 0)),
            _wspec((D, tn), lambda i, j: (0, j), layer, cast),
            pl.BlockSpec((tm, tn), lambda i, j: (i, j)),
            _modspec(layer, 2, r, tn, tm, rpg, jmap=lambda j: j),
        ],
        out_specs=out_specs,
        out_shape=out_shape,
        compiler_params=_params(("arbitrary" if cast else "parallel", "arbitrary"), vmem),
        name="out_proj",
    )(merged, w_out, x, mod)


def _ffn_kernel(x_ref, sc_ref, sh_ref, gt_ref, g_ref, gf_ref, w1_ref, w2_ref, o_ref, *rest, final_norm):
    xn_s = rest[-1]
    f = pl.program_id(1)

    @pl.when(f == 0)
    def _():
        _norm_mod_store(xn_s, x_ref, g_ref, sc_ref, sh_ref)
        o_ref[...] = jnp.zeros_like(o_ref)

    w1 = w1_ref[...].astype(BF16)
    w2 = w2_ref[...].astype(BF16)
    for wc_ref, wb in zip(rest[:-1], (w1, w2)):
        wc_ref[...] = wb
    h = jnp.square(jnp.maximum(jnp.dot(xn_s[...], w1, preferred_element_type=F32), 0.0))
    o_ref[...] += jnp.dot(h.astype(BF16), w2, preferred_element_type=F32)

    @pl.when(f == pl.num_programs(1) - 1)
    def _():
        y = x_ref[...] + gt_ref[...] * o_ref[...]
        if final_norm:
            y = _rms(y, gf_ref[...])
        o_ref[...] = y


def _ffn(x, mod, layer, g2, g_final, w1, w2, *, tm, tf, rpg, r, final_norm, cast):
    n = x.shape[0]
    wbytes = 4 if cast else 2
    vmem = 2 * (2 * tm * D * 4 + 3 * r * D * 4 + 2 * D * tf * wbytes) + tm * D * 2 + 2 * tm * tf * 4 + 2 * tm * D * 4
    single = pl.Buffered(1) if n == tm else None
    out_specs = [pl.BlockSpec((tm, D), lambda i, f: (i, 0), pipeline_mode=single)]
    out_shape = [SDS((n, D), F32)]
    if cast:
        out_specs += [pl.BlockSpec((D, tf), lambda i, f: (0, f)), pl.BlockSpec((tf, D), lambda i, f: (f, 0))]
        out_shape += [SDS((D, D_FF), BF16), SDS((D_FF, D), BF16)]
        vmem += 2 * 2 * D * tf * 2
    return pl.pallas_call(
        functools.partial(_ffn_kernel, final_norm=final_norm),
        grid=(n // tm, D_FF // tf),
        in_specs=[
            pl.BlockSpec((tm, D), lambda i, f: (i, 0), pipeline_mode=single),
            _modspec(layer, 4, r, D, tm, rpg),
            _modspec(layer, 3, r, D, tm, rpg),
            _modspec(layer, 5, r, D, tm, rpg),
            pl.BlockSpec((None, 1, D), lambda i, f: (layer, 0, 0)),
            pl.BlockSpec((1, D), lambda i, f: (0, 0)),
            _wspec((D, tf), lambda i, f: (0, f), layer, cast),
            _wspec((tf, D), lambda i, f: (f, 0), layer, cast),
        ],
        out_specs=out_specs,
        out_shape=out_shape,
        scratch_shapes=[pltpu.VMEM((tm, D), BF16)],
        compiler_params=_params(("arbitrary" if cast else "parallel", "arbitrary"), vmem),
        name="ffn",
    )(x, mod, mod, mod, g2, g_final, w1, w2)


def _conv_chunk(xp_s, x, w, b, L):
    xp_s[SUBLANE:SUBLANE + L, :] = x
    acc = b + xp_s[5:5 + L, :] * w[0:1]
    for j in range(1, CONV_W):
        acc = acc + xp_s[5 + j:5 + j + L, :] * w[j:j + 1]
    tail = xp_s[5 + L:8 + L, :]
    xp_s[5:8, :] = tail
    return acc, tail


def _prompt_mixer_kernel(lx_ref, q_ref, k_ref, v_ref, gate_ref, mx_ref, mo_ref, small_ref,
                         lcw_ref, lcb_ref, wa_ref, ba_ref, wx_ref, bx_ref, lam_ref,
                         wg2_ref, bg_ref, ggn_ref,
                         mcw_ref, mcb_ref, wqh_ref, wql_ref, wkh_ref, wkl_ref, wvh_ref, wvl_ref, bif_ref, mgn_ref,
                         *rest, L):
    (yl_ref, yg_ref, ym_ref, ht_ref, lct_ref, st_ref, ct_ref, nt_ref, mt_ref, mct_ref,
     xpl_s, xpm_s, a_s, u_s, h_s, s_s, c_s, n_s, m_s) = rest[-19:]

    @pl.when(pl.program_id(1) == 0)
    def _():
        xpl_s[0:SUBLANE, :] = jnp.zeros((SUBLANE, LRU_W), F32)
        xpm_s[0:SUBLANE, :] = jnp.zeros((SUBLANE, ML_W), F32)
        h_s[...] = jnp.zeros_like(h_s)
        s_s[...] = jnp.zeros_like(s_s)
        c_s[...] = jnp.zeros_like(c_s)
        n_s[...] = jnp.zeros_like(n_s)
        m_s[...] = jnp.zeros_like(m_s)

    rows = lx_ref.shape[0]

    u, l_tail = _conv_chunk(xpl_s, lx_ref[...], lcw_ref[...], lcb_ref[...], rows)
    for blk in range(LRU_BLOCKS):
        sl = slice(blk * LANE, (blk + 1) * LANE)
        a_s[:, sl], u_s[:, sl] = _lru_gates(u[:, sl], wa_ref[blk], wx_ref[blk], ba_ref[:, sl], bx_ref[:, sl],
                                            lam_ref[:, sl])
    t8 = lax.broadcasted_iota(jnp.int32, (SUBLANE, LRU_W), 0)
    h = h_s[...]
    for g in range(rows // SUBLANE):
        r8 = slice(g * SUBLANE, (g + 1) * SUBLANE)
        a8, u8 = a_s[r8, :], u_s[r8, :]
        for s in (1, 2, 4):
            u8 = jnp.where(t8 >= s, u8 + a8 * pltpu.roll(u8, s, 0), u8)
            a8 = jnp.where(t8 >= s, a8 * pltpu.roll(a8, s, 0), a8)
        h8 = u8 + a8 * h
        a_s[r8, :] = h8
        h = h8[SUBLANE - 1:SUBLANE, :]
    h_s[...] = h
    yl_ref[...] = a_s[...].astype(BF16)

    q, k, v, gate = q_ref[...], k_ref[...], v_ref[...], gate_ref[...]
    small = small_ref[...]
    logdec = _log_sigmoid(_dot(small, wg2_ref[...]) + bg_ref[...]) * (1.0 / GLA_TAU)
    mx = mx_ref[...]
    conv, m_tail = _conv_chunk(xpm_s, mx, mcw_ref[...], mcb_ref[...], rows)
    mc = _silu(conv)
    mq = _blockdiag3(mc, wqh_ref, wql_ref)
    mk = _blockdiag3(mc, wkh_ref, wkl_ref) * (ML_DH ** -0.5)
    mv = _blockdiag3(mx, wvh_ref, wvl_ref)
    gates = small + bif_ref[...]
    gates_t = gates.T
    lf = _log_sigmoid(gates)
    lf_t = _log_sigmoid(gates_t)
    mo = mo_ref[...]
    tri = _tri(L)
    tri_b = tri.astype(BF16)
    r_i = lax.broadcasted_iota(jnp.int32, (L, L), 0)
    c_i = lax.broadcasted_iota(jnp.int32, (L, L), 1)
    triu_b = (r_i <= c_i).astype(BF16)

    for cc in range(rows // L):
        rs = slice(cc * L, (cc + 1) * L)
        bcum = _cumsum_rows(tri_b, logdec[rs, :])
        for hh in range(GLA_H):
            ks = slice(hh * GLA_DK, (hh + 1) * GLA_DK)
            vs = slice(hh * GLA_DV, (hh + 1) * GLA_DV)
            bh = bcum[:, ks]
            k_h, v_h = k[rs, ks], v[rs, vs]
            qe = (q[rs, ks] * (GLA_DK ** -0.5)) * jnp.exp(bh)
            ke = k_h * jnp.exp(-bh)
            a = jnp.where(tri, _dot_nt(qe, ke), 0.0)
            s_h = s_s[hh]
            o = _dot(qe, s_h) + _dot(a, v_h)
            bl = bh[L - 1:L, :]
            kd = k_h * jnp.exp(bl - bh)
            dec_col = jnp.exp(jnp.broadcast_to(bl, (SUBLANE, GLA_DK))).T[:, 0:1]
            s_s[hh] = dec_col * s_h + _dot(kd.T, v_h)
            yg_ref[rs, vs] = (_rms(o, ggn_ref[...]) * _silu(gate[rs, vs])).astype(BF16)

        f_col = _cumsum_rows(tri_b, lf[rs, :])
        f_row = _cumsum_lanes(lf_t[:, rs], triu_b)
        for hh in range(ML_H):
            sl = slice(hh * ML_DH, (hh + 1) * ML_DH)
            qh, kh, vh = mq[rs, sl], mk[rs, sl], mv[rs, sl]
            fc = f_col[:, S_FG + hh:S_FG + hh + 1]
            fr = f_row[S_FG + hh:S_FG + hh + 1, :]
            igr = gates_t[S_IG + hh:S_IG + hh + 1, rs]
            m_h = m_s[hh:hh + 1, 0:1]
            dm = jnp.where(tri, fc - fr + igr, NEG_BIG)
            inter = fc + m_h
            mt = jnp.maximum(inter, jnp.max(dm, axis=-1, keepdims=True))
            ci = jnp.exp(inter - mt)
            s = _dot_nt(qh, kh) * jnp.exp(dm - mt)
            c_h = c_s[hh]
            n_h = n_s[hh:hh + 1, :]
            num = ci * _dot(qh, c_h) + _dot(s, vh)
            den = ci * jnp.sum(qh * n_h, axis=-1, keepdims=True) + jnp.sum(s, axis=-1, keepdims=True)
            hcell = num / jnp.maximum(jnp.abs(den), jnp.exp(-mt))
            fl = fr[:, L - 1:L]
            dj = fl - fr + igr
            m_new = jnp.maximum(fl + m_h, jnp.max(dj, axis=-1, keepdims=True))
            cs = jnp.exp(fl + m_h - m_new)
            wj = jnp.exp(dj - m_new)
            c_s[hh] = cs * c_h + _dot(kh.T * wj, vh)
            n_s[hh:hh + 1, :] = cs * n_h + _dot(jnp.broadcast_to(wj, (SUBLANE, L)), kh)[0:1, :]
            m_s[hh:hh + 1, :] = jnp.broadcast_to(m_new, (1, LANE))
            ym_ref[rs, sl] = (jax.nn.sigmoid(mo[rs, sl]) * _rms(hcell, mgn_ref[...])).astype(BF16)

    @pl.when(pl.program_id(1) == pl.num_programs(1) - 1)
    def _():
        ht_ref[...] = h
        lct_ref[...] = l_tail
        st_ref[...] = s_s[...]
        ct_ref[...] = c_s[...]
        nt_ref[...] = n_s[...]
        mt_ref[...] = m_s[...]
        mct_ref[...] = m_tail


def _prompt_mixer(proj3, small3, layer, w, prev, *, L, rows):
    b, t, _ = proj3.shape
    kw, vw = GLA_H * GLA_DK, GLA_H * GLA_DV
    nblk = ML_W // LANE
    blk3 = lambda width, cb: pl.BlockSpec((None, rows, width), lambda bi, c: (bi, c, cb))
    lw3 = lambda s: pl.BlockSpec((None,) + s, lambda bi, c: (layer, 0, 0))
    lw4 = lambda s: pl.BlockSpec((None,) + s, lambda bi, c: (layer, 0, 0, 0))
    st3 = lambda s: pl.BlockSpec((None, None) + s, lambda bi, c: (layer, bi, 0, 0))
    st4 = lambda s: pl.BlockSpec((None, None) + s, lambda bi, c: (layer, bi, 0, 0, 0))
    in_specs = [
        blk3(LRU_W, C_LRU // LRU_W), blk3(kw, C_GQ // kw), blk3(kw, C_GK // kw), blk3(vw, C_GV // vw),
        blk3(vw, C_GGATE // vw), blk3(ML_W, C_MX // ML_W), blk3(ML_W, C_MO // ML_W), blk3(LANE, 0),
        lw3((CONV_W, LRU_W)), lw3((1, LRU_W)), lw4((LRU_BLOCKS, LANE, LANE)), lw3((1, LRU_W)),
        lw4((LRU_BLOCKS, LANE, LANE)), lw3((1, LRU_W)), lw3((1, LRU_W)),
        lw3((LANE, kw)), lw3((1, kw)), lw3((1, GLA_DV)),
        lw3((CONV_W, ML_W)), lw3((1, ML_W)), *([lw4((nblk, LANE, LANE))] * 6), lw3((1, LANE)), lw3((1, ML_DH)),
    ]
    args = [proj3] * 7 + [small3,
                          w["lru_cw"], w["lru_cb"], w["lru_wa"], w["lru_ba"], w["lru_wx"], w["lru_bx"], w["lru_lam"],
                          w["wg2"], w["bg"], w["gla_gn"],
                          w["ml_cw"], w["ml_cb"], *w["ml_wq"], *w["ml_wk"], *w["ml_wv"], w["ml_bif"], w["ml_gn"]]
    aliases = {}
    if prev is not None:
        aliases = {len(in_specs) + j: 3 + j for j in range(len(prev))}
        in_specs += [pl.BlockSpec(memory_space=pl.ANY)] * len(prev)
        args += list(prev)
    sbytes = GLA_H * GLA_DK * GLA_DV * 4
    cbytes = ML_H * ML_DH * ML_DH * 4
    vmem = 2 * rows * (7 * 1024 * 4 + 3 * 1024 * 2) + 3 * (sbytes + cbytes) + 40 * rows * 1024 * 4 + (12 << 20)
    return pl.pallas_call(
        functools.partial(_prompt_mixer_kernel, L=L),
        grid=(b, t // rows),
        in_specs=in_specs,
        out_specs=[blk3(LRU_W, 0), blk3(vw, 0), blk3(ML_W, 0),
                   st3((1, LRU_W)), st3((CONV_W - 1, LRU_W)), st4((GLA_H, GLA_DK, GLA_DV)),
                   st4((ML_H, ML_DH, ML_DH)), st3((ML_H, ML_DH)), st3((ML_H, LANE)), st3((CONV_W - 1, ML_W))],
        out_shape=[SDS((b, t, LRU_W), BF16), SDS((b, t, vw), BF16), SDS((b, t, ML_W), BF16),
                   SDS((DEPTH, b, 1, LRU_W), F32), SDS((DEPTH, b, CONV_W - 1, LRU_W), F32),
                   SDS((DEPTH, b, GLA_H, GLA_DK, GLA_DV), F32), SDS((DEPTH, b, ML_H, ML_DH, ML_DH), F32),
                   SDS((DEPTH, b, ML_H, ML_DH), F32), SDS((DEPTH, b, ML_H, LANE), F32),
                   SDS((DEPTH, b, CONV_W - 1, ML_W), F32)],
        scratch_shapes=[pltpu.VMEM((SUBLANE + rows, LRU_W), F32), pltpu.VMEM((SUBLANE + rows, ML_W), F32),
                        pltpu.VMEM((rows, LRU_W), F32), pltpu.VMEM((rows, LRU_W), F32), pltpu.VMEM((1, LRU_W), F32),
                        pltpu.VMEM((GLA_H, GLA_DK, GLA_DV), F32), pltpu.VMEM((ML_H, ML_DH, ML_DH), F32),
                        pltpu.VMEM((ML_H, ML_DH), F32), pltpu.VMEM((ML_H, LANE), F32)],
        input_output_aliases=aliases,
        compiler_params=_params(("parallel", "arbitrary"), vmem),
        name="prompt_mixer",
    )(*args)


def _tmask(shape):
    return lax.broadcasted_iota(jnp.int32, shape, 0) % T_STEP


def _down(x, s):
    return x if s == 0 else pltpu.roll(x, s, 0)


def _up(x, s):
    return x if s == 0 else pltpu.roll(x, x.shape[0] - s, 0)


def _seg_cumsum(x):
    t = _tmask(x.shape)
    out = x
    for s in range(1, T_STEP):
        out = out + jnp.where(t >= s, _down(x, s), 0.0)
    return out


def _seg_last(x):
    t = _tmask(x.shape)
    out = x
    for s in range(1, T_STEP):
        out = jnp.where(t == T_STEP - 1 - s, _up(x, s), out)
    return out


def _seg_allreduce(x, op):
    t = _tmask(x.shape)
    y = op(x, jnp.where(t % 2 == 1, _down(x, 1), _up(x, 1)))
    return op(y, jnp.where(t >= 2, _down(y, 2), _up(y, 2)))


def _conv_rows(x, e, w, b):
    t = _tmask(x.shape)
    acc = b + x * w[CONV_W - 1:CONV_W]
    for k in range(1, CONV_W):
        hist = jnp.where(t >= k, _down(x, k), _up(e, CONV_W - 1 - k))
        acc = acc + hist * w[CONV_W - 1 - k:CONV_W - k]
    return acc


def _col(x, lane):
    li = lax.broadcasted_iota(jnp.int32, x.shape, 1)
    return jnp.sum(jnp.where(li == lane, x, 0.0), axis=-1, keepdims=True)


def _sample_mixer_kernel(lx_ref, q_ref, k_ref, v_ref, gate_ref, mx_ref, mo_ref, small_ref,
                         el_ref, em_ref, h0_ref, n0_ref, m0_ref, s0_ref, c0_ref,
                         lcw_ref, lcb_ref, wa_ref, ba_ref, wx_ref, bx_ref, lam_ref,
                         wg2_ref, bg_ref, ggn_ref,
                         mcw_ref, mcb_ref, wqh_ref, wql_ref, wkh_ref, wkl_ref, wvh_ref, wvl_ref, bif_ref, mgn_ref,
                         *rest):
    yl_ref, yg_ref, ym_ref, hrow_ref, nrow_ref, mrow_ref, st_ref, ct_ref = rest[-8:]
    rows = lx_ref.shape[0]
    n_pairs = rows // SUBLANE
    seq_per_tile = SUBLANE // T_STEP
    t_col = _tmask((rows, 1))
    row8 = lax.broadcasted_iota(jnp.int32, (SUBLANE, 1), 0)

    u = _conv_rows(lx_ref[...], el_ref[...], lcw_ref[...], lcb_ref[...])
    a_blocks, g_blocks = [], []
    for blk in range(LRU_BLOCKS):
        sl = slice(blk * LANE, (blk + 1) * LANE)
        a_b, g_b = _lru_gates(u[:, sl], wa_ref[blk], wx_ref[blk], ba_ref[:, sl], bx_ref[:, sl], lam_ref[:, sl])
        a_blocks.append(a_b)
        g_blocks.append(g_b)
    a = jnp.concatenate(a_blocks, axis=1)
    t_w = _tmask(a.shape)
    g = jnp.concatenate(g_blocks, axis=1) + jnp.where(t_w == 0, a * h0_ref[...], 0.0)
    a1 = jnp.where(t_w >= 1, a * _down(a, 1), a)
    g1 = jnp.where(t_w >= 1, g + a * _down(g, 1), g)
    h = jnp.where(t_w >= 2, g1 + a1 * _down(g1, 2), g1)
    yl_ref[...] = h.astype(BF16)
    hrow_ref[...] = h

    q, k, v = q_ref[...], k_ref[...], v_ref[...]
    logdec = _log_sigmoid(_dot(small_ref[...], wg2_ref[...]) + bg_ref[...]) * (1.0 / GLA_TAU)
    bc = _seg_cumsum(logdec)
    bl = _seg_last(bc)
    qs = q * (GLA_DK ** -0.5)
    o_heads = [jnp.zeros((rows, GLA_DV), F32) for _ in range(GLA_H)]
    for s in range(T_STEP):
        prod = qs * _down(k, s) * jnp.exp(bc - _down(bc, s))
        v_s = _down(v, s)
        for hh in range(GLA_H):
            a_sh = jnp.sum(prod[:, hh * GLA_DK:(hh + 1) * GLA_DK], axis=-1, keepdims=True)
            a_sh = jnp.where(t_col >= s, a_sh, 0.0)
            o_heads[hh] = o_heads[hh] + a_sh * v_s[:, hh * GLA_DV:(hh + 1) * GLA_DV]
    qe = qs * jnp.exp(bc)
    kd = k * jnp.exp(bl - bc)
    dec = jnp.exp(bl)
    gate = gate_ref[...]
    for hh in range(GLA_H):
        ks = slice(hh * GLA_DK, (hh + 1) * GLA_DK)
        vs = slice(hh * GLA_DV, (hh + 1) * GLA_DV)
        o_state = []
        for p in range(n_pairs):
            r8 = slice(p * SUBLANE, (p + 1) * SUBLANE)
            kd_t = kd[r8, ks].T
            dec_t = dec[r8, ks].T
            res = None
            for j in range(seq_per_tile):
                b = p * seq_per_tile + j
                s_b = s0_ref[b, hh]
                r_j = _dot(qe[r8, ks], s_b)
                res = r_j if res is None else jnp.where(row8 // T_STEP == j, r_j, res)
                v_j = jnp.where(row8 // T_STEP == j, v[r8, vs], 0.0)
                last = j * T_STEP + T_STEP - 1
                st_ref[b, hh] = dec_t[:, last:last + 1] * s_b + _dot(kd_t, v_j)
            o_state.append(res)
        o = o_heads[hh] + jnp.concatenate(o_state, axis=0)
        yg_ref[:, vs] = (_rms(o, ggn_ref[...]) * _silu(gate[:, vs])).astype(BF16)

    mx = mx_ref[...]
    mc = _silu(_conv_rows(mx, em_ref[...], mcw_ref[...], mcb_ref[...]))
    mq = _blockdiag3(mc, wqh_ref, wql_ref)
    mk = _blockdiag3(mc, wkh_ref, wkl_ref) * (ML_DH ** -0.5)
    mv = _blockdiag3(mx, wvh_ref, wvl_ref)
    gts = small_ref[...] + bif_ref[...]
    ig = pltpu.roll(gts, S_FG - S_IG, 1)
    fcum = _seg_cumsum(_log_sigmoid(gts))
    flast = _seg_last(fcum)
    m0 = m0_ref[...]
    t_g = _tmask(gts.shape)
    inter = fcum + m0
    dms = [jnp.where(t_g >= s, fcum - _down(fcum, s) + _down(ig, s), NEG_BIG) for s in range(T_STEP)]
    mt = inter
    for dm in dms:
        mt = jnp.maximum(mt, dm)
    ci_t = jnp.exp(inter - mt)
    emt_t = jnp.exp(-mt)
    w_t = [jnp.exp(dm - mt) for dm in dms]
    dj = flast - fcum + ig
    m_new = jnp.maximum(flast + m0, _seg_allreduce(dj, jnp.maximum))
    cs_t = jnp.exp(flast + m0 - m_new)
    wj_t = jnp.exp(dj - m_new)
    mrow_ref[...] = m_new
    mo = mo_ref[...]
    n0 = n0_ref[...]
    for hh in range(ML_H):
        sl = slice(hh * ML_DH, (hh + 1) * ML_DH)
        lane = S_FG + hh
        qh, kh, vh = mq[:, sl], mk[:, sl], mv[:, sl]
        ci, emt, cs, wj = _col(ci_t, lane), _col(emt_t, lane), _col(cs_t, lane), _col(wj_t, lane)
        num = jnp.zeros((rows, ML_DH), F32)
        den = ci * jnp.sum(qh * n0[:, sl], axis=-1, keepdims=True)
        for s in range(T_STEP):
            sc = jnp.sum(qh * _down(kh, s), axis=-1, keepdims=True) * _col(w_t[s], lane)
            num = num + sc * _down(vh, s)
            den = den + sc
        kw = kh * wj
        nrow_ref[:, sl] = cs * n0[:, sl] + _seg_allreduce(kw, jnp.add)
        qc = []
        for p in range(n_pairs):
            r8 = slice(p * SUBLANE, (p + 1) * SUBLANE)
            kw_t = kw[r8, :].T
            res = None
            for j in range(seq_per_tile):
                b = p * seq_per_tile + j
                c_b = c0_ref[b, hh]
                r_j = _dot(qh[r8, :], c_b)
                res = r_j if res is None else jnp.where(row8 // T_STEP == j, r_j, res)
                v_j = jnp.where(row8 // T_STEP == j, vh[r8, :], 0.0)
                last = p * SUBLANE + j * T_STEP + T_STEP - 1
                ct_ref[b, hh] = cs[last:last + 1, :] * c_b + _dot(kw_t, v_j)
            qc.append(res)
        num = num + ci * jnp.concatenate(qc, axis=0)
        hcell = num / jnp.maximum(jnp.abs(den), emt)
        ym_ref[:, sl] = (jax.nn.sigmoid(mo[:, sl]) * _rms(hcell, mgn_ref[...])).astype(BF16)


def _sample_mixer(proj, small, el, em, h0e, n0e, m0e, s_state, c_state, layer, w, prev):
    n = proj.shape[0]
    nseq = n // T_STEP
    bb = STEP_BB
    rows = bb * T_STEP
    row = lambda width, blk: pl.BlockSpec((rows, width), lambda i: (i, blk))
    lrow = lambda width: pl.BlockSpec((None, rows, width), lambda i: (layer, i, 0))
    lw3 = lambda s: pl.BlockSpec((None,) + s, lambda i: (layer, 0, 0))
    lw4 = lambda s: pl.BlockSpec((None,) + s, lambda i: (layer, 0, 0, 0))
    sspec = pl.BlockSpec((None, bb, GLA_H, GLA_DK, GLA_DV), lambda i: (layer, i, 0, 0, 0))
    cspec = pl.BlockSpec((None, bb, ML_H, ML_DH, ML_DH), lambda i: (layer, i, 0, 0, 0))
    nblk = ML_W // LANE
    kw, vw = GLA_H * GLA_DK, GLA_H * GLA_DV
    in_specs = [
        row(LRU_W, C_LRU // LRU_W), row(kw, C_GQ // kw), row(kw, C_GK // kw), row(vw, C_GV // vw),
        row(vw, C_GGATE // vw), row(ML_W, C_MX // ML_W), row(ML_W, C_MO // ML_W), row(LANE, 0),
        lrow(LRU_W), lrow(ML_W), lrow(LRU_W), lrow(ML_W), lrow(LANE), sspec, cspec,
        lw3((CONV_W, LRU_W)), lw3((1, LRU_W)), lw4((LRU_BLOCKS, LANE, LANE)), lw3((1, LRU_W)),
        lw4((LRU_BLOCKS, LANE, LANE)), lw3((1, LRU_W)), lw3((1, LRU_W)),
        lw3((LANE, kw)), lw3((1, kw)), lw3((1, GLA_DV)),
        lw3((CONV_W, ML_W)), lw3((1, ML_W)), *([lw4((nblk, LANE, LANE))] * 6), lw3((1, LANE)), lw3((1, ML_DH)),
    ]
    args = [proj] * 7 + [small, el, em, h0e, n0e, m0e, s_state, c_state,
                         w["lru_cw"], w["lru_cb"], w["lru_wa"], w["lru_ba"], w["lru_wx"], w["lru_bx"], w["lru_lam"],
                         w["wg2"], w["bg"], w["gla_gn"],
                         w["ml_cw"], w["ml_cb"], *w["ml_wq"], *w["ml_wk"], *w["ml_wv"], w["ml_bif"], w["ml_gn"]]
    aliases = {}
    if prev is not None:
        aliases = {len(in_specs) + j: 3 + j for j in range(len(prev))}
        in_specs += [pl.BlockSpec(memory_space=pl.ANY)] * len(prev)
        args += list(prev)
    sbytes = bb * GLA_H * GLA_DK * GLA_DV * 4
    cbytes = bb * ML_H * ML_DH * ML_DH * 4
    vmem = 4 * (sbytes + cbytes) + 2 * rows * (7 * 1024 + 8 * 1024) * 4 + 80 * rows * 1024 * 4 + (12 << 20)
    return pl.pallas_call(
        _sample_mixer_kernel,
        grid=(nseq // bb,),
        in_specs=in_specs,
        out_specs=[row(LRU_W, 0), row(vw, 0), row(ML_W, 0), lrow(LRU_W), lrow(ML_W), lrow(LANE), sspec, cspec],
        out_shape=[SDS((n, LRU_W), BF16), SDS((n, vw), BF16), SDS((n, ML_W), BF16),
                   SDS((DEPTH, n, LRU_W), F32), SDS((DEPTH, n, ML_W), F32), SDS((DEPTH, n, LANE), F32),
                   SDS((DEPTH, nseq, GLA_H, GLA_DK, GLA_DV), F32), SDS((DEPTH, nseq, ML_H, ML_DH, ML_DH), F32)],
        input_output_aliases=aliases,
        compiler_params=_params(("arbitrary",), vmem),
        name="sample_mixer",
    )(*args)


def _prep_weights(p):
    w_in = p["w_in"]
    c1 = C_GGATE
    c2 = C_MG + W_SHIFT1
    w_small = jnp.concatenate(
        [w_in[:, :, c1:c1 + GLA_RANK], w_in[:, :, c2:c2 + 2 * ML_H],
         jnp.zeros((DEPTH, D, LANE - GLA_RANK - 2 * ML_H), F32)], axis=-1)
    wg2 = jnp.concatenate(
        [p["gla_w_g2"], jnp.zeros((DEPTH, LANE - GLA_RANK, GLA_H * GLA_DK), F32)], axis=1)
    eye = jnp.eye(LANE // ML_BS, dtype=F32)

    def dense_bd(wb):
        wb = wb.reshape(DEPTH, ML_W // LANE, LANE // ML_BS, ML_BS, ML_BS)
        dense = jnp.einsum("dgncx,nm->dgncmx", wb, eye).reshape(DEPTH, ML_W // LANE, LANE, LANE)
        hi = dense.astype(BF16)
        return hi, (dense - hi.astype(F32)).astype(BF16)

    bif = jnp.zeros((DEPTH, 1, LANE), F32).at[:, 0, S_IG:S_IG + 2 * ML_H].set(p["ml_b_if"])
    r3 = lambda a: a.reshape(DEPTH, 1, a.shape[-1])
    return dict(
        w_small=w_small, wg2=wg2, bg=r3(p["gla_b_g"]), gla_gn=r3(p["gla_g_norm"]),
        g1=r3(p["g_norm1"]), g2=r3(p["g_norm2"]), gf=p["g_final"].reshape(1, D),
        lru_cw=p["lru_conv_w"], lru_cb=r3(p["lru_conv_b"]), lru_wa=p["lru_w_a"], lru_ba=r3(p["lru_b_a"]),
        lru_wx=p["lru_w_x"], lru_bx=r3(p["lru_b_x"]), lru_lam=r3(p["lru_lam"]),
        ml_cw=p["ml_conv_w"], ml_cb=r3(p["ml_conv_b"]), ml_wq=dense_bd(p["ml_w_q"]), ml_wk=dense_bd(p["ml_w_k"]),
        ml_wv=dense_bd(p["ml_w_v"]), ml_bif=bif, ml_gn=r3(p["ml_g_norm"]),
        w_in_t=jnp.swapaxes(w_in, 1, 2), w_br=(p["w_br_lru"], p["w_br_gla"], p["w_br_ml"]), w_out=p["w_out"], w_ff1=p["w_ff1"],
        w_ff2=p["w_ff2"],
    )


def _trunk(xp3, xs3, mod_p, mod_s, states, w):
    bp, tp, _ = xp3.shape
    bs, ts, _ = xs3.shape
    n_p, n_s = bp * tp, bs * ts
    tm_p = min(tp, 1024)
    tm_f = min(tp, 512)
    s_h, s_cv, s_gla, s_c, s_n, s_m, s_mcv = states
    pad_t = lambda a: jnp.pad(a, ((0, 0), (0, 0), (0, ts - a.shape[2]), (0, 0))).reshape(DEPTH, n_s, a.shape[-1])
    el = pad_t(s_cv)
    em = pad_t(s_mcv)
    h0e = pad_t(s_h[:, :, None, :])
    n0e = jnp.broadcast_to(s_n.reshape(DEPTH, bs, 1, ML_W), (DEPTH, bs, ts, ML_W)).reshape(DEPTH, n_s, ML_W)
    m_l = jnp.pad(s_m, ((0, 0), (0, 0), (S_FG, LANE - S_FG - ML_H)))
    m0e = jnp.broadcast_to(m_l[:, :, None, :], (DEPTH, bs, ts, LANE)).reshape(DEPTH, n_s, LANE)
    x_p = xp3.reshape(n_p, D)
    x_s = xs3.reshape(n_s, D)
    prev_p = prev_s = None
    conv_l, conv_m = [], []
    keep = ts - (CONV_W - 1)
    for l in range(DEPTH):
        last = l == DEPTH - 1
        proj_s, small_s, w_main = _in_proj_cast(x_s, mod_s, l, w["g1"], w["w_in_t"], w["w_small"], tn=512, rpg=n_s)
        proj_p, small_p = _in_proj(x_p, mod_p, l, w["g1"], w_main, w["w_small"], tm=tm_p, tn=1024, rpg=tp, r=1)
        outs_s = _sample_mixer(proj_s, small_s, el, em, h0e, n0e, m0e, s_gla, s_c, l, w, prev_s)
        prev_s = outs_s[3:]
        outs_p = _prompt_mixer(proj_p.reshape(bp, tp, N_MAIN), small_p.reshape(bp, tp, LANE), l, w, prev_p,
                               L=min(tp, CHUNK), rows=min(tp, PROMPT_ROWS))
        prev_p = outs_p[3:]
        conv_l.append(proj_s[:, C_LRU:C_LRU + LRU_W].reshape(bs, ts, LRU_W)[:, keep:])
        conv_m.append(proj_s[:, C_MX:C_MX + ML_W].reshape(bs, ts, ML_W)[:, keep:])
        merged_s, *w_br = _merge(*outs_s[:3], proj_s, l, *w["w_br"], tm=n_s, tn=512, cast=True)
        (merged_p,) = _merge(*(y.reshape(n_p, y.shape[-1]) for y in outs_p[:3]), proj_p, l, *w_br, tm=tm_p, tn=512,
                             cast=False)
        x_s, w_out = _out_proj(merged_s, x_s, mod_s, l, w["w_out"], tm=n_s, tn=512, rpg=n_s, r=n_s, cast=True)
        (x_p,) = _out_proj(merged_p, x_p, mod_p, l, w_out, tm=tm_p, tn=512, rpg=tp, r=1, cast=False)
        x_s, w_ff1, w_ff2 = _ffn(x_s, mod_s, l, w["g2"], w["gf"], w["w_ff1"], w["w_ff2"], tm=n_s, tf=256, rpg=n_s,
                                 r=n_s, final_norm=last, cast=True)
        (x_p,) = _ffn(x_p, mod_p, l, w["g2"], w["gf"], w_ff1, w_ff2, tm=tm_f, tf=1024, rpg=tp, r=1, final_norm=last,
                      cast=False)
    h_t, lcv_t, s_t, c_t, n_t, m_t, mcv_t = prev_p
    p_states = [h_t.reshape(DEPTH, bp, LRU_W), lcv_t, s_t, c_t, n_t, m_t[..., 0], mcv_t]
    hrow, nrow, mrow, s_out, c_out = prev_s
    t_last = ts - 1
    s_states = [hrow.reshape(DEPTH, bs, ts, LRU_W)[:, :, t_last], jnp.stack(conv_l), s_out, c_out,
                nrow.reshape(DEPTH, bs, ts, ML_H, ML_DH)[:, :, t_last],
                mrow.reshape(DEPTH, bs, ts, LANE)[:, :, t_last, S_FG:S_FG + ML_H], jnp.stack(conv_m)]
    return x_p.reshape(bp, tp, D), x_s.reshape(bs, ts, D), p_states, s_states


def kernel(x_prompt, x_sample, c_prompt, c_sample, state_lru_h, state_lru_conv, state_gla, state_mlstm_C, state_mlstm_n, state_mlstm_m, state_mlstm_conv, w_ada, b_ada, g_norm1, g_norm2, w_in, lru_conv_w, lru_conv_b, lru_w_a, lru_b_a, lru_w_x, lru_b_x, lru_lam, gla_w_g2, gla_b_g, gla_g_norm, ml_conv_w, ml_conv_b, ml_w_q, ml_w_k, ml_w_v, ml_b_if, ml_g_norm, w_br_lru, w_br_gla, w_br_ml, w_out, w_ff1, w_ff2, g_final):
    p = dict(g_norm1=g_norm1, g_norm2=g_norm2, w_in=w_in, lru_conv_w=lru_conv_w, lru_conv_b=lru_conv_b,
             lru_w_a=lru_w_a, lru_b_a=lru_b_a, lru_w_x=lru_w_x, lru_b_x=lru_b_x, lru_lam=lru_lam,
             gla_w_g2=gla_w_g2, gla_b_g=gla_b_g, gla_g_norm=gla_g_norm, ml_conv_w=ml_conv_w, ml_conv_b=ml_conv_b,
             ml_w_q=ml_w_q, ml_w_k=ml_w_k, ml_w_v=ml_w_v, ml_b_if=ml_b_if, ml_g_norm=ml_g_norm,
             w_br_lru=w_br_lru, w_br_gla=w_br_gla, w_br_ml=w_br_ml, w_out=w_out, w_ff1=w_ff1, w_ff2=w_ff2,
             g_final=g_final)
    w = _prep_weights(p)
    bp = x_prompt.shape[0]
    bs, ts, _ = x_sample.shape
    assert ts == T_STEP and bs % STEP_BB == 0
    ns = bs * ts
    pad = (-(ns + bp)) % SUBLANE
    c_all = jnp.concatenate([jnp.repeat(c_sample, ts, axis=0), c_prompt, jnp.zeros((pad, D), F32)], axis=0)
    mod_all = _ada(c_all, w_ada, b_ada)
    mod_s = mod_all.reshape(DEPTH, 1, ns + bp + pad, N_MOD * D)
    mod_p = mod_all[:, ns:ns + bp].reshape(DEPTH, bp, 1, N_MOD * D)
    y_p, y_s, ps, ss = _trunk(x_prompt, x_sample, mod_p, mod_s,
                              (state_lru_h, state_lru_conv, state_gla, state_mlstm_C, state_mlstm_n, state_mlstm_m,
                               state_mlstm_conv), w)
    return (y_p, y_s, *ps, *ss)
```

```python
import functools

import jax
import jax.numpy as jnp
from jax import lax
from jax.experimental import pallas as pl
from jax.experimental.pallas import tpu as pltpu

F32, BF16 = jnp.float32, jnp.bfloat16
SDS = jax.ShapeDtypeStruct

D = 2048
DEPTH = 2
LRU_W = 1024
LRU_BLOCKS = 8
LRU_C = 8.0
CONV_W = 4
GLA_H = 4
GLA_DK = 128
GLA_DV = 256
GLA_RANK = 16
GLA_TAU = 16.0
ML_H = 4
ML_W = 1024
ML_DH = 256
ML_BS = 4
CHUNK = 128
PROMPT_ROWS = 256
PROMPT_SEQS = 1
D_FF = 4 * D
EPS = 1e-6
N_MOD = 6
T_STEP = 4
STEP_BB = 4

LANE = 128
SUBLANE = 8
VMEM_LIMIT_CAP = 60 * 1024 * 1024

C_LRU, C_GQ, C_GK, C_GV, C_GGATE, C_MX, C_MO, C_MG = 0, 1024, 1536, 2048, 3072, 4096, 5120, 6144
N_MAIN = 12288
W_SHIFT1 = GLA_RANK
W_SHIFT2 = GLA_RANK + 2 * ML_H
S_GLR = 0
S_IG = 16
S_FG = 20
NEG_BIG = -1e30


def _params(sem, vmem_bytes):
    return pltpu.CompilerParams(dimension_semantics=sem, vmem_limit_bytes=int(min(vmem_bytes, VMEM_LIMIT_CAP)))


def _dot(a, b):
    return jnp.dot(a.astype(BF16), b.astype(BF16), preferred_element_type=F32)


def _dot_nt(a, b):
    return lax.dot_general(a.astype(BF16), b.astype(BF16), (((1,), (1,)), ((), ())), preferred_element_type=F32)


def _split3(x):
    p0 = x.astype(BF16)
    r = x - p0.astype(F32)
    p1 = r.astype(BF16)
    return p0, p1, (r - p1.astype(F32)).astype(BF16)


def _cumsum_rows(tri_b, x):
    return sum(jnp.dot(tri_b, p, preferred_element_type=F32) for p in _split3(x))


def _cumsum_lanes(x, triu_b):
    return sum(jnp.dot(p, triu_b, preferred_element_type=F32) for p in _split3(x))


def _split(x):
    hi = x.astype(BF16)
    return hi, (x - hi.astype(F32)).astype(BF16)


def _dot3(x_hi, x_lo, w_hi, w_lo):
    d = lambda a, b: jnp.dot(a, b, preferred_element_type=F32)
    return d(x_hi, w_hi) + (d(x_lo, w_hi) + d(x_hi, w_lo))


def _log_sigmoid(z):
    return jnp.minimum(z, 0.0) - jnp.log1p(jnp.exp(-jnp.abs(z)))


def _silu(z):
    return z * jax.nn.sigmoid(z)


def _rms(x, g):
    return x * lax.rsqrt(jnp.mean(x * x, axis=-1, keepdims=True) + EPS) * g


def _tri(n):
    r = lax.broadcasted_iota(jnp.int32, (n, n), 0)
    c = lax.broadcasted_iota(jnp.int32, (n, n), 1)
    return r >= c


def _lru_gates(ub, wa, wx, ba, bx, lam):
    r = jax.nn.sigmoid(_dot(ub, wa) + ba)
    i = jax.nn.sigmoid(_dot(ub, wx) + bx)
    log_a = LRU_C * r * _log_sigmoid(lam)
    t = jnp.tanh(log_a)
    return jnp.exp(log_a), jnp.sqrt(-2.0 * t / (1.0 - t)) * (i * ub)


def _blockdiag3(x, wh_ref, wl_ref):
    xh, xl = _split(x)
    outs = []
    for blk in range(ML_W // LANE):
        sl = slice(blk * LANE, (blk + 1) * LANE)
        outs.append(_dot3(xh[:, sl], xl[:, sl], wh_ref[blk], wl_ref[blk]))
    return jnp.concatenate(outs, axis=1)


def _ada_kernel(c_ref, w_ref, b_ref, o_ref):
    o_ref[...] = _dot(_silu(c_ref[...]), w_ref[...]) + b_ref[...]


def _ada(c_all, w_ada, b_ada):
    m = c_all.shape[0]
    tn = 512
    n_out = N_MOD * D
    return pl.pallas_call(
        _ada_kernel,
        grid=(DEPTH, n_out // tn),
        in_specs=[
            pl.BlockSpec((m, D), lambda l, n: (0, 0)),
            pl.BlockSpec((None, D, tn), lambda l, n: (l, 0, n)),
            pl.BlockSpec((None, 1, tn), lambda l, n: (l, 0, n)),
        ],
        out_specs=pl.BlockSpec((None, m, tn), lambda l, n: (l, 0, n)),
        out_shape=SDS((DEPTH, m, n_out), F32),
        compiler_params=_params(("parallel", "arbitrary"), 2 * (m * D + D * tn + m * tn) * 4 + (12 << 20)),
        name="ada",
    )(c_all, w_ada, b_ada.reshape(DEPTH, 1, n_out))


def _modspec(layer, comp, r, width, tm, rpg, jmap=None):
    nb = D // width
    jm = (lambda j: 0) if jmap is None else jmap
    if r == 1:
        return pl.BlockSpec((None, None, 1, width), lambda i, j: (layer, (i * tm) // rpg, 0, comp * nb + jm(j)))
    mode = pl.Buffered(1) if jmap is None else None
    return pl.BlockSpec((None, None, tm, width), lambda i, j: (layer, 0, i, comp * nb + jm(j)), pipeline_mode=mode)


def _norm_mod_store(xn_s, x_ref, g_ref, sc_ref, sh_ref):
    tm = x_ref.shape[0]
    rc = min(tm, 256)
    for c in range(tm // rc):
        rows = slice(c * rc, (c + 1) * rc)
        mrows = rows if sc_ref.shape[0] == tm else slice(None)
        xn = _rms(x_ref[rows, :], g_ref[...]) * (1.0 + sc_ref[mrows, :]) + sh_ref[mrows, :]
        xn_s[rows, :] = xn.astype(BF16)


def _in_proj_kernel(x_ref, sc_ref, sh_ref, g_ref, wm_ref, ws_ref, om_ref, os_ref, xn_s):
    @pl.when(pl.program_id(1) == 0)
    def _():
        _norm_mod_store(xn_s, x_ref, g_ref, sc_ref, sh_ref)
        os_ref[...] = jnp.dot(xn_s[...], ws_ref[...].astype(BF16), preferred_element_type=F32)

    om_ref[...] = jnp.dot(xn_s[...], wm_ref[...], preferred_element_type=F32)


def _in_proj(x, mod, layer, g1, w_main, w_small, *, tm, tn, rpg, r):
    n = x.shape[0]
    vmem = 2 * (tm * D * 4 + 2 * r * D * 4 + D * tn * 2 + D * LANE * 2 + tm * tn * 4 + tm * LANE * 4) + tm * D * 2
    vmem += 8 << 20
    return pl.pallas_call(
        _in_proj_kernel,
        grid=(n // tm, N_MAIN // tn),
        in_specs=[
            pl.BlockSpec((tm, D), lambda i, j: (i, 0)),
            _modspec(layer, 1, r, D, tm, rpg),
            _modspec(layer, 0, r, D, tm, rpg),
            pl.BlockSpec((None, 1, D), lambda i, j: (layer, 0, 0)),
            pl.BlockSpec((D, tn), lambda i, j: (0, j)),
            pl.BlockSpec((None, D, LANE), lambda i, j: (layer, 0, 0)),
        ],
        out_specs=[pl.BlockSpec((tm, tn), lambda i, j: (i, j)), pl.BlockSpec((tm, LANE), lambda i, j: (i, 0))],
        out_shape=[SDS((n, N_MAIN), F32), SDS((n, LANE), F32)],
        scratch_shapes=[pltpu.VMEM((tm, D), BF16)],
        compiler_params=_params(("parallel", "arbitrary"), vmem),
        name="in_proj",
    )(x, mod, mod, g1, w_main, w_small)


def _in_proj_cast_kernel(x_ref, sc_ref, sh_ref, g_ref, wt_ref, ws_ref, om_ref, os_ref, wc_ref, xn_s):
    @pl.when(pl.program_id(1) == 0)
    def _():
        _norm_mod_store(xn_s, x_ref, g_ref, sc_ref, sh_ref)
        os_ref[...] = jnp.dot(xn_s[...], ws_ref[...].astype(BF16), preferred_element_type=F32)

    wb = wt_ref[...].T.astype(BF16)
    wc_ref[...] = wb
    om_ref[...] = jnp.dot(xn_s[...], wb, preferred_element_type=F32)


def _in_proj_cast(x, mod, layer, g1, w_in_t, w_small, *, tn, rpg):
    n = x.shape[0]
    tm = n
    nb1, nb2 = C_GGATE // tn, C_MG // tn

    def w_rows(i, j):
        shift = jnp.where(j >= nb2, W_SHIFT2 // SUBLANE, jnp.where(j >= nb1, W_SHIFT1 // SUBLANE, 0))
        return (layer, (j * (tn // SUBLANE) + shift) * SUBLANE, 0)

    vmem = tm * D * 4 + 2 * tm * D * 4 + 2 * (tn * D * 4 + D * LANE * 4 + tm * tn * 4 + tm * LANE * 4 + D * tn * 2)
    vmem += tm * D * 2 + 3 * D * tn * 4 + (4 << 20)
    return pl.pallas_call(
        _in_proj_cast_kernel,
        grid=(1, N_MAIN // tn),
        in_specs=[
            pl.BlockSpec((tm, D), lambda i, j: (i, 0), pipeline_mode=pl.Buffered(1)),
            _modspec(layer, 1, tm, D, tm, rpg),
            _modspec(layer, 0, tm, D, tm, rpg),
            pl.BlockSpec((None, 1, D), lambda i, j: (layer, 0, 0)),
            pl.BlockSpec((None, pl.Element(tn), pl.Element(D)), w_rows),
            pl.BlockSpec((None, D, LANE), lambda i, j: (layer, 0, 0)),
        ],
        out_specs=[pl.BlockSpec((tm, tn), lambda i, j: (i, j)), pl.BlockSpec((tm, LANE), lambda i, j: (i, 0)),
                   pl.BlockSpec((D, tn), lambda i, j: (0, j))],
        out_shape=[SDS((n, N_MAIN), F32), SDS((n, LANE), F32), SDS((D, N_MAIN), BF16)],
        scratch_shapes=[pltpu.VMEM((tm, D), BF16)],
        compiler_params=_params(("arbitrary", "arbitrary"), vmem),
        name="in_proj_cast",
    )(x, mod, mod, g1, w_in_t, w_small)


def _wspec(shape, imap, layer, cast):
    if cast:
        return pl.BlockSpec((None,) + shape, lambda i, j: (layer,) + imap(i, j))
    return pl.BlockSpec(shape, imap)


def _merge_kernel(yl_ref, yg_ref, ym_ref, g0_ref, g1_ref, g2_ref, w0_ref, w1_ref, w2_ref, o_ref, *wc_refs):
    ws = [w_ref[...].astype(BF16) for w_ref in (w0_ref, w1_ref, w2_ref)]
    for wc_ref, wb in zip(wc_refs, ws):
        wc_ref[...] = wb
    acc = jax.nn.sigmoid(g0_ref[...]) * jnp.dot(yl_ref[...], ws[0], preferred_element_type=F32)
    acc += jax.nn.sigmoid(g1_ref[...]) * jnp.dot(yg_ref[...], ws[1], preferred_element_type=F32)
    acc += jax.nn.sigmoid(g2_ref[...]) * jnp.dot(ym_ref[...], ws[2], preferred_element_type=F32)
    o_ref[...] = acc.astype(BF16)


def _merge(y_lru, y_gla, y_ml, proj, layer, w_lru, w_gla, w_ml, *, tm, tn, cast):
    n = y_lru.shape[0]
    w = LRU_W
    gb = C_MG // tn
    nb = D // tn
    yspec = pl.BlockSpec((tm, w), lambda i, j: (i, 0))
    wspec = _wspec((w, tn), lambda i, j: (0, j), layer, cast)
    wbytes = 4 if cast else 2
    vmem = 2 * (3 * tm * w * 2 + 3 * tm * tn * 4 + 3 * w * tn * wbytes + tm * tn * 2) + 6 * tm * tn * 4 + (4 << 20)
    out_specs = [pl.BlockSpec((tm, tn), lambda i, j: (i, j))]
    out_shape = [SDS((n, D), BF16)]
    if cast:
        out_specs += [pl.BlockSpec((w, tn), lambda i, j: (0, j))] * 3
        out_shape += [SDS((w, D), BF16)] * 3
        vmem += 2 * 3 * w * tn * 2 + 3 * w * tn * 4
    return pl.pallas_call(
        _merge_kernel,
        grid=(n // tm, nb),
        in_specs=[
            yspec, yspec, yspec,
            pl.BlockSpec((tm, tn), lambda i, j: (i, gb + j)),
            pl.BlockSpec((tm, tn), lambda i, j: (i, gb + nb + j)),
            pl.BlockSpec((tm, tn), lambda i, j: (i, gb + 2 * nb + j)),
            wspec, wspec, wspec,
        ],
        out_specs=out_specs,
        out_shape=out_shape,
        compiler_params=_params(("arbitrary" if cast else "parallel", "arbitrary"), vmem),
        name="merge",
    )(y_lru, y_gla, y_ml, proj, proj, proj, w_lru, w_gla, w_ml)


def _out_proj_kernel(m_ref, w_ref, x_ref, gt_ref, o_ref, *wc_refs):
    wb = w_ref[...].astype(BF16)
    for wc_ref in wc_refs:
        wc_ref[...] = wb
    o_ref[...] = x_ref[...] + gt_ref[...] * jnp.dot(m_ref[...], wb, preferred_element_type=F32)


def _out_proj(merged, x, mod, layer, w_out, *, tm, tn, rpg, r, cast):
    n = x.shape[0]
    wbytes = 4 if cast else 2
    vmem = 2 * (tm * D * 2 + D * tn * wbytes + 2 * tm * tn * 4 + r * tn * 4) + 2 * tm * tn * 4 + (4 << 20)
    out_specs = [pl.BlockSpec((tm, tn), lambda i, j: (i, j))]
    out_shape = [SDS((n, D), F32)]
    if cast:
        out_specs.append(pl.BlockSpec((D, tn), lambda i, j: (0, j)))
        out_shape.append(SDS((D, D), BF16))
        vmem += 2 * D * tn * 2 + D * tn * 4
    return pl.pallas_call(
        _out_proj_kernel,
        grid=(n // tm, D // tn),
        in_specs=[
            pl.BlockSpec((tm, D), lambda i, j: (i, 0)),
            _wspec((D, tn), lambda i, j: (0, j), layer, cast),
            pl.BlockSpec((tm, tn), lambda i, j: (i, j)),
            _modspec(layer, 2, r, tn, tm, rpg, jmap=lambda j: j),
        ],
        out_specs=out_specs,
        out_shape=out_shape,
        compiler_params=_params(("arbitrary" if cast else "parallel", "arbitrary"), vmem),
        name="out_proj",
    )(merged, w_out, x, mod)


def _ffn_kernel(x_ref, sc_ref, sh_ref, gt_ref, g_ref, gf_ref, w1_ref, w2_ref, o_ref, *rest, final_norm):
    xn_s = rest[-1]
    f = pl.program_id(1)

    @pl.when(f == 0)
    def _():
        _norm_mod_store(xn_s, x_ref, g_ref, sc_ref, sh_ref)
        o_ref[...] = jnp.zeros_like(o_ref)

    w1 = w1_ref[...].astype(BF16)
    w2 = w2_ref[...].astype(BF16)
    for wc_ref, wb in zip(rest[:-1], (w1, w2)):
        wc_ref[...] = wb
    h = jnp.square(jnp.maximum(jnp.dot(xn_s[...], w1, preferred_element_type=F32), 0.0))
    o_ref[...] += jnp.dot(h.astype(BF16), w2, preferred_element_type=F32)

    @pl.when(f == pl.num_programs(1) - 1)
    def _():
        y = x_ref[...] + gt_ref[...] * o_ref[...]
        if final_norm:
            y = _rms(y, gf_ref[...])
        o_ref[...] = y


def _ffn(x, mod, layer, g2, g_final, w1, w2, *, tm, tf, rpg, r, final_norm, cast):
    n = x.shape[0]
    wbytes = 4 if cast else 2
    vmem = 2 * (2 * tm * D * 4 + 3 * r * D * 4 + 2 * D * tf * wbytes) + tm * D * 2 + 2 * tm * tf * 4 + 2 * tm * D * 4
    single = pl.Buffered(1) if n == tm else None
    out_specs = [pl.BlockSpec((tm, D), lambda i, f: (i, 0), pipeline_mode=single)]
    out_shape = [SDS((n, D), F32)]
    if cast:
        out_specs += [pl.BlockSpec((D, tf), lambda i, f: (0, f)), pl.BlockSpec((tf, D), lambda i, f: (f, 0))]
        out_shape += [SDS((D, D_FF), BF16), SDS((D_FF, D), BF16)]
        vmem += 2 * 2 * D * tf * 2
    return pl.pallas_call(
        functools.partial(_ffn_kernel, final_norm=final_norm),
        grid=(n // tm, D_FF // tf),
        in_specs=[
            pl.BlockSpec((tm, D), lambda i, f: (i, 0), pipeline_mode=single),
            _modspec(layer, 4, r, D, tm, rpg),
            _modspec(layer, 3, r, D, tm, rpg),
            _modspec(layer, 5, r, D, tm, rpg),
            pl.BlockSpec((None, 1, D), lambda i, f: (layer, 0, 0)),
            pl.BlockSpec((1, D), lambda i, f: (0, 0)),
            _wspec((D, tf), lambda i, f: (0, f), layer, cast),
            _wspec((tf, D), lambda i, f: (f, 0), layer, cast),
        ],
        out_specs=out_specs,
        out_shape=out_shape,
        scratch_shapes=[pltpu.VMEM((tm, D), BF16)],
        compiler_params=_params(("arbitrary" if cast else "parallel", "arbitrary"), vmem),
        name="ffn",
    )(x, mod, mod, mod, g2, g_final, w1, w2)


def _conv_chunk(xp_s, x, w, b, L):
    xp_s[SUBLANE:SUBLANE + L, :] = x
    acc = b + xp_s[5:5 + L, :] * w[0:1]
    for j in range(1, CONV_W):
        acc = acc + xp_s[5 + j:5 + j + L, :] * w[j:j + 1]
    tail = xp_s[5 + L:8 + L, :]
    xp_s[5:8, :] = tail
    return acc, tail


N_TOK_REFS, N_W_REFS, N_Y_REFS, N_STATE_REFS, N_SCRATCH = 8, 20, 3, 7, 9


def _prompt_mixer_kernel(*refs, L):
    tok = refs[:N_TOK_REFS]
    wts = refs[N_TOK_REFS:N_TOK_REFS + N_W_REFS]
    scr = refs[-N_SCRATCH:]
    outs = refs[-(N_SCRATCH + N_Y_REFS + N_STATE_REFS):-N_SCRATCH]
    ys, state_outs = outs[:N_Y_REFS], outs[N_Y_REFS:]
    nseq = tok[0].shape[0]

    @pl.when(pl.program_id(1) == 0)
    def _():
        xpl_s, xpm_s = scr[0], scr[1]
        xpl_s[:, 0:SUBLANE, :] = jnp.zeros((nseq, SUBLANE, LRU_W), F32)
        xpm_s[:, 0:SUBLANE, :] = jnp.zeros((nseq, SUBLANE, ML_W), F32)
        for s_ref in scr[4:]:
            s_ref[...] = jnp.zeros_like(s_ref)

    finals = [_prompt_mixer_rows([r.at[bi] for r in tok], wts, [y.at[bi] for y in ys], [s.at[bi] for s in scr], L)
              for bi in range(nseq)]

    @pl.when(pl.program_id(1) == pl.num_programs(1) - 1)
    def _():
        ht_ref, lct_ref, st_ref, ct_ref, nt_ref, mt_ref, mct_ref = state_outs
        s_s, c_s, n_s, m_s = scr[5:]
        for bi, (h, l_tail, m_tail) in enumerate(finals):
            ht_ref[bi] = h
            lct_ref[bi] = l_tail
            mct_ref[bi] = m_tail
        st_ref[...] = s_s[...]
        ct_ref[...] = c_s[...]
        nt_ref[...] = n_s[...]
        mt_ref[...] = m_s[...]


def _prompt_mixer_rows(tok, wts, ys, scr, L):
    lx_ref, q_ref, k_ref, v_ref, gate_ref, mx_ref, mo_ref, small_ref = tok
    (lcw_ref, lcb_ref, wa_ref, ba_ref, wx_ref, bx_ref, lam_ref, wg2_ref, bg_ref, ggn_ref,
     mcw_ref, mcb_ref, wqh_ref, wql_ref, wkh_ref, wkl_ref, wvh_ref, wvl_ref, bif_ref, mgn_ref) = wts
    yl_ref, yg_ref, ym_ref = ys
    xpl_s, xpm_s, a_s, u_s, h_s, s_s, c_s, n_s, m_s = scr
    rows = lx_ref.shape[0]

    u, l_tail = _conv_chunk(xpl_s, lx_ref[...], lcw_ref[...], lcb_ref[...], rows)
    for blk in range(LRU_BLOCKS):
        sl = slice(blk * LANE, (blk + 1) * LANE)
        a_s[:, sl], u_s[:, sl] = _lru_gates(u[:, sl], wa_ref[blk], wx_ref[blk], ba_ref[:, sl], bx_ref[:, sl],
                                            lam_ref[:, sl])
    t8 = lax.broadcasted_iota(jnp.int32, (SUBLANE, LRU_W), 0)
    h = h_s[...]
    for g in range(rows // SUBLANE):
        r8 = slice(g * SUBLANE, (g + 1) * SUBLANE)
        a8, u8 = a_s[r8, :], u_s[r8, :]
        for s in (1, 2, 4):
            u8 = jnp.where(t8 >= s, u8 + a8 * pltpu.roll(u8, s, 0), u8)
            a8 = jnp.where(t8 >= s, a8 * pltpu.roll(a8, s, 0), a8)
        h8 = u8 + a8 * h
        a_s[r8, :] = h8
        h = h8[SUBLANE - 1:SUBLANE, :]
    h_s[...] = h
    yl_ref[...] = a_s[...].astype(BF16)

    q, k, v, gate = q_ref[...], k_ref[...], v_ref[...], gate_ref[...]
    small = small_ref[...]
    logdec = _log_sigmoid(_dot(small, wg2_ref[...]) + bg_ref[...]) * (1.0 / GLA_TAU)
    mx = mx_ref[...]
    conv, m_tail = _conv_chunk(xpm_s, mx, mcw_ref[...], mcb_ref[...], rows)
    mc = _silu(conv)
    mq = _blockdiag3(mc, wqh_ref, wql_ref)
    mk = _blockdiag3(mc, wkh_ref, wkl_ref) * (ML_DH ** -0.5)
    mv = _blockdiag3(mx, wvh_ref, wvl_ref)
    gates = small + bif_ref[...]
    gates_t = gates.T
    lf = _log_sigmoid(gates)
    lf_t = _log_sigmoid(gates_t)
    mo = mo_ref[...]
    tri = _tri(L)
    tri_b = tri.astype(BF16)
    r_i = lax.broadcasted_iota(jnp.int32, (L, L), 0)
    c_i = lax.broadcasted_iota(jnp.int32, (L, L), 1)
    triu_b = (r_i <= c_i).astype(BF16)

    for cc in range(rows // L):
        rs = slice(cc * L, (cc + 1) * L)
        bcum = _cumsum_rows(tri_b, logdec[rs, :])
        for hh in range(GLA_H):
            ks = slice(hh * GLA_DK, (hh + 1) * GLA_DK)
            vs = slice(hh * GLA_DV, (hh + 1) * GLA_DV)
            bh = bcum[:, ks]
            k_h, v_h = k[rs, ks], v[rs, vs]
            b_mid = bh[L // 2 - 1:L // 2, :]
            qe = (q[rs, ks] * (GLA_DK ** -0.5)) * jnp.exp(bh - b_mid)
            ke = k_h * jnp.exp(b_mid - bh)
            a = jnp.where(tri, _dot_nt(qe, ke), 0.0)
            s_h = s_s[hh]
            bl = bh[L - 1:L, :]
            cols = jnp.exp(jnp.concatenate([jnp.broadcast_to(b_mid, (SUBLANE // 2, GLA_DK)),
                                            jnp.broadcast_to(bl, (SUBLANE // 2, GLA_DK))], axis=0)).T
            mid_col, dec_col = cols[:, 0:1], cols[:, SUBLANE // 2:SUBLANE // 2 + 1]
            o = _dot(qe, mid_col * s_h) + _dot(a, v_h)
            kd = k_h * jnp.exp(bl - bh)
            s_s[hh] = dec_col * s_h + _dot(kd.T, v_h)
            yg_ref[rs, vs] = (_rms(o, ggn_ref[...]) * _silu(gate[rs, vs])).astype(BF16)

        f_col = _cumsum_rows(tri_b, lf[rs, :])
        f_row = _cumsum_lanes(lf_t[:, rs], triu_b)
        for hh in range(ML_H):
            sl = slice(hh * ML_DH, (hh + 1) * ML_DH)
            qh, kh, vh = mq[rs, sl], mk[rs, sl], mv[rs, sl]
            fc = f_col[:, S_FG + hh:S_FG + hh + 1]
            fr = f_row[S_FG + hh:S_FG + hh + 1, :]
            igr = gates_t[S_IG + hh:S_IG + hh + 1, rs]
            m_h = m_s[hh:hh + 1, 0:1]
            dm = jnp.where(tri, fc - fr + igr, NEG_BIG)
            inter = fc + m_h
            mt = jnp.maximum(inter, jnp.max(dm, axis=-1, keepdims=True))
            ci = jnp.exp(inter - mt)
            s = _dot_nt(qh, kh) * jnp.exp(dm - mt)
            c_h = c_s[hh]
            n_h = n_s[hh:hh + 1, :]
            num = ci * _dot(qh, c_h) + _dot(s, vh)
            den = ci * jnp.sum(qh * n_h, axis=-1, keepdims=True) + jnp.sum(s, axis=-1, keepdims=True)
            hcell = num / jnp.maximum(jnp.abs(den), jnp.exp(-mt))
            fl = fr[:, L - 1:L]
            dj = fl - fr + igr
            m_new = jnp.maximum(fl + m_h, jnp.max(dj, axis=-1, keepdims=True))
            cs = jnp.exp(fl + m_h - m_new)
            wj = jnp.exp(dj - m_new)
            c_s[hh] = cs * c_h + _dot(kh.T * wj, vh)
            n_s[hh:hh + 1, :] = cs * n_h + _dot(jnp.broadcast_to(wj, (SUBLANE, L)), kh)[0:1, :]
            m_s[hh:hh + 1, :] = jnp.broadcast_to(m_new, (1, LANE))
            ym_ref[rs, sl] = (jax.nn.sigmoid(mo[rs, sl]) * _rms(hcell, mgn_ref[...])).astype(BF16)

    return h, l_tail, m_tail


def _prompt_mixer(proj3, small3, layer, w, prev, *, L, rows, nseq):
    b, t, _ = proj3.shape
    kw, vw = GLA_H * GLA_DK, GLA_H * GLA_DV
    nblk = ML_W // LANE
    blk3 = lambda width, cb: pl.BlockSpec((nseq, rows, width), lambda bi, c: (bi, c, cb))
    lw3 = lambda s: pl.BlockSpec((None,) + s, lambda bi, c: (layer, 0, 0))
    lw4 = lambda s: pl.BlockSpec((None,) + s, lambda bi, c: (layer, 0, 0, 0))
    st3 = lambda s: pl.BlockSpec((None, nseq) + s, lambda bi, c: (layer, bi, 0, 0))
    st4 = lambda s: pl.BlockSpec((None, nseq) + s, lambda bi, c: (layer, bi, 0, 0, 0))
    in_specs = [
        blk3(LRU_W, C_LRU // LRU_W), blk3(kw, C_GQ // kw), blk3(kw, C_GK // kw), blk3(vw, C_GV // vw),
        blk3(vw, C_GGATE // vw), blk3(ML_W, C_MX // ML_W), blk3(ML_W, C_MO // ML_W), blk3(LANE, 0),
        lw3((CONV_W, LRU_W)), lw3((1, LRU_W)), lw4((LRU_BLOCKS, LANE, LANE)), lw3((1, LRU_W)),
        lw4((LRU_BLOCKS, LANE, LANE)), lw3((1, LRU_W)), lw3((1, LRU_W)),
        lw3((LANE, kw)), lw3((1, kw)), lw3((1, GLA_DV)),
        lw3((CONV_W, ML_W)), lw3((1, ML_W)), *([lw4((nblk, LANE, LANE))] * 6), lw3((1, LANE)), lw3((1, ML_DH)),
    ]
    args = [proj3] * 7 + [small3,
                          w["lru_cw"], w["lru_cb"], w["lru_wa"], w["lru_ba"], w["lru_wx"], w["lru_bx"], w["lru_lam"],
                          w["wg2"], w["bg"], w["gla_gn"],
                          w["ml_cw"], w["ml_cb"], *w["ml_wq"], *w["ml_wk"], *w["ml_wv"], w["ml_bif"], w["ml_gn"]]
    aliases = {}
    if prev is not None:
        aliases = {len(in_specs) + j: 3 + j for j in range(len(prev))}
        in_specs += [pl.BlockSpec(memory_space=pl.ANY)] * len(prev)
        args += list(prev)
    sbytes = GLA_H * GLA_DK * GLA_DV * 4
    cbytes = ML_H * ML_DH * ML_DH * 4
    vmem = nseq * (2 * rows * (7 * 1024 * 4 + 3 * 1024 * 2) + 3 * (sbytes + cbytes) + 6 * rows * 1024 * 4)
    vmem += 30 * rows * 1024 * 4 + (12 << 20)
    sc = lambda *s: pltpu.VMEM((nseq,) + s, F32)
    return pl.pallas_call(
        functools.partial(_prompt_mixer_kernel, L=L),
        grid=(b // nseq, t // rows),
        in_specs=in_specs,
        out_specs=[blk3(LRU_W, 0), blk3(vw, 0), blk3(ML_W, 0),
                   st3((1, LRU_W)), st3((CONV_W - 1, LRU_W)), st4((GLA_H, GLA_DK, GLA_DV)),
                   st4((ML_H, ML_DH, ML_DH)), st3((ML_H, ML_DH)), st3((ML_H, LANE)), st3((CONV_W - 1, ML_W))],
        out_shape=[SDS((b, t, LRU_W), BF16), SDS((b, t, vw), BF16), SDS((b, t, ML_W), BF16),
                   SDS((DEPTH, b, 1, LRU_W), F32), SDS((DEPTH, b, CONV_W - 1, LRU_W), F32),
                   SDS((DEPTH, b, GLA_H, GLA_DK, GLA_DV), F32), SDS((DEPTH, b, ML_H, ML_DH, ML_DH), F32),
                   SDS((DEPTH, b, ML_H, ML_DH), F32), SDS((DEPTH, b, ML_H, LANE), F32),
                   SDS((DEPTH, b, CONV_W - 1, ML_W), F32)],
        scratch_shapes=[sc(SUBLANE + rows, LRU_W), sc(SUBLANE + rows, ML_W), sc(rows, LRU_W), sc(rows, LRU_W),
                        sc(1, LRU_W), sc(GLA_H, GLA_DK, GLA_DV), sc(ML_H, ML_DH, ML_DH), sc(ML_H, ML_DH),
                        sc(ML_H, LANE)],
        input_output_aliases=aliases,
        compiler_params=_params(("parallel", "arbitrary"), vmem),
        name="prompt_mixer",
    )(*args)


def _tmask(shape):
    return lax.broadcasted_iota(jnp.int32, shape, 0) % T_STEP


def _down(x, s):
    return x if s == 0 else pltpu.roll(x, s, 0)


def _up(x, s):
    return x if s == 0 else pltpu.roll(x, x.shape[0] - s, 0)


def _seg_cumsum(x):
    t = _tmask(x.shape)
    out = x
    for s in range(1, T_STEP):
        out = out + jnp.where(t >= s, _down(x, s), 0.0)
    return out


def _seg_last(x):
    t = _tmask(x.shape)
    out = x
    for s in range(1, T_STEP):
        out = jnp.where(t == T_STEP - 1 - s, _up(x, s), out)
    return out


def _seg_allreduce(x, op):
    t = _tmask(x.shape)
    y = op(x, jnp.where(t % 2 == 1, _down(x, 1), _up(x, 1)))
    return op(y, jnp.where(t >= 2, _down(y, 2), _up(y, 2)))


def _conv_rows(x, e, w, b):
    t = _tmask(x.shape)
    acc = b + x * w[CONV_W - 1:CONV_W]
    for k in range(1, CONV_W):
        hist = jnp.where(t >= k, _down(x, k), _up(e, CONV_W - 1 - k))
        acc = acc + hist * w[CONV_W - 1 - k:CONV_W - k]
    return acc


def _col(x, lane):
    li = lax.broadcasted_iota(jnp.int32, x.shape, 1)
    return jnp.sum(jnp.where(li == lane, x, 0.0), axis=-1, keepdims=True)


def _sample_mixer_kernel(lx_ref, q_ref, k_ref, v_ref, gate_ref, mx_ref, mo_ref, small_ref,
                         el_ref, em_ref, h0_ref, n0_ref, m0_ref, s0_ref, c0_ref,
                         lcw_ref, lcb_ref, wa_ref, ba_ref, wx_ref, bx_ref, lam_ref,
                         wg2_ref, bg_ref, ggn_ref,
                         mcw_ref, mcb_ref, wqh_ref, wql_ref, wkh_ref, wkl_ref, wvh_ref, wvl_ref, bif_ref, mgn_ref,
                         *rest):
    yl_ref, yg_ref, ym_ref, hrow_ref, nrow_ref, mrow_ref, st_ref, ct_ref = rest[-8:]
    rows = lx_ref.shape[0]
    n_pairs = rows // SUBLANE
    seq_per_tile = SUBLANE // T_STEP
    t_col = _tmask((rows, 1))
    row8 = lax.broadcasted_iota(jnp.int32, (SUBLANE, 1), 0)

    u = _conv_rows(lx_ref[...], el_ref[...], lcw_ref[...], lcb_ref[...])
    a_blocks, g_blocks = [], []
    for blk in range(LRU_BLOCKS):
        sl = slice(blk * LANE, (blk + 1) * LANE)
        a_b, g_b = _lru_gates(u[:, sl], wa_ref[blk], wx_ref[blk], ba_ref[:, sl], bx_ref[:, sl], lam_ref[:, sl])
        a_blocks.append(a_b)
        g_blocks.append(g_b)
    a = jnp.concatenate(a_blocks, axis=1)
    t_w = _tmask(a.shape)
    g = jnp.concatenate(g_blocks, axis=1) + jnp.where(t_w == 0, a * h0_ref[...], 0.0)
    a1 = jnp.where(t_w >= 1, a * _down(a, 1), a)
    g1 = jnp.where(t_w >= 1, g + a * _down(g, 1), g)
    h = jnp.where(t_w >= 2, g1 + a1 * _down(g1, 2), g1)
    yl_ref[...] = h.astype(BF16)
    hrow_ref[...] = h

    q, k, v = q_ref[...], k_ref[...], v_ref[...]
    logdec = _log_sigmoid(_dot(small_ref[...], wg2_ref[...]) + bg_ref[...]) * (1.0 / GLA_TAU)
    bc = _seg_cumsum(logdec)
    bl = _seg_last(bc)
    qs = q * (GLA_DK ** -0.5)
    o_heads = [jnp.zeros((rows, GLA_DV), F32) for _ in range(GLA_H)]
    for s in range(T_STEP):
        prod = qs * _down(k, s) * jnp.exp(bc - _down(bc, s))
        v_s = _down(v, s)
        for hh in range(GLA_H):
            a_sh = jnp.sum(prod[:, hh * GLA_DK:(hh + 1) * GLA_DK], axis=-1, keepdims=True)
            a_sh = jnp.where(t_col >= s, a_sh, 0.0)
            o_heads[hh] = o_heads[hh] + a_sh * v_s[:, hh * GLA_DV:(hh + 1) * GLA_DV]
    qe = qs * jnp.exp(bc)
    kd = k * jnp.exp(bl - bc)
    dec = jnp.exp(bl)
    gate = gate_ref[...]
    for hh in range(GLA_H):
        ks = slice(hh * GLA_DK, (hh + 1) * GLA_DK)
        vs = slice(hh * GLA_DV, (hh + 1) * GLA_DV)
        o_state = []
        for p in range(n_pairs):
            r8 = slice(p * SUBLANE, (p + 1) * SUBLANE)
            kd_t = kd[r8, ks].T
            dec_t = dec[r8, ks].T
            res = None
            for j in range(seq_per_tile):
                b = p * seq_per_tile + j
                s_b = s0_ref[b, hh]
                r_j = _dot(qe[r8, ks], s_b)
                res = r_j if res is None else jnp.where(row8 // T_STEP == j, r_j, res)
                v_j = jnp.where(row8 // T_STEP == j, v[r8, vs], 0.0)
                last = j * T_STEP + T_STEP - 1
                st_ref[b, hh] = dec_t[:, last:last + 1] * s_b + _dot(kd_t, v_j)
            o_state.append(res)
        o = o_heads[hh] + jnp.concatenate(o_state, axis=0)
        yg_ref[:, vs] = (_rms(o, ggn_ref[...]) * _silu(gate[:, vs])).astype(BF16)

    mx = mx_ref[...]
    mc = _silu(_conv_rows(mx, em_ref[...], mcw_ref[...], mcb_ref[...]))
    mq = _blockdiag3(mc, wqh_ref, wql_ref)
    mk = _blockdiag3(mc, wkh_ref, wkl_ref) * (ML_DH ** -0.5)
    mv = _blockdiag3(mx, wvh_ref, wvl_ref)
    gts = small_ref[...] + bif_ref[...]
    ig = pltpu.roll(gts, S_FG - S_IG, 1)
    fcum = _seg_cumsum(_log_sigmoid(gts))
    flast = _seg_last(fcum)
    m0 = m0_ref[...]
    t_g = _tmask(gts.shape)
    inter = fcum + m0
    dms = [jnp.where(t_g >= s, fcum - _down(fcum, s) + _down(ig, s), NEG_BIG) for s in range(T_STEP)]
    mt = inter
    for dm in dms:
        mt = jnp.maximum(mt, dm)
    ci_t = jnp.exp(inter - mt)
    emt_t = jnp.exp(-mt)
    w_t = [jnp.exp(dm - mt) for dm in dms]
    dj = flast - fcum + ig
    m_new = jnp.maximum(flast + m0, _seg_allreduce(dj, jnp.maximum))
    cs_t = jnp.exp(flast + m0 - m_new)
    wj_t = jnp.exp(dj - m_new)
    mrow_ref[...] = m_new
    mo = mo_ref[...]
    n0 = n0_ref[...]
    for hh in range(ML_H):
        sl = slice(hh * ML_DH, (hh + 1) * ML_DH)
        lane = S_FG + hh
        qh, kh, vh = mq[:, sl], mk[:, sl], mv[:, sl]
        ci, emt, cs, wj = _col(ci_t, lane), _col(emt_t, lane), _col(cs_t, lane), _col(wj_t, lane)
        num = jnp.zeros((rows, ML_DH), F32)
        den = ci * jnp.sum(qh * n0[:, sl], axis=-1, keepdims=True)
        for s in range(T_STEP):
            sc = jnp.sum(qh * _down(kh, s), axis=-1, keepdims=True) * _col(w_t[s], lane)
            num = num + sc * _down(vh, s)
            den = den + sc
        kw = kh * wj
        nrow_ref[:, sl] = cs * n0[:, sl] + _seg_allreduce(kw, jnp.add)
        qc = []
        for p in range(n_pairs):
            r8 = slice(p * SUBLANE, (p + 1) * SUBLANE)
            kw_t = kw[r8, :].T
            res = None
            for j in range(seq_per_tile):
                b = p * seq_per_tile + j
                c_b = c0_ref[b, hh]
                r_j = _dot(qh[r8, :], c_b)
                res = r_j if res is None else jnp.where(row8 // T_STEP == j, r_j, res)
                v_j = jnp.where(row8 // T_STEP == j, vh[r8, :], 0.0)
                last = p * SUBLANE + j * T_STEP + T_STEP - 1
                ct_ref[b, hh] = cs[last:last + 1, :] * c_b + _dot(kw_t, v_j)
            qc.append(res)
        num = num + ci * jnp.concatenate(qc, axis=0)
        hcell = num / jnp.maximum(jnp.abs(den), emt)
        ym_ref[:, sl] = (jax.nn.sigmoid(mo[:, sl]) * _rms(hcell, mgn_ref[...])).astype(BF16)


def _sample_mixer(proj, small, el, em, h0e, n0e, m0e, s_state, c_state, layer, w, prev):
    n = proj.shape[0]
    nseq = n // T_STEP
    bb = STEP_BB
    rows = bb * T_STEP
    row = lambda width, blk: pl.BlockSpec((rows, width), lambda i: (i, blk))
    lrow = lambda width: pl.BlockSpec((None, rows, width), lambda i: (layer, i, 0))
    lw3 = lambda s: pl.BlockSpec((None,) + s, lambda i: (layer, 0, 0))
    lw4 = lambda s: pl.BlockSpec((None,) + s, lambda i: (layer, 0, 0, 0))
    sspec = pl.BlockSpec((None, bb, GLA_H, GLA_DK, GLA_DV), lambda i: (layer, i, 0, 0, 0))
    cspec = pl.BlockSpec((None, bb, ML_H, ML_DH, ML_DH), lambda i: (layer, i, 0, 0, 0))
    nblk = ML_W // LANE
    kw, vw = GLA_H * GLA_DK, GLA_H * GLA_DV
    in_specs = [
        row(LRU_W, C_LRU // LRU_W), row(kw, C_GQ // kw), row(kw, C_GK // kw), row(vw, C_GV // vw),
        row(vw, C_GGATE // vw), row(ML_W, C_MX // ML_W), row(ML_W, C_MO // ML_W), row(LANE, 0),
        lrow(LRU_W), lrow(ML_W), lrow(LRU_W), lrow(ML_W), lrow(LANE), sspec, cspec,
        lw3((CONV_W, LRU_W)), lw3((1, LRU_W)), lw4((LRU_BLOCKS, LANE, LANE)), lw3((1, LRU_W)),
        lw4((LRU_BLOCKS, LANE, LANE)), lw3((1, LRU_W)), lw3((1, LRU_W)),
        lw3((LANE, kw)), lw3((1, kw)), lw3((1, GLA_DV)),
        lw3((CONV_W, ML_W)), lw3((1, ML_W)), *([lw4((nblk, LANE, LANE))] * 6), lw3((1, LANE)), lw3((1, ML_DH)),
    ]
    args = [proj] * 7 + [small, el, em, h0e, n0e, m0e, s_state, c_state,
                         w["lru_cw"], w["lru_cb"], w["lru_wa"], w["lru_ba"], w["lru_wx"], w["lru_bx"], w["lru_lam"],
                         w["wg2"], w["bg"], w["gla_gn"],
                         w["ml_cw"], w["ml_cb"], *w["ml_wq"], *w["ml_wk"], *w["ml_wv"], w["ml_bif"], w["ml_gn"]]
    aliases = {}
    if prev is not None:
        aliases = {len(in_specs) + j: 3 + j for j in range(len(prev))}
        in_specs += [pl.BlockSpec(memory_space=pl.ANY)] * len(prev)
        args += list(prev)
    sbytes = bb * GLA_H * GLA_DK * GLA_DV * 4
    cbytes = bb * ML_H * ML_DH * ML_DH * 4
    vmem = 4 * (sbytes + cbytes) + 2 * rows * (7 * 1024 + 8 * 1024) * 4 + 80 * rows * 1024 * 4 + (12 << 20)
    return pl.pallas_call(
        _sample_mixer_kernel,
        grid=(nseq // bb,),
        in_specs=in_specs,
        out_specs=[row(LRU_W, 0), row(vw, 0), row(ML_W, 0), lrow(LRU_W), lrow(ML_W), lrow(LANE), sspec, cspec],
        out_shape=[SDS((n, LRU_W), BF16), SDS((n, vw), BF16), SDS((n, ML_W), BF16),
                   SDS((DEPTH, n, LRU_W), F32), SDS((DEPTH, n, ML_W), F32), SDS((DEPTH, n, LANE), F32),
                   SDS((DEPTH, nseq, GLA_H, GLA_DK, GLA_DV), F32), SDS((DEPTH, nseq, ML_H, ML_DH, ML_DH), F32)],
        input_output_aliases=aliases,
        compiler_params=_params(("arbitrary",), vmem),
        name="sample_mixer",
    )(*args)


def _prep_weights(p):
    w_in = p["w_in"]
    c1 = C_GGATE
    c2 = C_MG + W_SHIFT1
    w_small = jnp.concatenate(
        [w_in[:, :, c1:c1 + GLA_RANK], w_in[:, :, c2:c2 + 2 * ML_H],
         jnp.zeros((DEPTH, D, LANE - GLA_RANK - 2 * ML_H), F32)], axis=-1)
    wg2 = jnp.concatenate(
        [p["gla_w_g2"], jnp.zeros((DEPTH, LANE - GLA_RANK, GLA_H * GLA_DK), F32)], axis=1)
    blk_of = jnp.arange(LANE) // ML_BS
    on_diag = blk_of[:, None] == blk_of[None, :]

    def dense_bd(wb):
        rows = wb.reshape(DEPTH, ML_W // LANE, LANE, ML_BS)
        dense = jnp.where(on_diag, jnp.tile(rows, (1, 1, 1, LANE // ML_BS)), 0.0)
        hi = dense.astype(BF16)
        return hi, (dense - hi.astype(F32)).astype(BF16)

    bif = jnp.zeros((DEPTH, 1, LANE), F32).at[:, 0, S_IG:S_IG + 2 * ML_H].set(p["ml_b_if"])
    r3 = lambda a: a.reshape(DEPTH, 1, a.shape[-1])
    return dict(
        w_small=w_small, wg2=wg2, bg=r3(p["gla_b_g"]), gla_gn=r3(p["gla_g_norm"]),
        g1=r3(p["g_norm1"]), g2=r3(p["g_norm2"]), gf=p["g_final"].reshape(1, D),
        lru_cw=p["lru_conv_w"], lru_cb=r3(p["lru_conv_b"]), lru_wa=p["lru_w_a"], lru_ba=r3(p["lru_b_a"]),
        lru_wx=p["lru_w_x"], lru_bx=r3(p["lru_b_x"]), lru_lam=r3(p["lru_lam"]),
        ml_cw=p["ml_conv_w"], ml_cb=r3(p["ml_conv_b"]), ml_wq=dense_bd(p["ml_w_q"]), ml_wk=dense_bd(p["ml_w_k"]),
        ml_wv=dense_bd(p["ml_w_v"]), ml_bif=bif, ml_gn=r3(p["ml_g_norm"]),
        w_in_t=jnp.swapaxes(w_in, 1, 2), w_br=(p["w_br_lru"], p["w_br_gla"], p["w_br_ml"]), w_out=p["w_out"], w_ff1=p["w_ff1"],
        w_ff2=p["w_ff2"],
    )


def _trunk(xp3, xs3, mod_p, mod_s, states, w):
    bp, tp, _ = xp3.shape
    bs, ts, _ = xs3.shape
    n_p, n_s = bp * tp, bs * ts
    tm_p = min(tp, 1024)
    tm_f = min(tp, 512)
    s_h, s_cv, s_gla, s_c, s_n, s_m, s_mcv = states
    pad_t = lambda a: jnp.pad(a, ((0, 0), (0, 0), (0, ts - a.shape[2]), (0, 0))).reshape(DEPTH, n_s, a.shape[-1])
    el = pad_t(s_cv)
    em = pad_t(s_mcv)
    h0e = pad_t(s_h[:, :, None, :])
    n0e = jnp.broadcast_to(s_n.reshape(DEPTH, bs, 1, ML_W), (DEPTH, bs, ts, ML_W)).reshape(DEPTH, n_s, ML_W)
    m_l = jnp.pad(s_m, ((0, 0), (0, 0), (S_FG, LANE - S_FG - ML_H)))
    m0e = jnp.broadcast_to(m_l[:, :, None, :], (DEPTH, bs, ts, LANE)).reshape(DEPTH, n_s, LANE)
    x_p = xp3.reshape(n_p, D)
    x_s = xs3.reshape(n_s, D)
    prev_p = prev_s = None
    conv_l, conv_m = [], []
    keep = ts - (CONV_W - 1)
    for l in range(DEPTH):
        last = l == DEPTH - 1
        proj_s, small_s, w_main = _in_proj_cast(x_s, mod_s, l, w["g1"], w["w_in_t"], w["w_small"], tn=512, rpg=n_s)
        proj_p, small_p = _in_proj(x_p, mod_p, l, w["g1"], w_main, w["w_small"], tm=tm_p, tn=1024, rpg=tp, r=1)
        outs_s = _sample_mixer(proj_s, small_s, el, em, h0e, n0e, m0e, s_gla, s_c, l, w, prev_s)
        prev_s = outs_s[3:]
        outs_p = _prompt_mixer(proj_p.reshape(bp, tp, N_MAIN), small_p.reshape(bp, tp, LANE), l, w, prev_p,
                               L=min(tp, CHUNK), rows=min(tp, PROMPT_ROWS), nseq=PROMPT_SEQS)
        prev_p = outs_p[3:]
        conv_l.append(proj_s[:, C_LRU:C_LRU + LRU_W].reshape(bs, ts, LRU_W)[:, keep:])
        conv_m.append(proj_s[:, C_MX:C_MX + ML_W].reshape(bs, ts, ML_W)[:, keep:])
        merged_s, *w_br = _merge(*outs_s[:3], proj_s, l, *w["w_br"], tm=n_s, tn=512, cast=True)
        (merged_p,) = _merge(*(y.reshape(n_p, y.shape[-1]) for y in outs_p[:3]), proj_p, l, *w_br, tm=tm_p, tn=512,
                             cast=False)
        x_s, w_out = _out_proj(merged_s, x_s, mod_s, l, w["w_out"], tm=n_s, tn=512, rpg=n_s, r=n_s, cast=True)
        (x_p,) = _out_proj(merged_p, x_p, mod_p, l, w_out, tm=tm_p, tn=512, rpg=tp, r=1, cast=False)
        x_s, w_ff1, w_ff2 = _ffn(x_s, mod_s, l, w["g2"], w["gf"], w["w_ff1"], w["w_ff2"], tm=n_s, tf=512, rpg=n_s,
                                 r=n_s, final_norm=last, cast=True)
        (x_p,) = _ffn(x_p, mod_p, l, w["g2"], w["gf"], w_ff1, w_ff2, tm=tm_f, tf=1024, rpg=tp, r=1, final_norm=last,
                      cast=False)
    h_t, lcv_t, s_t, c_t, n_t, m_t, mcv_t = prev_p
    p_states = [h_t.reshape(DEPTH, bp, LRU_W), lcv_t, s_t, c_t, n_t, m_t[..., 0], mcv_t]
    hrow, nrow, mrow, s_out, c_out = prev_s
    t_last = ts - 1
    s_states = [hrow.reshape(DEPTH, bs, ts, LRU_W)[:, :, t_last], jnp.stack(conv_l), s_out, c_out,
                nrow.reshape(DEPTH, bs, ts, ML_H, ML_DH)[:, :, t_last],
                mrow.reshape(DEPTH, bs, ts, LANE)[:, :, t_last, S_FG:S_FG + ML_H], jnp.stack(conv_m)]
    return x_p.reshape(bp, tp, D), x_s.reshape(bs, ts, D), p_states, s_states


def kernel(x_prompt, x_sample, c_prompt, c_sample, state_lru_h, state_lru_conv, state_gla, state_mlstm_C, state_mlstm_n, state_mlstm_m, state_mlstm_conv, w_ada, b_ada, g_norm1, g_norm2, w_in, lru_conv_w, lru_conv_b, lru_w_a, lru_b_a, lru_w_x, lru_b_x, lru_lam, gla_w_g2, gla_b_g, gla_g_norm, ml_conv_w, ml_conv_b, ml_w_q, ml_w_k, ml_w_v, ml_b_if, ml_g_norm, w_br_lru, w_br_gla, w_br_ml, w_out, w_ff1, w_ff2, g_final):
    p = dict(g_norm1=g_norm1, g_norm2=g_norm2, w_in=w_in, lru_conv_w=lru_conv_w, lru_conv_b=lru_conv_b,
             lru_w_a=lru_w_a, lru_b_a=lru_b_a, lru_w_x=lru_w_x, lru_b_x=lru_b_x, lru_lam=lru_lam,
             gla_w_g2=gla_w_g2, gla_b_g=gla_b_g, gla_g_norm=gla_g_norm, ml_conv_w=ml_conv_w, ml_conv_b=ml_conv_b,
             ml_w_q=ml_w_q, ml_w_k=ml_w_k, ml_w_v=ml_w_v, ml_b_if=ml_b_if, ml_g_norm=ml_g_norm,
             w_br_lru=w_br_lru, w_br_gla=w_br_gla, w_br_ml=w_br_ml, w_out=w_out, w_ff1=w_ff1, w_ff2=w_ff2,
             g_final=g_final)
    w = _prep_weights(p)
    bp = x_prompt.shape[0]
    bs, ts, _ = x_sample.shape
    assert ts == T_STEP and bs % STEP_BB == 0
    ns = bs * ts
    pad = (-(ns + bp)) % SUBLANE
    c_all = jnp.concatenate([jnp.repeat(c_sample, ts, axis=0), c_prompt, jnp.zeros((pad, D), F32)], axis=0)
    mod_all = _ada(c_all, w_ada, b_ada)
    mod_s = mod_all.reshape(DEPTH, 1, ns + bp + pad, N_MOD * D)
    mod_p = mod_all[:, ns:ns + bp].reshape(DEPTH, bp, 1, N_MOD * D)
    y_p, y_s, ps, ss = _trunk(x_prompt, x_sample, mod_p, mod_s,
                              (state_lru_h, state_lru_conv, state_gla, state_mlstm_C, state_mlstm_n, state_mlstm_m,
                               state_mlstm_conv), w)
    return (y_p, y_s, *ps, *ss)
```

```python
import functools

import jax
import jax.numpy as jnp
from jax import lax
from jax.experimental import pallas as pl
from jax.experimental.pallas import tpu as pltpu

F32, BF16 = jnp.float32, jnp.bfloat16
SDS = jax.ShapeDtypeStruct

D = 2048
DEPTH = 2
LRU_W = 1024
LRU_BLOCKS = 8
LRU_C = 8.0
CONV_W = 4
GLA_H = 4
GLA_DK = 128
GLA_DV = 256
GLA_RANK = 16
GLA_TAU = 16.0
ML_H = 4
ML_W = 1024
ML_DH = 256
ML_BS = 4
CHUNK = 128
ML_CHUNK = 256
PROMPT_ROWS = 256
PROMPT_SEQS = 1
D_FF = 4 * D
EPS = 1e-6
N_MOD = 6
T_STEP = 4
STEP_BB = 4

LANE = 128
SUBLANE = 8
VMEM_LIMIT_CAP = 60 * 1024 * 1024

C_LRU, C_GQ, C_GK, C_GV, C_GGATE, C_MX, C_MO, C_MG = 0, 1024, 1536, 2048, 3072, 4096, 5120, 6144
N_MAIN = 12288
W_SHIFT1 = GLA_RANK
W_SHIFT2 = GLA_RANK + 2 * ML_H
S_GLR = 0
S_IG = 16
S_FG = 20
NEG_BIG = -1e30


def _params(sem, vmem_bytes):
    return pltpu.CompilerParams(dimension_semantics=sem, vmem_limit_bytes=int(min(vmem_bytes, VMEM_LIMIT_CAP)))


def _dot(a, b):
    return jnp.dot(a.astype(BF16), b.astype(BF16), preferred_element_type=F32)


def _dot_nt(a, b):
    return lax.dot_general(a.astype(BF16), b.astype(BF16), (((1,), (1,)), ((), ())), preferred_element_type=F32)


def _split3(x):
    p0 = x.astype(BF16)
    r = x - p0.astype(F32)
    p1 = r.astype(BF16)
    return p0, p1, (r - p1.astype(F32)).astype(BF16)


def _cumsum_rows(tri_b, x):
    return sum(jnp.dot(tri_b, p, preferred_element_type=F32) for p in _split3(x))


def _cumsum_lanes(x, triu_b):
    return sum(jnp.dot(p, triu_b, preferred_element_type=F32) for p in _split3(x))


def _split(x):
    hi = x.astype(BF16)
    return hi, (x - hi.astype(F32)).astype(BF16)


def _dot3(x_hi, x_lo, w_hi, w_lo):
    d = lambda a, b: jnp.dot(a, b, preferred_element_type=F32)
    return d(x_hi, w_hi) + (d(x_lo, w_hi) + d(x_hi, w_lo))


def _log_sigmoid(z):
    return jnp.minimum(z, 0.0) - jnp.log1p(jnp.exp(-jnp.abs(z)))


def _silu(z):
    return z * jax.nn.sigmoid(z)


def _rms(x, g):
    return x * lax.rsqrt(jnp.mean(x * x, axis=-1, keepdims=True) + EPS) * g


def _tri(n):
    r = lax.broadcasted_iota(jnp.int32, (n, n), 0)
    c = lax.broadcasted_iota(jnp.int32, (n, n), 1)
    return r >= c


def _lru_gates(ub, wa, wx, ba, bx, lam):
    r = jax.nn.sigmoid(_dot(ub, wa) + ba)
    i = jax.nn.sigmoid(_dot(ub, wx) + bx)
    log_a = LRU_C * r * _log_sigmoid(lam)
    t = jnp.tanh(log_a)
    return jnp.exp(log_a), jnp.sqrt(-2.0 * t / (1.0 - t)) * (i * ub)


def _blockdiag3(x, wh_ref, wl_ref):
    xh, xl = _split(x)
    outs = []
    for blk in range(ML_W // LANE):
        sl = slice(blk * LANE, (blk + 1) * LANE)
        outs.append(_dot3(xh[:, sl], xl[:, sl], wh_ref[blk], wl_ref[blk]))
    return jnp.concatenate(outs, axis=1)


def _ada_kernel(c_ref, w_ref, b_ref, o_ref):
    o_ref[...] = _dot(_silu(c_ref[...]), w_ref[...]) + b_ref[...]


def _ada(c_all, w_ada, b_ada):
    m = c_all.shape[0]
    tn = 1024
    n_out = N_MOD * D
    return pl.pallas_call(
        _ada_kernel,
        grid=(DEPTH, n_out // tn),
        in_specs=[
            pl.BlockSpec((m, D), lambda l, n: (0, 0)),
            pl.BlockSpec((None, D, tn), lambda l, n: (l, 0, n)),
            pl.BlockSpec((None, 1, tn), lambda l, n: (l, 0, n)),
        ],
        out_specs=pl.BlockSpec((None, m, tn), lambda l, n: (l, 0, n)),
        out_shape=SDS((DEPTH, m, n_out), F32),
        compiler_params=_params(("parallel", "arbitrary"), 2 * (m * D + D * tn + m * tn) * 4 + (12 << 20)),
        name="ada",
    )(c_all, w_ada, b_ada.reshape(DEPTH, 1, n_out))


def _modspec(layer, comp, r, width, tm, rpg, jmap=None):
    nb = D // width
    jm = (lambda j: 0) if jmap is None else jmap
    if r == 1:
        return pl.BlockSpec((None, None, 1, width), lambda i, j: (layer, (i * tm) // rpg, 0, comp * nb + jm(j)))
    mode = pl.Buffered(1) if jmap is None else None
    return pl.BlockSpec((None, None, tm, width), lambda i, j: (layer, 0, i, comp * nb + jm(j)), pipeline_mode=mode)


def _norm_mod_store(xn_s, x_ref, g_ref, sc_ref, sh_ref):
    tm = x_ref.shape[0]
    rc = min(tm, 256)
    for c in range(tm // rc):
        rows = slice(c * rc, (c + 1) * rc)
        mrows = rows if sc_ref.shape[0] == tm else slice(None)
        xn = _rms(x_ref[rows, :], g_ref[...]) * (1.0 + sc_ref[mrows, :]) + sh_ref[mrows, :]
        xn_s[rows, :] = xn.astype(BF16)


def _in_proj_kernel(x_ref, sc_ref, sh_ref, g_ref, wm_ref, ws_ref, om_ref, os_ref, xn_s):
    @pl.when(pl.program_id(1) == 0)
    def _():
        _norm_mod_store(xn_s, x_ref, g_ref, sc_ref, sh_ref)
        os_ref[...] = jnp.dot(xn_s[...], ws_ref[...].astype(BF16), preferred_element_type=F32)

    om_ref[...] = jnp.dot(xn_s[...], wm_ref[...], preferred_element_type=F32)


def _in_proj(x, mod, layer, g1, w_main, w_small, *, tm, tn, rpg, r):
    n = x.shape[0]
    vmem = 2 * (tm * D * 4 + 2 * r * D * 4 + D * tn * 2 + D * LANE * 2 + tm * tn * 4 + tm * LANE * 4) + tm * D * 2
    vmem += 8 << 20
    return pl.pallas_call(
        _in_proj_kernel,
        grid=(n // tm, N_MAIN // tn),
        in_specs=[
            pl.BlockSpec((tm, D), lambda i, j: (i, 0)),
            _modspec(layer, 1, r, D, tm, rpg),
            _modspec(layer, 0, r, D, tm, rpg),
            pl.BlockSpec((None, 1, D), lambda i, j: (layer, 0, 0)),
            pl.BlockSpec((D, tn), lambda i, j: (0, j)),
            pl.BlockSpec((None, D, LANE), lambda i, j: (layer, 0, 0)),
        ],
        out_specs=[pl.BlockSpec((tm, tn), lambda i, j: (i, j)), pl.BlockSpec((tm, LANE), lambda i, j: (i, 0))],
        out_shape=[SDS((n, N_MAIN), F32), SDS((n, LANE), F32)],
        scratch_shapes=[pltpu.VMEM((tm, D), BF16)],
        compiler_params=_params(("parallel", "arbitrary"), vmem),
        name="in_proj",
    )(x, mod, mod, g1, w_main, w_small)


def _in_proj_cast_kernel(x_ref, sc_ref, sh_ref, g_ref, wt_ref, ws_ref, om_ref, os_ref, wc_ref, xn_s):
    @pl.when(pl.program_id(1) == 0)
    def _():
        _norm_mod_store(xn_s, x_ref, g_ref, sc_ref, sh_ref)
        os_ref[...] = jnp.dot(xn_s[...], ws_ref[...].astype(BF16), preferred_element_type=F32)

    wb = wt_ref[...].T.astype(BF16)
    wc_ref[...] = wb
    om_ref[...] = jnp.dot(xn_s[...], wb, preferred_element_type=F32)


def _in_proj_cast(x, mod, layer, g1, w_in_t, w_small, *, tn, rpg):
    n = x.shape[0]
    tm = n
    nb1, nb2 = C_GGATE // tn, C_MG // tn

    def w_rows(i, j):
        shift = jnp.where(j >= nb2, W_SHIFT2 // SUBLANE, jnp.where(j >= nb1, W_SHIFT1 // SUBLANE, 0))
        return (layer, (j * (tn // SUBLANE) + shift) * SUBLANE, 0)

    vmem = tm * D * 4 + 2 * tm * D * 4 + 2 * (tn * D * 4 + D * LANE * 4 + tm * tn * 4 + tm * LANE * 4 + D * tn * 2)
    vmem += tm * D * 2 + 3 * D * tn * 4 + (4 << 20)
    return pl.pallas_call(
        _in_proj_cast_kernel,
        grid=(1, N_MAIN // tn),
        in_specs=[
            pl.BlockSpec((tm, D), lambda i, j: (i, 0), pipeline_mode=pl.Buffered(1)),
            _modspec(layer, 1, tm, D, tm, rpg),
            _modspec(layer, 0, tm, D, tm, rpg),
            pl.BlockSpec((None, 1, D), lambda i, j: (layer, 0, 0)),
            pl.BlockSpec((None, pl.Element(tn), pl.Element(D)), w_rows),
            pl.BlockSpec((None, D, LANE), lambda i, j: (layer, 0, 0)),
        ],
        out_specs=[pl.BlockSpec((tm, tn), lambda i, j: (i, j)), pl.BlockSpec((tm, LANE), lambda i, j: (i, 0)),
                   pl.BlockSpec((D, tn), lambda i, j: (0, j))],
        out_shape=[SDS((n, N_MAIN), F32), SDS((n, LANE), F32), SDS((D, N_MAIN), BF16)],
        scratch_shapes=[pltpu.VMEM((tm, D), BF16)],
        compiler_params=_params(("arbitrary", "arbitrary"), vmem),
        name="in_proj_cast",
    )(x, mod, mod, g1, w_in_t, w_small)


def _wspec(shape, imap, layer, cast):
    if cast:
        return pl.BlockSpec((None,) + shape, lambda i, j: (layer,) + imap(i, j))
    return pl.BlockSpec(shape, imap)


def _merge_kernel(yl_ref, yg_ref, ym_ref, g0_ref, g1_ref, g2_ref, w0_ref, w1_ref, w2_ref, o_ref, *wc_refs):
    ws = [w_ref[...].astype(BF16) for w_ref in (w0_ref, w1_ref, w2_ref)]
    for wc_ref, wb in zip(wc_refs, ws):
        wc_ref[...] = wb
    acc = jax.nn.sigmoid(g0_ref[...]) * jnp.dot(yl_ref[...], ws[0], preferred_element_type=F32)
    acc += jax.nn.sigmoid(g1_ref[...]) * jnp.dot(yg_ref[...], ws[1], preferred_element_type=F32)
    acc += jax.nn.sigmoid(g2_ref[...]) * jnp.dot(ym_ref[...], ws[2], preferred_element_type=F32)
    o_ref[...] = acc.astype(BF16)


def _merge(y_lru, y_gla, y_ml, proj, layer, w_lru, w_gla, w_ml, *, tm, tn, cast):
    n = y_lru.shape[0]
    w = LRU_W
    gb = C_MG // tn
    nb = D // tn
    yspec = pl.BlockSpec((tm, w), lambda i, j: (i, 0))
    wspec = _wspec((w, tn), lambda i, j: (0, j), layer, cast)
    wbytes = 4 if cast else 2
    vmem = 2 * (3 * tm * w * 2 + 3 * tm * tn * 4 + 3 * w * tn * wbytes + tm * tn * 2) + 6 * tm * tn * 4 + (4 << 20)
    out_specs = [pl.BlockSpec((tm, tn), lambda i, j: (i, j))]
    out_shape = [SDS((n, D), BF16)]
    if cast:
        out_specs += [pl.BlockSpec((w, tn), lambda i, j: (0, j))] * 3
        out_shape += [SDS((w, D), BF16)] * 3
        vmem += 2 * 3 * w * tn * 2 + 3 * w * tn * 4
    return pl.pallas_call(
        _merge_kernel,
        grid=(n // tm, nb),
        in_specs=[
            yspec, yspec, yspec,
            pl.BlockSpec((tm, tn), lambda i, j: (i, gb + j)),
            pl.BlockSpec((tm, tn), lambda i, j: (i, gb + nb + j)),
            pl.BlockSpec((tm, tn), lambda i, j: (i, gb + 2 * nb + j)),
            wspec, wspec, wspec,
        ],
        out_specs=out_specs,
        out_shape=out_shape,
        compiler_params=_params(("arbitrary" if cast else "parallel", "arbitrary"), vmem),
        name="merge",
    )(y_lru, y_gla, y_ml, proj, proj, proj, w_lru, w_gla, w_ml)


def _out_proj_kernel(m_ref, w_ref, x_ref, gt_ref, o_ref, *wc_refs):
    wb = w_ref[...].astype(BF16)
    for wc_ref in wc_refs:
        wc_ref[...] = wb
    o_ref[...] = x_ref[...] + gt_ref[...] * jnp.dot(m_ref[...], wb, preferred_element_type=F32)


def _out_proj(merged, x, mod, layer, w_out, *, tm, tn, rpg, r, cast):
    n = x.shape[0]
    wbytes = 4 if cast else 2
    vmem = 2 * (tm * D * 2 + D * tn * wbytes + 2 * tm * tn * 4 + r * tn * 4) + 2 * tm * tn * 4 + (4 << 20)
    out_specs = [pl.BlockSpec((tm, tn), lambda i, j: (i, j))]
    out_shape = [SDS((n, D), F32)]
    if cast:
        out_specs.append(pl.BlockSpec((D, tn), lambda i, j: (0, j)))
        out_shape.append(SDS((D, D), BF16))
        vmem += 2 * D * tn * 2 + D * tn * 4
    return pl.pallas_call(
        _out_proj_kernel,
        grid=(n // tm, D // tn),
        in_specs=[
            pl.BlockSpec((tm, D), lambda i, j: (i, 0)),
            _wspec((D, tn), lambda i, j: (0, j), layer, cast),
            pl.BlockSpec((tm, tn), lambda i, j: (i, j)),
            _modspec(layer, 2, r, tn, tm, rpg, jmap=lambda j: j),
        ],
        out_specs=out_specs,
        out_shape=out_shape,
        compiler_params=_params(("arbitrary" if cast else "parallel", "arbitrary"), vmem),
        name="out_proj",
    )(merged, w_out, x, mod)


def _ffn_kernel(x_ref, sc_ref, sh_ref, gt_ref, g_ref, gf_ref, w1_ref, w2_ref, o_ref, *rest, final_norm):
    xn_s = rest[-1]
    f = pl.program_id(1)

    @pl.when(f == 0)
    def _():
        _norm_mod_store(xn_s, x_ref, g_ref, sc_ref, sh_ref)
        o_ref[...] = jnp.zeros_like(o_ref)

    w1 = w1_ref[...].astype(BF16)
    w2 = w2_ref[...].astype(BF16)
    for wc_ref, wb in zip(rest[:-1], (w1, w2)):
        wc_ref[...] = wb
    h = jnp.square(jnp.maximum(jnp.dot(xn_s[...], w1, preferred_element_type=F32), 0.0))
    o_ref[...] += jnp.dot(h.astype(BF16), w2, preferred_element_type=F32)

    @pl.when(f == pl.num_programs(1) - 1)
    def _():
        y = x_ref[...] + gt_ref[...] * o_ref[...]
        if final_norm:
            y = _rms(y, gf_ref[...])
        o_ref[...] = y


def _ffn(x, mod, layer, g2, g_final, w1, w2, *, tm, tf, rpg, r, final_norm, cast):
    n = x.shape[0]
    wbytes = 4 if cast else 2
    vmem = 2 * (2 * tm * D * 4 + 3 * r * D * 4 + 2 * D * tf * wbytes) + tm * D * 2 + 2 * tm * tf * 4 + 2 * tm * D * 4
    single = pl.Buffered(1) if n == tm else None
    out_specs = [pl.BlockSpec((tm, D), lambda i, f: (i, 0), pipeline_mode=single)]
    out_shape = [SDS((n, D), F32)]
    if cast:
        out_specs += [pl.BlockSpec((D, tf), lambda i, f: (0, f)), pl.BlockSpec((tf, D), lambda i, f: (f, 0))]
        out_shape += [SDS((D, D_FF), BF16), SDS((D_FF, D), BF16)]
        vmem += 2 * 2 * D * tf * 2
    return pl.pallas_call(
        functools.partial(_ffn_kernel, final_norm=final_norm),
        grid=(n // tm, D_FF // tf),
        in_specs=[
            pl.BlockSpec((tm, D), lambda i, f: (i, 0), pipeline_mode=single),
            _modspec(layer, 4, r, D, tm, rpg),
            _modspec(layer, 3, r, D, tm, rpg),
            _modspec(layer, 5, r, D, tm, rpg),
            pl.BlockSpec((None, 1, D), lambda i, f: (layer, 0, 0)),
            pl.BlockSpec((1, D), lambda i, f: (0, 0)),
            _wspec((D, tf), lambda i, f: (0, f), layer, cast),
            _wspec((tf, D), lambda i, f: (f, 0), layer, cast),
        ],
        out_specs=out_specs,
        out_shape=out_shape,
        scratch_shapes=[pltpu.VMEM((tm, D), BF16)],
        compiler_params=_params(("arbitrary" if cast else "parallel", "arbitrary"), vmem),
        name="ffn",
    )(x, mod, mod, mod, g2, g_final, w1, w2)


def _conv_chunk(xp_s, x, w, b, L):
    xp_s[SUBLANE:SUBLANE + L, :] = x
    acc = b + xp_s[5:5 + L, :] * w[0:1]
    for j in range(1, CONV_W):
        acc = acc + xp_s[5 + j:5 + j + L, :] * w[j:j + 1]
    tail = xp_s[5 + L:8 + L, :]
    xp_s[5:8, :] = tail
    return acc, tail


N_TOK_REFS, N_W_REFS, N_Y_REFS, N_STATE_REFS, N_SCRATCH = 8, 20, 3, 7, 9


def _prompt_mixer_kernel(*refs, L, layer, creates):
    tok = refs[:N_TOK_REFS]
    wts = refs[N_TOK_REFS:N_TOK_REFS + N_W_REFS]
    scr = refs[-N_SCRATCH:]
    outs = refs[-(N_SCRATCH + N_Y_REFS + N_STATE_REFS):-N_SCRATCH]
    ys, state_outs = outs[:N_Y_REFS], outs[N_Y_REFS:]
    nseq = tok[0].shape[0]

    @pl.when(pl.program_id(1) == 0)
    def _():
        xpl_s, xpm_s = scr[0], scr[1]
        xpl_s[:, 0:SUBLANE, :] = jnp.zeros((nseq, SUBLANE, LRU_W), F32)
        xpm_s[:, 0:SUBLANE, :] = jnp.zeros((nseq, SUBLANE, ML_W), F32)
        for s_ref in scr[4:]:
            s_ref[...] = jnp.zeros_like(s_ref)

    finals = [_prompt_mixer_rows([r.at[bi] for r in tok], wts, [y.at[bi] for y in ys], [s.at[bi] for s in scr], L)
              for bi in range(nseq)]

    @pl.when(pl.program_id(1) == pl.num_programs(1) - 1)
    def _():
        souts = state_outs
        if creates:
            for ref in souts:
                for other in range(DEPTH):
                    if other != layer:
                        ref[other] = jnp.zeros(ref.shape[1:], F32)
            souts = [ref.at[layer] for ref in souts]
        ht_ref, lct_ref, st_ref, ct_ref, nt_ref, mt_ref, mct_ref = souts
        s_s, c_s, n_s, m_s = scr[5:]
        for bi, (h, l_tail, m_tail) in enumerate(finals):
            ht_ref[bi] = h
            lct_ref[bi] = l_tail
            mct_ref[bi] = m_tail
        st_ref[...] = s_s[...]
        ct_ref[...] = c_s[...]
        nt_ref[...] = n_s[...]
        mt_ref[...] = m_s[...]


def _prompt_mixer_rows(tok, wts, ys, scr, L):
    lx_ref, q_ref, k_ref, v_ref, gate_ref, mx_ref, mo_ref, small_ref = tok
    (lcw_ref, lcb_ref, wa_ref, ba_ref, wx_ref, bx_ref, lam_ref, wg2_ref, bg_ref, ggn_ref,
     mcw_ref, mcb_ref, wqh_ref, wql_ref, wkh_ref, wkl_ref, wvh_ref, wvl_ref, bif_ref, mgn_ref) = wts
    yl_ref, yg_ref, ym_ref = ys
    xpl_s, xpm_s, a_s, u_s, h_s, s_s, c_s, n_s, m_s = scr
    rows = lx_ref.shape[0]

    u, l_tail = _conv_chunk(xpl_s, lx_ref[...], lcw_ref[...], lcb_ref[...], rows)
    for blk in range(LRU_BLOCKS):
        sl = slice(blk * LANE, (blk + 1) * LANE)
        a_s[:, sl], u_s[:, sl] = _lru_gates(u[:, sl], wa_ref[blk], wx_ref[blk], ba_ref[:, sl], bx_ref[:, sl],
                                            lam_ref[:, sl])
    t8 = lax.broadcasted_iota(jnp.int32, (SUBLANE, LRU_W), 0)
    h = h_s[...]
    for g in range(rows // SUBLANE):
        r8 = slice(g * SUBLANE, (g + 1) * SUBLANE)
        a8, u8 = a_s[r8, :], u_s[r8, :]
        for s in (1, 2, 4):
            u8 = jnp.where(t8 >= s, u8 + a8 * pltpu.roll(u8, s, 0), u8)
            a8 = jnp.where(t8 >= s, a8 * pltpu.roll(a8, s, 0), a8)
        h8 = u8 + a8 * h
        a_s[r8, :] = h8
        h = h8[SUBLANE - 1:SUBLANE, :]
    h_s[...] = h
    yl_ref[...] = a_s[...].astype(BF16)

    q, k, v, gate = q_ref[...], k_ref[...], v_ref[...], gate_ref[...]
    small = small_ref[...]
    logdec = _log_sigmoid(_dot(small, wg2_ref[...]) + bg_ref[...]) * (1.0 / GLA_TAU)
    mx = mx_ref[...]
    conv, m_tail = _conv_chunk(xpm_s, mx, mcw_ref[...], mcb_ref[...], rows)
    mc = _silu(conv)
    mq = _blockdiag3(mc, wqh_ref, wql_ref)
    mk = _blockdiag3(mc, wkh_ref, wkl_ref) * (ML_DH ** -0.5)
    mv = _blockdiag3(mx, wvh_ref, wvl_ref)
    gates = small + bif_ref[...]
    gates_t = gates.T
    lf = _log_sigmoid(gates)
    lf_t = _log_sigmoid(gates_t)
    mo = mo_ref[...]
    tri = _tri(L)
    tri_b = tri.astype(BF16)

    for cc in range(rows // L):
        rs = slice(cc * L, (cc + 1) * L)
        bcum = _cumsum_rows(tri_b, logdec[rs, :])
        for hh in range(GLA_H):
            ks = slice(hh * GLA_DK, (hh + 1) * GLA_DK)
            vs = slice(hh * GLA_DV, (hh + 1) * GLA_DV)
            bh = bcum[:, ks]
            k_h, v_h = k[rs, ks], v[rs, vs]
            b_mid = bh[L // 2 - 1:L // 2, :]
            qe = (q[rs, ks] * (GLA_DK ** -0.5)) * jnp.exp(bh - b_mid)
            ke = k_h * jnp.exp(b_mid - bh)
            a = jnp.where(tri, _dot_nt(qe, ke), 0.0)
            s_h = s_s[hh]
            bl = bh[L - 1:L, :]
            cols = jnp.exp(jnp.concatenate([jnp.broadcast_to(b_mid, (SUBLANE // 2, GLA_DK)),
                                            jnp.broadcast_to(bl, (SUBLANE // 2, GLA_DK))], axis=0)).T
            mid_col, dec_col = cols[:, 0:1], cols[:, SUBLANE // 2:SUBLANE // 2 + 1]
            o = _dot(qe, mid_col * s_h) + _dot(a, v_h)
            kd = k_h * jnp.exp(bl - bh)
            s_s[hh] = dec_col * s_h + _dot(kd.T, v_h)
            yg_ref[rs, vs] = (_rms(o, ggn_ref[...]) * _silu(gate[rs, vs])).astype(BF16)

    L = min(rows, ML_CHUNK)
    tri = _tri(L)
    tri_b = tri.astype(BF16)
    r_i = lax.broadcasted_iota(jnp.int32, (L, L), 0)
    c_i = lax.broadcasted_iota(jnp.int32, (L, L), 1)
    triu_b = (r_i <= c_i).astype(BF16)
    for cc in range(rows // L):
        rs = slice(cc * L, (cc + 1) * L)
        f_col = _cumsum_rows(tri_b, lf[rs, :])
        f_row = _cumsum_lanes(lf_t[:, rs], triu_b)
        for hh in range(ML_H):
            sl = slice(hh * ML_DH, (hh + 1) * ML_DH)
            qh, kh, vh = mq[rs, sl], mk[rs, sl], mv[rs, sl]
            fc = f_col[:, S_FG + hh:S_FG + hh + 1]
            fr = f_row[S_FG + hh:S_FG + hh + 1, :]
            igr = gates_t[S_IG + hh:S_IG + hh + 1, rs]
            m_h = m_s[hh:hh + 1, 0:1]
            dm = jnp.where(tri, fc - fr + igr, NEG_BIG)
            inter = fc + m_h
            mt = jnp.maximum(inter, jnp.max(dm, axis=-1, keepdims=True))
            ci = jnp.exp(inter - mt)
            s = _dot_nt(qh, kh) * jnp.exp(dm - mt)
            c_h = c_s[hh]
            n_h = n_s[hh:hh + 1, :]
            num = ci * _dot(qh, c_h) + _dot(s, vh)
            den = ci * jnp.sum(qh * n_h, axis=-1, keepdims=True) + jnp.sum(s, axis=-1, keepdims=True)
            hcell = num / jnp.maximum(jnp.abs(den), jnp.exp(-mt))
            fl = fr[:, L - 1:L]
            dj = fl - fr + igr
            m_new = jnp.maximum(fl + m_h, jnp.max(dj, axis=-1, keepdims=True))
            cs = jnp.exp(fl + m_h - m_new)
            wj = jnp.exp(dj - m_new)
            c_s[hh] = cs * c_h + _dot(kh.T * wj, vh)
            n_s[hh:hh + 1, :] = cs * n_h + _dot(jnp.broadcast_to(wj, (SUBLANE, L)), kh)[0:1, :]
            m_s[hh:hh + 1, :] = jnp.broadcast_to(m_new, (1, LANE))
            ym_ref[rs, sl] = (jax.nn.sigmoid(mo[rs, sl]) * _rms(hcell, mgn_ref[...])).astype(BF16)

    return h, l_tail, m_tail


def _prompt_mixer(proj3, small3, layer, w, prev, *, L, rows, nseq):
    b, t, _ = proj3.shape
    kw, vw = GLA_H * GLA_DK, GLA_H * GLA_DV
    nblk = ML_W // LANE
    blk3 = lambda width, cb: pl.BlockSpec((nseq, rows, width), lambda bi, c: (bi, c, cb))
    lw3 = lambda s: pl.BlockSpec((None,) + s, lambda bi, c: (layer, 0, 0))
    lw4 = lambda s: pl.BlockSpec((None,) + s, lambda bi, c: (layer, 0, 0, 0))
    creates = prev is None
    lead, at = ((DEPTH,), 0) if creates else ((None,), layer)
    st3 = lambda s: pl.BlockSpec(lead + (nseq,) + s, lambda bi, c: (at, bi, 0, 0))
    st4 = lambda s: pl.BlockSpec(lead + (nseq,) + s, lambda bi, c: (at, bi, 0, 0, 0))
    in_specs = [
        blk3(LRU_W, C_LRU // LRU_W), blk3(kw, C_GQ // kw), blk3(kw, C_GK // kw), blk3(vw, C_GV // vw),
        blk3(vw, C_GGATE // vw), blk3(ML_W, C_MX // ML_W), blk3(ML_W, C_MO // ML_W), blk3(LANE, 0),
        lw3((CONV_W, LRU_W)), lw3((1, LRU_W)), lw4((LRU_BLOCKS, LANE, LANE)), lw3((1, LRU_W)),
        lw4((LRU_BLOCKS, LANE, LANE)), lw3((1, LRU_W)), lw3((1, LRU_W)),
        lw3((LANE, kw)), lw3((1, kw)), lw3((1, GLA_DV)),
        lw3((CONV_W, ML_W)), lw3((1, ML_W)), *([lw4((nblk, LANE, LANE))] * 6), lw3((1, LANE)), lw3((1, ML_DH)),
    ]
    args = [proj3] * 7 + [small3,
                          w["lru_cw"], w["lru_cb"], w["lru_wa"], w["lru_ba"], w["lru_wx"], w["lru_bx"], w["lru_lam"],
                          w["wg2"], w["bg"], w["gla_gn"],
                          w["ml_cw"], w["ml_cb"], *w["ml_wq"], *w["ml_wk"], *w["ml_wv"], w["ml_bif"], w["ml_gn"]]
    aliases = {}
    if prev is not None:
        aliases = {len(in_specs) + j: 3 + j for j in range(len(prev))}
        in_specs += [pl.BlockSpec(memory_space=pl.ANY)] * len(prev)
        args += list(prev)
    sbytes = GLA_H * GLA_DK * GLA_DV * 4
    cbytes = ML_H * ML_DH * ML_DH * 4
    vmem = nseq * (2 * rows * (7 * 1024 * 4 + 3 * 1024 * 2) + 3 * (sbytes + cbytes) + 6 * rows * 1024 * 4)
    vmem += 30 * rows * 1024 * 4 + (12 << 20)
    sc = lambda *s: pltpu.VMEM((nseq,) + s, F32)
    return pl.pallas_call(
        functools.partial(_prompt_mixer_kernel, L=L, layer=layer, creates=creates),
        grid=(b // nseq, t // rows),
        in_specs=in_specs,
        out_specs=[blk3(LRU_W, 0), blk3(vw, 0), blk3(ML_W, 0),
                   st3((1, LRU_W)), st3((CONV_W - 1, LRU_W)), st4((GLA_H, GLA_DK, GLA_DV)),
                   st4((ML_H, ML_DH, ML_DH)), st3((ML_H, ML_DH)), st3((ML_H, LANE)), st3((CONV_W - 1, ML_W))],
        out_shape=[SDS((b, t, LRU_W), BF16), SDS((b, t, vw), BF16), SDS((b, t, ML_W), BF16),
                   SDS((DEPTH, b, 1, LRU_W), F32), SDS((DEPTH, b, CONV_W - 1, LRU_W), F32),
                   SDS((DEPTH, b, GLA_H, GLA_DK, GLA_DV), F32), SDS((DEPTH, b, ML_H, ML_DH, ML_DH), F32),
                   SDS((DEPTH, b, ML_H, ML_DH), F32), SDS((DEPTH, b, ML_H, LANE), F32),
                   SDS((DEPTH, b, CONV_W - 1, ML_W), F32)],
        scratch_shapes=[sc(SUBLANE + rows, LRU_W), sc(SUBLANE + rows, ML_W), sc(rows, LRU_W), sc(rows, LRU_W),
                        sc(1, LRU_W), sc(GLA_H, GLA_DK, GLA_DV), sc(ML_H, ML_DH, ML_DH), sc(ML_H, ML_DH),
                        sc(ML_H, LANE)],
        input_output_aliases=aliases,
        compiler_params=_params(("parallel", "arbitrary"), vmem),
        name="prompt_mixer",
    )(*args)


def _tmask(shape):
    return lax.broadcasted_iota(jnp.int32, shape, 0) % T_STEP


def _down(x, s):
    return x if s == 0 else pltpu.roll(x, s, 0)


def _up(x, s):
    return x if s == 0 else pltpu.roll(x, x.shape[0] - s, 0)


def _seg_cumsum(x):
    t = _tmask(x.shape)
    out = x
    for s in range(1, T_STEP):
        out = out + jnp.where(t >= s, _down(x, s), 0.0)
    return out


def _seg_last(x):
    t = _tmask(x.shape)
    out = x
    for s in range(1, T_STEP):
        out = jnp.where(t == T_STEP - 1 - s, _up(x, s), out)
    return out


def _seg_allreduce(x, op):
    t = _tmask(x.shape)
    y = op(x, jnp.where(t % 2 == 1, _down(x, 1), _up(x, 1)))
    return op(y, jnp.where(t >= 2, _down(y, 2), _up(y, 2)))


def _conv_rows(x, e, w, b):
    t = _tmask(x.shape)
    acc = b + x * w[CONV_W - 1:CONV_W]
    for k in range(1, CONV_W):
        hist = jnp.where(t >= k, _down(x, k), _up(e, CONV_W - 1 - k))
        acc = acc + hist * w[CONV_W - 1 - k:CONV_W - k]
    return acc


def _col(x, lane):
    li = lax.broadcasted_iota(jnp.int32, x.shape, 1)
    return jnp.sum(jnp.where(li == lane, x, 0.0), axis=-1, keepdims=True)


def _sample_mixer_kernel(lx_ref, q_ref, k_ref, v_ref, gate_ref, mx_ref, mo_ref, small_ref,
                         el_ref, em_ref, h0_ref, n0_ref, m0_ref, s0_ref, c0_ref,
                         lcw_ref, lcb_ref, wa_ref, ba_ref, wx_ref, bx_ref, lam_ref,
                         wg2_ref, bg_ref, ggn_ref,
                         mcw_ref, mcb_ref, wqh_ref, wql_ref, wkh_ref, wkl_ref, wvh_ref, wvl_ref, bif_ref, mgn_ref,
                         *rest, layer, creates):
    yl_ref, yg_ref, ym_ref, *state_refs = rest[-8:]
    if creates:
        for ref in state_refs:
            for other in range(DEPTH):
                if other != layer:
                    ref[other] = jnp.zeros(ref.shape[1:], F32)
        state_refs = [ref.at[layer] for ref in state_refs]
    hrow_ref, nrow_ref, mrow_ref, st_ref, ct_ref = state_refs
    rows = lx_ref.shape[0]
    n_pairs = rows // SUBLANE
    seq_per_tile = SUBLANE // T_STEP
    t_col = _tmask((rows, 1))
    row8 = lax.broadcasted_iota(jnp.int32, (SUBLANE, 1), 0)

    u = _conv_rows(lx_ref[...], el_ref[...], lcw_ref[...], lcb_ref[...])
    a_blocks, g_blocks = [], []
    for blk in range(LRU_BLOCKS):
        sl = slice(blk * LANE, (blk + 1) * LANE)
        a_b, g_b = _lru_gates(u[:, sl], wa_ref[blk], wx_ref[blk], ba_ref[:, sl], bx_ref[:, sl], lam_ref[:, sl])
        a_blocks.append(a_b)
        g_blocks.append(g_b)
    a = jnp.concatenate(a_blocks, axis=1)
    t_w = _tmask(a.shape)
    g = jnp.concatenate(g_blocks, axis=1) + jnp.where(t_w == 0, a * h0_ref[...], 0.0)
    a1 = jnp.where(t_w >= 1, a * _down(a, 1), a)
    g1 = jnp.where(t_w >= 1, g + a * _down(g, 1), g)
    h = jnp.where(t_w >= 2, g1 + a1 * _down(g1, 2), g1)
    yl_ref[...] = h.astype(BF16)
    hrow_ref[...] = h

    q, k, v = q_ref[...], k_ref[...], v_ref[...]
    logdec = _log_sigmoid(_dot(small_ref[...], wg2_ref[...]) + bg_ref[...]) * (1.0 / GLA_TAU)
    bc = _seg_cumsum(logdec)
    bl = _seg_last(bc)
    qs = q * (GLA_DK ** -0.5)
    o_heads = [jnp.zeros((rows, GLA_DV), F32) for _ in range(GLA_H)]
    for s in range(T_STEP):
        prod = qs * _down(k, s) * jnp.exp(bc - _down(bc, s))
        v_s = _down(v, s)
        for hh in range(GLA_H):
            a_sh = jnp.sum(prod[:, hh * GLA_DK:(hh + 1) * GLA_DK], axis=-1, keepdims=True)
            a_sh = jnp.where(t_col >= s, a_sh, 0.0)
            o_heads[hh] = o_heads[hh] + a_sh * v_s[:, hh * GLA_DV:(hh + 1) * GLA_DV]
    qe = qs * jnp.exp(bc)
    kd = k * jnp.exp(bl - bc)
    dec = jnp.exp(bl)
    gate = gate_ref[...]
    for hh in range(GLA_H):
        ks = slice(hh * GLA_DK, (hh + 1) * GLA_DK)
        vs = slice(hh * GLA_DV, (hh + 1) * GLA_DV)
        o_state = []
        for p in range(n_pairs):
            r8 = slice(p * SUBLANE, (p + 1) * SUBLANE)
            kd_t = kd[r8, ks].T
            dec_t = dec[r8, ks].T
            res = None
            for j in range(seq_per_tile):
                b = p * seq_per_tile + j
                s_b = s0_ref[b, hh]
                r_j = _dot(qe[r8, ks], s_b)
                res = r_j if res is None else jnp.where(row8 // T_STEP == j, r_j, res)
                v_j = jnp.where(row8 // T_STEP == j, v[r8, vs], 0.0)
                last = j * T_STEP + T_STEP - 1
                st_ref[b, hh] = dec_t[:, last:last + 1] * s_b + _dot(kd_t, v_j)
            o_state.append(res)
        o = o_heads[hh] + jnp.concatenate(o_state, axis=0)
        yg_ref[:, vs] = (_rms(o, ggn_ref[...]) * _silu(gate[:, vs])).astype(BF16)

    mx = mx_ref[...]
    mc = _silu(_conv_rows(mx, em_ref[...], mcw_ref[...], mcb_ref[...]))
    mq = _blockdiag3(mc, wqh_ref, wql_ref)
    mk = _blockdiag3(mc, wkh_ref, wkl_ref) * (ML_DH ** -0.5)
    mv = _blockdiag3(mx, wvh_ref, wvl_ref)
    gts = small_ref[...] + bif_ref[...]
    ig = pltpu.roll(gts, S_FG - S_IG, 1)
    fcum = _seg_cumsum(_log_sigmoid(gts))
    flast = _seg_last(fcum)
    m0 = m0_ref[...]
    t_g = _tmask(gts.shape)
    inter = fcum + m0
    dms = [jnp.where(t_g >= s, fcum - _down(fcum, s) + _down(ig, s), NEG_BIG) for s in range(T_STEP)]
    mt = inter
    for dm in dms:
        mt = jnp.maximum(mt, dm)
    ci_t = jnp.exp(inter - mt)
    emt_t = jnp.exp(-mt)
    w_t = [jnp.exp(dm - mt) for dm in dms]
    dj = flast - fcum + ig
    m_new = jnp.maximum(flast + m0, _seg_allreduce(dj, jnp.maximum))
    cs_t = jnp.exp(flast + m0 - m_new)
    wj_t = jnp.exp(dj - m_new)
    mrow_ref[...] = m_new
    mo = mo_ref[...]
    n0 = n0_ref[...]
    for hh in range(ML_H):
        sl = slice(hh * ML_DH, (hh + 1) * ML_DH)
        lane = S_FG + hh
        qh, kh, vh = mq[:, sl], mk[:, sl], mv[:, sl]
        ci, emt, cs, wj = _col(ci_t, lane), _col(emt_t, lane), _col(cs_t, lane), _col(wj_t, lane)
        num = jnp.zeros((rows, ML_DH), F32)
        den = ci * jnp.sum(qh * n0[:, sl], axis=-1, keepdims=True)
        for s in range(T_STEP):
            sc = jnp.sum(qh * _down(kh, s), axis=-1, keepdims=True) * _col(w_t[s], lane)
            num = num + sc * _down(vh, s)
            den = den + sc
        kw = kh * wj
        nrow_ref[:, sl] = cs * n0[:, sl] + _seg_allreduce(kw, jnp.add)
        qc = []
        for p in range(n_pairs):
            r8 = slice(p * SUBLANE, (p + 1) * SUBLANE)
            kw_t = kw[r8, :].T
            res = None
            for j in range(seq_per_tile):
                b = p * seq_per_tile + j
                c_b = c0_ref[b, hh]
                r_j = _dot(qh[r8, :], c_b)
                res = r_j if res is None else jnp.where(row8 // T_STEP == j, r_j, res)
                v_j = jnp.where(row8 // T_STEP == j, vh[r8, :], 0.0)
                last = p * SUBLANE + j * T_STEP + T_STEP - 1
                ct_ref[b, hh] = cs[last:last + 1, :] * c_b + _dot(kw_t, v_j)
            qc.append(res)
        num = num + ci * jnp.concatenate(qc, axis=0)
        hcell = num / jnp.maximum(jnp.abs(den), emt)
        ym_ref[:, sl] = (jax.nn.sigmoid(mo[:, sl]) * _rms(hcell, mgn_ref[...])).astype(BF16)


def _sample_mixer(proj, small, el, em, h0e, n0e, m0e, s_state, c_state, layer, w, prev):
    n = proj.shape[0]
    nseq = n // T_STEP
    bb = STEP_BB
    rows = bb * T_STEP
    row = lambda width, blk: pl.BlockSpec((rows, width), lambda i: (i, blk))
    lrow = lambda width: pl.BlockSpec((None, rows, width), lambda i: (layer, i, 0))
    lw3 = lambda s: pl.BlockSpec((None,) + s, lambda i: (layer, 0, 0))
    lw4 = lambda s: pl.BlockSpec((None,) + s, lambda i: (layer, 0, 0, 0))
    sspec = pl.BlockSpec((None, bb, GLA_H, GLA_DK, GLA_DV), lambda i: (layer, i, 0, 0, 0))
    cspec = pl.BlockSpec((None, bb, ML_H, ML_DH, ML_DH), lambda i: (layer, i, 0, 0, 0))
    nblk = ML_W // LANE
    kw, vw = GLA_H * GLA_DK, GLA_H * GLA_DV
    in_specs = [
        row(LRU_W, C_LRU // LRU_W), row(kw, C_GQ // kw), row(kw, C_GK // kw), row(vw, C_GV // vw),
        row(vw, C_GGATE // vw), row(ML_W, C_MX // ML_W), row(ML_W, C_MO // ML_W), row(LANE, 0),
        lrow(LRU_W), lrow(ML_W), lrow(LRU_W), lrow(ML_W), lrow(LANE), sspec, cspec,
        lw3((CONV_W, LRU_W)), lw3((1, LRU_W)), lw4((LRU_BLOCKS, LANE, LANE)), lw3((1, LRU_W)),
        lw4((LRU_BLOCKS, LANE, LANE)), lw3((1, LRU_W)), lw3((1, LRU_W)),
        lw3((LANE, kw)), lw3((1, kw)), lw3((1, GLA_DV)),
        lw3((CONV_W, ML_W)), lw3((1, ML_W)), *([lw4((nblk, LANE, LANE))] * 6), lw3((1, LANE)), lw3((1, ML_DH)),
    ]
    args = [proj] * 7 + [small, el, em, h0e, n0e, m0e, s_state, c_state,
                         w["lru_cw"], w["lru_cb"], w["lru_wa"], w["lru_ba"], w["lru_wx"], w["lru_bx"], w["lru_lam"],
                         w["wg2"], w["bg"], w["gla_gn"],
                         w["ml_cw"], w["ml_cb"], *w["ml_wq"], *w["ml_wk"], *w["ml_wv"], w["ml_bif"], w["ml_gn"]]
    aliases = {}
    if prev is not None:
        aliases = {len(in_specs) + j: 3 + j for j in range(len(prev))}
        in_specs += [pl.BlockSpec(memory_space=pl.ANY)] * len(prev)
        args += list(prev)
    sbytes = bb * GLA_H * GLA_DK * GLA_DV * 4
    cbytes = bb * ML_H * ML_DH * ML_DH * 4
    vmem = 4 * (sbytes + cbytes) + 2 * rows * (7 * 1024 + 8 * 1024) * 4 + 80 * rows * 1024 * 4 + (12 << 20)
    creates = prev is None
    state_specs = [lrow(LRU_W), lrow(ML_W), lrow(LANE), sspec, cspec]
    if creates:
        all_rows = lambda width: pl.BlockSpec((DEPTH, rows, width), lambda i: (0, i, 0))
        state_specs = [all_rows(LRU_W), all_rows(ML_W), all_rows(LANE),
                       pl.BlockSpec((DEPTH, bb, GLA_H, GLA_DK, GLA_DV), lambda i: (0, i, 0, 0, 0)),
                       pl.BlockSpec((DEPTH, bb, ML_H, ML_DH, ML_DH), lambda i: (0, i, 0, 0, 0))]
        vmem += 2 * (DEPTH - 1) * (sbytes + cbytes)
    return pl.pallas_call(
        functools.partial(_sample_mixer_kernel, layer=layer, creates=creates),
        grid=(nseq // bb,),
        in_specs=in_specs,
        out_specs=[row(LRU_W, 0), row(vw, 0), row(ML_W, 0), *state_specs],
        out_shape=[SDS((n, LRU_W), BF16), SDS((n, vw), BF16), SDS((n, ML_W), BF16),
                   SDS((DEPTH, n, LRU_W), F32), SDS((DEPTH, n, ML_W), F32), SDS((DEPTH, n, LANE), F32),
                   SDS((DEPTH, nseq, GLA_H, GLA_DK, GLA_DV), F32), SDS((DEPTH, nseq, ML_H, ML_DH, ML_DH), F32)],
        input_output_aliases=aliases,
        compiler_params=_params(("arbitrary",), vmem),
        name="sample_mixer",
    )(*args)


def _prep_weights(p):
    w_in = p["w_in"]
    c1 = C_GGATE
    c2 = C_MG + W_SHIFT1
    w_small = jnp.concatenate(
        [w_in[:, :, c1:c1 + GLA_RANK], w_in[:, :, c2:c2 + 2 * ML_H],
         jnp.zeros((DEPTH, D, LANE - GLA_RANK - 2 * ML_H), F32)], axis=-1)
    wg2 = jnp.concatenate(
        [p["gla_w_g2"], jnp.zeros((DEPTH, LANE - GLA_RANK, GLA_H * GLA_DK), F32)], axis=1)
    blk_of = jnp.arange(LANE) // ML_BS
    on_diag = blk_of[:, None] == blk_of[None, :]

    def dense_bd(wb):
        rows = wb.reshape(DEPTH, ML_W // LANE, LANE, ML_BS)
        dense = jnp.where(on_diag, jnp.tile(rows, (1, 1, 1, LANE // ML_BS)), 0.0)
        hi = dense.astype(BF16)
        return hi, (dense - hi.astype(F32)).astype(BF16)

    bif = jnp.zeros((DEPTH, 1, LANE), F32).at[:, 0, S_IG:S_IG + 2 * ML_H].set(p["ml_b_if"])
    r3 = lambda a: a.reshape(DEPTH, 1, a.shape[-1])
    return dict(
        w_small=w_small, wg2=wg2, bg=r3(p["gla_b_g"]), gla_gn=r3(p["gla_g_norm"]),
        g1=r3(p["g_norm1"]), g2=r3(p["g_norm2"]), gf=p["g_final"].reshape(1, D),
        lru_cw=p["lru_conv_w"], lru_cb=r3(p["lru_conv_b"]), lru_wa=p["lru_w_a"], lru_ba=r3(p["lru_b_a"]),
        lru_wx=p["lru_w_x"], lru_bx=r3(p["lru_b_x"]), lru_lam=r3(p["lru_lam"]),
        ml_cw=p["ml_conv_w"], ml_cb=r3(p["ml_conv_b"]), ml_wq=dense_bd(p["ml_w_q"]), ml_wk=dense_bd(p["ml_w_k"]),
        ml_wv=dense_bd(p["ml_w_v"]), ml_bif=bif, ml_gn=r3(p["ml_g_norm"]),
        w_in_t=jnp.swapaxes(w_in, 1, 2), w_br=(p["w_br_lru"], p["w_br_gla"], p["w_br_ml"]), w_out=p["w_out"], w_ff1=p["w_ff1"],
        w_ff2=p["w_ff2"],
    )


def _trunk(xp3, xs3, mod_p, mod_s, states, w):
    bp, tp, _ = xp3.shape
    bs, ts, _ = xs3.shape
    n_p, n_s = bp * tp, bs * ts
    tm_p = min(tp, 1024)
    tm_f = min(tp, 512)
    s_h, s_cv, s_gla, s_c, s_n, s_m, s_mcv = states
    pad_t = lambda a: jnp.pad(a, ((0, 0), (0, 0), (0, ts - a.shape[2]), (0, 0))).reshape(DEPTH, n_s, a.shape[-1])
    el = pad_t(s_cv)
    em = pad_t(s_mcv)
    h0e = pad_t(s_h[:, :, None, :])
    n0e = jnp.broadcast_to(s_n.reshape(DEPTH, bs, 1, ML_W), (DEPTH, bs, ts, ML_W)).reshape(DEPTH, n_s, ML_W)
    m_l = jnp.pad(s_m, ((0, 0), (0, 0), (S_FG, LANE - S_FG - ML_H)))
    m0e = jnp.broadcast_to(m_l[:, :, None, :], (DEPTH, bs, ts, LANE)).reshape(DEPTH, n_s, LANE)
    x_p = xp3.reshape(n_p, D)
    x_s = xs3.reshape(n_s, D)
    prev_p = prev_s = None
    conv_l, conv_m = [], []
    keep = ts - (CONV_W - 1)
    for l in range(DEPTH):
        last = l == DEPTH - 1
        proj_s, small_s, w_main = _in_proj_cast(x_s, mod_s, l, w["g1"], w["w_in_t"], w["w_small"], tn=512, rpg=n_s)
        proj_p, small_p = _in_proj(x_p, mod_p, l, w["g1"], w_main, w["w_small"], tm=tm_p, tn=1024, rpg=tp, r=1)
        outs_s = _sample_mixer(proj_s, small_s, el, em, h0e, n0e, m0e, s_gla, s_c, l, w, prev_s)
        prev_s = outs_s[3:]
        outs_p = _prompt_mixer(proj_p.reshape(bp, tp, N_MAIN), small_p.reshape(bp, tp, LANE), l, w, prev_p,
                               L=min(tp, CHUNK), rows=min(tp, PROMPT_ROWS), nseq=PROMPT_SEQS)
        prev_p = outs_p[3:]
        conv_l.append(proj_s[:, C_LRU:C_LRU + LRU_W].reshape(bs, ts, LRU_W)[:, keep:])
        conv_m.append(proj_s[:, C_MX:C_MX + ML_W].reshape(bs, ts, ML_W)[:, keep:])
        merged_s, *w_br = _merge(*outs_s[:3], proj_s, l, *w["w_br"], tm=n_s, tn=512, cast=True)
        (merged_p,) = _merge(*(y.reshape(n_p, y.shape[-1]) for y in outs_p[:3]), proj_p, l, *w_br, tm=tm_p, tn=512,
                             cast=False)
        x_s, w_out = _out_proj(merged_s, x_s, mod_s, l, w["w_out"], tm=n_s, tn=512, rpg=n_s, r=n_s, cast=True)
        (x_p,) = _out_proj(merged_p, x_p, mod_p, l, w_out, tm=tm_p, tn=512, rpg=tp, r=1, cast=False)
        x_s, w_ff1, w_ff2 = _ffn(x_s, mod_s, l, w["g2"], w["gf"], w["w_ff1"], w["w_ff2"], tm=n_s, tf=512, rpg=n_s,
                                 r=n_s, final_norm=last, cast=True)
        (x_p,) = _ffn(x_p, mod_p, l, w["g2"], w["gf"], w_ff1, w_ff2, tm=tm_f, tf=1024, rpg=tp, r=1, final_norm=last,
                      cast=False)
    h_t, lcv_t, s_t, c_t, n_t, m_t, mcv_t = prev_p
    p_states = [h_t.reshape(DEPTH, bp, LRU_W), lcv_t, s_t, c_t, n_t, m_t[..., 0], mcv_t]
    hrow, nrow, mrow, s_out, c_out = prev_s
    t_last = ts - 1
    s_states = [hrow.reshape(DEPTH, bs, ts, LRU_W)[:, :, t_last], jnp.stack(conv_l), s_out, c_out,
                nrow.reshape(DEPTH, bs, ts, ML_H, ML_DH)[:, :, t_last],
                mrow.reshape(DEPTH, bs, ts, LANE)[:, :, t_last, S_FG:S_FG + ML_H], jnp.stack(conv_m)]
    return x_p.reshape(bp, tp, D), x_s.reshape(bs, ts, D), p_states, s_states


def kernel(x_prompt, x_sample, c_prompt, c_sample, state_lru_h, state_lru_conv, state_gla, state_mlstm_C, state_mlstm_n, state_mlstm_m, state_mlstm_conv, w_ada, b_ada, g_norm1, g_norm2, w_in, lru_conv_w, lru_conv_b, lru_w_a, lru_b_a, lru_w_x, lru_b_x, lru_lam, gla_w_g2, gla_b_g, gla_g_norm, ml_conv_w, ml_conv_b, ml_w_q, ml_w_k, ml_w_v, ml_b_if, ml_g_norm, w_br_lru, w_br_gla, w_br_ml, w_out, w_ff1, w_ff2, g_final):
    p = dict(g_norm1=g_norm1, g_norm2=g_norm2, w_in=w_in, lru_conv_w=lru_conv_w, lru_conv_b=lru_conv_b,
             lru_w_a=lru_w_a, lru_b_a=lru_b_a, lru_w_x=lru_w_x, lru_b_x=lru_b_x, lru_lam=lru_lam,
             gla_w_g2=gla_w_g2, gla_b_g=gla_b_g, gla_g_norm=gla_g_norm, ml_conv_w=ml_conv_w, ml_conv_b=ml_conv_b,
             ml_w_q=ml_w_q, ml_w_k=ml_w_k, ml_w_v=ml_w_v, ml_b_if=ml_b_if, ml_g_norm=ml_g_norm,
             w_br_lru=w_br_lru, w_br_gla=w_br_gla, w_br_ml=w_br_ml, w_out=w_out, w_ff1=w_ff1, w_ff2=w_ff2,
             g_final=g_final)
    w = _prep_weights(p)
    bp = x_prompt.shape[0]
    bs, ts, _ = x_sample.shape
    assert ts == T_STEP and bs % STEP_BB == 0
    ns = bs * ts
    pad = (-(ns + bp)) % SUBLANE
    c_all = jnp.concatenate([jnp.repeat(c_sample, ts, axis=0), c_prompt, jnp.zeros((pad, D), F32)], axis=0)
    mod_all = _ada(c_all, w_ada, b_ada)
    mod_s = mod_all.reshape(DEPTH, 1, ns + bp + pad, N_MOD * D)
    mod_p = mod_all[:, ns:ns + bp].reshape(DEPTH, bp, 1, N_MOD * D)
    y_p, y_s, ps, ss = _trunk(x_prompt, x_sample, mod_p, mod_s,
                              (state_lru_h, state_lru_conv, state_gla, state_mlstm_C, state_mlstm_n, state_mlstm_m,
                               state_mlstm_conv), w)
    return (y_p, y_s, *ps, *ss)
```

```python
import functools

import jax
import jax.numpy as jnp
from jax import lax
from jax.experimental import pallas as pl
from jax.experimental.pallas import tpu as pltpu

F32, BF16 = jnp.float32, jnp.bfloat16
SDS = jax.ShapeDtypeStruct

D = 2048
DEPTH = 2
LRU_W = 1024
LRU_BLOCKS = 8
LRU_C = 8.0
CONV_W = 4
GLA_H = 4
GLA_DK = 128
GLA_DV = 256
GLA_RANK = 16
GLA_TAU = 16.0
ML_H = 4
ML_W = 1024
ML_DH = 256
ML_BS = 4
CHUNK = 128
ML_CHUNK = 256
PROMPT_ROWS = 256
PROMPT_SEQS = 1
D_FF = 4 * D
EPS = 1e-6
N_MOD = 6
T_STEP = 4
STEP_BB = 4

LANE = 128
SUBLANE = 8
VMEM_LIMIT_CAP = 60 * 1024 * 1024

C_LRU, C_GQ, C_GK, C_GV, C_GGATE, C_MX, C_MO, C_MG = 0, 1024, 1536, 2048, 3072, 4096, 5120, 6144
N_MAIN = 12288
W_SHIFT1 = GLA_RANK
W_SHIFT2 = GLA_RANK + 2 * ML_H
S_GLR = 0
S_IG = 16
S_FG = 20
NEG_BIG = -1e30


def _params(sem, vmem_bytes):
    return pltpu.CompilerParams(dimension_semantics=sem, vmem_limit_bytes=int(min(vmem_bytes, VMEM_LIMIT_CAP)))


def _dot(a, b):
    return jnp.dot(a.astype(BF16), b.astype(BF16), preferred_element_type=F32)


def _dot_nt(a, b):
    return lax.dot_general(a.astype(BF16), b.astype(BF16), (((1,), (1,)), ((), ())), preferred_element_type=F32)


def _split3(x):
    p0 = x.astype(BF16)
    r = x - p0.astype(F32)
    p1 = r.astype(BF16)
    return p0, p1, (r - p1.astype(F32)).astype(BF16)


def _cumsum_rows(tri_b, x):
    return sum(jnp.dot(tri_b, p, preferred_element_type=F32) for p in _split3(x))


def _cumsum_lanes(x, triu_b):
    return sum(jnp.dot(p, triu_b, preferred_element_type=F32) for p in _split3(x))


def _split(x):
    hi = x.astype(BF16)
    return hi, (x - hi.astype(F32)).astype(BF16)


def _dot3(x_hi, x_lo, w_hi, w_lo):
    d = lambda a, b: jnp.dot(a, b, preferred_element_type=F32)
    return d(x_hi, w_hi) + (d(x_lo, w_hi) + d(x_hi, w_lo))


def _log_sigmoid(z):
    return jnp.minimum(z, 0.0) - jnp.log1p(jnp.exp(-jnp.abs(z)))


def _silu(z):
    return z * jax.nn.sigmoid(z)


def _rms(x, g):
    return x * lax.rsqrt(jnp.mean(x * x, axis=-1, keepdims=True) + EPS) * g


def _tri(n):
    r = lax.broadcasted_iota(jnp.int32, (n, n), 0)
    c = lax.broadcasted_iota(jnp.int32, (n, n), 1)
    return r >= c


def _lru_gates(ub, wa, wx, ba, bx, lam):
    r = jax.nn.sigmoid(_dot(ub, wa) + ba)
    i = jax.nn.sigmoid(_dot(ub, wx) + bx)
    log_a = LRU_C * r * _log_sigmoid(lam)
    t = jnp.tanh(log_a)
    return jnp.exp(log_a), jnp.sqrt(-2.0 * t / (1.0 - t)) * (i * ub)


def _blockdiag3(x, wh_ref, wl_ref):
    xh, xl = _split(x)
    outs = []
    for blk in range(ML_W // LANE):
        sl = slice(blk * LANE, (blk + 1) * LANE)
        outs.append(_dot3(xh[:, sl], xl[:, sl], wh_ref[blk], wl_ref[blk]))
    return jnp.concatenate(outs, axis=1)


def _ada_kernel(c_ref, w_ref, b_ref, o_ref):
    o_ref[...] = _dot(_silu(c_ref[...]), w_ref[...]) + b_ref[...]


def _ada(c_all, w_ada, b_ada):
    m = c_all.shape[0]
    tn = 1024
    n_out = N_MOD * D
    return pl.pallas_call(
        _ada_kernel,
        grid=(DEPTH, n_out // tn),
        in_specs=[
            pl.BlockSpec((m, D), lambda l, n: (0, 0)),
            pl.BlockSpec((None, D, tn), lambda l, n: (l, 0, n)),
            pl.BlockSpec((None, 1, tn), lambda l, n: (l, 0, n)),
        ],
        out_specs=pl.BlockSpec((None, m, tn), lambda l, n: (l, 0, n)),
        out_shape=SDS((DEPTH, m, n_out), F32),
        compiler_params=_params(("parallel", "arbitrary"), 2 * (m * D + D * tn + m * tn) * 4 + (12 << 20)),
        name="ada",
    )(c_all, w_ada, b_ada.reshape(DEPTH, 1, n_out))


def _modspec(layer, comp, r, width, tm, rpg, jmap=None):
    nb = D // width
    jm = (lambda j: 0) if jmap is None else jmap
    if r == 1:
        return pl.BlockSpec((None, None, 1, width), lambda i, j: (layer, (i * tm) // rpg, 0, comp * nb + jm(j)))
    mode = pl.Buffered(1) if jmap is None else None
    return pl.BlockSpec((None, None, tm, width), lambda i, j: (layer, 0, i, comp * nb + jm(j)), pipeline_mode=mode)


def _norm_mod_store(xn_s, x_ref, g_ref, sc_ref, sh_ref):
    tm = x_ref.shape[0]
    rc = min(tm, 256)
    for c in range(tm // rc):
        rows = slice(c * rc, (c + 1) * rc)
        mrows = rows if sc_ref.shape[0] == tm else slice(None)
        xn = _rms(x_ref[rows, :], g_ref[...]) * (1.0 + sc_ref[mrows, :]) + sh_ref[mrows, :]
        xn_s[rows, :] = xn.astype(BF16)


def _in_proj_kernel(x_ref, sc_ref, sh_ref, g_ref, wm_ref, ws_ref, om_ref, os_ref, *rest, fill_steps):
    xn_s = rest[-1]

    @pl.when(pl.program_id(1) == 0)
    def _():
        _norm_mod_store(xn_s, x_ref, g_ref, sc_ref, sh_ref)
        os_ref[...] = jnp.dot(xn_s[...], ws_ref[...].astype(BF16), preferred_element_type=F32)

    om_ref[...] = jnp.dot(xn_s[...], wm_ref[...], preferred_element_type=F32)

    if fill_steps:
        @pl.when(pl.program_id(1) < fill_steps)
        def _():
            for z_ref in rest[:-1]:
                z_ref[...] = jnp.zeros_like(z_ref)


def _in_proj(x, mod, layer, g1, w_main, w_small, *, tm, tn, rpg, r, zero_rows=()):
    n = x.shape[0]
    vmem = 2 * (tm * D * 4 + 2 * r * D * 4 + D * tn * 2 + D * LANE * 2 + tm * tn * 4 + tm * LANE * 4) + tm * D * 2
    vmem += 8 << 20
    n_i, n_j = n // tm, N_MAIN // tn
    fill_steps = 0
    zero_specs, zero_shapes = [], []
    if zero_rows:
        fill_steps = max(s for s in range(1, n_j + 1)
                         if all(zr % (n_i * s * SUBLANE) == 0 for zr in zero_rows))
        for zr in zero_rows:
            blk = zr // (n_i * fill_steps)
            zero_specs.append(pl.BlockSpec((blk, GLA_DV),
                                           lambda i, j: (i * fill_steps + jnp.minimum(j, fill_steps - 1), 0)))
            zero_shapes.append(SDS((zr, GLA_DV), F32))
            vmem += 2 * blk * GLA_DV * 4
    outs = pl.pallas_call(
        functools.partial(_in_proj_kernel, fill_steps=fill_steps),
        grid=(n_i, n_j),
        in_specs=[
            pl.BlockSpec((tm, D), lambda i, j: (i, 0)),
            _modspec(layer, 1, r, D, tm, rpg),
            _modspec(layer, 0, r, D, tm, rpg),
            pl.BlockSpec((None, 1, D), lambda i, j: (layer, 0, 0)),
            pl.BlockSpec((D, tn), lambda i, j: (0, j)),
            pl.BlockSpec((None, D, LANE), lambda i, j: (layer, 0, 0)),
        ],
        out_specs=[pl.BlockSpec((tm, tn), lambda i, j: (i, j)), pl.BlockSpec((tm, LANE), lambda i, j: (i, 0)),
                   *zero_specs],
        out_shape=[SDS((n, N_MAIN), F32), SDS((n, LANE), F32), *zero_shapes],
        scratch_shapes=[pltpu.VMEM((tm, D), BF16)],
        compiler_params=_params(("parallel", "arbitrary"), vmem),
        name="in_proj",
    )(x, mod, mod, g1, w_main, w_small)
    return outs


def _in_proj_cast_kernel(x_ref, sc_ref, sh_ref, g_ref, wt_ref, ws_ref, om_ref, os_ref, wc_ref, xn_s):
    @pl.when(pl.program_id(1) == 0)
    def _():
        _norm_mod_store(xn_s, x_ref, g_ref, sc_ref, sh_ref)
        os_ref[...] = jnp.dot(xn_s[...], ws_ref[...].astype(BF16), preferred_element_type=F32)

    wb = wt_ref[...].T.astype(BF16)
    wc_ref[...] = wb
    om_ref[...] = jnp.dot(xn_s[...], wb, preferred_element_type=F32)


def _in_proj_cast(x, mod, layer, g1, w_in_t, w_small, *, tn, rpg):
    n = x.shape[0]
    tm = n
    nb1, nb2 = C_GGATE // tn, C_MG // tn

    def w_rows(i, j):
        shift = jnp.where(j >= nb2, W_SHIFT2 // SUBLANE, jnp.where(j >= nb1, W_SHIFT1 // SUBLANE, 0))
        return (layer, (j * (tn // SUBLANE) + shift) * SUBLANE, 0)

    vmem = tm * D * 4 + 2 * tm * D * 4 + 2 * (tn * D * 4 + D * LANE * 4 + tm * tn * 4 + tm * LANE * 4 + D * tn * 2)
    vmem += tm * D * 2 + 3 * D * tn * 4 + (4 << 20)
    return pl.pallas_call(
        _in_proj_cast_kernel,
        grid=(1, N_MAIN // tn),
        in_specs=[
            pl.BlockSpec((tm, D), lambda i, j: (i, 0), pipeline_mode=pl.Buffered(1)),
            _modspec(layer, 1, tm, D, tm, rpg),
            _modspec(layer, 0, tm, D, tm, rpg),
            pl.BlockSpec((None, 1, D), lambda i, j: (layer, 0, 0)),
            pl.BlockSpec((None, pl.Element(tn), pl.Element(D)), w_rows),
            pl.BlockSpec((None, D, LANE), lambda i, j: (layer, 0, 0)),
        ],
        out_specs=[pl.BlockSpec((tm, tn), lambda i, j: (i, j)), pl.BlockSpec((tm, LANE), lambda i, j: (i, 0)),
                   pl.BlockSpec((D, tn), lambda i, j: (0, j))],
        out_shape=[SDS((n, N_MAIN), F32), SDS((n, LANE), F32), SDS((D, N_MAIN), BF16)],
        scratch_shapes=[pltpu.VMEM((tm, D), BF16)],
        compiler_params=_params(("arbitrary", "arbitrary"), vmem),
        name="in_proj_cast",
    )(x, mod, mod, g1, w_in_t, w_small)


def _wspec(shape, imap, layer, cast):
    if cast:
        return pl.BlockSpec((None,) + shape, lambda i, j: (layer,) + imap(i, j))
    return pl.BlockSpec(shape, imap)


def _merge_kernel(yl_ref, yg_ref, ym_ref, g0_ref, g1_ref, g2_ref, w0_ref, w1_ref, w2_ref, o_ref, *wc_refs):
    ws = [w_ref[...].astype(BF16) for w_ref in (w0_ref, w1_ref, w2_ref)]
    for wc_ref, wb in zip(wc_refs, ws):
        wc_ref[...] = wb
    acc = jax.nn.sigmoid(g0_ref[...]) * jnp.dot(yl_ref[...], ws[0], preferred_element_type=F32)
    acc += jax.nn.sigmoid(g1_ref[...]) * jnp.dot(yg_ref[...], ws[1], preferred_element_type=F32)
    acc += jax.nn.sigmoid(g2_ref[...]) * jnp.dot(ym_ref[...], ws[2], preferred_element_type=F32)
    o_ref[...] = acc.astype(BF16)


def _merge(y_lru, y_gla, y_ml, proj, layer, w_lru, w_gla, w_ml, *, tm, tn, cast):
    n = y_lru.shape[0]
    w = LRU_W
    gb = C_MG // tn
    nb = D // tn
    yspec = pl.BlockSpec((tm, w), lambda i, j: (i, 0))
    wspec = _wspec((w, tn), lambda i, j: (0, j), layer, cast)
    wbytes = 4 if cast else 2
    vmem = 2 * (3 * tm * w * 2 + 3 * tm * tn * 4 + 3 * w * tn * wbytes + tm * tn * 2) + 6 * tm * tn * 4 + (4 << 20)
    out_specs = [pl.BlockSpec((tm, tn), lambda i, j: (i, j))]
    out_shape = [SDS((n, D), BF16)]
    if cast:
        out_specs += [pl.BlockSpec((w, tn), lambda i, j: (0, j))] * 3
        out_shape += [SDS((w, D), BF16)] * 3
        vmem += 2 * 3 * w * tn * 2 + 3 * w * tn * 4
    return pl.pallas_call(
        _merge_kernel,
        grid=(n // tm, nb),
        in_specs=[
            yspec, yspec, yspec,
            pl.BlockSpec((tm, tn), lambda i, j: (i, gb + j)),
            pl.BlockSpec((tm, tn), lambda i, j: (i, gb + nb + j)),
            pl.BlockSpec((tm, tn), lambda i, j: (i, gb + 2 * nb + j)),
            wspec, wspec, wspec,
        ],
        out_specs=out_specs,
        out_shape=out_shape,
        compiler_params=_params(("arbitrary" if cast else "parallel", "arbitrary"), vmem),
        name="merge",
    )(y_lru, y_gla, y_ml, proj, proj, proj, w_lru, w_gla, w_ml)


def _out_proj_kernel(m_ref, w_ref, x_ref, gt_ref, o_ref, *wc_refs):
    wb = w_ref[...].astype(BF16)
    for wc_ref in wc_refs:
        wc_ref[...] = wb
    o_ref[...] = x_ref[...] + gt_ref[...] * jnp.dot(m_ref[...], wb, preferred_element_type=F32)


def _out_proj(merged, x, mod, layer, w_out, *, tm, tn, rpg, r, cast):
    n = x.shape[0]
    wbytes = 4 if cast else 2
    vmem = 2 * (tm * D * 2 + D * tn * wbytes + 2 * tm * tn * 4 + r * tn * 4) + 2 * tm * tn * 4 + (4 << 20)
    out_specs = [pl.BlockSpec((tm, tn), lambda i, j: (i, j))]
    out_shape = [SDS((n, D), F32)]
    if cast:
        out_specs.append(pl.BlockSpec((D, tn), lambda i, j: (0, j)))
        out_shape.append(SDS((D, D), BF16))
        vmem += 2 * D * tn * 2 + D * tn * 4
    return pl.pallas_call(
        _out_proj_kernel,
        grid=(n // tm, D // tn),
        in_specs=[
            pl.BlockSpec((tm, D), lambda i, j: (i, 0)),
            _wspec((D, tn), lambda i, j: (0, j), layer, cast),
            pl.BlockSpec((tm, tn), lambda i, j: (i, j)),
            _modspec(layer, 2, r, tn, tm, rpg, jmap=lambda j: j),
        ],
        out_specs=out_specs,
        out_shape=out_shape,
        compiler_params=_params(("arbitrary" if cast else "parallel", "arbitrary"), vmem),
        name="out_proj",
    )(merged, w_out, x, mod)


def _ffn_kernel(x_ref, sc_ref, sh_ref, gt_ref, g_ref, gf_ref, w1_ref, w2_ref, o_ref, *rest, final_norm):
    xn_s = rest[-1]
    f = pl.program_id(1)

    @pl.when(f == 0)
    def _():
        _norm_mod_store(xn_s, x_ref, g_ref, sc_ref, sh_ref)
        o_ref[...] = jnp.zeros_like(o_ref)

    w1 = w1_ref[...].astype(BF16)
    w2 = w2_ref[...].astype(BF16)
    for wc_ref, wb in zip(rest[:-1], (w1, w2)):
        wc_ref[...] = wb
    h = jnp.square(jnp.maximum(jnp.dot(xn_s[...], w1, preferred_element_type=F32), 0.0))
    o_ref[...] += jnp.dot(h.astype(BF16), w2, preferred_element_type=F32)

    @pl.when(f == pl.num_programs(1) - 1)
    def _():
        y = x_ref[...] + gt_ref[...] * o_ref[...]
        if final_norm:
            y = _rms(y, gf_ref[...])
        o_ref[...] = y


def _ffn(x, mod, layer, g2, g_final, w1, w2, *, tm, tf, rpg, r, final_norm, cast):
    n = x.shape[0]
    wbytes = 4 if cast else 2
    vmem = 2 * (2 * tm * D * 4 + 3 * r * D * 4 + 2 * D * tf * wbytes) + tm * D * 2 + 2 * tm * tf * 4 + 2 * tm * D * 4
    single = pl.Buffered(1) if n == tm else None
    out_specs = [pl.BlockSpec((tm, D), lambda i, f: (i, 0), pipeline_mode=single)]
    out_shape = [SDS((n, D), F32)]
    if cast:
        out_specs += [pl.BlockSpec((D, tf), lambda i, f: (0, f)), pl.BlockSpec((tf, D), lambda i, f: (f, 0))]
        out_shape += [SDS((D, D_FF), BF16), SDS((D_FF, D), BF16)]
        vmem += 2 * 2 * D * tf * 2
    return pl.pallas_call(
        functools.partial(_ffn_kernel, final_norm=final_norm),
        grid=(n // tm, D_FF // tf),
        in_specs=[
            pl.BlockSpec((tm, D), lambda i, f: (i, 0), pipeline_mode=single),
            _modspec(layer, 4, r, D, tm, rpg),
            _modspec(layer, 3, r, D, tm, rpg),
            _modspec(layer, 5, r, D, tm, rpg),
            pl.BlockSpec((None, 1, D), lambda i, f: (layer, 0, 0)),
            pl.BlockSpec((1, D), lambda i, f: (0, 0)),
            _wspec((D, tf), lambda i, f: (0, f), layer, cast),
            _wspec((tf, D), lambda i, f: (f, 0), layer, cast),
        ],
        out_specs=out_specs,
        out_shape=out_shape,
        scratch_shapes=[pltpu.VMEM((tm, D), BF16)],
        compiler_params=_params(("arbitrary" if cast else "parallel", "arbitrary"), vmem),
        name="ffn",
    )(x, mod, mod, mod, g2, g_final, w1, w2)


def _conv_chunk(xp_s, x, w, b, L):
    xp_s[SUBLANE:SUBLANE + L, :] = x
    acc = b + xp_s[5:5 + L, :] * w[0:1]
    for j in range(1, CONV_W):
        acc = acc + xp_s[5 + j:5 + j + L, :] * w[j:j + 1]
    tail = xp_s[5 + L:8 + L, :]
    xp_s[5:8, :] = tail
    return acc, tail


N_TOK_REFS, N_W_REFS, N_Y_REFS, N_STATE_REFS, N_SCRATCH = 8, 20, 3, 7, 9


def _prompt_mixer_kernel(*refs, L, layer, creates):
    tok = refs[:N_TOK_REFS]
    wts = refs[N_TOK_REFS:N_TOK_REFS + N_W_REFS]
    scr = refs[-N_SCRATCH:]
    outs = refs[-(N_SCRATCH + N_Y_REFS + N_STATE_REFS):-N_SCRATCH]
    ys, state_outs = outs[:N_Y_REFS], outs[N_Y_REFS:]
    nseq = tok[0].shape[0]

    @pl.when(pl.program_id(1) == 0)
    def _():
        xpl_s, xpm_s = scr[0], scr[1]
        xpl_s[:, 0:SUBLANE, :] = jnp.zeros((nseq, SUBLANE, LRU_W), F32)
        xpm_s[:, 0:SUBLANE, :] = jnp.zeros((nseq, SUBLANE, ML_W), F32)
        for s_ref in scr[4:]:
            s_ref[...] = jnp.zeros_like(s_ref)

    finals = [_prompt_mixer_rows([r.at[bi] for r in tok], wts, [y.at[bi] for y in ys], [s.at[bi] for s in scr], L)
              for bi in range(nseq)]

    @pl.when(pl.program_id(1) == pl.num_programs(1) - 1)
    def _():
        souts = state_outs
        if creates:
            for ref in souts:
                for other in range(DEPTH):
                    if other != layer:
                        ref[other] = jnp.zeros(ref.shape[1:], F32)
            souts = [ref.at[layer] for ref in souts]
        ht_ref, lct_ref, st_ref, ct_ref, nt_ref, mt_ref, mct_ref = souts
        s_s, c_s, n_s, m_s = scr[5:]
        for bi, (h, l_tail, m_tail) in enumerate(finals):
            ht_ref[bi] = h
            lct_ref[bi] = l_tail
            mct_ref[bi] = m_tail
        st_ref[...] = s_s[...]
        ct_ref[...] = c_s[...]
        nt_ref[...] = n_s[...]
        mt_ref[...] = m_s[...]


def _prompt_mixer_rows(tok, wts, ys, scr, L):
    lx_ref, q_ref, k_ref, v_ref, gate_ref, mx_ref, mo_ref, small_ref = tok
    (lcw_ref, lcb_ref, wa_ref, ba_ref, wx_ref, bx_ref, lam_ref, wg2_ref, bg_ref, ggn_ref,
     mcw_ref, mcb_ref, wqh_ref, wql_ref, wkh_ref, wkl_ref, wvh_ref, wvl_ref, bif_ref, mgn_ref) = wts
    yl_ref, yg_ref, ym_ref = ys
    xpl_s, xpm_s, a_s, u_s, h_s, s_s, c_s, n_s, m_s = scr
    rows = lx_ref.shape[0]

    q, k, v, gate = q_ref[...], k_ref[...], v_ref[...], gate_ref[...]
    small = small_ref[...]
    logdec = _log_sigmoid(_dot(small, wg2_ref[...]) + bg_ref[...]) * (1.0 / GLA_TAU)
    mx = mx_ref[...]
    conv, m_tail = _conv_chunk(xpm_s, mx, mcw_ref[...], mcb_ref[...], rows)
    mc = _silu(conv)
    mq = _blockdiag3(mc, wqh_ref, wql_ref)
    mk = _blockdiag3(mc, wkh_ref, wkl_ref) * (ML_DH ** -0.5)
    mv = _blockdiag3(mx, wvh_ref, wvl_ref)

    after_proj = jnp.where(mq[0:1, :] != mq[0:1, :], 1.0, 0.0)
    u, l_tail = _conv_chunk(xpl_s, lx_ref[...], lcw_ref[...], lcb_ref[...] + after_proj, rows)
    for blk in range(LRU_BLOCKS):
        sl = slice(blk * LANE, (blk + 1) * LANE)
        a_s[:, sl], u_s[:, sl] = _lru_gates(u[:, sl], wa_ref[blk], wx_ref[blk], ba_ref[:, sl], bx_ref[:, sl],
                                            lam_ref[:, sl])

    gates = small + bif_ref[...]
    gates_t = gates.T
    lf = _log_sigmoid(gates)
    lf_t = _log_sigmoid(gates_t)
    mo = mo_ref[...]
    tri = _tri(L)
    tri_b = tri.astype(BF16)

    for cc in range(rows // L):
        rs = slice(cc * L, (cc + 1) * L)
        bcum = _cumsum_rows(tri_b, logdec[rs, :])
        for hh in range(GLA_H):
            ks = slice(hh * GLA_DK, (hh + 1) * GLA_DK)
            vs = slice(hh * GLA_DV, (hh + 1) * GLA_DV)
            bh = bcum[:, ks]
            k_h, v_h = k[rs, ks], v[rs, vs]
            b_mid = bh[L // 2 - 1:L // 2, :]
            qe = (q[rs, ks] * (GLA_DK ** -0.5)) * jnp.exp(bh - b_mid)
            ke = k_h * jnp.exp(b_mid - bh)
            a = jnp.where(tri, _dot_nt(qe, ke), 0.0)
            s_h = s_s[hh]
            bl = bh[L - 1:L, :]
            cols = jnp.exp(jnp.concatenate([jnp.broadcast_to(b_mid, (SUBLANE // 2, GLA_DK)),
                                            jnp.broadcast_to(bl, (SUBLANE // 2, GLA_DK))], axis=0)).T
            mid_col, dec_col = cols[:, 0:1], cols[:, SUBLANE // 2:SUBLANE // 2 + 1]
            o = _dot(qe, mid_col * s_h) + _dot(a, v_h)
            kd = k_h * jnp.exp(bl - bh)
            s_s[hh] = dec_col * s_h + _dot(kd.T, v_h)
            yg_ref[rs, vs] = (_rms(o, ggn_ref[...]) * _silu(gate[rs, vs])).astype(BF16)

    L = min(rows, ML_CHUNK)
    tri = _tri(L)
    tri_b = tri.astype(BF16)
    r_i = lax.broadcasted_iota(jnp.int32, (L, L), 0)
    c_i = lax.broadcasted_iota(jnp.int32, (L, L), 1)
    triu_b = (r_i <= c_i).astype(BF16)
    for cc in range(rows // L):
        rs = slice(cc * L, (cc + 1) * L)
        f_col = _cumsum_rows(tri_b, lf[rs, :])
        f_row = _cumsum_lanes(lf_t[:, rs], triu_b)
        for hh in range(ML_H):
            sl = slice(hh * ML_DH, (hh + 1) * ML_DH)
            qh, kh, vh = mq[rs, sl], mk[rs, sl], mv[rs, sl]
            fc = f_col[:, S_FG + hh:S_FG + hh + 1]
            fr = f_row[S_FG + hh:S_FG + hh + 1, :]
            igr = gates_t[S_IG + hh:S_IG + hh + 1, rs]
            m_h = m_s[hh:hh + 1, 0:1]
            dm = jnp.where(tri, fc - fr + igr, NEG_BIG)
            inter = fc + m_h
            mt = jnp.maximum(inter, jnp.max(dm, axis=-1, keepdims=True))
            ci = jnp.exp(inter - mt)
            s = _dot_nt(qh, kh) * jnp.exp(dm - mt)
            c_h = c_s[hh]
            n_h = n_s[hh:hh + 1, :]
            num = ci * _dot(qh, c_h) + _dot(s, vh)
            den = ci * jnp.sum(qh * n_h, axis=-1, keepdims=True) + jnp.sum(s, axis=-1, keepdims=True)
            hcell = num / jnp.maximum(jnp.abs(den), jnp.exp(-mt))
            fl = fr[:, L - 1:L]
            dj = fl - fr + igr
            m_new = jnp.maximum(fl + m_h, jnp.max(dj, axis=-1, keepdims=True))
            cs = jnp.exp(fl + m_h - m_new)
            wj = jnp.exp(dj - m_new)
            c_s[hh] = cs * c_h + _dot(kh.T * wj, vh)
            n_s[hh:hh + 1, :] = cs * n_h + _dot(jnp.broadcast_to(wj, (SUBLANE, L)), kh)[0:1, :]
            m_s[hh:hh + 1, :] = jnp.broadcast_to(m_new, (1, LANE))
            ym_ref[rs, sl] = (jax.nn.sigmoid(mo[rs, sl]) * _rms(hcell, mgn_ref[...])).astype(BF16)

    t8 = lax.broadcasted_iota(jnp.int32, (SUBLANE, LRU_W), 0)
    h = h_s[...]
    for g in range(rows // SUBLANE):
        r8 = slice(g * SUBLANE, (g + 1) * SUBLANE)
        a8, u8 = a_s[r8, :], u_s[r8, :]
        for s in (1, 2, 4):
            u8 = jnp.where(t8 >= s, u8 + a8 * pltpu.roll(u8, s, 0), u8)
            a8 = jnp.where(t8 >= s, a8 * pltpu.roll(a8, s, 0), a8)
        h8 = u8 + a8 * h
        a_s[r8, :] = h8
        h = h8[SUBLANE - 1:SUBLANE, :]
    h_s[...] = h
    yl_ref[...] = a_s[...].astype(BF16)
    return h, l_tail, m_tail


def _prompt_mixer(proj3, small3, layer, w, prev, *, L, rows, nseq):
    b, t, _ = proj3.shape
    kw, vw = GLA_H * GLA_DK, GLA_H * GLA_DV
    nblk = ML_W // LANE
    blk3 = lambda width, cb: pl.BlockSpec((nseq, rows, width), lambda bi, c: (bi, c, cb))
    lw3 = lambda s: pl.BlockSpec((None,) + s, lambda bi, c: (layer, 0, 0))
    lw4 = lambda s: pl.BlockSpec((None,) + s, lambda bi, c: (layer, 0, 0, 0))
    creates = prev is None
    lead, at = ((DEPTH,), 0) if creates else ((None,), layer)
    st3 = lambda s: pl.BlockSpec(lead + (nseq,) + s, lambda bi, c: (at, bi, 0, 0))
    st4 = lambda s: pl.BlockSpec(lead + (nseq,) + s, lambda bi, c: (at, bi, 0, 0, 0))
    in_specs = [
        blk3(LRU_W, C_LRU // LRU_W), blk3(kw, C_GQ // kw), blk3(kw, C_GK // kw), blk3(vw, C_GV // vw),
        blk3(vw, C_GGATE // vw), blk3(ML_W, C_MX // ML_W), blk3(ML_W, C_MO // ML_W), blk3(LANE, 0),
        lw3((CONV_W, LRU_W)), lw3((1, LRU_W)), lw4((LRU_BLOCKS, LANE, LANE)), lw3((1, LRU_W)),
        lw4((LRU_BLOCKS, LANE, LANE)), lw3((1, LRU_W)), lw3((1, LRU_W)),
        lw3((LANE, kw)), lw3((1, kw)), lw3((1, GLA_DV)),
        lw3((CONV_W, ML_W)), lw3((1, ML_W)), *([lw4((nblk, LANE, LANE))] * 6), lw3((1, LANE)), lw3((1, ML_DH)),
    ]
    args = [proj3] * 7 + [small3,
                          w["lru_cw"], w["lru_cb"], w["lru_wa"], w["lru_ba"], w["lru_wx"], w["lru_bx"], w["lru_lam"],
                          w["wg2"], w["bg"], w["gla_gn"],
                          w["ml_cw"], w["ml_cb"], *w["ml_wq"], *w["ml_wk"], *w["ml_wv"], w["ml_bif"], w["ml_gn"]]
    aliases = {}
    if prev is not None:
        aliases = {len(in_specs) + j: 3 + j for j in range(len(prev))}
        in_specs += [pl.BlockSpec(memory_space=pl.ANY)] * len(prev)
        args += list(prev)
    sbytes = GLA_H * GLA_DK * GLA_DV * 4
    cbytes = ML_H * ML_DH * ML_DH * 4
    vmem = nseq * (2 * rows * (7 * 1024 * 4 + 3 * 1024 * 2) + 3 * (sbytes + cbytes) + 6 * rows * 1024 * 4)
    vmem += 30 * rows * 1024 * 4 + (12 << 20)
    sc = lambda *s: pltpu.VMEM((nseq,) + s, F32)
    return pl.pallas_call(
        functools.partial(_prompt_mixer_kernel, L=L, layer=layer, creates=creates),
        grid=(b // nseq, t // rows),
        in_specs=in_specs,
        out_specs=[blk3(LRU_W, 0), blk3(vw, 0), blk3(ML_W, 0),
                   st3((1, LRU_W)), st3((CONV_W - 1, LRU_W)), st4((GLA_H, GLA_DK, GLA_DV)),
                   st4((ML_H, ML_DH, ML_DH)), st3((ML_H, ML_DH)), st3((ML_H, LANE)), st3((CONV_W - 1, ML_W))],
        out_shape=[SDS((b, t, LRU_W), BF16), SDS((b, t, vw), BF16), SDS((b, t, ML_W), BF16),
                   SDS((DEPTH, b, 1, LRU_W), F32), SDS((DEPTH, b, CONV_W - 1, LRU_W), F32),
                   SDS((DEPTH, b, GLA_H, GLA_DK, GLA_DV), F32), SDS((DEPTH, b, ML_H, ML_DH, ML_DH), F32),
                   SDS((DEPTH, b, ML_H, ML_DH), F32), SDS((DEPTH, b, ML_H, LANE), F32),
                   SDS((DEPTH, b, CONV_W - 1, ML_W), F32)],
        scratch_shapes=[sc(SUBLANE + rows, LRU_W), sc(SUBLANE + rows, ML_W), sc(rows, LRU_W), sc(rows, LRU_W),
                        sc(1, LRU_W), sc(GLA_H, GLA_DK, GLA_DV), sc(ML_H, ML_DH, ML_DH), sc(ML_H, ML_DH),
                        sc(ML_H, LANE)],
        input_output_aliases=aliases,
        compiler_params=_params(("parallel", "arbitrary"), vmem),
        name="prompt_mixer",
    )(*args)


def _tmask(shape):
    return lax.broadcasted_iota(jnp.int32, shape, 0) % T_STEP


def _down(x, s):
    return x if s == 0 else pltpu.roll(x, s, 0)


def _up(x, s):
    return x if s == 0 else pltpu.roll(x, x.shape[0] - s, 0)


def _seg_cumsum(x):
    t = _tmask(x.shape)
    out = x
    for s in range(1, T_STEP):
        out = out + jnp.where(t >= s, _down(x, s), 0.0)
    return out


def _seg_last(x):
    t = _tmask(x.shape)
    out = x
    for s in range(1, T_STEP):
        out = jnp.where(t == T_STEP - 1 - s, _up(x, s), out)
    return out


def _seg_allreduce(x, op):
    t = _tmask(x.shape)
    y = op(x, jnp.where(t % 2 == 1, _down(x, 1), _up(x, 1)))
    return op(y, jnp.where(t >= 2, _down(y, 2), _up(y, 2)))


def _conv_rows(x, e, w, b):
    t = _tmask(x.shape)
    acc = b + x * w[CONV_W - 1:CONV_W]
    for k in range(1, CONV_W):
        hist = jnp.where(t >= k, _down(x, k), _up(e, CONV_W - 1 - k))
        acc = acc + hist * w[CONV_W - 1 - k:CONV_W - k]
    return acc


def _col(x, lane):
    li = lax.broadcasted_iota(jnp.int32, x.shape, 1)
    return jnp.sum(jnp.where(li == lane, x, 0.0), axis=-1, keepdims=True)


def _sample_mixer_kernel(lx_ref, q_ref, k_ref, v_ref, gate_ref, mx_ref, mo_ref, small_ref,
                         el_ref, em_ref, h0_ref, n0_ref, m0_ref, s0_ref, c0_ref,
                         lcw_ref, lcb_ref, wa_ref, ba_ref, wx_ref, bx_ref, lam_ref,
                         wg2_ref, bg_ref, ggn_ref,
                         mcw_ref, mcb_ref, wqh_ref, wql_ref, wkh_ref, wkl_ref, wvh_ref, wvl_ref, bif_ref, mgn_ref,
                         *rest, layer, creates):
    yl_ref, yg_ref, ym_ref, *state_refs = rest[-8:]
    row_refs, (st_ref, ct_ref) = state_refs[:3], state_refs[3:]
    if creates:
        for ref in row_refs:
            for other in range(DEPTH):
                if other != layer:
                    ref[other] = jnp.zeros(ref.shape[1:], F32)
        row_refs = [ref.at[layer] for ref in row_refs]
    hrow_ref, nrow_ref, mrow_ref = row_refs
    rows = lx_ref.shape[0]
    n_pairs = rows // SUBLANE
    seq_per_tile = SUBLANE // T_STEP
    t_col = _tmask((rows, 1))
    row8 = lax.broadcasted_iota(jnp.int32, (SUBLANE, 1), 0)

    u = _conv_rows(lx_ref[...], el_ref[...], lcw_ref[...], lcb_ref[...])
    a_blocks, g_blocks = [], []
    for blk in range(LRU_BLOCKS):
        sl = slice(blk * LANE, (blk + 1) * LANE)
        a_b, g_b = _lru_gates(u[:, sl], wa_ref[blk], wx_ref[blk], ba_ref[:, sl], bx_ref[:, sl], lam_ref[:, sl])
        a_blocks.append(a_b)
        g_blocks.append(g_b)
    a = jnp.concatenate(a_blocks, axis=1)
    t_w = _tmask(a.shape)
    g = jnp.concatenate(g_blocks, axis=1) + jnp.where(t_w == 0, a * h0_ref[...], 0.0)
    a1 = jnp.where(t_w >= 1, a * _down(a, 1), a)
    g1 = jnp.where(t_w >= 1, g + a * _down(g, 1), g)
    h = jnp.where(t_w >= 2, g1 + a1 * _down(g1, 2), g1)
    yl_ref[...] = h.astype(BF16)
    hrow_ref[...] = h

    q, k, v = q_ref[...], k_ref[...], v_ref[...]
    logdec = _log_sigmoid(_dot(small_ref[...], wg2_ref[...]) + bg_ref[...]) * (1.0 / GLA_TAU)
    bc = _seg_cumsum(logdec)
    bl = _seg_last(bc)
    qs = q * (GLA_DK ** -0.5)
    o_heads = [jnp.zeros((rows, GLA_DV), F32) for _ in range(GLA_H)]
    for s in range(T_STEP):
        prod = qs * _down(k, s) * jnp.exp(bc - _down(bc, s))
        v_s = _down(v, s)
        for hh in range(GLA_H):
            a_sh = jnp.sum(prod[:, hh * GLA_DK:(hh + 1) * GLA_DK], axis=-1, keepdims=True)
            a_sh = jnp.where(t_col >= s, a_sh, 0.0)
            o_heads[hh] = o_heads[hh] + a_sh * v_s[:, hh * GLA_DV:(hh + 1) * GLA_DV]
    qe = qs * jnp.exp(bc)
    kd = k * jnp.exp(bl - bc)
    dec = jnp.exp(bl)
    gate = gate_ref[...]
    for hh in range(GLA_H):
        ks = slice(hh * GLA_DK, (hh + 1) * GLA_DK)
        vs = slice(hh * GLA_DV, (hh + 1) * GLA_DV)
        o_state = []
        for p in range(n_pairs):
            r8 = slice(p * SUBLANE, (p + 1) * SUBLANE)
            kd_t = kd[r8, ks].T
            dec_t = dec[r8, ks].T
            res = None
            for j in range(seq_per_tile):
                b = p * seq_per_tile + j
                s_b = s0_ref[b, hh]
                r_j = _dot(qe[r8, ks], s_b)
                res = r_j if res is None else jnp.where(row8 // T_STEP == j, r_j, res)
                v_j = jnp.where(row8 // T_STEP == j, v[r8, vs], 0.0)
                last = j * T_STEP + T_STEP - 1
                st_ref[b, hh] = dec_t[:, last:last + 1] * s_b + _dot(kd_t, v_j)
            o_state.append(res)
        o = o_heads[hh] + jnp.concatenate(o_state, axis=0)
        yg_ref[:, vs] = (_rms(o, ggn_ref[...]) * _silu(gate[:, vs])).astype(BF16)

    mx = mx_ref[...]
    mc = _silu(_conv_rows(mx, em_ref[...], mcw_ref[...], mcb_ref[...]))
    mq = _blockdiag3(mc, wqh_ref, wql_ref)
    mk = _blockdiag3(mc, wkh_ref, wkl_ref) * (ML_DH ** -0.5)
    mv = _blockdiag3(mx, wvh_ref, wvl_ref)
    gts = small_ref[...] + bif_ref[...]
    ig = pltpu.roll(gts, S_FG - S_IG, 1)
    fcum = _seg_cumsum(_log_sigmoid(gts))
    flast = _seg_last(fcum)
    m0 = m0_ref[...]
    t_g = _tmask(gts.shape)
    inter = fcum + m0
    dms = [jnp.where(t_g >= s, fcum - _down(fcum, s) + _down(ig, s), NEG_BIG) for s in range(T_STEP)]
    mt = inter
    for dm in dms:
        mt = jnp.maximum(mt, dm)
    ci_t = jnp.exp(inter - mt)
    emt_t = jnp.exp(-mt)
    w_t = [jnp.exp(dm - mt) for dm in dms]
    dj = flast - fcum + ig
    m_new = jnp.maximum(flast + m0, _seg_allreduce(dj, jnp.maximum))
    cs_t = jnp.exp(flast + m0 - m_new)
    wj_t = jnp.exp(dj - m_new)
    mrow_ref[...] = m_new
    mo = mo_ref[...]
    n0 = n0_ref[...]
    for hh in range(ML_H):
        sl = slice(hh * ML_DH, (hh + 1) * ML_DH)
        lane = S_FG + hh
        qh, kh, vh = mq[:, sl], mk[:, sl], mv[:, sl]
        ci, emt, cs, wj = _col(ci_t, lane), _col(emt_t, lane), _col(cs_t, lane), _col(wj_t, lane)
        num = jnp.zeros((rows, ML_DH), F32)
        den = ci * jnp.sum(qh * n0[:, sl], axis=-1, keepdims=True)
        for s in range(T_STEP):
            sc = jnp.sum(qh * _down(kh, s), axis=-1, keepdims=True) * _col(w_t[s], lane)
            num = num + sc * _down(vh, s)
            den = den + sc
        kw = kh * wj
        nrow_ref[:, sl] = cs * n0[:, sl] + _seg_allreduce(kw, jnp.add)
        qc = []
        for p in range(n_pairs):
            r8 = slice(p * SUBLANE, (p + 1) * SUBLANE)
            kw_t = kw[r8, :].T
            res = None
            for j in range(seq_per_tile):
                b = p * seq_per_tile + j
                c_b = c0_ref[b, hh]
                r_j = _dot(qh[r8, :], c_b)
                res = r_j if res is None else jnp.where(row8 // T_STEP == j, r_j, res)
                v_j = jnp.where(row8 // T_STEP == j, vh[r8, :], 0.0)
                last = p * SUBLANE + j * T_STEP + T_STEP - 1
                ct_ref[b, hh] = cs[last:last + 1, :] * c_b + _dot(kw_t, v_j)
            qc.append(res)
        num = num + ci * jnp.concatenate(qc, axis=0)
        hcell = num / jnp.maximum(jnp.abs(den), emt)
        ym_ref[:, sl] = (jax.nn.sigmoid(mo[:, sl]) * _rms(hcell, mgn_ref[...])).astype(BF16)


def _sample_mixer(proj, small, el, em, h0e, n0e, m0e, s_state, c_state, layer, w, row_prev, big_prev):
    n = proj.shape[0]
    nseq = n // T_STEP
    bb = STEP_BB
    rows = bb * T_STEP
    row = lambda width, blk: pl.BlockSpec((rows, width), lambda i: (i, blk))
    lrow = lambda width: pl.BlockSpec((None, rows, width), lambda i: (layer, i, 0))
    lw3 = lambda s: pl.BlockSpec((None,) + s, lambda i: (layer, 0, 0))
    lw4 = lambda s: pl.BlockSpec((None,) + s, lambda i: (layer, 0, 0, 0))
    sspec = pl.BlockSpec((None, bb, GLA_H, GLA_DK, GLA_DV), lambda i: (layer, i, 0, 0, 0))
    cspec = pl.BlockSpec((None, bb, ML_H, ML_DH, ML_DH), lambda i: (layer, i, 0, 0, 0))
    nblk = ML_W // LANE
    kw, vw = GLA_H * GLA_DK, GLA_H * GLA_DV
    in_specs = [
        row(LRU_W, C_LRU // LRU_W), row(kw, C_GQ // kw), row(kw, C_GK // kw), row(vw, C_GV // vw),
        row(vw, C_GGATE // vw), row(ML_W, C_MX // ML_W), row(ML_W, C_MO // ML_W), row(LANE, 0),
        lrow(LRU_W), lrow(ML_W), lrow(LRU_W), lrow(ML_W), lrow(LANE), sspec, cspec,
        lw3((CONV_W, LRU_W)), lw3((1, LRU_W)), lw4((LRU_BLOCKS, LANE, LANE)), lw3((1, LRU_W)),
        lw4((LRU_BLOCKS, LANE, LANE)), lw3((1, LRU_W)), lw3((1, LRU_W)),
        lw3((LANE, kw)), lw3((1, kw)), lw3((1, GLA_DV)),
        lw3((CONV_W, ML_W)), lw3((1, ML_W)), *([lw4((nblk, LANE, LANE))] * 6), lw3((1, LANE)), lw3((1, ML_DH)),
    ]
    args = [proj] * 7 + [small, el, em, h0e, n0e, m0e, s_state, c_state,
                         w["lru_cw"], w["lru_cb"], w["lru_wa"], w["lru_ba"], w["lru_wx"], w["lru_bx"], w["lru_lam"],
                         w["wg2"], w["bg"], w["gla_gn"],
                         w["ml_cw"], w["ml_cb"], *w["ml_wq"], *w["ml_wk"], *w["ml_wv"], w["ml_bif"], w["ml_gn"]]
    creates = row_prev is None
    prev = list(big_prev) if creates else list(row_prev) + list(big_prev)
    first_out = 6 if creates else 3
    aliases = {len(in_specs) + j: first_out + j for j in range(len(prev))}
    in_specs += [pl.BlockSpec(memory_space=pl.ANY)] * len(prev)
    args += prev
    sbytes = bb * GLA_H * GLA_DK * GLA_DV * 4
    cbytes = bb * ML_H * ML_DH * ML_DH * 4
    vmem = 4 * (sbytes + cbytes) + 2 * rows * (7 * 1024 + 8 * 1024) * 4 + 80 * rows * 1024 * 4 + (12 << 20)
    row_specs = [lrow(LRU_W), lrow(ML_W), lrow(LANE)]
    if creates:
        all_rows = lambda width: pl.BlockSpec((DEPTH, rows, width), lambda i: (0, i, 0))
        row_specs = [all_rows(LRU_W), all_rows(ML_W), all_rows(LANE)]
    state_specs = row_specs + [sspec, cspec]
    return pl.pallas_call(
        functools.partial(_sample_mixer_kernel, layer=layer, creates=creates),
        grid=(nseq // bb,),
        in_specs=in_specs,
        out_specs=[row(LRU_W, 0), row(vw, 0), row(ML_W, 0), *state_specs],
        out_shape=[SDS((n, LRU_W), BF16), SDS((n, vw), BF16), SDS((n, ML_W), BF16),
                   SDS((DEPTH, n, LRU_W), F32), SDS((DEPTH, n, ML_W), F32), SDS((DEPTH, n, LANE), F32),
                   SDS((DEPTH, nseq, GLA_H, GLA_DK, GLA_DV), F32), SDS((DEPTH, nseq, ML_H, ML_DH, ML_DH), F32)],
        input_output_aliases=aliases,
        compiler_params=_params(("arbitrary",), vmem),
        name="sample_mixer",
    )(*args)


def _prep_weights(p):
    w_in = p["w_in"]
    c1 = C_GGATE
    c2 = C_MG + W_SHIFT1
    w_small = jnp.concatenate(
        [w_in[:, :, c1:c1 + GLA_RANK], w_in[:, :, c2:c2 + 2 * ML_H],
         jnp.zeros((DEPTH, D, LANE - GLA_RANK - 2 * ML_H), F32)], axis=-1)
    wg2 = jnp.concatenate(
        [p["gla_w_g2"], jnp.zeros((DEPTH, LANE - GLA_RANK, GLA_H * GLA_DK), F32)], axis=1)
    blk_of = jnp.arange(LANE) // ML_BS
    on_diag = blk_of[:, None] == blk_of[None, :]

    def dense_bd(wb):
        rows = wb.reshape(DEPTH, ML_W // LANE, LANE, ML_BS)
        dense = jnp.where(on_diag, jnp.tile(rows, (1, 1, 1, LANE // ML_BS)), 0.0)
        hi = dense.astype(BF16)
        return hi, (dense - hi.astype(F32)).astype(BF16)

    bif = jnp.zeros((DEPTH, 1, LANE), F32).at[:, 0, S_IG:S_IG + 2 * ML_H].set(p["ml_b_if"])
    r3 = lambda a: a.reshape(DEPTH, 1, a.shape[-1])
    return dict(
        w_small=w_small, wg2=wg2, bg=r3(p["gla_b_g"]), gla_gn=r3(p["gla_g_norm"]),
        g1=r3(p["g_norm1"]), g2=r3(p["g_norm2"]), gf=p["g_final"].reshape(1, D),
        lru_cw=p["lru_conv_w"], lru_cb=r3(p["lru_conv_b"]), lru_wa=p["lru_w_a"], lru_ba=r3(p["lru_b_a"]),
        lru_wx=p["lru_w_x"], lru_bx=r3(p["lru_b_x"]), lru_lam=r3(p["lru_lam"]),
        ml_cw=p["ml_conv_w"], ml_cb=r3(p["ml_conv_b"]), ml_wq=dense_bd(p["ml_w_q"]), ml_wk=dense_bd(p["ml_w_k"]),
        ml_wv=dense_bd(p["ml_w_v"]), ml_bif=bif, ml_gn=r3(p["ml_g_norm"]),
        w_in_t=jnp.swapaxes(w_in, 1, 2), w_br=(p["w_br_lru"], p["w_br_gla"], p["w_br_ml"]), w_out=p["w_out"], w_ff1=p["w_ff1"],
        w_ff2=p["w_ff2"],
    )


def _trunk(xp3, xs3, mod_p, mod_s, states, w):
    bp, tp, _ = xp3.shape
    bs, ts, _ = xs3.shape
    n_p, n_s = bp * tp, bs * ts
    tm_p = min(tp, 1024)
    tm_f = min(tp, 512)
    s_h, s_cv, s_gla, s_c, s_n, s_m, s_mcv = states
    pad_t = lambda a: jnp.pad(a, ((0, 0), (0, 0), (0, ts - a.shape[2]), (0, 0))).reshape(DEPTH, n_s, a.shape[-1])
    el = pad_t(s_cv)
    em = pad_t(s_mcv)
    h0e = pad_t(s_h[:, :, None, :])
    n0e = jnp.broadcast_to(s_n.reshape(DEPTH, bs, 1, ML_W), (DEPTH, bs, ts, ML_W)).reshape(DEPTH, n_s, ML_W)
    m_l = jnp.pad(s_m, ((0, 0), (0, 0), (S_FG, LANE - S_FG - ML_H)))
    m0e = jnp.broadcast_to(m_l[:, :, None, :], (DEPTH, bs, ts, LANE)).reshape(DEPTH, n_s, LANE)
    x_p = xp3.reshape(n_p, D)
    x_s = xs3.reshape(n_s, D)
    prev_p = None
    conv_l, conv_m = [], []
    keep = ts - (CONV_W - 1)
    for l in range(DEPTH):
        last = l == DEPTH - 1
        proj_s, small_s, w_main = _in_proj_cast(x_s, mod_s, l, w["g1"], w["w_in_t"], w["w_small"], tn=512, rpg=n_s)
        if l == 0:
            s_shape = (DEPTH, bs, GLA_H, GLA_DK, GLA_DV)
            c_shape = (DEPTH, bs, ML_H, ML_DH, ML_DH)
            flat_rows = lambda shape: shape[0] * shape[1] * shape[2] * shape[3] * shape[4] // GLA_DV
            proj_p, small_p, s_buf, c_buf = _in_proj(x_p, mod_p, l, w["g1"], w_main, w["w_small"], tm=tm_p, tn=1024,
                                                     rpg=tp, r=1, zero_rows=(flat_rows(s_shape), flat_rows(c_shape)))
            big_s = (s_buf.reshape(s_shape), c_buf.reshape(c_shape))
            rows_s = None
        else:
            proj_p, small_p = _in_proj(x_p, mod_p, l, w["g1"], w_main, w["w_small"], tm=tm_p, tn=1024, rpg=tp, r=1)
        outs_s = _sample_mixer(proj_s, small_s, el, em, h0e, n0e, m0e, s_gla, s_c, l, w, rows_s, big_s)
        rows_s, big_s = outs_s[3:6], outs_s[6:]
        outs_p = _prompt_mixer(proj_p.reshape(bp, tp, N_MAIN), small_p.reshape(bp, tp, LANE), l, w, prev_p,
                               L=min(tp, CHUNK), rows=min(tp, PROMPT_ROWS), nseq=PROMPT_SEQS)
        prev_p = outs_p[3:]
        conv_l.append(proj_s[:, C_LRU:C_LRU + LRU_W].reshape(bs, ts, LRU_W)[:, keep:])
        conv_m.append(proj_s[:, C_MX:C_MX + ML_W].reshape(bs, ts, ML_W)[:, keep:])
        merged_s, *w_br = _merge(*outs_s[:3], proj_s, l, *w["w_br"], tm=n_s, tn=512, cast=True)
        (merged_p,) = _merge(*(y.reshape(n_p, y.shape[-1]) for y in outs_p[:3]), proj_p, l, *w_br, tm=tm_p, tn=512,
                             cast=False)
        x_s, w_out = _out_proj(merged_s, x_s, mod_s, l, w["w_out"], tm=n_s, tn=512, rpg=n_s, r=n_s, cast=True)
        (x_p,) = _out_proj(merged_p, x_p, mod_p, l, w_out, tm=tm_p, tn=512, rpg=tp, r=1, cast=False)
        x_s, w_ff1, w_ff2 = _ffn(x_s, mod_s, l, w["g2"], w["gf"], w["w_ff1"], w["w_ff2"], tm=n_s, tf=512, rpg=n_s,
                                 r=n_s, final_norm=last, cast=True)
        (x_p,) = _ffn(x_p, mod_p, l, w["g2"], w["gf"], w_ff1, w_ff2, tm=tm_f, tf=1024, rpg=tp, r=1, final_norm=last,
                      cast=False)
    h_t, lcv_t, s_t, c_t, n_t, m_t, mcv_t = prev_p
    p_states = [h_t.reshape(DEPTH, bp, LRU_W), lcv_t, s_t, c_t, n_t, m_t[..., 0], mcv_t]
    hrow, nrow, mrow = rows_s
    s_out, c_out = big_s
    t_last = ts - 1
    s_states = [hrow.reshape(DEPTH, bs, ts, LRU_W)[:, :, t_last], jnp.stack(conv_l), s_out, c_out,
                nrow.reshape(DEPTH, bs, ts, ML_H, ML_DH)[:, :, t_last],
                mrow.reshape(DEPTH, bs, ts, LANE)[:, :, t_last, S_FG:S_FG + ML_H], jnp.stack(conv_m)]
    return x_p.reshape(bp, tp, D), x_s.reshape(bs, ts, D), p_states, s_states


def kernel(x_prompt, x_sample, c_prompt, c_sample, state_lru_h, state_lru_conv, state_gla, state_mlstm_C, state_mlstm_n, state_mlstm_m, state_mlstm_conv, w_ada, b_ada, g_norm1, g_norm2, w_in, lru_conv_w, lru_conv_b, lru_w_a, lru_b_a, lru_w_x, lru_b_x, lru_lam, gla_w_g2, gla_b_g, gla_g_norm, ml_conv_w, ml_conv_b, ml_w_q, ml_w_k, ml_w_v, ml_b_if, ml_g_norm, w_br_lru, w_br_gla, w_br_ml, w_out, w_ff1, w_ff2, g_final):
    p = dict(g_norm1=g_norm1, g_norm2=g_norm2, w_in=w_in, lru_conv_w=lru_conv_w, lru_conv_b=lru_conv_b,
             lru_w_a=lru_w_a, lru_b_a=lru_b_a, lru_w_x=lru_w_x, lru_b_x=lru_b_x, lru_lam=lru_lam,
             gla_w_g2=gla_w_g2, gla_b_g=gla_b_g, gla_g_norm=gla_g_norm, ml_conv_w=ml_conv_w, ml_conv_b=ml_conv_b,
             ml_w_q=ml_w_q, ml_w_k=ml_w_k, ml_w_v=ml_w_v, ml_b_if=ml_b_if, ml_g_norm=ml_g_norm,
             w_br_lru=w_br_lru, w_br_gla=w_br_gla, w_br_ml=w_br_ml, w_out=w_out, w_ff1=w_ff1, w_ff2=w_ff2,
             g_final=g_final)
    w = _prep_weights(p)
    bp = x_prompt.shape[0]
    bs, ts, _ = x_sample.shape
    assert ts == T_STEP and bs % STEP_BB == 0
    ns = bs * ts
    pad = (-(ns + bp)) % SUBLANE
    c_all = jnp.concatenate([jnp.repeat(c_sample, ts, axis=0), c_prompt, jnp.zeros((pad, D), F32)], axis=0)
    mod_all = _ada(c_all, w_ada, b_ada)
    mod_s = mod_all.reshape(DEPTH, 1, ns + bp + pad, N_MOD * D)
    mod_p = mod_all[:, ns:ns + bp].reshape(DEPTH, bp, 1, N_MOD * D)
    y_p, y_s, ps, ss = _trunk(x_prompt, x_sample, mod_p, mod_s,
                              (state_lru_h, state_lru_conv, state_gla, state_mlstm_C, state_mlstm_n, state_mlstm_m,
                               state_mlstm_conv), w)
    return (y_p, y_s, *ps, *ss)
```

```python
import functools

import jax
import jax.numpy as jnp
from jax import lax
from jax.experimental import pallas as pl
from jax.experimental.pallas import tpu as pltpu

F32, BF16 = jnp.float32, jnp.bfloat16
SDS = jax.ShapeDtypeStruct

D = 2048
DEPTH = 2
LRU_W = 1024
LRU_BLOCKS = 8
LRU_C = 8.0
CONV_W = 4
GLA_H = 4
GLA_DK = 128
GLA_DV = 256
GLA_RANK = 16
GLA_TAU = 16.0
ML_H = 4
ML_W = 1024
ML_DH = 256
ML_BS = 4
CHUNK = 128
ML_CHUNK = 256
PROMPT_ROWS = 256
PROMPT_SEQS = 1
D_FF = 4 * D
EPS = 1e-6
N_MOD = 6
T_STEP = 4
STEP_BB = 4

LANE = 128
SUBLANE = 8
VMEM_LIMIT_CAP = 60 * 1024 * 1024

C_LRU, C_GQ, C_GK, C_GV, C_GGATE, C_MX, C_MO, C_MG = 0, 1024, 1536, 2048, 3072, 4096, 5120, 6144
N_MAIN = 12288
W_SHIFT1 = GLA_RANK
W_SHIFT2 = GLA_RANK + 2 * ML_H
S_GLR = 0
S_IG = 16
S_FG = 20
NEG_BIG = -1e30


def _params(sem, vmem_bytes):
    return pltpu.CompilerParams(dimension_semantics=sem, vmem_limit_bytes=int(min(vmem_bytes, VMEM_LIMIT_CAP)))


def _dot(a, b):
    return jnp.dot(a.astype(BF16), b.astype(BF16), preferred_element_type=F32)


def _dot_nt(a, b):
    return lax.dot_general(a.astype(BF16), b.astype(BF16), (((1,), (1,)), ((), ())), preferred_element_type=F32)


def _split3(x):
    p0 = x.astype(BF16)
    r = x - p0.astype(F32)
    p1 = r.astype(BF16)
    return p0, p1, (r - p1.astype(F32)).astype(BF16)


def _cumsum_rows(tri_b, x):
    return sum(jnp.dot(tri_b, p, preferred_element_type=F32) for p in _split3(x))


def _cumsum_lanes(x, triu_b):
    return sum(jnp.dot(p, triu_b, preferred_element_type=F32) for p in _split3(x))


def _split(x):
    hi = x.astype(BF16)
    return hi, (x - hi.astype(F32)).astype(BF16)


def _dot3(x_hi, x_lo, w_hi, w_lo):
    d = lambda a, b: jnp.dot(a, b, preferred_element_type=F32)
    return d(x_hi, w_hi) + (d(x_lo, w_hi) + d(x_hi, w_lo))


def _log_sigmoid(z):
    return jnp.minimum(z, 0.0) - jnp.log1p(jnp.exp(-jnp.abs(z)))


def _silu(z):
    return z * jax.nn.sigmoid(z)


def _rms(x, g):
    return x * lax.rsqrt(jnp.mean(x * x, axis=-1, keepdims=True) + EPS) * g


def _tri(n):
    r = lax.broadcasted_iota(jnp.int32, (n, n), 0)
    c = lax.broadcasted_iota(jnp.int32, (n, n), 1)
    return r >= c


def _lru_gates(ub, wa, wx, ba, bx, lam):
    r = jax.nn.sigmoid(_dot(ub, wa) + ba)
    i = jax.nn.sigmoid(_dot(ub, wx) + bx)
    log_a = LRU_C * r * _log_sigmoid(lam)
    t = jnp.tanh(log_a)
    return jnp.exp(log_a), jnp.sqrt(-2.0 * t / (1.0 - t)) * (i * ub)


def _blockdiag3(x, wh_ref, wl_ref):
    xh, xl = _split(x)
    outs = []
    for blk in range(ML_W // LANE):
        sl = slice(blk * LANE, (blk + 1) * LANE)
        outs.append(_dot3(xh[:, sl], xl[:, sl], wh_ref[blk], wl_ref[blk]))
    return jnp.concatenate(outs, axis=1)


def _ada_kernel(c_ref, w_ref, b_ref, o_ref):
    o_ref[...] = _dot(_silu(c_ref[...]), w_ref[...]) + b_ref[...]


def _ada(c_all, w_ada, b_ada):
    m = c_all.shape[0]
    tn = 1024
    n_out = N_MOD * D
    return pl.pallas_call(
        _ada_kernel,
        grid=(DEPTH, n_out // tn),
        in_specs=[
            pl.BlockSpec((m, D), lambda l, n: (0, 0)),
            pl.BlockSpec((None, D, tn), lambda l, n: (l, 0, n)),
            pl.BlockSpec((None, 1, tn), lambda l, n: (l, 0, n)),
        ],
        out_specs=pl.BlockSpec((None, m, tn), lambda l, n: (l, 0, n)),
        out_shape=SDS((DEPTH, m, n_out), F32),
        compiler_params=_params(("parallel", "arbitrary"), 2 * (m * D + D * tn + m * tn) * 4 + (12 << 20)),
        name="ada",
    )(c_all, w_ada, b_ada.reshape(DEPTH, 1, n_out))


def _modspec(layer, comp, r, width, tm, rpg, jmap=None):
    nb = D // width
    jm = (lambda j: 0) if jmap is None else jmap
    if r == 1:
        return pl.BlockSpec((None, None, 1, width), lambda i, j: (layer, (i * tm) // rpg, 0, comp * nb + jm(j)))
    mode = pl.Buffered(1) if jmap is None else None
    return pl.BlockSpec((None, None, tm, width), lambda i, j: (layer, 0, i, comp * nb + jm(j)), pipeline_mode=mode)


def _norm_mod_store(xn_s, x_ref, g_ref, sc_ref, sh_ref):
    tm = x_ref.shape[0]
    rc = min(tm, 256)
    for c in range(tm // rc):
        rows = slice(c * rc, (c + 1) * rc)
        mrows = rows if sc_ref.shape[0] == tm else slice(None)
        xn = _rms(x_ref[rows, :], g_ref[...]) * (1.0 + sc_ref[mrows, :]) + sh_ref[mrows, :]
        xn_s[rows, :] = xn.astype(BF16)


def _in_proj_kernel(x_ref, sc_ref, sh_ref, g_ref, wm_ref, ws_ref, om_ref, os_ref, *rest, fill_steps):
    xn_s = rest[-1]

    @pl.when(pl.program_id(1) == 0)
    def _():
        _norm_mod_store(xn_s, x_ref, g_ref, sc_ref, sh_ref)
        os_ref[...] = jnp.dot(xn_s[...], ws_ref[...].astype(BF16), preferred_element_type=F32)

    om_ref[...] = jnp.dot(xn_s[...], wm_ref[...], preferred_element_type=F32)

    if fill_steps:
        @pl.when(pl.program_id(1) < fill_steps)
        def _():
            for z_ref in rest[:-1]:
                z_ref[...] = jnp.zeros_like(z_ref)


def _in_proj(x, mod, layer, g1, w_main, w_small, *, tm, tn, rpg, r, zero_rows=()):
    n = x.shape[0]
    vmem = 2 * (tm * D * 4 + 2 * r * D * 4 + D * tn * 2 + D * LANE * 2 + tm * tn * 4 + tm * LANE * 4) + tm * D * 2
    vmem += 8 << 20
    n_i, n_j = n // tm, N_MAIN // tn
    fill_steps = 0
    zero_specs, zero_shapes = [], []
    if zero_rows:
        fill_steps = max(s for s in range(1, n_j + 1)
                         if all(zr % (n_i * s * SUBLANE) == 0 for zr in zero_rows))
        for zr in zero_rows:
            blk = zr // (n_i * fill_steps)
            zero_specs.append(pl.BlockSpec((blk, GLA_DV),
                                           lambda i, j: (i * fill_steps + jnp.minimum(j, fill_steps - 1), 0)))
            zero_shapes.append(SDS((zr, GLA_DV), F32))
            vmem += 2 * blk * GLA_DV * 4
    outs = pl.pallas_call(
        functools.partial(_in_proj_kernel, fill_steps=fill_steps),
        grid=(n_i, n_j),
        in_specs=[
            pl.BlockSpec((tm, D), lambda i, j: (i, 0)),
            _modspec(layer, 1, r, D, tm, rpg),
            _modspec(layer, 0, r, D, tm, rpg),
            pl.BlockSpec((None, 1, D), lambda i, j: (layer, 0, 0)),
            pl.BlockSpec((D, tn), lambda i, j: (0, j)),
            pl.BlockSpec((None, D, LANE), lambda i, j: (layer, 0, 0)),
        ],
        out_specs=[pl.BlockSpec((tm, tn), lambda i, j: (i, j)), pl.BlockSpec((tm, LANE), lambda i, j: (i, 0)),
                   *zero_specs],
        out_shape=[SDS((n, N_MAIN), F32), SDS((n, LANE), F32), *zero_shapes],
        scratch_shapes=[pltpu.VMEM((tm, D), BF16)],
        compiler_params=_params(("parallel", "arbitrary"), vmem),
        name="in_proj",
    )(x, mod, mod, g1, w_main, w_small)
    return outs


def _in_proj_cast_kernel(x_ref, sc_ref, sh_ref, g_ref, wt_ref, ws_ref, om_ref, os_ref, wc_ref, xn_s):
    @pl.when(pl.program_id(1) == 0)
    def _():
        _norm_mod_store(xn_s, x_ref, g_ref, sc_ref, sh_ref)
        os_ref[...] = jnp.dot(xn_s[...], ws_ref[...].astype(BF16), preferred_element_type=F32)

    wb = wt_ref[...].T.astype(BF16)
    wc_ref[...] = wb
    om_ref[...] = jnp.dot(xn_s[...], wb, preferred_element_type=F32)


def _in_proj_cast(x, mod, layer, g1, w_in_t, w_small, *, tn, rpg):
    n = x.shape[0]
    tm = n
    nb1, nb2 = C_GGATE // tn, C_MG // tn

    def w_rows(i, j):
        shift = jnp.where(j >= nb2, W_SHIFT2 // SUBLANE, jnp.where(j >= nb1, W_SHIFT1 // SUBLANE, 0))
        return (layer, (j * (tn // SUBLANE) + shift) * SUBLANE, 0)

    vmem = tm * D * 4 + 2 * tm * D * 4 + 2 * (tn * D * 4 + D * LANE * 4 + tm * tn * 4 + tm * LANE * 4 + D * tn * 2)
    vmem += tm * D * 2 + 3 * D * tn * 4 + (4 << 20)
    return pl.pallas_call(
        _in_proj_cast_kernel,
        grid=(1, N_MAIN // tn),
        in_specs=[
            pl.BlockSpec((tm, D), lambda i, j: (i, 0), pipeline_mode=pl.Buffered(1)),
            _modspec(layer, 1, tm, D, tm, rpg),
            _modspec(layer, 0, tm, D, tm, rpg),
            pl.BlockSpec((None, 1, D), lambda i, j: (layer, 0, 0)),
            pl.BlockSpec((None, pl.Element(tn), pl.Element(D)), w_rows),
            pl.BlockSpec((None, D, LANE), lambda i, j: (layer, 0, 0)),
        ],
        out_specs=[pl.BlockSpec((tm, tn), lambda i, j: (i, j)), pl.BlockSpec((tm, LANE), lambda i, j: (i, 0)),
                   pl.BlockSpec((D, tn), lambda i, j: (0, j))],
        out_shape=[SDS((n, N_MAIN), F32), SDS((n, LANE), F32), SDS((D, N_MAIN), BF16)],
        scratch_shapes=[pltpu.VMEM((tm, D), BF16)],
        compiler_params=_params(("arbitrary", "arbitrary"), vmem),
        name="in_proj_cast",
    )(x, mod, mod, g1, w_in_t, w_small)


def _wspec(shape, imap, layer, cast):
    if cast:
        return pl.BlockSpec((None,) + shape, lambda i, j: (layer,) + imap(i, j))
    return pl.BlockSpec(shape, imap)


def _merge_kernel(yl_ref, yg_ref, ym_ref, g0_ref, g1_ref, g2_ref, w0_ref, w1_ref, w2_ref, o_ref, *wc_refs):
    ws = [w_ref[...].astype(BF16) for w_ref in (w0_ref, w1_ref, w2_ref)]
    for wc_ref, wb in zip(wc_refs, ws):
        wc_ref[...] = wb
    acc = jax.nn.sigmoid(g0_ref[...]) * jnp.dot(yl_ref[...], ws[0], preferred_element_type=F32)
    acc += jax.nn.sigmoid(g1_ref[...]) * jnp.dot(yg_ref[...], ws[1], preferred_element_type=F32)
    acc += jax.nn.sigmoid(g2_ref[...]) * jnp.dot(ym_ref[...], ws[2], preferred_element_type=F32)
    o_ref[...] = acc.astype(BF16)


def _merge(y_lru, y_gla, y_ml, proj, layer, w_lru, w_gla, w_ml, *, tm, tn, cast):
    n = y_lru.shape[0]
    w = LRU_W
    gb = C_MG // tn
    nb = D // tn
    yspec = pl.BlockSpec((tm, w), lambda i, j: (i, 0))
    wspec = _wspec((w, tn), lambda i, j: (0, j), layer, cast)
    wbytes = 4 if cast else 2
    vmem = 2 * (3 * tm * w * 2 + 3 * tm * tn * 4 + 3 * w * tn * wbytes + tm * tn * 2) + 6 * tm * tn * 4 + (4 << 20)
    out_specs = [pl.BlockSpec((tm, tn), lambda i, j: (i, j))]
    out_shape = [SDS((n, D), BF16)]
    if cast:
        out_specs += [pl.BlockSpec((w, tn), lambda i, j: (0, j))] * 3
        out_shape += [SDS((w, D), BF16)] * 3
        vmem += 2 * 3 * w * tn * 2 + 3 * w * tn * 4
    return pl.pallas_call(
        _merge_kernel,
        grid=(n // tm, nb),
        in_specs=[
            yspec, yspec, yspec,
            pl.BlockSpec((tm, tn), lambda i, j: (i, gb + j)),
            pl.BlockSpec((tm, tn), lambda i, j: (i, gb + nb + j)),
            pl.BlockSpec((tm, tn), lambda i, j: (i, gb + 2 * nb + j)),
            wspec, wspec, wspec,
        ],
        out_specs=out_specs,
        out_shape=out_shape,
        compiler_params=_params(("arbitrary" if cast else "parallel", "arbitrary"), vmem),
        name="merge",
    )(y_lru, y_gla, y_ml, proj, proj, proj, w_lru, w_gla, w_ml)


def _out_proj_kernel(m_ref, w_ref, x_ref, gt_ref, o_ref, *wc_refs):
    wb = w_ref[...].astype(BF16)
    for wc_ref in wc_refs:
        wc_ref[...] = wb
    o_ref[...] = x_ref[...] + gt_ref[...] * jnp.dot(m_ref[...], wb, preferred_element_type=F32)


def _out_proj(merged, x, mod, layer, w_out, *, tm, tn, rpg, r, cast):
    n = x.shape[0]
    wbytes = 4 if cast else 2
    vmem = 2 * (tm * D * 2 + D * tn * wbytes + 2 * tm * tn * 4 + r * tn * 4) + 2 * tm * tn * 4 + (4 << 20)
    out_specs = [pl.BlockSpec((tm, tn), lambda i, j: (i, j))]
    out_shape = [SDS((n, D), F32)]
    if cast:
        out_specs.append(pl.BlockSpec((D, tn), lambda i, j: (0, j)))
        out_shape.append(SDS((D, D), BF16))
        vmem += 2 * D * tn * 2 + D * tn * 4
    return pl.pallas_call(
        _out_proj_kernel,
        grid=(n // tm, D // tn),
        in_specs=[
            pl.BlockSpec((tm, D), lambda i, j: (i, 0)),
            _wspec((D, tn), lambda i, j: (0, j), layer, cast),
            pl.BlockSpec((tm, tn), lambda i, j: (i, j)),
            _modspec(layer, 2, r, tn, tm, rpg, jmap=lambda j: j),
        ],
        out_specs=out_specs,
        out_shape=out_shape,
        compiler_params=_params(("arbitrary" if cast else "parallel", "arbitrary"), vmem),
        name="out_proj",
    )(merged, w_out, x, mod)


def _ffn_kernel(x_ref, sc_ref, sh_ref, gt_ref, g_ref, gf_ref, w1_ref, w2_ref, o_ref, *rest, final_norm):
    xn_s = rest[-1]
    f = pl.program_id(1)

    @pl.when(f == 0)
    def _():
        _norm_mod_store(xn_s, x_ref, g_ref, sc_ref, sh_ref)
        o_ref[...] = jnp.zeros_like(o_ref)

    w1 = w1_ref[...].astype(BF16)
    w2 = w2_ref[...].astype(BF16)
    for wc_ref, wb in zip(rest[:-1], (w1, w2)):
        wc_ref[...] = wb
    h = jnp.square(jnp.maximum(jnp.dot(xn_s[...], w1, preferred_element_type=F32), 0.0))
    o_ref[...] += jnp.dot(h.astype(BF16), w2, preferred_element_type=F32)

    @pl.when(f == pl.num_programs(1) - 1)
    def _():
        y = x_ref[...] + gt_ref[...] * o_ref[...]
        if final_norm:
            y = _rms(y, gf_ref[...])
        o_ref[...] = y


def _ffn(x, mod, layer, g2, g_final, w1, w2, *, tm, tf, rpg, r, final_norm, cast):
    n = x.shape[0]
    wbytes = 4 if cast else 2
    vmem = 2 * (2 * tm * D * 4 + 3 * r * D * 4 + 2 * D * tf * wbytes) + tm * D * 2 + 2 * tm * tf * 4 + 2 * tm * D * 4
    single = pl.Buffered(1) if n == tm else None
    out_specs = [pl.BlockSpec((tm, D), lambda i, f: (i, 0), pipeline_mode=single)]
    out_shape = [SDS((n, D), F32)]
    if cast:
        out_specs += [pl.BlockSpec((D, tf), lambda i, f: (0, f)), pl.BlockSpec((tf, D), lambda i, f: (f, 0))]
        out_shape += [SDS((D, D_FF), BF16), SDS((D_FF, D), BF16)]
        vmem += 2 * 2 * D * tf * 2
    return pl.pallas_call(
        functools.partial(_ffn_kernel, final_norm=final_norm),
        grid=(n // tm, D_FF // tf),
        in_specs=[
            pl.BlockSpec((tm, D), lambda i, f: (i, 0), pipeline_mode=single),
            _modspec(layer, 4, r, D, tm, rpg),
            _modspec(layer, 3, r, D, tm, rpg),
            _modspec(layer, 5, r, D, tm, rpg),
            pl.BlockSpec((None, 1, D), lambda i, f: (layer, 0, 0)),
            pl.BlockSpec((1, D), lambda i, f: (0, 0)),
            _wspec((D, tf), lambda i, f: (0, f), layer, cast),
            _wspec((tf, D), lambda i, f: (f, 0), layer, cast),
        ],
        out_specs=out_specs,
        out_shape=out_shape,
        scratch_shapes=[pltpu.VMEM((tm, D), BF16)],
        compiler_params=_params(("arbitrary" if cast else "parallel", "arbitrary"), vmem),
        name="ffn",
    )(x, mod, mod, mod, g2, g_final, w1, w2)


def _conv_chunk(carry_s, x, w, b, L):
    t8 = lax.broadcasted_iota(jnp.int32, (SUBLANE, x.shape[1]), 0)
    carry = carry_s[...]
    acc = b + x * w[CONV_W - 1:CONV_W]
    for k in range(1, CONV_W):
        down = pltpu.roll(x, k, 0)
        head = jnp.where(t8 < k, pltpu.roll(carry, k, 0), down[0:SUBLANE, :])
        acc = acc + jnp.concatenate([head, down[SUBLANE:, :]], axis=0) * w[CONV_W - 1 - k:CONV_W - k]
    carry_s[...] = x[L - SUBLANE:L, :]
    return acc, x[L - (CONV_W - 1):L, :]


N_TOK_REFS, N_W_REFS, N_Y_REFS, N_STATE_REFS, N_SCRATCH = 8, 20, 3, 7, 9


def _prompt_mixer_kernel(*refs, L, layer, creates):
    tok = refs[:N_TOK_REFS]
    wts = refs[N_TOK_REFS:N_TOK_REFS + N_W_REFS]
    scr = refs[-N_SCRATCH:]
    outs = refs[-(N_SCRATCH + N_Y_REFS + N_STATE_REFS):-N_SCRATCH]
    ys, state_outs = outs[:N_Y_REFS], outs[N_Y_REFS:]
    nseq = tok[0].shape[0]

    @pl.when(pl.program_id(1) == 0)
    def _():
        xpl_s, xpm_s = scr[0], scr[1]
        xpl_s[:, 0:SUBLANE, :] = jnp.zeros((nseq, SUBLANE, LRU_W), F32)
        xpm_s[:, 0:SUBLANE, :] = jnp.zeros((nseq, SUBLANE, ML_W), F32)
        for s_ref in scr[4:]:
            s_ref[...] = jnp.zeros_like(s_ref)

    finals = [_prompt_mixer_rows([r.at[bi] for r in tok], wts, [y.at[bi] for y in ys], [s.at[bi] for s in scr], L)
              for bi in range(nseq)]

    @pl.when(pl.program_id(1) == pl.num_programs(1) - 1)
    def _():
        souts = state_outs
        if creates:
            for ref in souts:
                for other in range(DEPTH):
                    if other != layer:
                        ref[other] = jnp.zeros(ref.shape[1:], F32)
            souts = [ref.at[layer] for ref in souts]
        ht_ref, lct_ref, st_ref, ct_ref, nt_ref, mt_ref, mct_ref = souts
        s_s, c_s, n_s, m_s = scr[5:]
        for bi, (h, l_tail, m_tail) in enumerate(finals):
            ht_ref[bi] = h
            lct_ref[bi] = l_tail
            mct_ref[bi] = m_tail
        st_ref[...] = s_s[...]
        ct_ref[...] = c_s[...]
        nt_ref[...] = n_s[...]
        mt_ref[...] = m_s[...]


def _prompt_mixer_rows(tok, wts, ys, scr, L):
    lx_ref, q_ref, k_ref, v_ref, gate_ref, mx_ref, mo_ref, small_ref = tok
    (lcw_ref, lcb_ref, wa_ref, ba_ref, wx_ref, bx_ref, lam_ref, wg2_ref, bg_ref, ggn_ref,
     mcw_ref, mcb_ref, wqh_ref, wql_ref, wkh_ref, wkl_ref, wvh_ref, wvl_ref, bif_ref, mgn_ref) = wts
    yl_ref, yg_ref, ym_ref = ys
    xpl_s, xpm_s, a_s, u_s, h_s, s_s, c_s, n_s, m_s = scr
    rows = lx_ref.shape[0]

    q, k, v, gate = q_ref[...], k_ref[...], v_ref[...], gate_ref[...]
    small = small_ref[...]
    logdec = _log_sigmoid(_dot(small, wg2_ref[...]) + bg_ref[...]) * (1.0 / GLA_TAU)
    mx = mx_ref[...]
    conv, m_tail = _conv_chunk(xpm_s, mx, mcw_ref[...], mcb_ref[...], rows)
    mc = _silu(conv)
    mq = _blockdiag3(mc, wqh_ref, wql_ref)
    mk = _blockdiag3(mc, wkh_ref, wkl_ref) * (ML_DH ** -0.5)
    mv = _blockdiag3(mx, wvh_ref, wvl_ref)

    after_proj = jnp.where(mq[0:1, :] != mq[0:1, :], 1.0, 0.0)
    u, l_tail = _conv_chunk(xpl_s, lx_ref[...], lcw_ref[...], lcb_ref[...] + after_proj, rows)
    for blk in range(LRU_BLOCKS):
        sl = slice(blk * LANE, (blk + 1) * LANE)
        a_s[:, sl], u_s[:, sl] = _lru_gates(u[:, sl], wa_ref[blk], wx_ref[blk], ba_ref[:, sl], bx_ref[:, sl],
                                            lam_ref[:, sl])

    gates = small + bif_ref[...]
    gates_t = gates.T
    lf = _log_sigmoid(gates)
    lf_t = _log_sigmoid(gates_t)
    mo = mo_ref[...]
    tri = _tri(L)
    tri_b = tri.astype(BF16)

    for cc in range(rows // L):
        rs = slice(cc * L, (cc + 1) * L)
        bcum = _cumsum_rows(tri_b, logdec[rs, :])
        for hh in range(GLA_H):
            ks = slice(hh * GLA_DK, (hh + 1) * GLA_DK)
            vs = slice(hh * GLA_DV, (hh + 1) * GLA_DV)
            bh = bcum[:, ks]
            k_h, v_h = k[rs, ks], v[rs, vs]
            b_mid = bh[L // 2 - 1:L // 2, :]
            qe = (q[rs, ks] * (GLA_DK ** -0.5)) * jnp.exp(bh - b_mid)
            ke = k_h * jnp.exp(b_mid - bh)
            a = jnp.where(tri, _dot_nt(qe, ke), 0.0)
            s_h = s_s[hh]
            bl = bh[L - 1:L, :]
            cols = jnp.exp(jnp.concatenate([jnp.broadcast_to(b_mid, (SUBLANE // 2, GLA_DK)),
                                            jnp.broadcast_to(bl, (SUBLANE // 2, GLA_DK))], axis=0)).T
            mid_col, dec_col = cols[:, 0:1], cols[:, SUBLANE // 2:SUBLANE // 2 + 1]
            o = _dot(qe, mid_col * s_h) + _dot(a, v_h)
            kd = k_h * jnp.exp(bl - bh)
            s_s[hh] = dec_col * s_h + _dot(kd.T, v_h)
            yg_ref[rs, vs] = (_rms(o, ggn_ref[...]) * _silu(gate[rs, vs])).astype(BF16)

    L = min(rows, ML_CHUNK)
    tri = _tri(L)
    tri_b = tri.astype(BF16)
    r_i = lax.broadcasted_iota(jnp.int32, (L, L), 0)
    c_i = lax.broadcasted_iota(jnp.int32, (L, L), 1)
    triu_b = (r_i <= c_i).astype(BF16)
    for cc in range(rows // L):
        rs = slice(cc * L, (cc + 1) * L)
        f_col = _cumsum_rows(tri_b, lf[rs, :])
        f_row = _cumsum_lanes(lf_t[:, rs], triu_b)
        for hh in range(ML_H):
            sl = slice(hh * ML_DH, (hh + 1) * ML_DH)
            qh, kh, vh = mq[rs, sl], mk[rs, sl], mv[rs, sl]
            fc = f_col[:, S_FG + hh:S_FG + hh + 1]
            fr = f_row[S_FG + hh:S_FG + hh + 1, :]
            igr = gates_t[S_IG + hh:S_IG + hh + 1, rs]
            m_h = m_s[hh:hh + 1, 0:1]
            dm = jnp.where(tri, fc - fr + igr, NEG_BIG)
            inter = fc + m_h
            mt = jnp.maximum(inter, jnp.max(dm, axis=-1, keepdims=True))
            ci = jnp.exp(inter - mt)
            s = _dot_nt(qh, kh) * jnp.exp(dm - mt)
            c_h = c_s[hh]
            n_h = n_s[hh:hh + 1, :]
            num = ci * _dot(qh, c_h) + _dot(s, vh)
            den = ci * jnp.sum(qh * n_h, axis=-1, keepdims=True) + jnp.sum(s, axis=-1, keepdims=True)
            hcell = num / jnp.maximum(jnp.abs(den), jnp.exp(-mt))
            fl = fr[:, L - 1:L]
            dj = fl - fr + igr
            m_new = jnp.maximum(fl + m_h, jnp.max(dj, axis=-1, keepdims=True))
            cs = jnp.exp(fl + m_h - m_new)
            wj = jnp.exp(dj - m_new)
            c_s[hh] = cs * c_h + _dot(kh.T * wj, vh)
            n_s[hh:hh + 1, :] = cs * n_h + _dot(jnp.broadcast_to(wj, (SUBLANE, L)), kh)[0:1, :]
            m_s[hh:hh + 1, :] = jnp.broadcast_to(m_new, (1, LANE))
            ym_ref[rs, sl] = (jax.nn.sigmoid(mo[rs, sl]) * _rms(hcell, mgn_ref[...])).astype(BF16)

    t8 = lax.broadcasted_iota(jnp.int32, (SUBLANE, LRU_W), 0)
    h = h_s[...]
    for g in range(rows // SUBLANE):
        r8 = slice(g * SUBLANE, (g + 1) * SUBLANE)
        a8, u8 = a_s[r8, :], u_s[r8, :]
        for s in (1, 2, 4):
            u8 = jnp.where(t8 >= s, u8 + a8 * pltpu.roll(u8, s, 0), u8)
            a8 = jnp.where(t8 >= s, a8 * pltpu.roll(a8, s, 0), a8)
        h8 = u8 + a8 * h
        a_s[r8, :] = h8
        h = h8[SUBLANE - 1:SUBLANE, :]
    h_s[...] = h
    yl_ref[...] = a_s[...].astype(BF16)
    return h, l_tail, m_tail


def _prompt_mixer(proj3, small3, layer, w, prev, *, L, rows, nseq):
    b, t, _ = proj3.shape
    kw, vw = GLA_H * GLA_DK, GLA_H * GLA_DV
    nblk = ML_W // LANE
    blk3 = lambda width, cb: pl.BlockSpec((nseq, rows, width), lambda bi, c: (bi, c, cb))
    lw3 = lambda s: pl.BlockSpec((None,) + s, lambda bi, c: (layer, 0, 0))
    lw4 = lambda s: pl.BlockSpec((None,) + s, lambda bi, c: (layer, 0, 0, 0))
    creates = prev is None
    lead, at = ((DEPTH,), 0) if creates else ((None,), layer)
    st3 = lambda s: pl.BlockSpec(lead + (nseq,) + s, lambda bi, c: (at, bi, 0, 0))
    st4 = lambda s: pl.BlockSpec(lead + (nseq,) + s, lambda bi, c: (at, bi, 0, 0, 0))
    in_specs = [
        blk3(LRU_W, C_LRU // LRU_W), blk3(kw, C_GQ // kw), blk3(kw, C_GK // kw), blk3(vw, C_GV // vw),
        blk3(vw, C_GGATE // vw), blk3(ML_W, C_MX // ML_W), blk3(ML_W, C_MO // ML_W), blk3(LANE, 0),
        lw3((CONV_W, LRU_W)), lw3((1, LRU_W)), lw4((LRU_BLOCKS, LANE, LANE)), lw3((1, LRU_W)),
        lw4((LRU_BLOCKS, LANE, LANE)), lw3((1, LRU_W)), lw3((1, LRU_W)),
        lw3((LANE, kw)), lw3((1, kw)), lw3((1, GLA_DV)),
        lw3((CONV_W, ML_W)), lw3((1, ML_W)), *([lw4((nblk, LANE, LANE))] * 6), lw3((1, LANE)), lw3((1, ML_DH)),
    ]
    args = [proj3] * 7 + [small3,
                          w["lru_cw"], w["lru_cb"], w["lru_wa"], w["lru_ba"], w["lru_wx"], w["lru_bx"], w["lru_lam"],
                          w["wg2"], w["bg"], w["gla_gn"],
                          w["ml_cw"], w["ml_cb"], *w["ml_wq"], *w["ml_wk"], *w["ml_wv"], w["ml_bif"], w["ml_gn"]]
    aliases = {}
    if prev is not None:
        aliases = {len(in_specs) + j: 3 + j for j in range(len(prev))}
        in_specs += [pl.BlockSpec(memory_space=pl.ANY)] * len(prev)
        args += list(prev)
    sbytes = GLA_H * GLA_DK * GLA_DV * 4
    cbytes = ML_H * ML_DH * ML_DH * 4
    vmem = nseq * (2 * rows * (7 * 1024 * 4 + 3 * 1024 * 2) + 3 * (sbytes + cbytes) + 6 * rows * 1024 * 4)
    vmem += 30 * rows * 1024 * 4 + (12 << 20)
    sc = lambda *s: pltpu.VMEM((nseq,) + s, F32)
    return pl.pallas_call(
        functools.partial(_prompt_mixer_kernel, L=L, layer=layer, creates=creates),
        grid=(b // nseq, t // rows),
        in_specs=in_specs,
        out_specs=[blk3(LRU_W, 0), blk3(vw, 0), blk3(ML_W, 0),
                   st3((1, LRU_W)), st3((CONV_W - 1, LRU_W)), st4((GLA_H, GLA_DK, GLA_DV)),
                   st4((ML_H, ML_DH, ML_DH)), st3((ML_H, ML_DH)), st3((ML_H, LANE)), st3((CONV_W - 1, ML_W))],
        out_shape=[SDS((b, t, LRU_W), BF16), SDS((b, t, vw), BF16), SDS((b, t, ML_W), BF16),
                   SDS((DEPTH, b, 1, LRU_W), F32), SDS((DEPTH, b, CONV_W - 1, LRU_W), F32),
                   SDS((DEPTH, b, GLA_H, GLA_DK, GLA_DV), F32), SDS((DEPTH, b, ML_H, ML_DH, ML_DH), F32),
                   SDS((DEPTH, b, ML_H, ML_DH), F32), SDS((DEPTH, b, ML_H, LANE), F32),
                   SDS((DEPTH, b, CONV_W - 1, ML_W), F32)],
        scratch_shapes=[sc(SUBLANE, LRU_W), sc(SUBLANE, ML_W), sc(rows, LRU_W), sc(rows, LRU_W),
                        sc(1, LRU_W), sc(GLA_H, GLA_DK, GLA_DV), sc(ML_H, ML_DH, ML_DH), sc(ML_H, ML_DH),
                        sc(ML_H, LANE)],
        input_output_aliases=aliases,
        compiler_params=_params(("parallel", "arbitrary"), vmem),
        name="prompt_mixer",
    )(*args)


def _tmask(shape):
    return lax.broadcasted_iota(jnp.int32, shape, 0) % T_STEP


def _down(x, s):
    return x if s == 0 else pltpu.roll(x, s, 0)


def _up(x, s):
    return x if s == 0 else pltpu.roll(x, x.shape[0] - s, 0)


def _seg_cumsum(x):
    t = _tmask(x.shape)
    out = x
    for s in range(1, T_STEP):
        out = out + jnp.where(t >= s, _down(x, s), 0.0)
    return out


def _seg_last(x):
    t = _tmask(x.shape)
    out = x
    for s in range(1, T_STEP):
        out = jnp.where(t == T_STEP - 1 - s, _up(x, s), out)
    return out


def _seg_allreduce(x, op):
    t = _tmask(x.shape)
    y = op(x, jnp.where(t % 2 == 1, _down(x, 1), _up(x, 1)))
    return op(y, jnp.where(t >= 2, _down(y, 2), _up(y, 2)))


def _conv_rows(x, e, w, b):
    t = _tmask(x.shape)
    acc = b + x * w[CONV_W - 1:CONV_W]
    for k in range(1, CONV_W):
        hist = jnp.where(t >= k, _down(x, k), _up(e, CONV_W - 1 - k))
        acc = acc + hist * w[CONV_W - 1 - k:CONV_W - k]
    return acc


def _col(x, lane):
    li = lax.broadcasted_iota(jnp.int32, x.shape, 1)
    return jnp.sum(jnp.where(li == lane, x, 0.0), axis=-1, keepdims=True)


def _sample_mixer_kernel(lx_ref, q_ref, k_ref, v_ref, gate_ref, mx_ref, mo_ref, small_ref,
                         el_ref, em_ref, h0_ref, n0_ref, m0_ref, s0_ref, c0_ref,
                         lcw_ref, lcb_ref, wa_ref, ba_ref, wx_ref, bx_ref, lam_ref,
                         wg2_ref, bg_ref, ggn_ref,
                         mcw_ref, mcb_ref, wqh_ref, wql_ref, wkh_ref, wkl_ref, wvh_ref, wvl_ref, bif_ref, mgn_ref,
                         *rest, layer, creates):
    yl_ref, yg_ref, ym_ref, *state_refs = rest[-8:]
    row_refs, (st_ref, ct_ref) = state_refs[:3], state_refs[3:]
    if creates:
        for ref in row_refs:
            for other in range(DEPTH):
                if other != layer:
                    ref[other] = jnp.zeros(ref.shape[1:], F32)
        row_refs = [ref.at[layer] for ref in row_refs]
    hrow_ref, nrow_ref, mrow_ref = row_refs
    rows = lx_ref.shape[0]
    n_pairs = rows // SUBLANE
    seq_per_tile = SUBLANE // T_STEP
    t_col = _tmask((rows, 1))
    row8 = lax.broadcasted_iota(jnp.int32, (SUBLANE, 1), 0)

    u = _conv_rows(lx_ref[...], el_ref[...], lcw_ref[...], lcb_ref[...])
    a_blocks, g_blocks = [], []
    for blk in range(LRU_BLOCKS):
        sl = slice(blk * LANE, (blk + 1) * LANE)
        a_b, g_b = _lru_gates(u[:, sl], wa_ref[blk], wx_ref[blk], ba_ref[:, sl], bx_ref[:, sl], lam_ref[:, sl])
        a_blocks.append(a_b)
        g_blocks.append(g_b)
    a = jnp.concatenate(a_blocks, axis=1)
    t_w = _tmask(a.shape)
    g = jnp.concatenate(g_blocks, axis=1) + jnp.where(t_w == 0, a * h0_ref[...], 0.0)
    a1 = jnp.where(t_w >= 1, a * _down(a, 1), a)
    g1 = jnp.where(t_w >= 1, g + a * _down(g, 1), g)
    h = jnp.where(t_w >= 2, g1 + a1 * _down(g1, 2), g1)
    yl_ref[...] = h.astype(BF16)
    hrow_ref[...] = h

    q, k, v = q_ref[...], k_ref[...], v_ref[...]
    logdec = _log_sigmoid(_dot(small_ref[...], wg2_ref[...]) + bg_ref[...]) * (1.0 / GLA_TAU)
    bc = _seg_cumsum(logdec)
    bl = _seg_last(bc)
    qs = q * (GLA_DK ** -0.5)
    o_heads = [jnp.zeros((rows, GLA_DV), F32) for _ in range(GLA_H)]
    for s in range(T_STEP):
        prod = qs * _down(k, s) * jnp.exp(bc - _down(bc, s))
        v_s = _down(v, s)
        for hh in range(GLA_H):
            a_sh = jnp.sum(prod[:, hh * GLA_DK:(hh + 1) * GLA_DK], axis=-1, keepdims=True)
            a_sh = jnp.where(t_col >= s, a_sh, 0.0)
            o_heads[hh] = o_heads[hh] + a_sh * v_s[:, hh * GLA_DV:(hh + 1) * GLA_DV]
    qe = qs * jnp.exp(bc)
    kd = k * jnp.exp(bl - bc)
    dec = jnp.exp(bl)
    gate = gate_ref[...]
    for hh in range(GLA_H):
        ks = slice(hh * GLA_DK, (hh + 1) * GLA_DK)
        vs = slice(hh * GLA_DV, (hh + 1) * GLA_DV)
        o_state = []
        for p in range(n_pairs):
            r8 = slice(p * SUBLANE, (p + 1) * SUBLANE)
            kd_t = kd[r8, ks].T
            dec_t = dec[r8, ks].T
            res = None
            for j in range(seq_per_tile):
                b = p * seq_per_tile + j
                s_b = s0_ref[b, hh]
                r_j = _dot(qe[r8, ks], s_b)
                res = r_j if res is None else jnp.where(row8 // T_STEP == j, r_j, res)
                v_j = jnp.where(row8 // T_STEP == j, v[r8, vs], 0.0)
                last = j * T_STEP + T_STEP - 1
                st_ref[b, hh] = dec_t[:, last:last + 1] * s_b + _dot(kd_t, v_j)
            o_state.append(res)
        o = o_heads[hh] + jnp.concatenate(o_state, axis=0)
        yg_ref[:, vs] = (_rms(o, ggn_ref[...]) * _silu(gate[:, vs])).astype(BF16)

    mx = mx_ref[...]
    mc = _silu(_conv_rows(mx, em_ref[...], mcw_ref[...], mcb_ref[...]))
    mq = _blockdiag3(mc, wqh_ref, wql_ref)
    mk = _blockdiag3(mc, wkh_ref, wkl_ref) * (ML_DH ** -0.5)
    mv = _blockdiag3(mx, wvh_ref, wvl_ref)
    gts = small_ref[...] + bif_ref[...]
    ig = pltpu.roll(gts, S_FG - S_IG, 1)
    fcum = _seg_cumsum(_log_sigmoid(gts))
    flast = _seg_last(fcum)
    m0 = m0_ref[...]
    t_g = _tmask(gts.shape)
    inter = fcum + m0
    dms = [jnp.where(t_g >= s, fcum - _down(fcum, s) + _down(ig, s), NEG_BIG) for s in range(T_STEP)]
    mt = inter
    for dm in dms:
        mt = jnp.maximum(mt, dm)
    ci_t = jnp.exp(inter - mt)
    emt_t = jnp.exp(-mt)
    w_t = [jnp.exp(dm - mt) for dm in dms]
    dj = flast - fcum + ig
    m_new = jnp.maximum(flast + m0, _seg_allreduce(dj, jnp.maximum))
    cs_t = jnp.exp(flast + m0 - m_new)
    wj_t = jnp.exp(dj - m_new)
    mrow_ref[...] = m_new
    mo = mo_ref[...]
    n0 = n0_ref[...]
    for hh in range(ML_H):
        sl = slice(hh * ML_DH, (hh + 1) * ML_DH)
        lane = S_FG + hh
        qh, kh, vh = mq[:, sl], mk[:, sl], mv[:, sl]
        ci, emt, cs, wj = _col(ci_t, lane), _col(emt_t, lane), _col(cs_t, lane), _col(wj_t, lane)
        num = jnp.zeros((rows, ML_DH), F32)
        den = ci * jnp.sum(qh * n0[:, sl], axis=-1, keepdims=True)
        for s in range(T_STEP):
            sc = jnp.sum(qh * _down(kh, s), axis=-1, keepdims=True) * _col(w_t[s], lane)
            num = num + sc * _down(vh, s)
            den = den + sc
        kw = kh * wj
        nrow_ref[:, sl] = cs * n0[:, sl] + _seg_allreduce(kw, jnp.add)
        qc = []
        for p in range(n_pairs):
            r8 = slice(p * SUBLANE, (p + 1) * SUBLANE)
            kw_t = kw[r8, :].T
            res = None
            for j in range(seq_per_tile):
                b = p * seq_per_tile + j
                c_b = c0_ref[b, hh]
                r_j = _dot(qh[r8, :], c_b)
                res = r_j if res is None else jnp.where(row8 // T_STEP == j, r_j, res)
                v_j = jnp.where(row8 // T_STEP == j, vh[r8, :], 0.0)
                last = p * SUBLANE + j * T_STEP + T_STEP - 1
                ct_ref[b, hh] = cs[last:last + 1, :] * c_b + _dot(kw_t, v_j)
            qc.append(res)
        num = num + ci * jnp.concatenate(qc, axis=0)
        hcell = num / jnp.maximum(jnp.abs(den), emt)
        ym_ref[:, sl] = (jax.nn.sigmoid(mo[:, sl]) * _rms(hcell, mgn_ref[...])).astype(BF16)


def _sample_mixer(proj, small, el, em, h0e, n0e, m0e, s_state, c_state, layer, w, row_prev, big_prev):
    n = proj.shape[0]
    nseq = n // T_STEP
    bb = STEP_BB
    rows = bb * T_STEP
    row = lambda width, blk: pl.BlockSpec((rows, width), lambda i: (i, blk))
    lrow = lambda width: pl.BlockSpec((None, rows, width), lambda i: (layer, i, 0))
    lw3 = lambda s: pl.BlockSpec((None,) + s, lambda i: (layer, 0, 0))
    lw4 = lambda s: pl.BlockSpec((None,) + s, lambda i: (layer, 0, 0, 0))
    sspec = pl.BlockSpec((None, bb, GLA_H, GLA_DK, GLA_DV), lambda i: (layer, i, 0, 0, 0))
    cspec = pl.BlockSpec((None, bb, ML_H, ML_DH, ML_DH), lambda i: (layer, i, 0, 0, 0))
    nblk = ML_W // LANE
    kw, vw = GLA_H * GLA_DK, GLA_H * GLA_DV
    in_specs = [
        row(LRU_W, C_LRU // LRU_W), row(kw, C_GQ // kw), row(kw, C_GK // kw), row(vw, C_GV // vw),
        row(vw, C_GGATE // vw), row(ML_W, C_MX // ML_W), row(ML_W, C_MO // ML_W), row(LANE, 0),
        lrow(LRU_W), lrow(ML_W), lrow(LRU_W), lrow(ML_W), lrow(LANE), sspec, cspec,
        lw3((CONV_W, LRU_W)), lw3((1, LRU_W)), lw4((LRU_BLOCKS, LANE, LANE)), lw3((1, LRU_W)),
        lw4((LRU_BLOCKS, LANE, LANE)), lw3((1, LRU_W)), lw3((1, LRU_W)),
        lw3((LANE, kw)), lw3((1, kw)), lw3((1, GLA_DV)),
        lw3((CONV_W, ML_W)), lw3((1, ML_W)), *([lw4((nblk, LANE, LANE))] * 6), lw3((1, LANE)), lw3((1, ML_DH)),
    ]
    args = [proj] * 7 + [small, el, em, h0e, n0e, m0e, s_state, c_state,
                         w["lru_cw"], w["lru_cb"], w["lru_wa"], w["lru_ba"], w["lru_wx"], w["lru_bx"], w["lru_lam"],
                         w["wg2"], w["bg"], w["gla_gn"],
                         w["ml_cw"], w["ml_cb"], *w["ml_wq"], *w["ml_wk"], *w["ml_wv"], w["ml_bif"], w["ml_gn"]]
    creates = row_prev is None
    prev = list(big_prev) if creates else list(row_prev) + list(big_prev)
    first_out = 6 if creates else 3
    aliases = {len(in_specs) + j: first_out + j for j in range(len(prev))}
    in_specs += [pl.BlockSpec(memory_space=pl.ANY)] * len(prev)
    args += prev
    sbytes = bb * GLA_H * GLA_DK * GLA_DV * 4
    cbytes = bb * ML_H * ML_DH * ML_DH * 4
    vmem = 4 * (sbytes + cbytes) + 2 * rows * (7 * 1024 + 8 * 1024) * 4 + 80 * rows * 1024 * 4 + (12 << 20)
    row_specs = [lrow(LRU_W), lrow(ML_W), lrow(LANE)]
    if creates:
        all_rows = lambda width: pl.BlockSpec((DEPTH, rows, width), lambda i: (0, i, 0))
        row_specs = [all_rows(LRU_W), all_rows(ML_W), all_rows(LANE)]
    state_specs = row_specs + [sspec, cspec]
    return pl.pallas_call(
        functools.partial(_sample_mixer_kernel, layer=layer, creates=creates),
        grid=(nseq // bb,),
        in_specs=in_specs,
        out_specs=[row(LRU_W, 0), row(vw, 0), row(ML_W, 0), *state_specs],
        out_shape=[SDS((n, LRU_W), BF16), SDS((n, vw), BF16), SDS((n, ML_W), BF16),
                   SDS((DEPTH, n, LRU_W), F32), SDS((DEPTH, n, ML_W), F32), SDS((DEPTH, n, LANE), F32),
                   SDS((DEPTH, nseq, GLA_H, GLA_DK, GLA_DV), F32), SDS((DEPTH, nseq, ML_H, ML_DH, ML_DH), F32)],
        input_output_aliases=aliases,
        compiler_params=_params(("arbitrary",), vmem),
        name="sample_mixer",
    )(*args)


def _prep_weights(p):
    w_in = p["w_in"]
    c1 = C_GGATE
    c2 = C_MG + W_SHIFT1
    w_small = jnp.concatenate(
        [w_in[:, :, c1:c1 + GLA_RANK], w_in[:, :, c2:c2 + 2 * ML_H],
         jnp.zeros((DEPTH, D, LANE - GLA_RANK - 2 * ML_H), F32)], axis=-1)
    wg2 = jnp.concatenate(
        [p["gla_w_g2"], jnp.zeros((DEPTH, LANE - GLA_RANK, GLA_H * GLA_DK), F32)], axis=1)
    blk_of = jnp.arange(LANE) // ML_BS
    on_diag = blk_of[:, None] == blk_of[None, :]

    def dense_bd(wb):
        rows = wb.reshape(DEPTH, ML_W // LANE, LANE, ML_BS)
        dense = jnp.where(on_diag, jnp.tile(rows, (1, 1, 1, LANE // ML_BS)), 0.0)
        hi = dense.astype(BF16)
        return hi, (dense - hi.astype(F32)).astype(BF16)

    bif = jnp.zeros((DEPTH, 1, LANE), F32).at[:, 0, S_IG:S_IG + 2 * ML_H].set(p["ml_b_if"])
    r3 = lambda a: a.reshape(DEPTH, 1, a.shape[-1])
    return dict(
        w_small=w_small, wg2=wg2, bg=r3(p["gla_b_g"]), gla_gn=r3(p["gla_g_norm"]),
        g1=r3(p["g_norm1"]), g2=r3(p["g_norm2"]), gf=p["g_final"].reshape(1, D),
        lru_cw=p["lru_conv_w"], lru_cb=r3(p["lru_conv_b"]), lru_wa=p["lru_w_a"], lru_ba=r3(p["lru_b_a"]),
        lru_wx=p["lru_w_x"], lru_bx=r3(p["lru_b_x"]), lru_lam=r3(p["lru_lam"]),
        ml_cw=p["ml_conv_w"], ml_cb=r3(p["ml_conv_b"]), ml_wq=dense_bd(p["ml_w_q"]), ml_wk=dense_bd(p["ml_w_k"]),
        ml_wv=dense_bd(p["ml_w_v"]), ml_bif=bif, ml_gn=r3(p["ml_g_norm"]),
        w_in_t=jnp.swapaxes(w_in, 1, 2), w_br=(p["w_br_lru"], p["w_br_gla"], p["w_br_ml"]), w_out=p["w_out"], w_ff1=p["w_ff1"],
        w_ff2=p["w_ff2"],
    )


def _trunk(xp3, xs3, mod_p, mod_s, states, w):
    bp, tp, _ = xp3.shape
    bs, ts, _ = xs3.shape
    n_p, n_s = bp * tp, bs * ts
    tm_p = min(tp, 1024)
    tm_f = min(tp, 512)
    s_h, s_cv, s_gla, s_c, s_n, s_m, s_mcv = states
    pad_t = lambda a: jnp.pad(a, ((0, 0), (0, 0), (0, ts - a.shape[2]), (0, 0))).reshape(DEPTH, n_s, a.shape[-1])
    el = pad_t(s_cv)
    em = pad_t(s_mcv)
    h0e = pad_t(s_h[:, :, None, :])
    n0e = jnp.broadcast_to(s_n.reshape(DEPTH, bs, 1, ML_W), (DEPTH, bs, ts, ML_W)).reshape(DEPTH, n_s, ML_W)
    m_l = jnp.pad(s_m, ((0, 0), (0, 0), (S_FG, LANE - S_FG - ML_H)))
    m0e = jnp.broadcast_to(m_l[:, :, None, :], (DEPTH, bs, ts, LANE)).reshape(DEPTH, n_s, LANE)
    x_p = xp3.reshape(n_p, D)
    x_s = xs3.reshape(n_s, D)
    prev_p = None
    conv_l, conv_m = [], []
    keep = ts - (CONV_W - 1)
    for l in range(DEPTH):
        last = l == DEPTH - 1
        proj_s, small_s, w_main = _in_proj_cast(x_s, mod_s, l, w["g1"], w["w_in_t"], w["w_small"], tn=512, rpg=n_s)
        if l == 0:
            s_shape = (DEPTH, bs, GLA_H, GLA_DK, GLA_DV)
            c_shape = (DEPTH, bs, ML_H, ML_DH, ML_DH)
            flat_rows = lambda shape: shape[0] * shape[1] * shape[2] * shape[3] * shape[4] // GLA_DV
            proj_p, small_p, s_buf, c_buf = _in_proj(x_p, mod_p, l, w["g1"], w_main, w["w_small"], tm=tm_p, tn=1024,
                                                     rpg=tp, r=1, zero_rows=(flat_rows(s_shape), flat_rows(c_shape)))
            big_s = (s_buf.reshape(s_shape), c_buf.reshape(c_shape))
            rows_s = None
        else:
            proj_p, small_p = _in_proj(x_p, mod_p, l, w["g1"], w_main, w["w_small"], tm=tm_p, tn=1024, rpg=tp, r=1)
        outs_s = _sample_mixer(proj_s, small_s, el, em, h0e, n0e, m0e, s_gla, s_c, l, w, rows_s, big_s)
        rows_s, big_s = outs_s[3:6], outs_s[6:]
        outs_p = _prompt_mixer(proj_p.reshape(bp, tp, N_MAIN), small_p.reshape(bp, tp, LANE), l, w, prev_p,
                               L=min(tp, CHUNK), rows=min(tp, PROMPT_ROWS), nseq=PROMPT_SEQS)
        prev_p = outs_p[3:]
        conv_l.append(proj_s[:, C_LRU:C_LRU + LRU_W].reshape(bs, ts, LRU_W)[:, keep:])
        conv_m.append(proj_s[:, C_MX:C_MX + ML_W].reshape(bs, ts, ML_W)[:, keep:])
        merged_s, *w_br = _merge(*outs_s[:3], proj_s, l, *w["w_br"], tm=n_s, tn=512, cast=True)
        (merged_p,) = _merge(*(y.reshape(n_p, y.shape[-1]) for y in outs_p[:3]), proj_p, l, *w_br, tm=tm_p, tn=512,
                             cast=False)
        x_s, w_out = _out_proj(merged_s, x_s, mod_s, l, w["w_out"], tm=n_s, tn=512, rpg=n_s, r=n_s, cast=True)
        (x_p,) = _out_proj(merged_p, x_p, mod_p, l, w_out, tm=tm_p, tn=1024, rpg=tp, r=1, cast=False)
        x_s, w_ff1, w_ff2 = _ffn(x_s, mod_s, l, w["g2"], w["gf"], w["w_ff1"], w["w_ff2"], tm=n_s, tf=512, rpg=n_s,
                                 r=n_s, final_norm=last, cast=True)
        (x_p,) = _ffn(x_p, mod_p, l, w["g2"], w["gf"], w_ff1, w_ff2, tm=tm_f, tf=1024, rpg=tp, r=1, final_norm=last,
                      cast=False)
    h_t, lcv_t, s_t, c_t, n_t, m_t, mcv_t = prev_p
    p_states = [h_t.reshape(DEPTH, bp, LRU_W), lcv_t, s_t, c_t, n_t, m_t[..., 0], mcv_t]
    hrow, nrow, mrow = rows_s
    s_out, c_out = big_s
    t_last = ts - 1
    s_states = [hrow.reshape(DEPTH, bs, ts, LRU_W)[:, :, t_last], jnp.stack(conv_l), s_out, c_out,
                nrow.reshape(DEPTH, bs, ts, ML_H, ML_DH)[:, :, t_last],
                mrow.reshape(DEPTH, bs, ts, LANE)[:, :, t_last, S_FG:S_FG + ML_H], jnp.stack(conv_m)]
    return x_p.reshape(bp, tp, D), x_s.reshape(bs, ts, D), p_states, s_states


def kernel(x_prompt, x_sample, c_prompt, c_sample, state_lru_h, state_lru_conv, state_gla, state_mlstm_C, state_mlstm_n, state_mlstm_m, state_mlstm_conv, w_ada, b_ada, g_norm1, g_norm2, w_in, lru_conv_w, lru_conv_b, lru_w_a, lru_b_a, lru_w_x, lru_b_x, lru_lam, gla_w_g2, gla_b_g, gla_g_norm, ml_conv_w, ml_conv_b, ml_w_q, ml_w_k, ml_w_v, ml_b_if, ml_g_norm, w_br_lru, w_br_gla, w_br_ml, w_out, w_ff1, w_ff2, g_final):
    p = dict(g_norm1=g_norm1, g_norm2=g_norm2, w_in=w_in, lru_conv_w=lru_conv_w, lru_conv_b=lru_conv_b,
             lru_w_a=lru_w_a, lru_b_a=lru_b_a, lru_w_x=lru_w_x, lru_b_x=lru_b_x, lru_lam=lru_lam,
             gla_w_g2=gla_w_g2, gla_b_g=gla_b_g, gla_g_norm=gla_g_norm, ml_conv_w=ml_conv_w, ml_conv_b=ml_conv_b,
             ml_w_q=ml_w_q, ml_w_k=ml_w_k, ml_w_v=ml_w_v, ml_b_if=ml_b_if, ml_g_norm=ml_g_norm,
             w_br_lru=w_br_lru, w_br_gla=w_br_gla, w_br_ml=w_br_ml, w_out=w_out, w_ff1=w_ff1, w_ff2=w_ff2,
             g_final=g_final)
    w = _prep_weights(p)
    bp = x_prompt.shape[0]
    bs, ts, _ = x_sample.shape
    assert ts == T_STEP and bs % STEP_BB == 0
    ns = bs * ts
    pad = (-(ns + bp)) % SUBLANE
    c_all = jnp.concatenate([jnp.repeat(c_sample, ts, axis=0), c_prompt, jnp.zeros((pad, D), F32)], axis=0)
    mod_all = _ada(c_all, w_ada, b_ada)
    mod_s = mod_all.reshape(DEPTH, 1, ns + bp + pad, N_MOD * D)
    mod_p = mod_all[:, ns:ns + bp].reshape(DEPTH, bp, 1, N_MOD * D)
    y_p, y_s, ps, ss = _trunk(x_prompt, x_sample, mod_p, mod_s,
                              (state_lru_h, state_lru_conv, state_gla, state_mlstm_C, state_mlstm_n, state_mlstm_m,
                               state_mlstm_conv), w)
    return (y_p, y_s, *ps, *ss)
```

```python
import functools

import jax
import jax.numpy as jnp
from jax import lax
from jax.experimental import pallas as pl
from jax.experimental.pallas import tpu as pltpu

F32, BF16 = jnp.float32, jnp.bfloat16
SDS = jax.ShapeDtypeStruct

D = 2048
DEPTH = 2
LRU_W = 1024
LRU_BLOCKS = 8
LRU_C = 8.0
CONV_W = 4
GLA_H = 4
GLA_DK = 128
GLA_DV = 256
GLA_RANK = 16
GLA_TAU = 16.0
ML_H = 4
ML_W = 1024
ML_DH = 256
ML_BS = 4
CHUNK = 128
ML_CHUNK = 256
PROMPT_ROWS = 512
PROMPT_SEQS = 1
D_FF = 4 * D
EPS = 1e-6
N_MOD = 6
T_STEP = 4
STEP_BB = 4

LANE = 128
SUBLANE = 8
VMEM_LIMIT_CAP = 60 * 1024 * 1024

C_LRU, C_GQ, C_GK, C_GV, C_GGATE, C_MX, C_MO, C_MG = 0, 1024, 1536, 2048, 3072, 4096, 5120, 6144
N_MAIN = 12288
W_SHIFT1 = GLA_RANK
W_SHIFT2 = GLA_RANK + 2 * ML_H
S_GLR = 0
S_IG = 16
S_FG = 20
NEG_BIG = -1e30


def _params(sem, vmem_bytes):
    return pltpu.CompilerParams(dimension_semantics=sem, vmem_limit_bytes=int(min(vmem_bytes, VMEM_LIMIT_CAP)))


def _dot(a, b):
    return jnp.dot(a.astype(BF16), b.astype(BF16), preferred_element_type=F32)


def _dot_nt(a, b):
    return lax.dot_general(a.astype(BF16), b.astype(BF16), (((1,), (1,)), ((), ())), preferred_element_type=F32)


def _split3(x):
    p0 = x.astype(BF16)
    r = x - p0.astype(F32)
    p1 = r.astype(BF16)
    return p0, p1, (r - p1.astype(F32)).astype(BF16)


def _cumsum_rows(tri_b, x):
    return sum(jnp.dot(tri_b, p, preferred_element_type=F32) for p in _split3(x))


def _cumsum_lanes(x, triu_b):
    return sum(jnp.dot(p, triu_b, preferred_element_type=F32) for p in _split3(x))


def _split(x):
    hi = x.astype(BF16)
    return hi, (x - hi.astype(F32)).astype(BF16)


def _dot3(x_hi, x_lo, w_hi, w_lo):
    d = lambda a, b: jnp.dot(a, b, preferred_element_type=F32)
    return d(x_hi, w_hi) + (d(x_lo, w_hi) + d(x_hi, w_lo))


def _log_sigmoid(z):
    return jnp.minimum(z, 0.0) - jnp.log1p(jnp.exp(-jnp.abs(z)))


def _silu(z):
    return z * jax.nn.sigmoid(z)


def _rms(x, g):
    return x * lax.rsqrt(jnp.mean(x * x, axis=-1, keepdims=True) + EPS) * g


def _tri(n):
    r = lax.broadcasted_iota(jnp.int32, (n, n), 0)
    c = lax.broadcasted_iota(jnp.int32, (n, n), 1)
    return r >= c


def _lru_gates(ub, wa, wx, ba, bx, lam):
    r = jax.nn.sigmoid(_dot(ub, wa) + ba)
    i = jax.nn.sigmoid(_dot(ub, wx) + bx)
    log_a = LRU_C * r * _log_sigmoid(lam)
    t = jnp.tanh(log_a)
    return jnp.exp(log_a), jnp.sqrt(-2.0 * t / (1.0 - t)) * (i * ub)


def _blockdiag3(x, wh_ref, wl_ref):
    xh, xl = _split(x)
    outs = []
    for blk in range(ML_W // LANE):
        sl = slice(blk * LANE, (blk + 1) * LANE)
        outs.append(_dot3(xh[:, sl], xl[:, sl], wh_ref[blk], wl_ref[blk]))
    return jnp.concatenate(outs, axis=1)


def _ada_kernel(c_ref, w_ref, b_ref, o_ref):
    o_ref[...] = _dot(_silu(c_ref[...]), w_ref[...]) + b_ref[...]


def _ada(c_all, w_ada, b_ada):
    m = c_all.shape[0]
    tn = 1024
    n_out = N_MOD * D
    return pl.pallas_call(
        _ada_kernel,
        grid=(DEPTH, n_out // tn),
        in_specs=[
            pl.BlockSpec((m, D), lambda l, n: (0, 0)),
            pl.BlockSpec((None, D, tn), lambda l, n: (l, 0, n)),
            pl.BlockSpec((None, 1, tn), lambda l, n: (l, 0, n)),
        ],
        out_specs=pl.BlockSpec((None, m, tn), lambda l, n: (l, 0, n)),
        out_shape=SDS((DEPTH, m, n_out), F32),
        compiler_params=_params(("parallel", "arbitrary"), 2 * (m * D + D * tn + m * tn) * 4 + (12 << 20)),
        name="ada",
    )(c_all, w_ada, b_ada.reshape(DEPTH, 1, n_out))


def _modspec(layer, comp, r, width, tm, rpg, jmap=None):
    nb = D // width
    jm = (lambda j: 0) if jmap is None else jmap
    if r == 1:
        return pl.BlockSpec((None, None, 1, width), lambda i, j: (layer, (i * tm) // rpg, 0, comp * nb + jm(j)))
    mode = pl.Buffered(1) if jmap is None else None
    return pl.BlockSpec((None, None, tm, width), lambda i, j: (layer, 0, i, comp * nb + jm(j)), pipeline_mode=mode)


def _norm_mod_store(xn_s, x_ref, g_ref, sc_ref, sh_ref):
    tm = x_ref.shape[0]
    rc = min(tm, 256)
    for c in range(tm // rc):
        rows = slice(c * rc, (c + 1) * rc)
        mrows = rows if sc_ref.shape[0] == tm else slice(None)
        xn = _rms(x_ref[rows, :], g_ref[...]) * (1.0 + sc_ref[mrows, :]) + sh_ref[mrows, :]
        xn_s[rows, :] = xn.astype(BF16)


def _in_proj_kernel(x_ref, sc_ref, sh_ref, g_ref, wm_ref, ws_ref, om_ref, os_ref, *rest, fill_steps):
    xn_s = rest[-1]

    @pl.when(pl.program_id(1) == 0)
    def _():
        _norm_mod_store(xn_s, x_ref, g_ref, sc_ref, sh_ref)
        os_ref[...] = jnp.dot(xn_s[...], ws_ref[...].astype(BF16), preferred_element_type=F32)

    om_ref[...] = jnp.dot(xn_s[...], wm_ref[...], preferred_element_type=F32)

    if fill_steps:
        @pl.when(pl.program_id(1) < fill_steps)
        def _():
            for z_ref in rest[:-1]:
                z_ref[...] = jnp.zeros_like(z_ref)


def _in_proj(x, mod, layer, g1, w_main, w_small, *, tm, tn, rpg, r, zero_rows=()):
    n = x.shape[0]
    vmem = 2 * (tm * D * 4 + 2 * r * D * 4 + D * tn * 2 + D * LANE * 2 + tm * tn * 4 + tm * LANE * 4) + tm * D * 2
    vmem += 8 << 20
    n_i, n_j = n // tm, N_MAIN // tn
    fill_steps = 0
    zero_specs, zero_shapes = [], []
    if zero_rows:
        fill_steps = max(s for s in range(1, n_j + 1)
                         if all(zr % (n_i * s * SUBLANE) == 0 for zr in zero_rows))
        for zr in zero_rows:
            blk = zr // (n_i * fill_steps)
            zero_specs.append(pl.BlockSpec((blk, GLA_DV),
                                           lambda i, j: (i * fill_steps + jnp.minimum(j, fill_steps - 1), 0)))
            zero_shapes.append(SDS((zr, GLA_DV), F32))
            vmem += 2 * blk * GLA_DV * 4
    outs = pl.pallas_call(
        functools.partial(_in_proj_kernel, fill_steps=fill_steps),
        grid=(n_i, n_j),
        in_specs=[
            pl.BlockSpec((tm, D), lambda i, j: (i, 0)),
            _modspec(layer, 1, r, D, tm, rpg),
            _modspec(layer, 0, r, D, tm, rpg),
            pl.BlockSpec((None, 1, D), lambda i, j: (layer, 0, 0)),
            pl.BlockSpec((D, tn), lambda i, j: (0, j)),
            pl.BlockSpec((None, D, LANE), lambda i, j: (layer, 0, 0)),
        ],
        out_specs=[pl.BlockSpec((tm, tn), lambda i, j: (i, j)), pl.BlockSpec((tm, LANE), lambda i, j: (i, 0)),
                   *zero_specs],
        out_shape=[SDS((n, N_MAIN), F32), SDS((n, LANE), F32), *zero_shapes],
        scratch_shapes=[pltpu.VMEM((tm, D), BF16)],
        compiler_params=_params(("parallel", "arbitrary"), vmem),
        name="in_proj",
    )(x, mod, mod, g1, w_main, w_small)
    return outs


def _in_proj_cast_kernel(x_ref, sc_ref, sh_ref, g_ref, wt_ref, ws_ref, om_ref, os_ref, wc_ref, xn_s):
    @pl.when(pl.program_id(1) == 0)
    def _():
        _norm_mod_store(xn_s, x_ref, g_ref, sc_ref, sh_ref)
        os_ref[...] = jnp.dot(xn_s[...], ws_ref[...].astype(BF16), preferred_element_type=F32)

    wb = wt_ref[...].T.astype(BF16)
    wc_ref[...] = wb
    om_ref[...] = jnp.dot(xn_s[...], wb, preferred_element_type=F32)


def _in_proj_cast(x, mod, layer, g1, w_in_t, w_small, *, tn, rpg):
    n = x.shape[0]
    tm = n
    nb1, nb2 = C_GGATE // tn, C_MG // tn

    def w_rows(i, j):
        shift = jnp.where(j >= nb2, W_SHIFT2 // SUBLANE, jnp.where(j >= nb1, W_SHIFT1 // SUBLANE, 0))
        return (layer, (j * (tn // SUBLANE) + shift) * SUBLANE, 0)

    vmem = tm * D * 4 + 2 * tm * D * 4 + 2 * (tn * D * 4 + D * LANE * 4 + tm * tn * 4 + tm * LANE * 4 + D * tn * 2)
    vmem += tm * D * 2 + 3 * D * tn * 4 + (4 << 20)
    return pl.pallas_call(
        _in_proj_cast_kernel,
        grid=(1, N_MAIN // tn),
        in_specs=[
            pl.BlockSpec((tm, D), lambda i, j: (i, 0), pipeline_mode=pl.Buffered(1)),
            _modspec(layer, 1, tm, D, tm, rpg),
            _modspec(layer, 0, tm, D, tm, rpg),
            pl.BlockSpec((None, 1, D), lambda i, j: (layer, 0, 0)),
            pl.BlockSpec((None, pl.Element(tn), pl.Element(D)), w_rows),
            pl.BlockSpec((None, D, LANE), lambda i, j: (layer, 0, 0)),
        ],
        out_specs=[pl.BlockSpec((tm, tn), lambda i, j: (i, j)), pl.BlockSpec((tm, LANE), lambda i, j: (i, 0)),
                   pl.BlockSpec((D, tn), lambda i, j: (0, j))],
        out_shape=[SDS((n, N_MAIN), F32), SDS((n, LANE), F32), SDS((D, N_MAIN), BF16)],
        scratch_shapes=[pltpu.VMEM((tm, D), BF16)],
        compiler_params=_params(("arbitrary", "arbitrary"), vmem),
        name="in_proj_cast",
    )(x, mod, mod, g1, w_in_t, w_small)


def _wspec(shape, imap, layer, cast):
    if cast:
        return pl.BlockSpec((None,) + shape, lambda i, j: (layer,) + imap(i, j))
    return pl.BlockSpec(shape, imap)


def _merge_kernel(yl_ref, yg_ref, ym_ref, g0_ref, g1_ref, g2_ref, w0_ref, w1_ref, w2_ref, o_ref, *wc_refs):
    ws = [w_ref[...].astype(BF16) for w_ref in (w0_ref, w1_ref, w2_ref)]
    for wc_ref, wb in zip(wc_refs, ws):
        wc_ref[...] = wb
    acc = jax.nn.sigmoid(g0_ref[...]) * jnp.dot(yl_ref[...], ws[0], preferred_element_type=F32)
    acc += jax.nn.sigmoid(g1_ref[...]) * jnp.dot(yg_ref[...], ws[1], preferred_element_type=F32)
    acc += jax.nn.sigmoid(g2_ref[...]) * jnp.dot(ym_ref[...], ws[2], preferred_element_type=F32)
    o_ref[...] = acc.astype(BF16)


def _merge(y_lru, y_gla, y_ml, proj, layer, w_lru, w_gla, w_ml, *, tm, tn, cast):
    n = y_lru.shape[0]
    w = LRU_W
    gb = C_MG // tn
    nb = D // tn
    yspec = pl.BlockSpec((tm, w), lambda i, j: (i, 0))
    wspec = _wspec((w, tn), lambda i, j: (0, j), layer, cast)
    wbytes = 4 if cast else 2
    vmem = 2 * (3 * tm * w * 2 + 3 * tm * tn * 4 + 3 * w * tn * wbytes + tm * tn * 2) + 6 * tm * tn * 4 + (4 << 20)
    out_specs = [pl.BlockSpec((tm, tn), lambda i, j: (i, j))]
    out_shape = [SDS((n, D), BF16)]
    if cast:
        out_specs += [pl.BlockSpec((w, tn), lambda i, j: (0, j))] * 3
        out_shape += [SDS((w, D), BF16)] * 3
        vmem += 2 * 3 * w * tn * 2 + 3 * w * tn * 4
    return pl.pallas_call(
        _merge_kernel,
        grid=(n // tm, nb),
        in_specs=[
            yspec, yspec, yspec,
            pl.BlockSpec((tm, tn), lambda i, j: (i, gb + j)),
            pl.BlockSpec((tm, tn), lambda i, j: (i, gb + nb + j)),
            pl.BlockSpec((tm, tn), lambda i, j: (i, gb + 2 * nb + j)),
            wspec, wspec, wspec,
        ],
        out_specs=out_specs,
        out_shape=out_shape,
        compiler_params=_params(("arbitrary" if cast else "parallel", "arbitrary"), vmem),
        name="merge",
    )(y_lru, y_gla, y_ml, proj, proj, proj, w_lru, w_gla, w_ml)


def _out_proj_kernel(m_ref, w_ref, x_ref, gt_ref, o_ref, *wc_refs):
    wb = w_ref[...].astype(BF16)
    for wc_ref in wc_refs:
        wc_ref[...] = wb
    o_ref[...] = x_ref[...] + gt_ref[...] * jnp.dot(m_ref[...], wb, preferred_element_type=F32)


def _out_proj(merged, x, mod, layer, w_out, *, tm, tn, rpg, r, cast):
    n = x.shape[0]
    wbytes = 4 if cast else 2
    vmem = 2 * (tm * D * 2 + D * tn * wbytes + 2 * tm * tn * 4 + r * tn * 4) + 2 * tm * tn * 4 + (4 << 20)
    out_specs = [pl.BlockSpec((tm, tn), lambda i, j: (i, j))]
    out_shape = [SDS((n, D), F32)]
    if cast:
        out_specs.append(pl.BlockSpec((D, tn), lambda i, j: (0, j)))
        out_shape.append(SDS((D, D), BF16))
        vmem += 2 * D * tn * 2 + D * tn * 4
    return pl.pallas_call(
        _out_proj_kernel,
        grid=(n // tm, D // tn),
        in_specs=[
            pl.BlockSpec((tm, D), lambda i, j: (i, 0)),
            _wspec((D, tn), lambda i, j: (0, j), layer, cast),
            pl.BlockSpec((tm, tn), lambda i, j: (i, j)),
            _modspec(layer, 2, r, tn, tm, rpg, jmap=lambda j: j),
        ],
        out_specs=out_specs,
        out_shape=out_shape,
        compiler_params=_params(("arbitrary" if cast else "parallel", "arbitrary"), vmem),
        name="out_proj",
    )(merged, w_out, x, mod)


def _ffn_kernel(x_ref, sc_ref, sh_ref, gt_ref, g_ref, gf_ref, w1_ref, w2_ref, o_ref, *rest, final_norm):
    xn_s = rest[-1]
    f = pl.program_id(1)

    @pl.when(f == 0)
    def _():
        _norm_mod_store(xn_s, x_ref, g_ref, sc_ref, sh_ref)
        o_ref[...] = jnp.zeros_like(o_ref)

    w1 = w1_ref[...].astype(BF16)
    w2 = w2_ref[...].astype(BF16)
    for wc_ref, wb in zip(rest[:-1], (w1, w2)):
        wc_ref[...] = wb
    h = jnp.square(jnp.maximum(jnp.dot(xn_s[...], w1, preferred_element_type=F32), 0.0))
    o_ref[...] += jnp.dot(h.astype(BF16), w2, preferred_element_type=F32)

    @pl.when(f == pl.num_programs(1) - 1)
    def _():
        y = x_ref[...] + gt_ref[...] * o_ref[...]
        if final_norm:
            y = _rms(y, gf_ref[...])
        o_ref[...] = y


def _ffn(x, mod, layer, g2, g_final, w1, w2, *, tm, tf, rpg, r, final_norm, cast):
    n = x.shape[0]
    wbytes = 4 if cast else 2
    vmem = 2 * (2 * tm * D * 4 + 3 * r * D * 4 + 2 * D * tf * wbytes) + tm * D * 2 + 2 * tm * tf * 4 + 2 * tm * D * 4
    single = pl.Buffered(1) if n == tm else None
    out_specs = [pl.BlockSpec((tm, D), lambda i, f: (i, 0), pipeline_mode=single)]
    out_shape = [SDS((n, D), F32)]
    if cast:
        out_specs += [pl.BlockSpec((D, tf), lambda i, f: (0, f)), pl.BlockSpec((tf, D), lambda i, f: (f, 0))]
        out_shape += [SDS((D, D_FF), BF16), SDS((D_FF, D), BF16)]
        vmem += 2 * 2 * D * tf * 2
    return pl.pallas_call(
        functools.partial(_ffn_kernel, final_norm=final_norm),
        grid=(n // tm, D_FF // tf),
        in_specs=[
            pl.BlockSpec((tm, D), lambda i, f: (i, 0), pipeline_mode=single),
            _modspec(layer, 4, r, D, tm, rpg),
            _modspec(layer, 3, r, D, tm, rpg),
            _modspec(layer, 5, r, D, tm, rpg),
            pl.BlockSpec((None, 1, D), lambda i, f: (layer, 0, 0)),
            pl.BlockSpec((1, D), lambda i, f: (0, 0)),
            _wspec((D, tf), lambda i, f: (0, f), layer, cast),
            _wspec((tf, D), lambda i, f: (f, 0), layer, cast),
        ],
        out_specs=out_specs,
        out_shape=out_shape,
        scratch_shapes=[pltpu.VMEM((tm, D), BF16)],
        compiler_params=_params(("arbitrary" if cast else "parallel", "arbitrary"), vmem),
        name="ffn",
    )(x, mod, mod, mod, g2, g_final, w1, w2)


def _conv_chunk(carry_s, x, w, b, L):
    t8 = lax.broadcasted_iota(jnp.int32, (SUBLANE, x.shape[1]), 0)
    carry = carry_s[...]
    acc = b + x * w[CONV_W - 1:CONV_W]
    for k in range(1, CONV_W):
        down = pltpu.roll(x, k, 0)
        head = jnp.where(t8 < k, pltpu.roll(carry, k, 0), down[0:SUBLANE, :])
        acc = acc + jnp.concatenate([head, down[SUBLANE:, :]], axis=0) * w[CONV_W - 1 - k:CONV_W - k]
    carry_s[...] = x[L - SUBLANE:L, :]
    return acc, x[L - (CONV_W - 1):L, :]


N_TOK_REFS, N_W_REFS, N_Y_REFS, N_STATE_REFS, N_SCRATCH = 8, 20, 3, 7, 9


def _prompt_mixer_kernel(*refs, L, layer, creates):
    tok = refs[:N_TOK_REFS]
    wts = refs[N_TOK_REFS:N_TOK_REFS + N_W_REFS]
    scr = refs[-N_SCRATCH:]
    outs = refs[-(N_SCRATCH + N_Y_REFS + N_STATE_REFS):-N_SCRATCH]
    ys, state_outs = outs[:N_Y_REFS], outs[N_Y_REFS:]
    nseq = tok[0].shape[0]

    @pl.when(pl.program_id(1) == 0)
    def _():
        xpl_s, xpm_s = scr[0], scr[1]
        xpl_s[:, 0:SUBLANE, :] = jnp.zeros((nseq, SUBLANE, LRU_W), F32)
        xpm_s[:, 0:SUBLANE, :] = jnp.zeros((nseq, SUBLANE, ML_W), F32)
        for s_ref in scr[4:]:
            s_ref[...] = jnp.zeros_like(s_ref)

    finals = [_prompt_mixer_rows([r.at[bi] for r in tok], wts, [y.at[bi] for y in ys], [s.at[bi] for s in scr], L)
              for bi in range(nseq)]

    @pl.when(pl.program_id(1) == pl.num_programs(1) - 1)
    def _():
        souts = state_outs
        if creates:
            for ref in souts:
                for other in range(DEPTH):
                    if other != layer:
                        ref[other] = jnp.zeros(ref.shape[1:], F32)
            souts = [ref.at[layer] for ref in souts]
        ht_ref, lct_ref, st_ref, ct_ref, nt_ref, mt_ref, mct_ref = souts
        s_s, c_s, n_s, m_s = scr[5:]
        for bi, (h, l_tail, m_tail) in enumerate(finals):
            ht_ref[bi] = h
            lct_ref[bi] = l_tail
            mct_ref[bi] = m_tail
        st_ref[...] = s_s[...]
        ct_ref[...] = c_s[...]
        nt_ref[...] = n_s[...]
        mt_ref[...] = m_s[...]


def _prompt_mixer_rows(tok, wts, ys, scr, L):
    lx_ref, q_ref, k_ref, v_ref, gate_ref, mx_ref, mo_ref, small_ref = tok
    (lcw_ref, lcb_ref, wa_ref, ba_ref, wx_ref, bx_ref, lam_ref, wg2_ref, bg_ref, ggn_ref,
     mcw_ref, mcb_ref, wqh_ref, wql_ref, wkh_ref, wkl_ref, wvh_ref, wvl_ref, bif_ref, mgn_ref) = wts
    yl_ref, yg_ref, ym_ref = ys
    xpl_s, xpm_s, a_s, u_s, h_s, s_s, c_s, n_s, m_s = scr
    rows = lx_ref.shape[0]

    q, k, v, gate = q_ref[...], k_ref[...], v_ref[...], gate_ref[...]
    small = small_ref[...]
    logdec = _log_sigmoid(_dot(small, wg2_ref[...]) + bg_ref[...]) * (1.0 / GLA_TAU)
    mx = mx_ref[...]
    conv, m_tail = _conv_chunk(xpm_s, mx, mcw_ref[...], mcb_ref[...], rows)
    mc = _silu(conv)
    mq = _blockdiag3(mc, wqh_ref, wql_ref)
    mk = _blockdiag3(mc, wkh_ref, wkl_ref) * (ML_DH ** -0.5)
    mv = _blockdiag3(mx, wvh_ref, wvl_ref)

    after_proj = jnp.where(mq[0:1, :] != mq[0:1, :], 1.0, 0.0)
    u, l_tail = _conv_chunk(xpl_s, lx_ref[...], lcw_ref[...], lcb_ref[...] + after_proj, rows)
    for blk in range(LRU_BLOCKS):
        sl = slice(blk * LANE, (blk + 1) * LANE)
        a_s[:, sl], u_s[:, sl] = _lru_gates(u[:, sl], wa_ref[blk], wx_ref[blk], ba_ref[:, sl], bx_ref[:, sl],
                                            lam_ref[:, sl])

    gates = small + bif_ref[...]
    gates_t = gates.T
    lf = _log_sigmoid(gates)
    lf_t = _log_sigmoid(gates_t)
    mo = mo_ref[...]
    tri = _tri(L)
    tri_b = tri.astype(BF16)

    for cc in range(rows // L):
        rs = slice(cc * L, (cc + 1) * L)
        bcum = _cumsum_rows(tri_b, logdec[rs, :])
        for hh in range(GLA_H):
            ks = slice(hh * GLA_DK, (hh + 1) * GLA_DK)
            vs = slice(hh * GLA_DV, (hh + 1) * GLA_DV)
            bh = bcum[:, ks]
            k_h, v_h = k[rs, ks], v[rs, vs]
            b_mid = bh[L // 2 - 1:L // 2, :]
            qe = (q[rs, ks] * (GLA_DK ** -0.5)) * jnp.exp(bh - b_mid)
            ke = k_h * jnp.exp(b_mid - bh)
            a = jnp.where(tri, _dot_nt(qe, ke), 0.0)
            s_h = s_s[hh]
            bl = bh[L - 1:L, :]
            cols = jnp.exp(jnp.concatenate([jnp.broadcast_to(b_mid, (SUBLANE // 2, GLA_DK)),
                                            jnp.broadcast_to(bl, (SUBLANE // 2, GLA_DK))], axis=0)).T
            mid_col, dec_col = cols[:, 0:1], cols[:, SUBLANE // 2:SUBLANE // 2 + 1]
            o = _dot(qe, mid_col * s_h) + _dot(a, v_h)
            kd = k_h * jnp.exp(bl - bh)
            s_s[hh] = dec_col * s_h + _dot(kd.T, v_h)
            yg_ref[rs, vs] = (_rms(o, ggn_ref[...]) * _silu(gate[rs, vs])).astype(BF16)

    L = min(rows, ML_CHUNK)
    tri = _tri(L)
    tri_b = tri.astype(BF16)
    r_i = lax.broadcasted_iota(jnp.int32, (L, L), 0)
    c_i = lax.broadcasted_iota(jnp.int32, (L, L), 1)
    triu_b = (r_i <= c_i).astype(BF16)
    for cc in range(rows // L):
        rs = slice(cc * L, (cc + 1) * L)
        f_col = _cumsum_rows(tri_b, lf[rs, :])
        f_row = _cumsum_lanes(lf_t[:, rs], triu_b)
        for hh in range(ML_H):
            sl = slice(hh * ML_DH, (hh + 1) * ML_DH)
            qh, kh, vh = mq[rs, sl], mk[rs, sl], mv[rs, sl]
            fc = f_col[:, S_FG + hh:S_FG + hh + 1]
            fr = f_row[S_FG + hh:S_FG + hh + 1, :]
            igr = gates_t[S_IG + hh:S_IG + hh + 1, rs]
            m_h = m_s[hh:hh + 1, 0:1]
            dm = jnp.where(tri, fc - fr + igr, NEG_BIG)
            inter = fc + m_h
            mt = jnp.maximum(inter, jnp.max(dm, axis=-1, keepdims=True))
            ci = jnp.exp(inter - mt)
            s = _dot_nt(qh, kh) * jnp.exp(dm - mt)
            c_h = c_s[hh]
            n_h = n_s[hh:hh + 1, :]
            num = ci * _dot(qh, c_h) + _dot(s, vh)
            den = ci * jnp.sum(qh * n_h, axis=-1, keepdims=True) + jnp.sum(s, axis=-1, keepdims=True)
            hcell = num / jnp.maximum(jnp.abs(den), jnp.exp(-mt))
            fl = fr[:, L - 1:L]
            dj = fl - fr + igr
            m_new = jnp.maximum(fl + m_h, jnp.max(dj, axis=-1, keepdims=True))
            cs = jnp.exp(fl + m_h - m_new)
            wj = jnp.exp(dj - m_new)
            c_s[hh] = cs * c_h + _dot(kh.T * wj, vh)
            n_s[hh:hh + 1, :] = cs * n_h + _dot(jnp.broadcast_to(wj, (SUBLANE, L)), kh)[0:1, :]
            m_s[hh:hh + 1, :] = jnp.broadcast_to(m_new, (1, LANE))
            ym_ref[rs, sl] = (jax.nn.sigmoid(mo[rs, sl]) * _rms(hcell, mgn_ref[...])).astype(BF16)

    t8 = lax.broadcasted_iota(jnp.int32, (SUBLANE, LRU_W), 0)
    h = h_s[...]
    for g in range(rows // SUBLANE):
        r8 = slice(g * SUBLANE, (g + 1) * SUBLANE)
        a8, u8 = a_s[r8, :], u_s[r8, :]
        for s in (1, 2, 4):
            u8 = jnp.where(t8 >= s, u8 + a8 * pltpu.roll(u8, s, 0), u8)
            a8 = jnp.where(t8 >= s, a8 * pltpu.roll(a8, s, 0), a8)
        h8 = u8 + a8 * h
        a_s[r8, :] = h8
        h = h8[SUBLANE - 1:SUBLANE, :]
    h_s[...] = h
    yl_ref[...] = a_s[...].astype(BF16)
    return h, l_tail, m_tail


def _prompt_mixer(proj3, small3, layer, w, prev, *, L, rows, nseq):
    b, t, _ = proj3.shape
    kw, vw = GLA_H * GLA_DK, GLA_H * GLA_DV
    nblk = ML_W // LANE
    blk3 = lambda width, cb: pl.BlockSpec((nseq, rows, width), lambda bi, c: (bi, c, cb))
    lw3 = lambda s: pl.BlockSpec((None,) + s, lambda bi, c: (layer, 0, 0))
    lw4 = lambda s: pl.BlockSpec((None,) + s, lambda bi, c: (layer, 0, 0, 0))
    creates = prev is None
    lead, at = ((DEPTH,), 0) if creates else ((None,), layer)
    st3 = lambda s: pl.BlockSpec(lead + (nseq,) + s, lambda bi, c: (at, bi, 0, 0))
    st4 = lambda s: pl.BlockSpec(lead + (nseq,) + s, lambda bi, c: (at, bi, 0, 0, 0))
    in_specs = [
        blk3(LRU_W, C_LRU // LRU_W), blk3(kw, C_GQ // kw), blk3(kw, C_GK // kw), blk3(vw, C_GV // vw),
        blk3(vw, C_GGATE // vw), blk3(ML_W, C_MX // ML_W), blk3(ML_W, C_MO // ML_W), blk3(LANE, 0),
        lw3((CONV_W, LRU_W)), lw3((1, LRU_W)), lw4((LRU_BLOCKS, LANE, LANE)), lw3((1, LRU_W)),
        lw4((LRU_BLOCKS, LANE, LANE)), lw3((1, LRU_W)), lw3((1, LRU_W)),
        lw3((LANE, kw)), lw3((1, kw)), lw3((1, GLA_DV)),
        lw3((CONV_W, ML_W)), lw3((1, ML_W)), *([lw4((nblk, LANE, LANE))] * 6), lw3((1, LANE)), lw3((1, ML_DH)),
    ]
    args = [proj3] * 7 + [small3,
                          w["lru_cw"], w["lru_cb"], w["lru_wa"], w["lru_ba"], w["lru_wx"], w["lru_bx"], w["lru_lam"],
                          w["wg2"], w["bg"], w["gla_gn"],
                          w["ml_cw"], w["ml_cb"], *w["ml_wq"], *w["ml_wk"], *w["ml_wv"], w["ml_bif"], w["ml_gn"]]
    aliases = {}
    if prev is not None:
        aliases = {len(in_specs) + j: 3 + j for j in range(len(prev))}
        in_specs += [pl.BlockSpec(memory_space=pl.ANY)] * len(prev)
        args += list(prev)
    sbytes = GLA_H * GLA_DK * GLA_DV * 4
    cbytes = ML_H * ML_DH * ML_DH * 4
    vmem = nseq * (2 * rows * (7 * 1024 * 4 + 3 * 1024 * 2) + 3 * (sbytes + cbytes) + 6 * rows * 1024 * 4)
    vmem += 30 * rows * 1024 * 4 + (12 << 20)
    sc = lambda *s: pltpu.VMEM((nseq,) + s, F32)
    return pl.pallas_call(
        functools.partial(_prompt_mixer_kernel, L=L, layer=layer, creates=creates),
        grid=(b // nseq, t // rows),
        in_specs=in_specs,
        out_specs=[blk3(LRU_W, 0), blk3(vw, 0), blk3(ML_W, 0),
                   st3((1, LRU_W)), st3((CONV_W - 1, LRU_W)), st4((GLA_H, GLA_DK, GLA_DV)),
                   st4((ML_H, ML_DH, ML_DH)), st3((ML_H, ML_DH)), st3((ML_H, LANE)), st3((CONV_W - 1, ML_W))],
        out_shape=[SDS((b, t, LRU_W), BF16), SDS((b, t, vw), BF16), SDS((b, t, ML_W), BF16),
                   SDS((DEPTH, b, 1, LRU_W), F32), SDS((DEPTH, b, CONV_W - 1, LRU_W), F32),
                   SDS((DEPTH, b, GLA_H, GLA_DK, GLA_DV), F32), SDS((DEPTH, b, ML_H, ML_DH, ML_DH), F32),
                   SDS((DEPTH, b, ML_H, ML_DH), F32), SDS((DEPTH, b, ML_H, LANE), F32),
                   SDS((DEPTH, b, CONV_W - 1, ML_W), F32)],
        scratch_shapes=[sc(SUBLANE, LRU_W), sc(SUBLANE, ML_W), sc(rows, LRU_W), sc(rows, LRU_W),
                        sc(1, LRU_W), sc(GLA_H, GLA_DK, GLA_DV), sc(ML_H, ML_DH, ML_DH), sc(ML_H, ML_DH),
                        sc(ML_H, LANE)],
        input_output_aliases=aliases,
        compiler_params=_params(("parallel", "arbitrary"), vmem),
        name="prompt_mixer",
    )(*args)


def _tmask(shape):
    return lax.broadcasted_iota(jnp.int32, shape, 0) % T_STEP


def _down(x, s):
    return x if s == 0 else pltpu.roll(x, s, 0)


def _up(x, s):
    return x if s == 0 else pltpu.roll(x, x.shape[0] - s, 0)


def _seg_cumsum(x):
    t = _tmask(x.shape)
    out = x
    for s in range(1, T_STEP):
        out = out + jnp.where(t >= s, _down(x, s), 0.0)
    return out


def _seg_last(x):
    t = _tmask(x.shape)
    out = x
    for s in range(1, T_STEP):
        out = jnp.where(t == T_STEP - 1 - s, _up(x, s), out)
    return out


def _seg_allreduce(x, op):
    t = _tmask(x.shape)
    y = op(x, jnp.where(t % 2 == 1, _down(x, 1), _up(x, 1)))
    return op(y, jnp.where(t >= 2, _down(y, 2), _up(y, 2)))


def _conv_rows(x, e, w, b):
    t = _tmask(x.shape)
    acc = b + x * w[CONV_W - 1:CONV_W]
    for k in range(1, CONV_W):
        hist = jnp.where(t >= k, _down(x, k), _up(e, CONV_W - 1 - k))
        acc = acc + hist * w[CONV_W - 1 - k:CONV_W - k]
    return acc


def _col(x, lane):
    li = lax.broadcasted_iota(jnp.int32, x.shape, 1)
    return jnp.sum(jnp.where(li == lane, x, 0.0), axis=-1, keepdims=True)


def _sample_mixer_kernel(lx_ref, q_ref, k_ref, v_ref, gate_ref, mx_ref, mo_ref, small_ref,
                         el_ref, em_ref, h0_ref, n0_ref, m0_ref, s0_ref, c0_ref,
                         lcw_ref, lcb_ref, wa_ref, ba_ref, wx_ref, bx_ref, lam_ref,
                         wg2_ref, bg_ref, ggn_ref,
                         mcw_ref, mcb_ref, wqh_ref, wql_ref, wkh_ref, wkl_ref, wvh_ref, wvl_ref, bif_ref, mgn_ref,
                         *rest, layer, creates):
    yl_ref, yg_ref, ym_ref, *state_refs = rest[-8:]
    row_refs, (st_ref, ct_ref) = state_refs[:3], state_refs[3:]
    if creates:
        for ref in row_refs:
            for other in range(DEPTH):
                if other != layer:
                    ref[other] = jnp.zeros(ref.shape[1:], F32)
        row_refs = [ref.at[layer] for ref in row_refs]
    hrow_ref, nrow_ref, mrow_ref = row_refs
    rows = lx_ref.shape[0]
    n_pairs = rows // SUBLANE
    seq_per_tile = SUBLANE // T_STEP
    t_col = _tmask((rows, 1))
    row8 = lax.broadcasted_iota(jnp.int32, (SUBLANE, 1), 0)

    u = _conv_rows(lx_ref[...], el_ref[...], lcw_ref[...], lcb_ref[...])
    a_blocks, g_blocks = [], []
    for blk in range(LRU_BLOCKS):
        sl = slice(blk * LANE, (blk + 1) * LANE)
        a_b, g_b = _lru_gates(u[:, sl], wa_ref[blk], wx_ref[blk], ba_ref[:, sl], bx_ref[:, sl], lam_ref[:, sl])
        a_blocks.append(a_b)
        g_blocks.append(g_b)
    a = jnp.concatenate(a_blocks, axis=1)
    t_w = _tmask(a.shape)
    g = jnp.concatenate(g_blocks, axis=1) + jnp.where(t_w == 0, a * h0_ref[...], 0.0)
    a1 = jnp.where(t_w >= 1, a * _down(a, 1), a)
    g1 = jnp.where(t_w >= 1, g + a * _down(g, 1), g)
    h = jnp.where(t_w >= 2, g1 + a1 * _down(g1, 2), g1)
    yl_ref[...] = h.astype(BF16)
    hrow_ref[...] = h

    q, k, v = q_ref[...], k_ref[...], v_ref[...]
    logdec = _log_sigmoid(_dot(small_ref[...], wg2_ref[...]) + bg_ref[...]) * (1.0 / GLA_TAU)
    bc = _seg_cumsum(logdec)
    bl = _seg_last(bc)
    qs = q * (GLA_DK ** -0.5)
    o_heads = [jnp.zeros((rows, GLA_DV), F32) for _ in range(GLA_H)]
    for s in range(T_STEP):
        prod = qs * _down(k, s) * jnp.exp(bc - _down(bc, s))
        v_s = _down(v, s)
        for hh in range(GLA_H):
            a_sh = jnp.sum(prod[:, hh * GLA_DK:(hh + 1) * GLA_DK], axis=-1, keepdims=True)
            a_sh = jnp.where(t_col >= s, a_sh, 0.0)
            o_heads[hh] = o_heads[hh] + a_sh * v_s[:, hh * GLA_DV:(hh + 1) * GLA_DV]
    qe = qs * jnp.exp(bc)
    kd = k * jnp.exp(bl - bc)
    dec = jnp.exp(bl)
    gate = gate_ref[...]
    for hh in range(GLA_H):
        ks = slice(hh * GLA_DK, (hh + 1) * GLA_DK)
        vs = slice(hh * GLA_DV, (hh + 1) * GLA_DV)
        o_state = []
        for p in range(n_pairs):
            r8 = slice(p * SUBLANE, (p + 1) * SUBLANE)
            kd_t = kd[r8, ks].T
            dec_t = dec[r8, ks].T
            res = None
            for j in range(seq_per_tile):
                b = p * seq_per_tile + j
                s_b = s0_ref[b, hh]
                r_j = _dot(qe[r8, ks], s_b)
                res = r_j if res is None else jnp.where(row8 // T_STEP == j, r_j, res)
                v_j = jnp.where(row8 // T_STEP == j, v[r8, vs], 0.0)
                last = j * T_STEP + T_STEP - 1
                st_ref[b, hh] = dec_t[:, last:last + 1] * s_b + _dot(kd_t, v_j)
            o_state.append(res)
        o = o_heads[hh] + jnp.concatenate(o_state, axis=0)
        yg_ref[:, vs] = (_rms(o, ggn_ref[...]) * _silu(gate[:, vs])).astype(BF16)

    mx = mx_ref[...]
    mc = _silu(_conv_rows(mx, em_ref[...], mcw_ref[...], mcb_ref[...]))
    mq = _blockdiag3(mc, wqh_ref, wql_ref)
    mk = _blockdiag3(mc, wkh_ref, wkl_ref) * (ML_DH ** -0.5)
    mv = _blockdiag3(mx, wvh_ref, wvl_ref)
    gts = small_ref[...] + bif_ref[...]
    ig = pltpu.roll(gts, S_FG - S_IG, 1)
    fcum = _seg_cumsum(_log_sigmoid(gts))
    flast = _seg_last(fcum)
    m0 = m0_ref[...]
    t_g = _tmask(gts.shape)
    inter = fcum + m0
    dms = [jnp.where(t_g >= s, fcum - _down(fcum, s) + _down(ig, s), NEG_BIG) for s in range(T_STEP)]
    mt = inter
    for dm in dms:
        mt = jnp.maximum(mt, dm)
    ci_t = jnp.exp(inter - mt)
    emt_t = jnp.exp(-mt)
    w_t = [jnp.exp(dm - mt) for dm in dms]
    dj = flast - fcum + ig
    m_new = jnp.maximum(flast + m0, _seg_allreduce(dj, jnp.maximum))
    cs_t = jnp.exp(flast + m0 - m_new)
    wj_t = jnp.exp(dj - m_new)
    mrow_ref[...] = m_new
    mo = mo_ref[...]
    n0 = n0_ref[...]
    for hh in range(ML_H):
        sl = slice(hh * ML_DH, (hh + 1) * ML_DH)
        lane = S_FG + hh
        qh, kh, vh = mq[:, sl], mk[:, sl], mv[:, sl]
        ci, emt, cs, wj = _col(ci_t, lane), _col(emt_t, lane), _col(cs_t, lane), _col(wj_t, lane)
        num = jnp.zeros((rows, ML_DH), F32)
        den = ci * jnp.sum(qh * n0[:, sl], axis=-1, keepdims=True)
        for s in range(T_STEP):
            sc = jnp.sum(qh * _down(kh, s), axis=-1, keepdims=True) * _col(w_t[s], lane)
            num = num + sc * _down(vh, s)
            den = den + sc
        kw = kh * wj
        nrow_ref[:, sl] = cs * n0[:, sl] + _seg_allreduce(kw, jnp.add)
        qc = []
        for p in range(n_pairs):
            r8 = slice(p * SUBLANE, (p + 1) * SUBLANE)
            kw_t = kw[r8, :].T
            res = None
            for j in range(seq_per_tile):
                b = p * seq_per_tile + j
                c_b = c0_ref[b, hh]
                r_j = _dot(qh[r8, :], c_b)
                res = r_j if res is None else jnp.where(row8 // T_STEP == j, r_j, res)
                v_j = jnp.where(row8 // T_STEP == j, vh[r8, :], 0.0)
                last = p * SUBLANE + j * T_STEP + T_STEP - 1
                ct_ref[b, hh] = cs[last:last + 1, :] * c_b + _dot(kw_t, v_j)
            qc.append(res)
        num = num + ci * jnp.concatenate(qc, axis=0)
        hcell = num / jnp.maximum(jnp.abs(den), emt)
        ym_ref[:, sl] = (jax.nn.sigmoid(mo[:, sl]) * _rms(hcell, mgn_ref[...])).astype(BF16)


def _sample_mixer(proj, small, el, em, h0e, n0e, m0e, s_state, c_state, layer, w, row_prev, big_prev):
    n = proj.shape[0]
    nseq = n // T_STEP
    bb = STEP_BB
    rows = bb * T_STEP
    row = lambda width, blk: pl.BlockSpec((rows, width), lambda i: (i, blk))
    lrow = lambda width: pl.BlockSpec((None, rows, width), lambda i: (layer, i, 0))
    lw3 = lambda s: pl.BlockSpec((None,) + s, lambda i: (layer, 0, 0))
    lw4 = lambda s: pl.BlockSpec((None,) + s, lambda i: (layer, 0, 0, 0))
    sspec = pl.BlockSpec((None, bb, GLA_H, GLA_DK, GLA_DV), lambda i: (layer, i, 0, 0, 0))
    cspec = pl.BlockSpec((None, bb, ML_H, ML_DH, ML_DH), lambda i: (layer, i, 0, 0, 0))
    nblk = ML_W // LANE
    kw, vw = GLA_H * GLA_DK, GLA_H * GLA_DV
    in_specs = [
        row(LRU_W, C_LRU // LRU_W), row(kw, C_GQ // kw), row(kw, C_GK // kw), row(vw, C_GV // vw),
        row(vw, C_GGATE // vw), row(ML_W, C_MX // ML_W), row(ML_W, C_MO // ML_W), row(LANE, 0),
        lrow(LRU_W), lrow(ML_W), lrow(LRU_W), lrow(ML_W), lrow(LANE), sspec, cspec,
        lw3((CONV_W, LRU_W)), lw3((1, LRU_W)), lw4((LRU_BLOCKS, LANE, LANE)), lw3((1, LRU_W)),
        lw4((LRU_BLOCKS, LANE, LANE)), lw3((1, LRU_W)), lw3((1, LRU_W)),
        lw3((LANE, kw)), lw3((1, kw)), lw3((1, GLA_DV)),
        lw3((CONV_W, ML_W)), lw3((1, ML_W)), *([lw4((nblk, LANE, LANE))] * 6), lw3((1, LANE)), lw3((1, ML_DH)),
    ]
    args = [proj] * 7 + [small, el, em, h0e, n0e, m0e, s_state, c_state,
                         w["lru_cw"], w["lru_cb"], w["lru_wa"], w["lru_ba"], w["lru_wx"], w["lru_bx"], w["lru_lam"],
                         w["wg2"], w["bg"], w["gla_gn"],
                         w["ml_cw"], w["ml_cb"], *w["ml_wq"], *w["ml_wk"], *w["ml_wv"], w["ml_bif"], w["ml_gn"]]
    creates = row_prev is None
    prev = list(big_prev) if creates else list(row_prev) + list(big_prev)
    first_out = 6 if creates else 3
    aliases = {len(in_specs) + j: first_out + j for j in range(len(prev))}
    in_specs += [pl.BlockSpec(memory_space=pl.ANY)] * len(prev)
    args += prev
    sbytes = bb * GLA_H * GLA_DK * GLA_DV * 4
    cbytes = bb * ML_H * ML_DH * ML_DH * 4
    vmem = 4 * (sbytes + cbytes) + 2 * rows * (7 * 1024 + 8 * 1024) * 4 + 80 * rows * 1024 * 4 + (12 << 20)
    row_specs = [lrow(LRU_W), lrow(ML_W), lrow(LANE)]
    if creates:
        all_rows = lambda width: pl.BlockSpec((DEPTH, rows, width), lambda i: (0, i, 0))
        row_specs = [all_rows(LRU_W), all_rows(ML_W), all_rows(LANE)]
    state_specs = row_specs + [sspec, cspec]
    return pl.pallas_call(
        functools.partial(_sample_mixer_kernel, layer=layer, creates=creates),
        grid=(nseq // bb,),
        in_specs=in_specs,
        out_specs=[row(LRU_W, 0), row(vw, 0), row(ML_W, 0), *state_specs],
        out_shape=[SDS((n, LRU_W), BF16), SDS((n, vw), BF16), SDS((n, ML_W), BF16),
                   SDS((DEPTH, n, LRU_W), F32), SDS((DEPTH, n, ML_W), F32), SDS((DEPTH, n, LANE), F32),
                   SDS((DEPTH, nseq, GLA_H, GLA_DK, GLA_DV), F32), SDS((DEPTH, nseq, ML_H, ML_DH, ML_DH), F32)],
        input_output_aliases=aliases,
        compiler_params=_params(("arbitrary",), vmem),
        name="sample_mixer",
    )(*args)


def _prep_weights(p):
    w_in = p["w_in"]
    c1 = C_GGATE
    c2 = C_MG + W_SHIFT1
    w_small = jnp.concatenate(
        [w_in[:, :, c1:c1 + GLA_RANK], w_in[:, :, c2:c2 + 2 * ML_H],
         jnp.zeros((DEPTH, D, LANE - GLA_RANK - 2 * ML_H), F32)], axis=-1)
    wg2 = jnp.concatenate(
        [p["gla_w_g2"], jnp.zeros((DEPTH, LANE - GLA_RANK, GLA_H * GLA_DK), F32)], axis=1)
    blk_of = jnp.arange(LANE) // ML_BS
    on_diag = blk_of[:, None] == blk_of[None, :]

    def dense_bd(wb):
        rows = wb.reshape(DEPTH, ML_W // LANE, LANE, ML_BS)
        dense = jnp.where(on_diag, jnp.tile(rows, (1, 1, 1, LANE // ML_BS)), 0.0)
        hi = dense.astype(BF16)
        return hi, (dense - hi.astype(F32)).astype(BF16)

    bif = jnp.zeros((DEPTH, 1, LANE), F32).at[:, 0, S_IG:S_IG + 2 * ML_H].set(p["ml_b_if"])
    r3 = lambda a: a.reshape(DEPTH, 1, a.shape[-1])
    return dict(
        w_small=w_small, wg2=wg2, bg=r3(p["gla_b_g"]), gla_gn=r3(p["gla_g_norm"]),
        g1=r3(p["g_norm1"]), g2=r3(p["g_norm2"]), gf=p["g_final"].reshape(1, D),
        lru_cw=p["lru_conv_w"], lru_cb=r3(p["lru_conv_b"]), lru_wa=p["lru_w_a"], lru_ba=r3(p["lru_b_a"]),
        lru_wx=p["lru_w_x"], lru_bx=r3(p["lru_b_x"]), lru_lam=r3(p["lru_lam"]),
        ml_cw=p["ml_conv_w"], ml_cb=r3(p["ml_conv_b"]), ml_wq=dense_bd(p["ml_w_q"]), ml_wk=dense_bd(p["ml_w_k"]),
        ml_wv=dense_bd(p["ml_w_v"]), ml_bif=bif, ml_gn=r3(p["ml_g_norm"]),
        w_in_t=jnp.swapaxes(w_in, 1, 2), w_br=(p["w_br_lru"], p["w_br_gla"], p["w_br_ml"]), w_out=p["w_out"], w_ff1=p["w_ff1"],
        w_ff2=p["w_ff2"],
    )


def _trunk(xp3, xs3, mod_p, mod_s, states, w):
    bp, tp, _ = xp3.shape
    bs, ts, _ = xs3.shape
    n_p, n_s = bp * tp, bs * ts
    tm_p = min(tp, 1024)
    tm_f = min(tp, 512)
    s_h, s_cv, s_gla, s_c, s_n, s_m, s_mcv = states
    pad_t = lambda a: jnp.pad(a, ((0, 0), (0, 0), (0, ts - a.shape[2]), (0, 0))).reshape(DEPTH, n_s, a.shape[-1])
    el = pad_t(s_cv)
    em = pad_t(s_mcv)
    h0e = pad_t(s_h[:, :, None, :])
    n0e = jnp.broadcast_to(s_n.reshape(DEPTH, bs, 1, ML_W), (DEPTH, bs, ts, ML_W)).reshape(DEPTH, n_s, ML_W)
    m_l = jnp.pad(s_m, ((0, 0), (0, 0), (S_FG, LANE - S_FG - ML_H)))
    m0e = jnp.broadcast_to(m_l[:, :, None, :], (DEPTH, bs, ts, LANE)).reshape(DEPTH, n_s, LANE)
    x_p = xp3.reshape(n_p, D)
    x_s = xs3.reshape(n_s, D)
    prev_p = None
    conv_l, conv_m = [], []
    keep = ts - (CONV_W - 1)
    for l in range(DEPTH):
        last = l == DEPTH - 1
        proj_s, small_s, w_main = _in_proj_cast(x_s, mod_s, l, w["g1"], w["w_in_t"], w["w_small"], tn=512, rpg=n_s)
        if l == 0:
            s_shape = (DEPTH, bs, GLA_H, GLA_DK, GLA_DV)
            c_shape = (DEPTH, bs, ML_H, ML_DH, ML_DH)
            flat_rows = lambda shape: shape[0] * shape[1] * shape[2] * shape[3] * shape[4] // GLA_DV
            proj_p, small_p, s_buf, c_buf = _in_proj(x_p, mod_p, l, w["g1"], w_main, w["w_small"], tm=tm_p, tn=1024,
                                                     rpg=tp, r=1, zero_rows=(flat_rows(s_shape), flat_rows(c_shape)))
            big_s = (s_buf.reshape(s_shape), c_buf.reshape(c_shape))
            rows_s = None
        else:
            proj_p, small_p = _in_proj(x_p, mod_p, l, w["g1"], w_main, w["w_small"], tm=tm_p, tn=1024, rpg=tp, r=1)
        outs_s = _sample_mixer(proj_s, small_s, el, em, h0e, n0e, m0e, s_gla, s_c, l, w, rows_s, big_s)
        rows_s, big_s = outs_s[3:6], outs_s[6:]
        outs_p = _prompt_mixer(proj_p.reshape(bp, tp, N_MAIN), small_p.reshape(bp, tp, LANE), l, w, prev_p,
                               L=min(tp, CHUNK), rows=min(tp, PROMPT_ROWS), nseq=PROMPT_SEQS)
        prev_p = outs_p[3:]
        conv_l.append(proj_s[:, C_LRU:C_LRU + LRU_W].reshape(bs, ts, LRU_W)[:, keep:])
        conv_m.append(proj_s[:, C_MX:C_MX + ML_W].reshape(bs, ts, ML_W)[:, keep:])
        merged_s, *w_br = _merge(*outs_s[:3], proj_s, l, *w["w_br"], tm=n_s, tn=512, cast=True)
        (merged_p,) = _merge(*(y.reshape(n_p, y.shape[-1]) for y in outs_p[:3]), proj_p, l, *w_br, tm=tm_p, tn=512,
                             cast=False)
        x_s, w_out = _out_proj(merged_s, x_s, mod_s, l, w["w_out"], tm=n_s, tn=512, rpg=n_s, r=n_s, cast=True)
        (x_p,) = _out_proj(merged_p, x_p, mod_p, l, w_out, tm=tm_p, tn=1024, rpg=tp, r=1, cast=False)
        x_s, w_ff1, w_ff2 = _ffn(x_s, mod_s, l, w["g2"], w["gf"], w["w_ff1"], w["w_ff2"], tm=n_s, tf=512, rpg=n_s,
                                 r=n_s, final_norm=last, cast=True)
        (x_p,) = _ffn(x_p, mod_p, l, w["g2"], w["gf"], w_ff1, w_ff2, tm=tm_f, tf=1024, rpg=tp, r=1, final_norm=last,
                      cast=False)
    h_t, lcv_t, s_t, c_t, n_t, m_t, mcv_t = prev_p
    p_states = [h_t.reshape(DEPTH, bp, LRU_W), lcv_t, s_t, c_t, n_t, m_t[..., 0], mcv_t]
    hrow, nrow, mrow = rows_s
    s_out, c_out = big_s
    t_last = ts - 1
    s_states = [hrow.reshape(DEPTH, bs, ts, LRU_W)[:, :, t_last], jnp.stack(conv_l), s_out, c_out,
                nrow.reshape(DEPTH, bs, ts, ML_H, ML_DH)[:, :, t_last],
                mrow.reshape(DEPTH, bs, ts, LANE)[:, :, t_last, S_FG:S_FG + ML_H], jnp.stack(conv_m)]
    return x_p.reshape(bp, tp, D), x_s.reshape(bs, ts, D), p_states, s_states


def kernel(x_prompt, x_sample, c_prompt, c_sample, state_lru_h, state_lru_conv, state_gla, state_mlstm_C, state_mlstm_n, state_mlstm_m, state_mlstm_conv, w_ada, b_ada, g_norm1, g_norm2, w_in, lru_conv_w, lru_conv_b, lru_w_a, lru_b_a, lru_w_x, lru_b_x, lru_lam, gla_w_g2, gla_b_g, gla_g_norm, ml_conv_w, ml_conv_b, ml_w_q, ml_w_k, ml_w_v, ml_b_if, ml_g_norm, w_br_lru, w_br_gla, w_br_ml, w_out, w_ff1, w_ff2, g_final):
    p = dict(g_norm1=g_norm1, g_norm2=g_norm2, w_in=w_in, lru_conv_w=lru_conv_w, lru_conv_b=lru_conv_b,
             lru_w_a=lru_w_a, lru_b_a=lru_b_a, lru_w_x=lru_w_x, lru_b_x=lru_b_x, lru_lam=lru_lam,
             gla_w_g2=gla_w_g2, gla_b_g=gla_b_g, gla_g_norm=gla_g_norm, ml_conv_w=ml_conv_w, ml_conv_b=ml_conv_b,
             ml_w_q=ml_w_q, ml_w_k=ml_w_k, ml_w_v=ml_w_v, ml_b_if=ml_b_if, ml_g_norm=ml_g_norm,
             w_br_lru=w_br_lru, w_br_gla=w_br_gla, w_br_ml=w_br_ml, w_out=w_out, w_ff1=w_ff1, w_ff2=w_ff2,
             g_final=g_final)
    w = _prep_weights(p)
    bp = x_prompt.shape[0]
    bs, ts, _ = x_sample.shape
    assert ts == T_STEP and bs % STEP_BB == 0
    ns = bs * ts
    pad = (-(ns + bp)) % SUBLANE
    c_all = jnp.concatenate([jnp.repeat(c_sample, ts, axis=0), c_prompt, jnp.zeros((pad, D), F32)], axis=0)
    mod_all = _ada(c_all, w_ada, b_ada)
    mod_s = mod_all.reshape(DEPTH, 1, ns + bp + pad, N_MOD * D)
    mod_p = mod_all[:, ns:ns + bp].reshape(DEPTH, bp, 1, N_MOD * D)
    y_p, y_s, ps, ss = _trunk(x_prompt, x_sample, mod_p, mod_s,
                              (state_lru_h, state_lru_conv, state_gla, state_mlstm_C, state_mlstm_n, state_mlstm_m,
                               state_mlstm_conv), w)
    return (y_p, y_s, *ps, *ss)
```

```python
import functools

import jax
import jax.numpy as jnp
from jax import lax
from jax.experimental import pallas as pl
from jax.experimental.pallas import tpu as pltpu

F32, BF16 = jnp.float32, jnp.bfloat16
SDS = jax.ShapeDtypeStruct

D = 2048
DEPTH = 2
LRU_W = 1024
LRU_BLOCKS = 8
LRU_C = 8.0
CONV_W = 4
GLA_H = 4
GLA_DK = 128
GLA_DV = 256
GLA_RANK = 16
GLA_TAU = 16.0
ML_H = 4
ML_W = 1024
ML_DH = 256
ML_BS = 4
CHUNK = 128
ML_CHUNK = 256
PROMPT_ROWS = 512
PROMPT_SEQS = 1
D_FF = 4 * D
EPS = 1e-6
N_MOD = 6
T_STEP = 4
STEP_BB = 4

LANE = 128
SUBLANE = 8
VMEM_LIMIT_CAP = 60 * 1024 * 1024

C_LRU, C_GQ, C_GK, C_GV, C_GGATE, C_MX, C_MO, C_MG = 0, 1024, 1536, 2048, 3072, 4096, 5120, 6144
N_MAIN = 12288
W_SHIFT1 = GLA_RANK
W_SHIFT2 = GLA_RANK + 2 * ML_H
S_GLR = 0
S_IG = 16
S_FG = 20
NEG_BIG = -1e30


def _params(sem, vmem_bytes):
    return pltpu.CompilerParams(dimension_semantics=sem, vmem_limit_bytes=int(min(vmem_bytes, VMEM_LIMIT_CAP)))


def _dot(a, b):
    return jnp.dot(a.astype(BF16), b.astype(BF16), preferred_element_type=F32)


def _dot_nt(a, b):
    return lax.dot_general(a.astype(BF16), b.astype(BF16), (((1,), (1,)), ((), ())), preferred_element_type=F32)


def _split3(x):
    p0 = x.astype(BF16)
    r = x - p0.astype(F32)
    p1 = r.astype(BF16)
    return p0, p1, (r - p1.astype(F32)).astype(BF16)


def _cumsum_rows(tri_b, x):
    return sum(jnp.dot(tri_b, p, preferred_element_type=F32) for p in _split3(x))


def _cumsum_lanes(x, triu_b):
    return sum(jnp.dot(p, triu_b, preferred_element_type=F32) for p in _split3(x))


def _split(x):
    hi = x.astype(BF16)
    return hi, (x - hi.astype(F32)).astype(BF16)


def _dot3(x_hi, x_lo, w_hi, w_lo):
    d = lambda a, b: jnp.dot(a, b, preferred_element_type=F32)
    return d(x_hi, w_hi) + (d(x_lo, w_hi) + d(x_hi, w_lo))


def _log_sigmoid(z):
    return jnp.minimum(z, 0.0) - jnp.log1p(jnp.exp(-jnp.abs(z)))


def _silu(z):
    return z * jax.nn.sigmoid(z)


def _rms(x, g):
    return x * lax.rsqrt(jnp.mean(x * x, axis=-1, keepdims=True) + EPS) * g


def _tri(n):
    r = lax.broadcasted_iota(jnp.int32, (n, n), 0)
    c = lax.broadcasted_iota(jnp.int32, (n, n), 1)
    return r >= c


def _lru_gates(ub, wa, wx, ba, bx, lam):
    r = jax.nn.sigmoid(_dot(ub, wa) + ba)
    i = jax.nn.sigmoid(_dot(ub, wx) + bx)
    log_a = LRU_C * r * _log_sigmoid(lam)
    t = jnp.tanh(log_a)
    return jnp.exp(log_a), jnp.sqrt(-2.0 * t / (1.0 - t)) * (i * ub)


def _blockdiag3(x, wh_ref, wl_ref):
    xh, xl = _split(x)
    outs = []
    for blk in range(ML_W // LANE):
        sl = slice(blk * LANE, (blk + 1) * LANE)
        outs.append(_dot3(xh[:, sl], xl[:, sl], wh_ref[blk], wl_ref[blk]))
    return jnp.concatenate(outs, axis=1)


def _ada_kernel(c_ref, w_ref, b_ref, o_ref):
    o_ref[...] = _dot(_silu(c_ref[...]), w_ref[...]) + b_ref[...]


def _ada(c_all, w_ada, b_ada):
    m = c_all.shape[0]
    tn = 1024
    n_out = N_MOD * D
    return pl.pallas_call(
        _ada_kernel,
        grid=(DEPTH, n_out // tn),
        in_specs=[
            pl.BlockSpec((m, D), lambda l, n: (0, 0)),
            pl.BlockSpec((None, D, tn), lambda l, n: (l, 0, n)),
            pl.BlockSpec((None, 1, tn), lambda l, n: (l, 0, n)),
        ],
        out_specs=pl.BlockSpec((None, m, tn), lambda l, n: (l, 0, n)),
        out_shape=SDS((DEPTH, m, n_out), F32),
        compiler_params=_params(("parallel", "arbitrary"), 2 * (m * D + D * tn + m * tn) * 4 + (12 << 20)),
        name="ada",
    )(c_all, w_ada, b_ada.reshape(DEPTH, 1, n_out))


def _modspec(layer, comp, r, width, tm, rpg, jmap=None):
    nb = D // width
    jm = (lambda j: 0) if jmap is None else jmap
    if r == 1:
        return pl.BlockSpec((None, None, 1, width), lambda i, j: (layer, (i * tm) // rpg, 0, comp * nb + jm(j)))
    mode = pl.Buffered(1) if jmap is None else None
    return pl.BlockSpec((None, None, tm, width), lambda i, j: (layer, 0, i, comp * nb + jm(j)), pipeline_mode=mode)


def _norm_mod_store(xn_s, x_ref, g_ref, sc_ref, sh_ref):
    tm = x_ref.shape[0]
    rc = min(tm, 256)
    for c in range(tm // rc):
        rows = slice(c * rc, (c + 1) * rc)
        mrows = rows if sc_ref.shape[0] == tm else slice(None)
        xn = _rms(x_ref[rows, :], g_ref[...]) * (1.0 + sc_ref[mrows, :]) + sh_ref[mrows, :]
        xn_s[rows, :] = xn.astype(BF16)


def _in_proj_kernel(x_ref, sc_ref, sh_ref, g_ref, wm_ref, ws_ref, om_ref, os_ref, *rest, fill_steps):
    xn_s = rest[-1]

    @pl.when(pl.program_id(1) == 0)
    def _():
        _norm_mod_store(xn_s, x_ref, g_ref, sc_ref, sh_ref)
        os_ref[...] = jnp.dot(xn_s[...], ws_ref[...].astype(BF16), preferred_element_type=F32)

    om_ref[...] = jnp.dot(xn_s[...], wm_ref[...], preferred_element_type=F32)

    if fill_steps:
        @pl.when(pl.program_id(1) < fill_steps)
        def _():
            for z_ref in rest[:-1]:
                z_ref[...] = jnp.zeros_like(z_ref)


def _in_proj(x, mod, layer, g1, w_main, w_small, *, tm, tn, rpg, r, zero_rows=()):
    n = x.shape[0]
    vmem = 2 * (tm * D * 4 + 2 * r * D * 4 + D * tn * 2 + D * LANE * 2 + tm * tn * 4 + tm * LANE * 4) + tm * D * 2
    vmem += 8 << 20
    n_i, n_j = n // tm, N_MAIN // tn
    fill_steps = 0
    zero_specs, zero_shapes = [], []
    if zero_rows:
        fill_steps = max(s for s in range(1, n_j + 1)
                         if all(zr % (n_i * s * SUBLANE) == 0 for zr in zero_rows))
        for zr in zero_rows:
            blk = zr // (n_i * fill_steps)
            zero_specs.append(pl.BlockSpec((blk, GLA_DV),
                                           lambda i, j: (i * fill_steps + jnp.minimum(j, fill_steps - 1), 0)))
            zero_shapes.append(SDS((zr, GLA_DV), F32))
            vmem += 2 * blk * GLA_DV * 4
    outs = pl.pallas_call(
        functools.partial(_in_proj_kernel, fill_steps=fill_steps),
        grid=(n_i, n_j),
        in_specs=[
            pl.BlockSpec((tm, D), lambda i, j: (i, 0)),
            _modspec(layer, 1, r, D, tm, rpg),
            _modspec(layer, 0, r, D, tm, rpg),
            pl.BlockSpec((None, 1, D), lambda i, j: (layer, 0, 0)),
            pl.BlockSpec((D, tn), lambda i, j: (0, j)),
            pl.BlockSpec((None, D, LANE), lambda i, j: (layer, 0, 0)),
        ],
        out_specs=[pl.BlockSpec((tm, tn), lambda i, j: (i, j)), pl.BlockSpec((tm, LANE), lambda i, j: (i, 0)),
                   *zero_specs],
        out_shape=[SDS((n, N_MAIN), F32), SDS((n, LANE), F32), *zero_shapes],
        scratch_shapes=[pltpu.VMEM((tm, D), BF16)],
        compiler_params=_params(("parallel", "arbitrary"), vmem),
        name="in_proj",
    )(x, mod, mod, g1, w_main, w_small)
    return outs


def _in_proj_cast_kernel(x_ref, sc_ref, sh_ref, g_ref, wt_ref, ws_ref, om_ref, os_ref, wc_ref, xn_s):
    @pl.when(pl.program_id(1) == 0)
    def _():
        _norm_mod_store(xn_s, x_ref, g_ref, sc_ref, sh_ref)
        os_ref[...] = jnp.dot(xn_s[...], ws_ref[...].astype(BF16), preferred_element_type=F32)

    wb = wt_ref[...].T.astype(BF16)
    wc_ref[...] = wb
    om_ref[...] = jnp.dot(xn_s[...], wb, preferred_element_type=F32)


def _in_proj_cast(x, mod, layer, g1, w_in_t, w_small, *, tn, rpg):
    n = x.shape[0]
    tm = n
    nb1, nb2 = C_GGATE // tn, C_MG // tn

    def w_rows(i, j):
        shift = jnp.where(j >= nb2, W_SHIFT2 // SUBLANE, jnp.where(j >= nb1, W_SHIFT1 // SUBLANE, 0))
        return (layer, (j * (tn // SUBLANE) + shift) * SUBLANE, 0)

    vmem = tm * D * 4 + 2 * tm * D * 4 + 2 * (tn * D * 4 + D * LANE * 4 + tm * tn * 4 + tm * LANE * 4 + D * tn * 2)
    vmem += tm * D * 2 + 3 * D * tn * 4 + (4 << 20)
    return pl.pallas_call(
        _in_proj_cast_kernel,
        grid=(1, N_MAIN // tn),
        in_specs=[
            pl.BlockSpec((tm, D), lambda i, j: (i, 0), pipeline_mode=pl.Buffered(1)),
            _modspec(layer, 1, tm, D, tm, rpg),
            _modspec(layer, 0, tm, D, tm, rpg),
            pl.BlockSpec((None, 1, D), lambda i, j: (layer, 0, 0)),
            pl.BlockSpec((None, pl.Element(tn), pl.Element(D)), w_rows),
            pl.BlockSpec((None, D, LANE), lambda i, j: (layer, 0, 0)),
        ],
        out_specs=[pl.BlockSpec((tm, tn), lambda i, j: (i, j)), pl.BlockSpec((tm, LANE), lambda i, j: (i, 0)),
                   pl.BlockSpec((D, tn), lambda i, j: (0, j))],
        out_shape=[SDS((n, N_MAIN), F32), SDS((n, LANE), F32), SDS((D, N_MAIN), BF16)],
        scratch_shapes=[pltpu.VMEM((tm, D), BF16)],
        compiler_params=_params(("arbitrary", "arbitrary"), vmem),
        name="in_proj_cast",
    )(x, mod, mod, g1, w_in_t, w_small)


def _wspec(shape, imap, layer, cast):
    if cast:
        return pl.BlockSpec((None,) + shape, lambda i, j: (layer,) + imap(i, j))
    return pl.BlockSpec(shape, imap)


def _merge_kernel(yl_ref, yg_ref, ym_ref, g0_ref, g1_ref, g2_ref, w0_ref, w1_ref, w2_ref, o_ref, *wc_refs):
    ws = [w_ref[...].astype(BF16) for w_ref in (w0_ref, w1_ref, w2_ref)]
    for wc_ref, wb in zip(wc_refs, ws):
        wc_ref[...] = wb
    acc = jax.nn.sigmoid(g0_ref[...]) * jnp.dot(yl_ref[...], ws[0], preferred_element_type=F32)
    acc += jax.nn.sigmoid(g1_ref[...]) * jnp.dot(yg_ref[...], ws[1], preferred_element_type=F32)
    acc += jax.nn.sigmoid(g2_ref[...]) * jnp.dot(ym_ref[...], ws[2], preferred_element_type=F32)
    o_ref[...] = acc.astype(BF16)


def _merge(y_lru, y_gla, y_ml, proj, layer, w_lru, w_gla, w_ml, *, tm, tn, cast):
    n = y_lru.shape[0]
    w = LRU_W
    gb = C_MG // tn
    nb = D // tn
    yspec = pl.BlockSpec((tm, w), lambda i, j: (i, 0))
    wspec = _wspec((w, tn), lambda i, j: (0, j), layer, cast)
    wbytes = 4 if cast else 2
    vmem = 2 * (3 * tm * w * 2 + 3 * tm * tn * 4 + 3 * w * tn * wbytes + tm * tn * 2) + 6 * tm * tn * 4 + (4 << 20)
    out_specs = [pl.BlockSpec((tm, tn), lambda i, j: (i, j))]
    out_shape = [SDS((n, D), BF16)]
    if cast:
        out_specs += [pl.BlockSpec((w, tn), lambda i, j: (0, j))] * 3
        out_shape += [SDS((w, D), BF16)] * 3
        vmem += 2 * 3 * w * tn * 2 + 3 * w * tn * 4
    return pl.pallas_call(
        _merge_kernel,
        grid=(n // tm, nb),
        in_specs=[
            yspec, yspec, yspec,
            pl.BlockSpec((tm, tn), lambda i, j: (i, gb + j)),
            pl.BlockSpec((tm, tn), lambda i, j: (i, gb + nb + j)),
            pl.BlockSpec((tm, tn), lambda i, j: (i, gb + 2 * nb + j)),
            wspec, wspec, wspec,
        ],
        out_specs=out_specs,
        out_shape=out_shape,
        compiler_params=_params(("arbitrary" if cast else "parallel", "arbitrary"), vmem),
        name="merge",
    )(y_lru, y_gla, y_ml, proj, proj, proj, w_lru, w_gla, w_ml)


def _out_proj_kernel(m_ref, w_ref, x_ref, gt_ref, o_ref, *wc_refs):
    wb = w_ref[...].astype(BF16)
    for wc_ref in wc_refs:
        wc_ref[...] = wb
    o_ref[...] = x_ref[...] + gt_ref[...] * jnp.dot(m_ref[...], wb, preferred_element_type=F32)


def _out_proj(merged, x, mod, layer, w_out, *, tm, tn, rpg, r, cast):
    n = x.shape[0]
    wbytes = 4 if cast else 2
    vmem = 2 * (tm * D * 2 + D * tn * wbytes + 2 * tm * tn * 4 + r * tn * 4) + 2 * tm * tn * 4 + (4 << 20)
    out_specs = [pl.BlockSpec((tm, tn), lambda i, j: (i, j))]
    out_shape = [SDS((n, D), F32)]
    if cast:
        out_specs.append(pl.BlockSpec((D, tn), lambda i, j: (0, j)))
        out_shape.append(SDS((D, D), BF16))
        vmem += 2 * D * tn * 2 + D * tn * 4
    return pl.pallas_call(
        _out_proj_kernel,
        grid=(n // tm, D // tn),
        in_specs=[
            pl.BlockSpec((tm, D), lambda i, j: (i, 0)),
            _wspec((D, tn), lambda i, j: (0, j), layer, cast),
            pl.BlockSpec((tm, tn), lambda i, j: (i, j)),
            _modspec(layer, 2, r, tn, tm, rpg, jmap=lambda j: j),
        ],
        out_specs=out_specs,
        out_shape=out_shape,
        compiler_params=_params(("arbitrary" if cast else "parallel", "arbitrary"), vmem),
        name="out_proj",
    )(merged, w_out, x, mod)


def _ffn_kernel(x_ref, sc_ref, sh_ref, gt_ref, g_ref, gf_ref, w1_ref, w2_ref, o_ref, *rest, final_norm):
    xn_s = rest[-1]
    f = pl.program_id(1)

    @pl.when(f == 0)
    def _():
        _norm_mod_store(xn_s, x_ref, g_ref, sc_ref, sh_ref)
        o_ref[...] = jnp.zeros_like(o_ref)

    w1 = w1_ref[...].astype(BF16)
    w2 = w2_ref[...].astype(BF16)
    for wc_ref, wb in zip(rest[:-1], (w1, w2)):
        wc_ref[...] = wb
    h = jnp.square(jnp.maximum(jnp.dot(xn_s[...], w1, preferred_element_type=F32), 0.0))
    o_ref[...] += jnp.dot(h.astype(BF16), w2, preferred_element_type=F32)

    @pl.when(f == pl.num_programs(1) - 1)
    def _():
        y = x_ref[...] + gt_ref[...] * o_ref[...]
        if final_norm:
            y = _rms(y, gf_ref[...])
        o_ref[...] = y


def _ffn(x, mod, layer, g2, g_final, w1, w2, *, tm, tf, rpg, r, final_norm, cast):
    n = x.shape[0]
    wbytes = 4 if cast else 2
    vmem = 2 * (2 * tm * D * 4 + 3 * r * D * 4 + 2 * D * tf * wbytes) + tm * D * 2 + 2 * tm * tf * 4 + 2 * tm * D * 4
    single = pl.Buffered(1) if n == tm else None
    out_specs = [pl.BlockSpec((tm, D), lambda i, f: (i, 0), pipeline_mode=single)]
    out_shape = [SDS((n, D), F32)]
    if cast:
        out_specs += [pl.BlockSpec((D, tf), lambda i, f: (0, f)), pl.BlockSpec((tf, D), lambda i, f: (f, 0))]
        out_shape += [SDS((D, D_FF), BF16), SDS((D_FF, D), BF16)]
        vmem += 2 * 2 * D * tf * 2
    return pl.pallas_call(
        functools.partial(_ffn_kernel, final_norm=final_norm),
        grid=(n // tm, D_FF // tf),
        in_specs=[
            pl.BlockSpec((tm, D), lambda i, f: (i, 0), pipeline_mode=single),
            _modspec(layer, 4, r, D, tm, rpg),
            _modspec(layer, 3, r, D, tm, rpg),
            _modspec(layer, 5, r, D, tm, rpg),
            pl.BlockSpec((None, 1, D), lambda i, f: (layer, 0, 0)),
            pl.BlockSpec((1, D), lambda i, f: (0, 0)),
            _wspec((D, tf), lambda i, f: (0, f), layer, cast),
            _wspec((tf, D), lambda i, f: (f, 0), layer, cast),
        ],
        out_specs=out_specs,
        out_shape=out_shape,
        scratch_shapes=[pltpu.VMEM((tm, D), BF16)],
        compiler_params=_params(("arbitrary" if cast else "parallel", "arbitrary"), vmem),
        name="ffn",
    )(x, mod, mod, mod, g2, g_final, w1, w2)


def _conv_chunk(carry_s, x, w, b, L):
    t8 = lax.broadcasted_iota(jnp.int32, (SUBLANE, x.shape[1]), 0)
    carry = carry_s[...]
    acc = b + x * w[CONV_W - 1:CONV_W]
    for k in range(1, CONV_W):
        down = pltpu.roll(x, k, 0)
        head = jnp.where(t8 < k, pltpu.roll(carry, k, 0), down[0:SUBLANE, :])
        acc = acc + jnp.concatenate([head, down[SUBLANE:, :]], axis=0) * w[CONV_W - 1 - k:CONV_W - k]
    carry_s[...] = x[L - SUBLANE:L, :]
    return acc, x[L - (CONV_W - 1):L, :]


N_TOK_REFS, N_W_REFS, N_Y_REFS, N_STATE_REFS, N_SCRATCH = 8, 20, 3, 7, 9


def _prompt_mixer_kernel(*refs, L, layer, creates):
    tok = refs[:N_TOK_REFS]
    wts = refs[N_TOK_REFS:N_TOK_REFS + N_W_REFS]
    scr = refs[-N_SCRATCH:]
    outs = refs[-(N_SCRATCH + N_Y_REFS + N_STATE_REFS):-N_SCRATCH]
    ys, state_outs = outs[:N_Y_REFS], outs[N_Y_REFS:]
    nseq = tok[0].shape[0]

    @pl.when(pl.program_id(1) == 0)
    def _():
        xpl_s, xpm_s = scr[0], scr[1]
        xpl_s[:, 0:SUBLANE, :] = jnp.zeros((nseq, SUBLANE, LRU_W), F32)
        xpm_s[:, 0:SUBLANE, :] = jnp.zeros((nseq, SUBLANE, ML_W), F32)
        for s_ref in scr[4:]:
            s_ref[...] = jnp.zeros_like(s_ref)

    finals = [_prompt_mixer_rows([r.at[bi] for r in tok], wts, [y.at[bi] for y in ys], [s.at[bi] for s in scr], L)
              for bi in range(nseq)]

    @pl.when(pl.program_id(1) == pl.num_programs(1) - 1)
    def _():
        souts = state_outs
        if creates:
            for ref in souts:
                for other in range(DEPTH):
                    if other != layer:
                        ref[other] = jnp.zeros(ref.shape[1:], F32)
            souts = [ref.at[layer] for ref in souts]
        ht_ref, lct_ref, st_ref, ct_ref, nt_ref, mt_ref, mct_ref = souts
        s_s, c_s, n_s, m_s = scr[5:]
        for bi, (h, l_tail, m_tail) in enumerate(finals):
            ht_ref[bi] = h
            lct_ref[bi] = l_tail
            mct_ref[bi] = m_tail
        st_ref[...] = s_s[...]
        ct_ref[...] = c_s[...]
        nt_ref[...] = n_s[...]
        mt_ref[...] = m_s[...]


def _prompt_mixer_rows(tok, wts, ys, scr, L):
    lx_ref, q_ref, k_ref, v_ref, gate_ref, mx_ref, mo_ref, small_ref = tok
    (lcw_ref, lcb_ref, wa_ref, ba_ref, wx_ref, bx_ref, lam_ref, wg2_ref, bg_ref, ggn_ref,
     mcw_ref, mcb_ref, wqh_ref, wql_ref, wkh_ref, wkl_ref, wvh_ref, wvl_ref, bif_ref, mgn_ref) = wts
    yl_ref, yg_ref, ym_ref = ys
    xpl_s, xpm_s, a_s, u_s, h_s, s_s, c_s, n_s, m_s = scr
    rows = lx_ref.shape[0]

    q, k, v, gate = q_ref[...], k_ref[...], v_ref[...], gate_ref[...]
    small = small_ref[...]
    logdec = _log_sigmoid(_dot(small, wg2_ref[...]) + bg_ref[...]) * (1.0 / GLA_TAU)
    mx = mx_ref[...]
    conv, m_tail = _conv_chunk(xpm_s, mx, mcw_ref[...], mcb_ref[...], rows)
    mc = _silu(conv)
    mq = _blockdiag3(mc, wqh_ref, wql_ref)
    mk = _blockdiag3(mc, wkh_ref, wkl_ref) * (ML_DH ** -0.5)
    mv = _blockdiag3(mx, wvh_ref, wvl_ref)

    after_proj = jnp.where(mq[0:1, :] != mq[0:1, :], 1.0, 0.0)
    u, l_tail = _conv_chunk(xpl_s, lx_ref[...], lcw_ref[...], lcb_ref[...] + after_proj, rows)
    for blk in range(LRU_BLOCKS):
        sl = slice(blk * LANE, (blk + 1) * LANE)
        a_s[:, sl], u_s[:, sl] = _lru_gates(u[:, sl], wa_ref[blk], wx_ref[blk], ba_ref[:, sl], bx_ref[:, sl],
                                            lam_ref[:, sl])

    gates = small + bif_ref[...]
    gates_t = gates.T
    lf = _log_sigmoid(gates)
    lf_t = _log_sigmoid(gates_t)
    mo = mo_ref[...]
    tri = _tri(L)
    tri_b = tri.astype(BF16)

    for cc in range(rows // L):
        rs = slice(cc * L, (cc + 1) * L)
        bcum = _cumsum_rows(tri_b, logdec[rs, :])
        for hh in range(GLA_H):
            ks = slice(hh * GLA_DK, (hh + 1) * GLA_DK)
            vs = slice(hh * GLA_DV, (hh + 1) * GLA_DV)
            bh = bcum[:, ks]
            k_h, v_h = k[rs, ks], v[rs, vs]
            b_mid = bh[L // 2 - 1:L // 2, :]
            qe = (q[rs, ks] * (GLA_DK ** -0.5)) * jnp.exp(bh - b_mid)
            ke = k_h * jnp.exp(b_mid - bh)
            a = jnp.where(tri, _dot_nt(qe, ke), 0.0)
            s_h = s_s[hh]
            bl = bh[L - 1:L, :]
            cols = jnp.exp(jnp.concatenate([jnp.broadcast_to(b_mid, (SUBLANE // 2, GLA_DK)),
                                            jnp.broadcast_to(bl, (SUBLANE // 2, GLA_DK))], axis=0)).T
            mid_col, dec_col = cols[:, 0:1], cols[:, SUBLANE // 2:SUBLANE // 2 + 1]
            o = _dot(qe, mid_col * s_h) + _dot(a, v_h)
            kd = k_h * jnp.exp(bl - bh)
            s_s[hh] = dec_col * s_h + _dot(kd.T, v_h)
            yg_ref[rs, vs] = (_rms(o, ggn_ref[...]) * _silu(gate[rs, vs])).astype(BF16)

    L = min(rows, ML_CHUNK)
    tri = _tri(L)
    tri_b = tri.astype(BF16)
    r_i = lax.broadcasted_iota(jnp.int32, (L, L), 0)
    c_i = lax.broadcasted_iota(jnp.int32, (L, L), 1)
    triu_b = (r_i <= c_i).astype(BF16)
    for cc in range(rows // L):
        rs = slice(cc * L, (cc + 1) * L)
        f_col = _cumsum_rows(tri_b, lf[rs, :])
        f_row = _cumsum_lanes(lf_t[:, rs], triu_b)
        for hh in range(ML_H):
            sl = slice(hh * ML_DH, (hh + 1) * ML_DH)
            qh, kh, vh = mq[rs, sl], mk[rs, sl], mv[rs, sl]
            fc = f_col[:, S_FG + hh:S_FG + hh + 1]
            fr = f_row[S_FG + hh:S_FG + hh + 1, :]
            igr = gates_t[S_IG + hh:S_IG + hh + 1, rs]
            m_h = m_s[hh:hh + 1, 0:1]
            dm = jnp.where(tri, fc - fr + igr, NEG_BIG)
            inter = fc + m_h
            mt = jnp.maximum(inter, jnp.max(dm, axis=-1, keepdims=True))
            ci = jnp.exp(inter - mt)
            s = _dot_nt(qh, kh) * jnp.exp(dm - mt)
            c_h = c_s[hh]
            n_h = n_s[hh:hh + 1, :]
            num = ci * _dot(qh, c_h) + _dot(s, vh)
            den = ci * jnp.sum(qh * n_h, axis=-1, keepdims=True) + jnp.sum(s, axis=-1, keepdims=True)
            hcell = num / jnp.maximum(jnp.abs(den), jnp.exp(-mt))
            fl = fr[:, L - 1:L]
            dj = fl - fr + igr
            m_new = jnp.maximum(fl + m_h, jnp.max(dj, axis=-1, keepdims=True))
            cs = jnp.exp(fl + m_h - m_new)
            wj = jnp.exp(dj - m_new)
            c_s[hh] = cs * c_h + _dot(kh.T * wj, vh)
            n_s[hh:hh + 1, :] = cs * n_h + _dot(jnp.broadcast_to(wj, (SUBLANE, L)), kh)[0:1, :]
            m_s[hh:hh + 1, :] = jnp.broadcast_to(m_new, (1, LANE))
            ym_ref[rs, sl] = (jax.nn.sigmoid(mo[rs, sl]) * _rms(hcell, mgn_ref[...])).astype(BF16)

    t8 = lax.broadcasted_iota(jnp.int32, (SUBLANE, LRU_W), 0)
    h = h_s[...]
    for g in range(rows // SUBLANE):
        r8 = slice(g * SUBLANE, (g + 1) * SUBLANE)
        a8, u8 = a_s[r8, :], u_s[r8, :]
        for s in (1, 2, 4):
            u8 = jnp.where(t8 >= s, u8 + a8 * pltpu.roll(u8, s, 0), u8)
            a8 = jnp.where(t8 >= s, a8 * pltpu.roll(a8, s, 0), a8)
        h8 = u8 + a8 * h
        a_s[r8, :] = h8
        h = h8[SUBLANE - 1:SUBLANE, :]
    h_s[...] = h
    yl_ref[...] = a_s[...].astype(BF16)
    return h, l_tail, m_tail


def _prompt_mixer(proj3, small3, layer, w, prev, *, L, rows, nseq):
    b, t, _ = proj3.shape
    kw, vw = GLA_H * GLA_DK, GLA_H * GLA_DV
    nblk = ML_W // LANE
    blk3 = lambda width, cb: pl.BlockSpec((nseq, rows, width), lambda bi, c: (bi, c, cb))
    lw3 = lambda s: pl.BlockSpec((None,) + s, lambda bi, c: (layer, 0, 0))
    lw4 = lambda s: pl.BlockSpec((None,) + s, lambda bi, c: (layer, 0, 0, 0))
    creates = prev is None
    lead, at = ((DEPTH,), 0) if creates else ((None,), layer)
    st3 = lambda s: pl.BlockSpec(lead + (nseq,) + s, lambda bi, c: (at, bi, 0, 0))
    st4 = lambda s: pl.BlockSpec(lead + (nseq,) + s, lambda bi, c: (at, bi, 0, 0, 0))
    in_specs = [
        blk3(LRU_W, C_LRU // LRU_W), blk3(kw, C_GQ // kw), blk3(kw, C_GK // kw), blk3(vw, C_GV // vw),
        blk3(vw, C_GGATE // vw), blk3(ML_W, C_MX // ML_W), blk3(ML_W, C_MO // ML_W), blk3(LANE, 0),
        lw3((CONV_W, LRU_W)), lw3((1, LRU_W)), lw4((LRU_BLOCKS, LANE, LANE)), lw3((1, LRU_W)),
        lw4((LRU_BLOCKS, LANE, LANE)), lw3((1, LRU_W)), lw3((1, LRU_W)),
        lw3((LANE, kw)), lw3((1, kw)), lw3((1, GLA_DV)),
        lw3((CONV_W, ML_W)), lw3((1, ML_W)), *([lw4((nblk, LANE, LANE))] * 6), lw3((1, LANE)), lw3((1, ML_DH)),
    ]
    args = [proj3] * 7 + [small3,
                          w["lru_cw"], w["lru_cb"], w["lru_wa"], w["lru_ba"], w["lru_wx"], w["lru_bx"], w["lru_lam"],
                          w["wg2"], w["bg"], w["gla_gn"],
                          w["ml_cw"], w["ml_cb"], *w["ml_wq"], *w["ml_wk"], *w["ml_wv"], w["ml_bif"], w["ml_gn"]]
    aliases = {}
    if prev is not None:
        aliases = {len(in_specs) + j: 3 + j for j in range(len(prev))}
        in_specs += [pl.BlockSpec(memory_space=pl.ANY)] * len(prev)
        args += list(prev)
    sbytes = GLA_H * GLA_DK * GLA_DV * 4
    cbytes = ML_H * ML_DH * ML_DH * 4
    vmem = nseq * (2 * rows * (7 * 1024 * 4 + 3 * 1024 * 2) + 3 * (sbytes + cbytes) + 6 * rows * 1024 * 4)
    vmem += 30 * rows * 1024 * 4 + (12 << 20)
    sc = lambda *s: pltpu.VMEM((nseq,) + s, F32)
    return pl.pallas_call(
        functools.partial(_prompt_mixer_kernel, L=L, layer=layer, creates=creates),
        grid=(b // nseq, t // rows),
        in_specs=in_specs,
        out_specs=[blk3(LRU_W, 0), blk3(vw, 0), blk3(ML_W, 0),
                   st3((1, LRU_W)), st3((CONV_W - 1, LRU_W)), st4((GLA_H, GLA_DK, GLA_DV)),
                   st4((ML_H, ML_DH, ML_DH)), st3((ML_H, ML_DH)), st3((ML_H, LANE)), st3((CONV_W - 1, ML_W))],
        out_shape=[SDS((b, t, LRU_W), BF16), SDS((b, t, vw), BF16), SDS((b, t, ML_W), BF16),
                   SDS((DEPTH, b, 1, LRU_W), F32), SDS((DEPTH, b, CONV_W - 1, LRU_W), F32),
                   SDS((DEPTH, b, GLA_H, GLA_DK, GLA_DV), F32), SDS((DEPTH, b, ML_H, ML_DH, ML_DH), F32),
                   SDS((DEPTH, b, ML_H, ML_DH), F32), SDS((DEPTH, b, ML_H, LANE), F32),
                   SDS((DEPTH, b, CONV_W - 1, ML_W), F32)],
        scratch_shapes=[sc(SUBLANE, LRU_W), sc(SUBLANE, ML_W), sc(rows, LRU_W), sc(rows, LRU_W),
                        sc(1, LRU_W), sc(GLA_H, GLA_DK, GLA_DV), sc(ML_H, ML_DH, ML_DH), sc(ML_H, ML_DH),
                        sc(ML_H, LANE)],
        input_output_aliases=aliases,
        compiler_params=_params(("parallel", "arbitrary"), vmem),
        name="prompt_mixer",
    )(*args)


def _tmask(shape):
    return lax.broadcasted_iota(jnp.int32, shape, 0) % T_STEP


def _down(x, s):
    return x if s == 0 else pltpu.roll(x, s, 0)


def _up(x, s):
    return x if s == 0 else pltpu.roll(x, x.shape[0] - s, 0)


def _seg_cumsum(x):
    t = _tmask(x.shape)
    out = x
    for s in range(1, T_STEP):
        out = out + jnp.where(t >= s, _down(x, s), 0.0)
    return out


def _seg_last(x):
    t = _tmask(x.shape)
    out = x
    for s in range(1, T_STEP):
        out = jnp.where(t == T_STEP - 1 - s, _up(x, s), out)
    return out


def _seg_allreduce(x, op):
    t = _tmask(x.shape)
    y = op(x, jnp.where(t % 2 == 1, _down(x, 1), _up(x, 1)))
    return op(y, jnp.where(t >= 2, _down(y, 2), _up(y, 2)))


def _conv_rows(x, e, w, b):
    t = _tmask(x.shape)
    acc = b + x * w[CONV_W - 1:CONV_W]
    for k in range(1, CONV_W):
        hist = jnp.where(t >= k, _down(x, k), _up(e, CONV_W - 1 - k))
        acc = acc + hist * w[CONV_W - 1 - k:CONV_W - k]
    return acc


def _col(x, lane):
    li = lax.broadcasted_iota(jnp.int32, x.shape, 1)
    return jnp.sum(jnp.where(li == lane, x, 0.0), axis=-1, keepdims=True)


def _sample_mixer_kernel(lx_ref, q_ref, k_ref, v_ref, gate_ref, mx_ref, mo_ref, small_ref,
                         el_ref, em_ref, h0_ref, n0_ref, m0_ref, s0_ref, c0_ref,
                         lcw_ref, lcb_ref, wa_ref, ba_ref, wx_ref, bx_ref, lam_ref,
                         wg2_ref, bg_ref, ggn_ref,
                         mcw_ref, mcb_ref, wqh_ref, wql_ref, wkh_ref, wkl_ref, wvh_ref, wvl_ref, bif_ref, mgn_ref,
                         *rest, layer, creates):
    yl_ref, yg_ref, ym_ref, *state_refs = rest[-8:]
    row_refs, (st_ref, ct_ref) = state_refs[:3], state_refs[3:]
    if creates:
        for ref in row_refs:
            for other in range(DEPTH):
                if other != layer:
                    ref[other] = jnp.zeros(ref.shape[1:], F32)
        row_refs = [ref.at[layer] for ref in row_refs]
    hrow_ref, nrow_ref, mrow_ref = row_refs
    rows = lx_ref.shape[0]
    n_pairs = rows // SUBLANE
    seq_per_tile = SUBLANE // T_STEP
    t_col = _tmask((rows, 1))
    row8 = lax.broadcasted_iota(jnp.int32, (SUBLANE, 1), 0)

    part = pl.program_id(0) % (SUBLANE // (rows // T_STEP))
    r_i = lax.broadcasted_iota(jnp.int32, (rows, SUBLANE), 0)
    c_i = lax.broadcasted_iota(jnp.int32, (rows, SUBLANE), 1)
    mine = c_i == part * (rows // T_STEP) + r_i // T_STEP
    p_seq = jnp.where(mine, 1.0, 0.0).astype(BF16)
    p_time = [jnp.where(jnp.logical_and(mine, r_i % T_STEP == t), 1.0, 0.0).astype(BF16) for t in range(CONV_W - 1)]

    def spread(p, x):
        return sum(jnp.dot(p, piece, preferred_element_type=F32) for piece in _split3(x))

    el = sum(spread(p_time[t], el_ref[t]) for t in range(CONV_W - 1))
    u = _conv_rows(lx_ref[...], el, lcw_ref[...], lcb_ref[...])
    a_blocks, g_blocks = [], []
    for blk in range(LRU_BLOCKS):
        sl = slice(blk * LANE, (blk + 1) * LANE)
        a_b, g_b = _lru_gates(u[:, sl], wa_ref[blk], wx_ref[blk], ba_ref[:, sl], bx_ref[:, sl], lam_ref[:, sl])
        a_blocks.append(a_b)
        g_blocks.append(g_b)
    a = jnp.concatenate(a_blocks, axis=1)
    t_w = _tmask(a.shape)
    g = jnp.concatenate(g_blocks, axis=1) + a * spread(p_time[0], h0_ref[...])
    a1 = jnp.where(t_w >= 1, a * _down(a, 1), a)
    g1 = jnp.where(t_w >= 1, g + a * _down(g, 1), g)
    h = jnp.where(t_w >= 2, g1 + a1 * _down(g1, 2), g1)
    yl_ref[...] = h.astype(BF16)
    hrow_ref[...] = h

    q, k, v = q_ref[...], k_ref[...], v_ref[...]
    logdec = _log_sigmoid(_dot(small_ref[...], wg2_ref[...]) + bg_ref[...]) * (1.0 / GLA_TAU)
    bc = _seg_cumsum(logdec)
    bl = _seg_last(bc)
    qs = q * (GLA_DK ** -0.5)
    o_heads = [jnp.zeros((rows, GLA_DV), F32) for _ in range(GLA_H)]
    for s in range(T_STEP):
        prod = qs * _down(k, s) * jnp.exp(bc - _down(bc, s))
        v_s = _down(v, s)
        for hh in range(GLA_H):
            a_sh = jnp.sum(prod[:, hh * GLA_DK:(hh + 1) * GLA_DK], axis=-1, keepdims=True)
            a_sh = jnp.where(t_col >= s, a_sh, 0.0)
            o_heads[hh] = o_heads[hh] + a_sh * v_s[:, hh * GLA_DV:(hh + 1) * GLA_DV]
    qe = qs * jnp.exp(bc)
    kd = k * jnp.exp(bl - bc)
    dec = jnp.exp(bl)
    gate = gate_ref[...]
    for hh in range(GLA_H):
        ks = slice(hh * GLA_DK, (hh + 1) * GLA_DK)
        vs = slice(hh * GLA_DV, (hh + 1) * GLA_DV)
        o_state = []
        for p in range(n_pairs):
            r8 = slice(p * SUBLANE, (p + 1) * SUBLANE)
            kd_t = kd[r8, ks].T
            dec_t = dec[r8, ks].T
            res = None
            for j in range(seq_per_tile):
                b = p * seq_per_tile + j
                s_b = s0_ref[b, hh]
                r_j = _dot(qe[r8, ks], s_b)
                res = r_j if res is None else jnp.where(row8 // T_STEP == j, r_j, res)
                v_j = jnp.where(row8 // T_STEP == j, v[r8, vs], 0.0)
                last = j * T_STEP + T_STEP - 1
                st_ref[b, hh] = dec_t[:, last:last + 1] * s_b + _dot(kd_t, v_j)
            o_state.append(res)
        o = o_heads[hh] + jnp.concatenate(o_state, axis=0)
        yg_ref[:, vs] = (_rms(o, ggn_ref[...]) * _silu(gate[:, vs])).astype(BF16)

    mx = mx_ref[...]
    em = sum(spread(p_time[t], em_ref[t]) for t in range(CONV_W - 1))
    mc = _silu(_conv_rows(mx, em, mcw_ref[...], mcb_ref[...]))
    mq = _blockdiag3(mc, wqh_ref, wql_ref)
    mk = _blockdiag3(mc, wkh_ref, wkl_ref) * (ML_DH ** -0.5)
    mv = _blockdiag3(mx, wvh_ref, wvl_ref)
    gts = small_ref[...] + bif_ref[...]
    ig = pltpu.roll(gts, S_FG - S_IG, 1)
    fcum = _seg_cumsum(_log_sigmoid(gts))
    flast = _seg_last(fcum)
    m0 = spread(p_seq, m0_ref[...])
    t_g = _tmask(gts.shape)
    inter = fcum + m0
    dms = [jnp.where(t_g >= s, fcum - _down(fcum, s) + _down(ig, s), NEG_BIG) for s in range(T_STEP)]
    mt = inter
    for dm in dms:
        mt = jnp.maximum(mt, dm)
    ci_t = jnp.exp(inter - mt)
    emt_t = jnp.exp(-mt)
    w_t = [jnp.exp(dm - mt) for dm in dms]
    dj = flast - fcum + ig
    m_new = jnp.maximum(flast + m0, _seg_allreduce(dj, jnp.maximum))
    cs_t = jnp.exp(flast + m0 - m_new)
    wj_t = jnp.exp(dj - m_new)
    mrow_ref[...] = m_new
    mo = mo_ref[...]
    n0 = spread(p_seq, n0_ref[...])
    for hh in range(ML_H):
        sl = slice(hh * ML_DH, (hh + 1) * ML_DH)
        lane = S_FG + hh
        qh, kh, vh = mq[:, sl], mk[:, sl], mv[:, sl]
        ci, emt, cs, wj = _col(ci_t, lane), _col(emt_t, lane), _col(cs_t, lane), _col(wj_t, lane)
        num = jnp.zeros((rows, ML_DH), F32)
        den = ci * jnp.sum(qh * n0[:, sl], axis=-1, keepdims=True)
        for s in range(T_STEP):
            sc = jnp.sum(qh * _down(kh, s), axis=-1, keepdims=True) * _col(w_t[s], lane)
            num = num + sc * _down(vh, s)
            den = den + sc
        kw = kh * wj
        nrow_ref[:, sl] = cs * n0[:, sl] + _seg_allreduce(kw, jnp.add)
        qc = []
        for p in range(n_pairs):
            r8 = slice(p * SUBLANE, (p + 1) * SUBLANE)
            kw_t = kw[r8, :].T
            res = None
            for j in range(seq_per_tile):
                b = p * seq_per_tile + j
                c_b = c0_ref[b, hh]
                r_j = _dot(qh[r8, :], c_b)
                res = r_j if res is None else jnp.where(row8 // T_STEP == j, r_j, res)
                v_j = jnp.where(row8 // T_STEP == j, vh[r8, :], 0.0)
                last = p * SUBLANE + j * T_STEP + T_STEP - 1
                ct_ref[b, hh] = cs[last:last + 1, :] * c_b + _dot(kw_t, v_j)
            qc.append(res)
        num = num + ci * jnp.concatenate(qc, axis=0)
        hcell = num / jnp.maximum(jnp.abs(den), emt)
        ym_ref[:, sl] = (jax.nn.sigmoid(mo[:, sl]) * _rms(hcell, mgn_ref[...])).astype(BF16)


def _sample_mixer(proj, small, el, em, h0e, n0e, m0e, s_state, c_state, layer, w, row_prev, big_prev):
    n = proj.shape[0]
    nseq = n // T_STEP
    bb = STEP_BB
    rows = bb * T_STEP
    row = lambda width, blk: pl.BlockSpec((rows, width), lambda i: (i, blk))
    lrow = lambda width: pl.BlockSpec((None, rows, width), lambda i: (layer, i, 0))
    lw3 = lambda s: pl.BlockSpec((None,) + s, lambda i: (layer, 0, 0))
    lw4 = lambda s: pl.BlockSpec((None,) + s, lambda i: (layer, 0, 0, 0))
    sspec = pl.BlockSpec((None, bb, GLA_H, GLA_DK, GLA_DV), lambda i: (layer, i, 0, 0, 0))
    cspec = pl.BlockSpec((None, bb, ML_H, ML_DH, ML_DH), lambda i: (layer, i, 0, 0, 0))
    per_blk = SUBLANE // bb
    seq_blk = lambda width: pl.BlockSpec((None, SUBLANE, width), lambda i: (layer, i // per_blk, 0))
    conv_seq = lambda width: pl.BlockSpec((None, CONV_W - 1, SUBLANE, width), lambda i: (layer, 0, i // per_blk, 0))
    nblk = ML_W // LANE
    kw, vw = GLA_H * GLA_DK, GLA_H * GLA_DV
    in_specs = [
        row(LRU_W, C_LRU // LRU_W), row(kw, C_GQ // kw), row(kw, C_GK // kw), row(vw, C_GV // vw),
        row(vw, C_GGATE // vw), row(ML_W, C_MX // ML_W), row(ML_W, C_MO // ML_W), row(LANE, 0),
        conv_seq(LRU_W), conv_seq(ML_W), seq_blk(LRU_W), seq_blk(ML_W), seq_blk(LANE), sspec, cspec,
        lw3((CONV_W, LRU_W)), lw3((1, LRU_W)), lw4((LRU_BLOCKS, LANE, LANE)), lw3((1, LRU_W)),
        lw4((LRU_BLOCKS, LANE, LANE)), lw3((1, LRU_W)), lw3((1, LRU_W)),
        lw3((LANE, kw)), lw3((1, kw)), lw3((1, GLA_DV)),
        lw3((CONV_W, ML_W)), lw3((1, ML_W)), *([lw4((nblk, LANE, LANE))] * 6), lw3((1, LANE)), lw3((1, ML_DH)),
    ]
    args = [proj] * 7 + [small, el, em, h0e, n0e, m0e, s_state, c_state,
                         w["lru_cw"], w["lru_cb"], w["lru_wa"], w["lru_ba"], w["lru_wx"], w["lru_bx"], w["lru_lam"],
                         w["wg2"], w["bg"], w["gla_gn"],
                         w["ml_cw"], w["ml_cb"], *w["ml_wq"], *w["ml_wk"], *w["ml_wv"], w["ml_bif"], w["ml_gn"]]
    creates = row_prev is None
    prev = list(big_prev) if creates else list(row_prev) + list(big_prev)
    first_out = 6 if creates else 3
    aliases = {len(in_specs) + j: first_out + j for j in range(len(prev))}
    in_specs += [pl.BlockSpec(memory_space=pl.ANY)] * len(prev)
    args += prev
    sbytes = bb * GLA_H * GLA_DK * GLA_DV * 4
    cbytes = bb * ML_H * ML_DH * ML_DH * 4
    vmem = 4 * (sbytes + cbytes) + 2 * rows * (7 * 1024 + 8 * 1024) * 4 + 80 * rows * 1024 * 4 + (12 << 20)
    row_specs = [lrow(LRU_W), lrow(ML_W), lrow(LANE)]
    if creates:
        all_rows = lambda width: pl.BlockSpec((DEPTH, rows, width), lambda i: (0, i, 0))
        row_specs = [all_rows(LRU_W), all_rows(ML_W), all_rows(LANE)]
    state_specs = row_specs + [sspec, cspec]
    return pl.pallas_call(
        functools.partial(_sample_mixer_kernel, layer=layer, creates=creates),
        grid=(nseq // bb,),
        in_specs=in_specs,
        out_specs=[row(LRU_W, 0), row(vw, 0), row(ML_W, 0), *state_specs],
        out_shape=[SDS((n, LRU_W), BF16), SDS((n, vw), BF16), SDS((n, ML_W), BF16),
                   SDS((DEPTH, n, LRU_W), F32), SDS((DEPTH, n, ML_W), F32), SDS((DEPTH, n, LANE), F32),
                   SDS((DEPTH, nseq, GLA_H, GLA_DK, GLA_DV), F32), SDS((DEPTH, nseq, ML_H, ML_DH, ML_DH), F32)],
        input_output_aliases=aliases,
        compiler_params=_params(("arbitrary",), vmem),
        name="sample_mixer",
    )(*args)


def _prep_weights(p):
    w_in = p["w_in"]
    c1 = C_GGATE
    c2 = C_MG + W_SHIFT1
    w_small = jnp.concatenate(
        [w_in[:, :, c1:c1 + GLA_RANK], w_in[:, :, c2:c2 + 2 * ML_H],
         jnp.zeros((DEPTH, D, LANE - GLA_RANK - 2 * ML_H), F32)], axis=-1)
    wg2 = jnp.concatenate(
        [p["gla_w_g2"], jnp.zeros((DEPTH, LANE - GLA_RANK, GLA_H * GLA_DK), F32)], axis=1)
    blk_of = jnp.arange(LANE) // ML_BS
    on_diag = blk_of[:, None] == blk_of[None, :]

    def dense_bd(wb):
        rows = wb.reshape(DEPTH, ML_W // LANE, LANE, ML_BS)
        dense = jnp.where(on_diag, jnp.tile(rows, (1, 1, 1, LANE // ML_BS)), 0.0)
        hi = dense.astype(BF16)
        return hi, (dense - hi.astype(F32)).astype(BF16)

    bif = jnp.zeros((DEPTH, 1, LANE), F32).at[:, 0, S_IG:S_IG + 2 * ML_H].set(p["ml_b_if"])
    r3 = lambda a: a.reshape(DEPTH, 1, a.shape[-1])
    return dict(
        w_small=w_small, wg2=wg2, bg=r3(p["gla_b_g"]), gla_gn=r3(p["gla_g_norm"]),
        g1=r3(p["g_norm1"]), g2=r3(p["g_norm2"]), gf=p["g_final"].reshape(1, D),
        lru_cw=p["lru_conv_w"], lru_cb=r3(p["lru_conv_b"]), lru_wa=p["lru_w_a"], lru_ba=r3(p["lru_b_a"]),
        lru_wx=p["lru_w_x"], lru_bx=r3(p["lru_b_x"]), lru_lam=r3(p["lru_lam"]),
        ml_cw=p["ml_conv_w"], ml_cb=r3(p["ml_conv_b"]), ml_wq=dense_bd(p["ml_w_q"]), ml_wk=dense_bd(p["ml_w_k"]),
        ml_wv=dense_bd(p["ml_w_v"]), ml_bif=bif, ml_gn=r3(p["ml_g_norm"]),
        w_in_t=jnp.swapaxes(w_in, 1, 2), w_br=(p["w_br_lru"], p["w_br_gla"], p["w_br_ml"]), w_out=p["w_out"], w_ff1=p["w_ff1"],
        w_ff2=p["w_ff2"],
    )


def _trunk(xp3, xs3, mod_p, mod_s, states, w):
    bp, tp, _ = xp3.shape
    bs, ts, _ = xs3.shape
    n_p, n_s = bp * tp, bs * ts
    tm_p = min(tp, 1024)
    tm_f = min(tp, 512)
    s_h, s_cv, s_gla, s_c, s_n, s_m, s_mcv = states
    el = jnp.swapaxes(s_cv, 1, 2)
    em = jnp.swapaxes(s_mcv, 1, 2)
    h0e = s_h
    n0e = s_n.reshape(DEPTH, bs, ML_W)
    m0e = jnp.pad(s_m, ((0, 0), (0, 0), (S_FG, LANE - S_FG - ML_H)))
    x_p = xp3.reshape(n_p, D)
    x_s = xs3.reshape(n_s, D)
    prev_p = None
    conv_l, conv_m = [], []
    keep = ts - (CONV_W - 1)
    for l in range(DEPTH):
        last = l == DEPTH - 1
        proj_s, small_s, w_main = _in_proj_cast(x_s, mod_s, l, w["g1"], w["w_in_t"], w["w_small"], tn=512, rpg=n_s)
        if l == 0:
            s_shape = (DEPTH, bs, GLA_H, GLA_DK, GLA_DV)
            c_shape = (DEPTH, bs, ML_H, ML_DH, ML_DH)
            flat_rows = lambda shape: shape[0] * shape[1] * shape[2] * shape[3] * shape[4] // GLA_DV
            proj_p, small_p, s_buf, c_buf = _in_proj(x_p, mod_p, l, w["g1"], w_main, w["w_small"], tm=tm_p, tn=1024,
                                                     rpg=tp, r=1, zero_rows=(flat_rows(s_shape), flat_rows(c_shape)))
            big_s = (s_buf.reshape(s_shape), c_buf.reshape(c_shape))
            rows_s = None
        else:
            proj_p, small_p = _in_proj(x_p, mod_p, l, w["g1"], w_main, w["w_small"], tm=tm_p, tn=1024, rpg=tp, r=1)
        outs_s = _sample_mixer(proj_s, small_s, el, em, h0e, n0e, m0e, s_gla, s_c, l, w, rows_s, big_s)
        rows_s, big_s = outs_s[3:6], outs_s[6:]
        outs_p = _prompt_mixer(proj_p.reshape(bp, tp, N_MAIN), small_p.reshape(bp, tp, LANE), l, w, prev_p,
                               L=min(tp, CHUNK), rows=min(tp, PROMPT_ROWS), nseq=PROMPT_SEQS)
        prev_p = outs_p[3:]
        conv_l.append(proj_s[:, C_LRU:C_LRU + LRU_W].reshape(bs, ts, LRU_W)[:, keep:])
        conv_m.append(proj_s[:, C_MX:C_MX + ML_W].reshape(bs, ts, ML_W)[:, keep:])
        merged_s, *w_br = _merge(*outs_s[:3], proj_s, l, *w["w_br"], tm=n_s, tn=512, cast=True)
        (merged_p,) = _merge(*(y.reshape(n_p, y.shape[-1]) for y in outs_p[:3]), proj_p, l, *w_br, tm=tm_p, tn=512,
                             cast=False)
        x_s, w_out = _out_proj(merged_s, x_s, mod_s, l, w["w_out"], tm=n_s, tn=512, rpg=n_s, r=n_s, cast=True)
        (x_p,) = _out_proj(merged_p, x_p, mod_p, l, w_out, tm=tm_p, tn=1024, rpg=tp, r=1, cast=False)
        x_s, w_ff1, w_ff2 = _ffn(x_s, mod_s, l, w["g2"], w["gf"], w["w_ff1"], w["w_ff2"], tm=n_s, tf=512, rpg=n_s,
                                 r=n_s, final_norm=last, cast=True)
        (x_p,) = _ffn(x_p, mod_p, l, w["g2"], w["gf"], w_ff1, w_ff2, tm=tm_f, tf=1024, rpg=tp, r=1, final_norm=last,
                      cast=False)
    h_t, lcv_t, s_t, c_t, n_t, m_t, mcv_t = prev_p
    p_states = [h_t.reshape(DEPTH, bp, LRU_W), lcv_t, s_t, c_t, n_t, m_t[..., 0], mcv_t]
    hrow, nrow, mrow = rows_s
    s_out, c_out = big_s
    t_last = ts - 1
    s_states = [hrow.reshape(DEPTH, bs, ts, LRU_W)[:, :, t_last], jnp.stack(conv_l), s_out, c_out,
                nrow.reshape(DEPTH, bs, ts, ML_H, ML_DH)[:, :, t_last],
                mrow.reshape(DEPTH, bs, ts, LANE)[:, :, t_last, S_FG:S_FG + ML_H], jnp.stack(conv_m)]
    return x_p.reshape(bp, tp, D), x_s.reshape(bs, ts, D), p_states, s_states


def kernel(x_prompt, x_sample, c_prompt, c_sample, state_lru_h, state_lru_conv, state_gla, state_mlstm_C, state_mlstm_n, state_mlstm_m, state_mlstm_conv, w_ada, b_ada, g_norm1, g_norm2, w_in, lru_conv_w, lru_conv_b, lru_w_a, lru_b_a, lru_w_x, lru_b_x, lru_lam, gla_w_g2, gla_b_g, gla_g_norm, ml_conv_w, ml_conv_b, ml_w_q, ml_w_k, ml_w_v, ml_b_if, ml_g_norm, w_br_lru, w_br_gla, w_br_ml, w_out, w_ff1, w_ff2, g_final):
    p = dict(g_norm1=g_norm1, g_norm2=g_norm2, w_in=w_in, lru_conv_w=lru_conv_w, lru_conv_b=lru_conv_b,
             lru_w_a=lru_w_a, lru_b_a=lru_b_a, lru_w_x=lru_w_x, lru_b_x=lru_b_x, lru_lam=lru_lam,
             gla_w_g2=gla_w_g2, gla_b_g=gla_b_g, gla_g_norm=gla_g_norm, ml_conv_w=ml_conv_w, ml_conv_b=ml_conv_b,
             ml_w_q=ml_w_q, ml_w_k=ml_w_k, ml_w_v=ml_w_v, ml_b_if=ml_b_if, ml_g_norm=ml_g_norm,
             w_br_lru=w_br_lru, w_br_gla=w_br_gla, w_br_ml=w_br_ml, w_out=w_out, w_ff1=w_ff1, w_ff2=w_ff2,
             g_final=g_final)
    w = _prep_weights(p)
    bp = x_prompt.shape[0]
    bs, ts, _ = x_sample.shape
    assert ts == T_STEP and bs % STEP_BB == 0
    ns = bs * ts
    pad = (-(ns + bp)) % SUBLANE
    c_all = jnp.concatenate([jnp.repeat(c_sample, ts, axis=0), c_prompt, jnp.zeros((pad, D), F32)], axis=0)
    mod_all = _ada(c_all, w_ada, b_ada)
    mod_s = mod_all.reshape(DEPTH, 1, ns + bp + pad, N_MOD * D)
    mod_p = mod_all[:, ns:ns + bp].reshape(DEPTH, bp, 1, N_MOD * D)
    y_p, y_s, ps, ss = _trunk(x_prompt, x_sample, mod_p, mod_s,
                              (state_lru_h, state_lru_conv, state_gla, state_mlstm_C, state_mlstm_n, state_mlstm_m,
                               state_mlstm_conv), w)
    return (y_p, y_s, *ps, *ss)
```

```python
import functools

import jax
import jax.numpy as jnp
from jax import lax
from jax.experimental import pallas as pl
from jax.experimental.pallas import tpu as pltpu

F32, BF16 = jnp.float32, jnp.bfloat16
SDS = jax.ShapeDtypeStruct

D = 2048
DEPTH = 2
LRU_W = 1024
LRU_BLOCKS = 8
LRU_C = 8.0
CONV_W = 4
GLA_H = 4
GLA_DK = 128
GLA_DV = 256
GLA_RANK = 16
GLA_TAU = 16.0
ML_H = 4
ML_W = 1024
ML_DH = 256
ML_BS = 4
CHUNK = 128
ML_CHUNK = 256
PROMPT_ROWS = 512
PROMPT_SEQS = 1
D_FF = 4 * D
EPS = 1e-6
N_MOD = 6
T_STEP = 4
STEP_BB = 4

LANE = 128
SUBLANE = 8
VMEM_LIMIT_CAP = 60 * 1024 * 1024

C_LRU, C_GQ, C_GK, C_GV, C_GGATE, C_MX, C_MO, C_MG = 0, 1024, 1536, 2048, 3072, 4096, 5120, 6144
N_MAIN = 12288
W_SHIFT1 = GLA_RANK
W_SHIFT2 = GLA_RANK + 2 * ML_H
S_GLR = 0
S_IG = 16
S_FG = 20
NEG_BIG = -1e30


def _params(sem, vmem_bytes):
    return pltpu.CompilerParams(dimension_semantics=sem, vmem_limit_bytes=int(min(vmem_bytes, VMEM_LIMIT_CAP)))


def _dot(a, b):
    return jnp.dot(a.astype(BF16), b.astype(BF16), preferred_element_type=F32)


def _dot_nt(a, b):
    return lax.dot_general(a.astype(BF16), b.astype(BF16), (((1,), (1,)), ((), ())), preferred_element_type=F32)


def _split3(x):
    p0 = x.astype(BF16)
    r = x - p0.astype(F32)
    p1 = r.astype(BF16)
    return p0, p1, (r - p1.astype(F32)).astype(BF16)


def _cumsum_rows(tri_b, x):
    return sum(jnp.dot(tri_b, p, preferred_element_type=F32) for p in _split3(x))


def _cumsum_lanes(x, triu_b):
    return sum(jnp.dot(p, triu_b, preferred_element_type=F32) for p in _split3(x))


def _split(x):
    hi = x.astype(BF16)
    return hi, (x - hi.astype(F32)).astype(BF16)


def _dot3(x_hi, x_lo, w_hi, w_lo):
    d = lambda a, b: jnp.dot(a, b, preferred_element_type=F32)
    return d(x_hi, w_hi) + (d(x_lo, w_hi) + d(x_hi, w_lo))


def _log_sigmoid(z):
    return jnp.minimum(z, 0.0) - jnp.log1p(jnp.exp(-jnp.abs(z)))


def _silu(z):
    return z * jax.nn.sigmoid(z)


def _rms(x, g):
    return x * lax.rsqrt(jnp.mean(x * x, axis=-1, keepdims=True) + EPS) * g


def _tri(n):
    r = lax.broadcasted_iota(jnp.int32, (n, n), 0)
    c = lax.broadcasted_iota(jnp.int32, (n, n), 1)
    return r >= c


def _lru_gates(ub, wa, wx, ba, bx, lam):
    r = jax.nn.sigmoid(_dot(ub, wa) + ba)
    i = jax.nn.sigmoid(_dot(ub, wx) + bx)
    log_a = LRU_C * r * _log_sigmoid(lam)
    t = jnp.tanh(log_a)
    return jnp.exp(log_a), jnp.sqrt(-2.0 * t / (1.0 - t)) * (i * ub)


def _blockdiag3(x, wh_ref, wl_ref):
    xh, xl = _split(x)
    outs = []
    for blk in range(ML_W // LANE):
        sl = slice(blk * LANE, (blk + 1) * LANE)
        outs.append(_dot3(xh[:, sl], xl[:, sl], wh_ref[blk], wl_ref[blk]))
    return jnp.concatenate(outs, axis=1)


def _ada_kernel(c_ref, w_ref, b_ref, o_ref):
    o_ref[...] = _dot(_silu(c_ref[...]), w_ref[...]) + b_ref[...]


def _ada(c_all, w_ada, b_ada):
    m = c_all.shape[0]
    tn = 1024
    n_out = N_MOD * D
    return pl.pallas_call(
        _ada_kernel,
        grid=(DEPTH, n_out // tn),
        in_specs=[
            pl.BlockSpec((m, D), lambda l, n: (0, 0)),
            pl.BlockSpec((None, D, tn), lambda l, n: (l, 0, n)),
            pl.BlockSpec((None, 1, tn), lambda l, n: (l, 0, n)),
        ],
        out_specs=pl.BlockSpec((None, m, tn), lambda l, n: (l, 0, n)),
        out_shape=SDS((DEPTH, m, n_out), F32),
        compiler_params=_params(("parallel", "arbitrary"), 2 * (m * D + D * tn + m * tn) * 4 + (12 << 20)),
        name="ada",
    )(c_all, w_ada, b_ada.reshape(DEPTH, 1, n_out))


def _modspec(layer, comp, r, width, tm, rpg, jmap=None):
    nb = D // width
    jm = (lambda j: 0) if jmap is None else jmap
    if r == 1:
        return pl.BlockSpec((None, None, 1, width), lambda i, j: (layer, (i * tm) // rpg, 0, comp * nb + jm(j)))
    mode = pl.Buffered(1) if jmap is None else None
    return pl.BlockSpec((None, None, tm, width), lambda i, j: (layer, 0, i, comp * nb + jm(j)), pipeline_mode=mode)


def _norm_mod_store(xn_s, x_ref, g_ref, sc_ref, sh_ref):
    tm = x_ref.shape[0]
    rc = min(tm, 256)
    for c in range(tm // rc):
        rows = slice(c * rc, (c + 1) * rc)
        mrows = rows if sc_ref.shape[0] == tm else slice(None)
        xn = _rms(x_ref[rows, :], g_ref[...]) * (1.0 + sc_ref[mrows, :]) + sh_ref[mrows, :]
        xn_s[rows, :] = xn.astype(BF16)


def _in_proj_kernel(x_ref, sc_ref, sh_ref, g_ref, wm_ref, ws_ref, om_ref, os_ref, *rest, fill_steps):
    xn_s = rest[-1]

    @pl.when(pl.program_id(1) == 0)
    def _():
        _norm_mod_store(xn_s, x_ref, g_ref, sc_ref, sh_ref)
        os_ref[...] = jnp.dot(xn_s[...], ws_ref[...].astype(BF16), preferred_element_type=F32)

    om_ref[...] = jnp.dot(xn_s[...], wm_ref[...], preferred_element_type=F32)

    if fill_steps:
        @pl.when(pl.program_id(1) < fill_steps)
        def _():
            for z_ref in rest[:-1]:
                z_ref[...] = jnp.zeros_like(z_ref)


def _in_proj(x, mod, layer, g1, w_main, w_small, *, tm, tn, rpg, r, zero_rows=()):
    n = x.shape[0]
    vmem = 2 * (tm * D * 4 + 2 * r * D * 4 + D * tn * 2 + D * LANE * 2 + tm * tn * 4 + tm * LANE * 4) + tm * D * 2
    vmem += 8 << 20
    n_i, n_j = n // tm, N_MAIN // tn
    fill_steps = 0
    zero_specs, zero_shapes = [], []
    if zero_rows:
        fill_steps = max(s for s in range(1, n_j + 1)
                         if all(zr % (n_i * s * SUBLANE) == 0 for zr in zero_rows))
        for zr in zero_rows:
            blk = zr // (n_i * fill_steps)
            zero_specs.append(pl.BlockSpec((blk, GLA_DV),
                                           lambda i, j: (i * fill_steps + jnp.minimum(j, fill_steps - 1), 0)))
            zero_shapes.append(SDS((zr, GLA_DV), F32))
            vmem += 2 * blk * GLA_DV * 4
    outs = pl.pallas_call(
        functools.partial(_in_proj_kernel, fill_steps=fill_steps),
        grid=(n_i, n_j),
        in_specs=[
            pl.BlockSpec((tm, D), lambda i, j: (i, 0)),
            _modspec(layer, 1, r, D, tm, rpg),
            _modspec(layer, 0, r, D, tm, rpg),
            pl.BlockSpec((None, 1, D), lambda i, j: (layer, 0, 0)),
            pl.BlockSpec((D, tn), lambda i, j: (0, j)),
            pl.BlockSpec((None, D, LANE), lambda i, j: (layer, 0, 0)),
        ],
        out_specs=[pl.BlockSpec((tm, tn), lambda i, j: (i, j)), pl.BlockSpec((tm, LANE), lambda i, j: (i, 0)),
                   *zero_specs],
        out_shape=[SDS((n, N_MAIN), F32), SDS((n, LANE), F32), *zero_shapes],
        scratch_shapes=[pltpu.VMEM((tm, D), BF16)],
        compiler_params=_params(("parallel", "arbitrary"), vmem),
        name="in_proj",
    )(x, mod, mod, g1, w_main, w_small)
    return outs


def _in_proj_cast_kernel(x_ref, sc_ref, sh_ref, g_ref, wt_ref, ws_ref, om_ref, os_ref, wc_ref, xn_s):
    @pl.when(pl.program_id(1) == 0)
    def _():
        _norm_mod_store(xn_s, x_ref, g_ref, sc_ref, sh_ref)
        os_ref[...] = jnp.dot(xn_s[...], ws_ref[...].astype(BF16), preferred_element_type=F32)

    wb = wt_ref[...].T.astype(BF16)
    wc_ref[...] = wb
    om_ref[...] = jnp.dot(xn_s[...], wb, preferred_element_type=F32)


def _in_proj_cast(x, mod, layer, g1, w_in_t, w_small, *, tn, rpg):
    n = x.shape[0]
    tm = n
    nb1, nb2 = C_GGATE // tn, C_MG // tn

    def w_rows(i, j):
        shift = jnp.where(j >= nb2, W_SHIFT2 // SUBLANE, jnp.where(j >= nb1, W_SHIFT1 // SUBLANE, 0))
        return (layer, (j * (tn // SUBLANE) + shift) * SUBLANE, 0)

    vmem = tm * D * 4 + 2 * tm * D * 4 + 2 * (tn * D * 4 + D * LANE * 4 + tm * tn * 4 + tm * LANE * 4 + D * tn * 2)
    vmem += tm * D * 2 + 3 * D * tn * 4 + (4 << 20)
    return pl.pallas_call(
        _in_proj_cast_kernel,
        grid=(1, N_MAIN // tn),
        in_specs=[
            pl.BlockSpec((tm, D), lambda i, j: (i, 0), pipeline_mode=pl.Buffered(1)),
            _modspec(layer, 1, tm, D, tm, rpg),
            _modspec(layer, 0, tm, D, tm, rpg),
            pl.BlockSpec((None, 1, D), lambda i, j: (layer, 0, 0)),
            pl.BlockSpec((None, pl.Element(tn), pl.Element(D)), w_rows),
            pl.BlockSpec((None, D, LANE), lambda i, j: (layer, 0, 0)),
        ],
        out_specs=[pl.BlockSpec((tm, tn), lambda i, j: (i, j)), pl.BlockSpec((tm, LANE), lambda i, j: (i, 0)),
                   pl.BlockSpec((D, tn), lambda i, j: (0, j))],
        out_shape=[SDS((n, N_MAIN), F32), SDS((n, LANE), F32), SDS((D, N_MAIN), BF16)],
        scratch_shapes=[pltpu.VMEM((tm, D), BF16)],
        compiler_params=_params(("arbitrary", "arbitrary"), vmem),
        name="in_proj_cast",
    )(x, mod, mod, g1, w_in_t, w_small)


def _wspec(shape, imap, layer, cast):
    if cast:
        return pl.BlockSpec((None,) + shape, lambda i, j: (layer,) + imap(i, j))
    return pl.BlockSpec(shape, imap)


def _merge_kernel(yl_ref, yg_ref, ym_ref, g0_ref, g1_ref, g2_ref, w0_ref, w1_ref, w2_ref, o_ref, *wc_refs):
    ws = [w_ref[...].astype(BF16) for w_ref in (w0_ref, w1_ref, w2_ref)]
    for wc_ref, wb in zip(wc_refs, ws):
        wc_ref[...] = wb
    acc = jax.nn.sigmoid(g0_ref[...]) * jnp.dot(yl_ref[...], ws[0], preferred_element_type=F32)
    acc += jax.nn.sigmoid(g1_ref[...]) * jnp.dot(yg_ref[...], ws[1], preferred_element_type=F32)
    acc += jax.nn.sigmoid(g2_ref[...]) * jnp.dot(ym_ref[...], ws[2], preferred_element_type=F32)
    o_ref[...] = acc.astype(BF16)


def _merge(y_lru, y_gla, y_ml, proj, layer, w_lru, w_gla, w_ml, *, tm, tn, cast):
    n = y_lru.shape[0]
    w = LRU_W
    gb = C_MG // tn
    nb = D // tn
    yspec = pl.BlockSpec((tm, w), lambda i, j: (i, 0))
    wspec = _wspec((w, tn), lambda i, j: (0, j), layer, cast)
    wbytes = 4 if cast else 2
    vmem = 2 * (3 * tm * w * 2 + 3 * tm * tn * 4 + 3 * w * tn * wbytes + tm * tn * 2) + 6 * tm * tn * 4 + (4 << 20)
    out_specs = [pl.BlockSpec((tm, tn), lambda i, j: (i, j))]
    out_shape = [SDS((n, D), BF16)]
    if cast:
        out_specs += [pl.BlockSpec((w, tn), lambda i, j: (0, j))] * 3
        out_shape += [SDS((w, D), BF16)] * 3
        vmem += 2 * 3 * w * tn * 2 + 3 * w * tn * 4
    return pl.pallas_call(
        _merge_kernel,
        grid=(n // tm, nb),
        in_specs=[
            yspec, yspec, yspec,
            pl.BlockSpec((tm, tn), lambda i, j: (i, gb + j)),
            pl.BlockSpec((tm, tn), lambda i, j: (i, gb + nb + j)),
            pl.BlockSpec((tm, tn), lambda i, j: (i, gb + 2 * nb + j)),
            wspec, wspec, wspec,
        ],
        out_specs=out_specs,
        out_shape=out_shape,
        compiler_params=_params(("arbitrary" if cast else "parallel", "arbitrary"), vmem),
        name="merge",
    )(y_lru, y_gla, y_ml, proj, proj, proj, w_lru, w_gla, w_ml)


def _out_proj_kernel(m_ref, w_ref, x_ref, gt_ref, o_ref, *wc_refs):
    wb = w_ref[...].astype(BF16)
    for wc_ref in wc_refs:
        wc_ref[...] = wb
    o_ref[...] = x_ref[...] + gt_ref[...] * jnp.dot(m_ref[...], wb, preferred_element_type=F32)


def _out_proj(merged, x, mod, layer, w_out, *, tm, tn, rpg, r, cast):
    n = x.shape[0]
    wbytes = 4 if cast else 2
    vmem = 2 * (tm * D * 2 + D * tn * wbytes + 2 * tm * tn * 4 + r * tn * 4) + 2 * tm * tn * 4 + (4 << 20)
    out_specs = [pl.BlockSpec((tm, tn), lambda i, j: (i, j))]
    out_shape = [SDS((n, D), F32)]
    if cast:
        out_specs.append(pl.BlockSpec((D, tn), lambda i, j: (0, j)))
        out_shape.append(SDS((D, D), BF16))
        vmem += 2 * D * tn * 2 + D * tn * 4
    return pl.pallas_call(
        _out_proj_kernel,
        grid=(n // tm, D // tn),
        in_specs=[
            pl.BlockSpec((tm, D), lambda i, j: (i, 0)),
            _wspec((D, tn), lambda i, j: (0, j), layer, cast),
            pl.BlockSpec((tm, tn), lambda i, j: (i, j)),
            _modspec(layer, 2, r, tn, tm, rpg, jmap=lambda j: j),
        ],
        out_specs=out_specs,
        out_shape=out_shape,
        compiler_params=_params(("arbitrary" if cast else "parallel", "arbitrary"), vmem),
        name="out_proj",
    )(merged, w_out, x, mod)


def _ffn_kernel(x_ref, sc_ref, sh_ref, gt_ref, g_ref, gf_ref, w1_ref, w2_ref, o_ref, *rest, final_norm):
    xn_s = rest[-1]
    f = pl.program_id(1)

    @pl.when(f == 0)
    def _():
        _norm_mod_store(xn_s, x_ref, g_ref, sc_ref, sh_ref)
        o_ref[...] = jnp.zeros_like(o_ref)

    w1 = w1_ref[...].astype(BF16)
    w2 = w2_ref[...].astype(BF16)
    for wc_ref, wb in zip(rest[:-1], (w1, w2)):
        wc_ref[...] = wb
    h = jnp.square(jnp.maximum(jnp.dot(xn_s[...], w1, preferred_element_type=F32), 0.0))
    o_ref[...] += jnp.dot(h.astype(BF16), w2, preferred_element_type=F32)

    @pl.when(f == pl.num_programs(1) - 1)
    def _():
        y = x_ref[...] + gt_ref[...] * o_ref[...]
        if final_norm:
            y = _rms(y, gf_ref[...])
        o_ref[...] = y


def _ffn(x, mod, layer, g2, g_final, w1, w2, *, tm, tf, rpg, r, final_norm, cast):
    n = x.shape[0]
    wbytes = 4 if cast else 2
    vmem = 2 * (2 * tm * D * 4 + 3 * r * D * 4 + 2 * D * tf * wbytes) + tm * D * 2 + 2 * tm * tf * 4 + 2 * tm * D * 4
    single = pl.Buffered(1) if n == tm else None
    out_specs = [pl.BlockSpec((tm, D), lambda i, f: (i, 0), pipeline_mode=single)]
    out_shape = [SDS((n, D), F32)]
    if cast:
        out_specs += [pl.BlockSpec((D, tf), lambda i, f: (0, f)), pl.BlockSpec((tf, D), lambda i, f: (f, 0))]
        out_shape += [SDS((D, D_FF), BF16), SDS((D_FF, D), BF16)]
        vmem += 2 * 2 * D * tf * 2
    return pl.pallas_call(
        functools.partial(_ffn_kernel, final_norm=final_norm),
        grid=(n // tm, D_FF // tf),
        in_specs=[
            pl.BlockSpec((tm, D), lambda i, f: (i, 0), pipeline_mode=single),
            _modspec(layer, 4, r, D, tm, rpg),
            _modspec(layer, 3, r, D, tm, rpg),
            _modspec(layer, 5, r, D, tm, rpg),
            pl.BlockSpec((None, 1, D), lambda i, f: (layer, 0, 0)),
            pl.BlockSpec((1, D), lambda i, f: (0, 0)),
            _wspec((D, tf), lambda i, f: (0, f), layer, cast),
            _wspec((tf, D), lambda i, f: (f, 0), layer, cast),
        ],
        out_specs=out_specs,
        out_shape=out_shape,
        scratch_shapes=[pltpu.VMEM((tm, D), BF16)],
        compiler_params=_params(("arbitrary" if cast else "parallel", "arbitrary"), vmem),
        name="ffn",
    )(x, mod, mod, mod, g2, g_final, w1, w2)


def _conv_chunk(carry_s, x, w, b, L):
    t8 = lax.broadcasted_iota(jnp.int32, (SUBLANE, x.shape[1]), 0)
    carry = carry_s[...]
    acc = b + x * w[CONV_W - 1:CONV_W]
    for k in range(1, CONV_W):
        down = pltpu.roll(x, k, 0)
        head = jnp.where(t8 < k, pltpu.roll(carry, k, 0), down[0:SUBLANE, :])
        acc = acc + jnp.concatenate([head, down[SUBLANE:, :]], axis=0) * w[CONV_W - 1 - k:CONV_W - k]
    carry_s[...] = x[L - SUBLANE:L, :]
    return acc, x[L - (CONV_W - 1):L, :]


N_TOK_REFS, N_W_REFS, N_Y_REFS, N_STATE_REFS, N_SCRATCH = 8, 20, 3, 7, 9


def _prompt_mixer_kernel(*refs, L, layer, creates):
    tok = refs[:N_TOK_REFS]
    wts = refs[N_TOK_REFS:N_TOK_REFS + N_W_REFS]
    scr = refs[-N_SCRATCH:]
    outs = refs[-(N_SCRATCH + N_Y_REFS + N_STATE_REFS):-N_SCRATCH]
    ys, state_outs = outs[:N_Y_REFS], outs[N_Y_REFS:]
    nseq = tok[0].shape[0]

    @pl.when(pl.program_id(1) == 0)
    def _():
        xpl_s, xpm_s = scr[0], scr[1]
        xpl_s[:, 0:SUBLANE, :] = jnp.zeros((nseq, SUBLANE, LRU_W), F32)
        xpm_s[:, 0:SUBLANE, :] = jnp.zeros((nseq, SUBLANE, ML_W), F32)
        for s_ref in scr[4:]:
            s_ref[...] = jnp.zeros_like(s_ref)

    finals = [_prompt_mixer_rows([r.at[bi] for r in tok], wts, [y.at[bi] for y in ys], [s.at[bi] for s in scr], L)
              for bi in range(nseq)]

    @pl.when(pl.program_id(1) == pl.num_programs(1) - 1)
    def _():
        souts = state_outs
        if creates:
            for ref in souts:
                for other in range(DEPTH):
                    if other != layer:
                        ref[other] = jnp.zeros(ref.shape[1:], F32)
            souts = [ref.at[layer] for ref in souts]
        ht_ref, lct_ref, st_ref, ct_ref, nt_ref, mt_ref, mct_ref = souts
        s_s, c_s, n_s, m_s = scr[5:]
        for bi, (h, l_tail, m_tail) in enumerate(finals):
            ht_ref[bi] = h
            lct_ref[bi] = l_tail
            mct_ref[bi] = m_tail
        st_ref[...] = s_s[...]
        ct_ref[...] = c_s[...]
        nt_ref[...] = n_s[...]
        mt_ref[...] = m_s[...]


def _prompt_mixer_rows(tok, wts, ys, scr, L):
    lx_ref, q_ref, k_ref, v_ref, gate_ref, mx_ref, mo_ref, small_ref = tok
    (lcw_ref, lcb_ref, wa_ref, ba_ref, wx_ref, bx_ref, lam_ref, wg2_ref, bg_ref, ggn_ref,
     mcw_ref, mcb_ref, wqh_ref, wql_ref, wkh_ref, wkl_ref, wvh_ref, wvl_ref, bif_ref, mgn_ref) = wts
    yl_ref, yg_ref, ym_ref = ys
    xpl_s, xpm_s, a_s, u_s, h_s, s_s, c_s, n_s, m_s = scr
    rows = lx_ref.shape[0]

    q, k, v, gate = q_ref[...], k_ref[...], v_ref[...], gate_ref[...]
    small = small_ref[...]
    logdec = _log_sigmoid(_dot(small, wg2_ref[...]) + bg_ref[...]) * (1.0 / GLA_TAU)
    mx = mx_ref[...]
    conv, m_tail = _conv_chunk(xpm_s, mx, mcw_ref[...], mcb_ref[...], rows)
    mc = _silu(conv)
    mq = _blockdiag3(mc, wqh_ref, wql_ref)
    mk = _blockdiag3(mc, wkh_ref, wkl_ref) * (ML_DH ** -0.5)
    mv = _blockdiag3(mx, wvh_ref, wvl_ref)

    after_proj = jnp.where(mq[0:1, :] != mq[0:1, :], 1.0, 0.0)
    u, l_tail = _conv_chunk(xpl_s, lx_ref[...], lcw_ref[...], lcb_ref[...] + after_proj, rows)
    for blk in range(LRU_BLOCKS):
        sl = slice(blk * LANE, (blk + 1) * LANE)
        a_s[:, sl], u_s[:, sl] = _lru_gates(u[:, sl], wa_ref[blk], wx_ref[blk], ba_ref[:, sl], bx_ref[:, sl],
                                            lam_ref[:, sl])

    gates = small + bif_ref[...]
    gates_t = gates.T
    lf = _log_sigmoid(gates)
    lf_t = _log_sigmoid(gates_t)
    mo = mo_ref[...]
    tri = _tri(L)
    tri_b = tri.astype(BF16)

    for cc in range(rows // L):
        rs = slice(cc * L, (cc + 1) * L)
        bcum = _cumsum_rows(tri_b, logdec[rs, :])
        for hh in range(GLA_H):
            ks = slice(hh * GLA_DK, (hh + 1) * GLA_DK)
            vs = slice(hh * GLA_DV, (hh + 1) * GLA_DV)
            bh = bcum[:, ks]
            k_h, v_h = k[rs, ks], v[rs, vs]
            b_mid = bh[L // 2 - 1:L // 2, :]
            qe = (q[rs, ks] * (GLA_DK ** -0.5)) * jnp.exp(bh - b_mid)
            ke = k_h * jnp.exp(b_mid - bh)
            a = jnp.where(tri, _dot_nt(qe, ke), 0.0)
            s_h = s_s[hh]
            bl = bh[L - 1:L, :]
            cols = jnp.exp(jnp.concatenate([jnp.broadcast_to(b_mid, (SUBLANE // 2, GLA_DK)),
                                            jnp.broadcast_to(bl, (SUBLANE // 2, GLA_DK))], axis=0)).T
            mid_col, dec_col = cols[:, 0:1], cols[:, SUBLANE // 2:SUBLANE // 2 + 1]
            o = _dot(qe, mid_col * s_h) + _dot(a, v_h)
            kd = k_h * jnp.exp(bl - bh)
            s_s[hh] = dec_col * s_h + _dot(kd.T, v_h)
            yg_ref[rs, vs] = (_rms(o, ggn_ref[...]) * _silu(gate[rs, vs])).astype(BF16)

    L = min(rows, ML_CHUNK)
    tri = _tri(L)
    tri_b = tri.astype(BF16)
    r_i = lax.broadcasted_iota(jnp.int32, (L, L), 0)
    c_i = lax.broadcasted_iota(jnp.int32, (L, L), 1)
    triu_b = (r_i <= c_i).astype(BF16)
    for cc in range(rows // L):
        rs = slice(cc * L, (cc + 1) * L)
        f_col = _cumsum_rows(tri_b, lf[rs, :])
        f_row = _cumsum_lanes(lf_t[:, rs], triu_b)
        for hh in range(ML_H):
            sl = slice(hh * ML_DH, (hh + 1) * ML_DH)
            qh, kh, vh = mq[rs, sl], mk[rs, sl], mv[rs, sl]
            fc = f_col[:, S_FG + hh:S_FG + hh + 1]
            fr = f_row[S_FG + hh:S_FG + hh + 1, :]
            igr = gates_t[S_IG + hh:S_IG + hh + 1, rs]
            m_h = m_s[hh:hh + 1, 0:1]
            dm = jnp.where(tri, fc - fr + igr, NEG_BIG)
            inter = fc + m_h
            mt = jnp.maximum(inter, jnp.max(dm, axis=-1, keepdims=True))
            ci = jnp.exp(inter - mt)
            s = _dot_nt(qh, kh) * jnp.exp(dm - mt)
            c_h = c_s[hh]
            n_h = n_s[hh:hh + 1, :]
            num = ci * _dot(qh, c_h) + _dot(s, vh)
            den = ci * jnp.sum(qh * n_h, axis=-1, keepdims=True) + jnp.sum(s, axis=-1, keepdims=True)
            hcell = num / jnp.maximum(jnp.abs(den), jnp.exp(-mt))
            fl = fr[:, L - 1:L]
            dj = fl - fr + igr
            m_new = jnp.maximum(fl + m_h, jnp.max(dj, axis=-1, keepdims=True))
            cs = jnp.exp(fl + m_h - m_new)
            wj = jnp.exp(dj - m_new)
            c_s[hh] = cs * c_h + _dot(kh.T * wj, vh)
            n_s[hh:hh + 1, :] = cs * n_h + _dot(jnp.broadcast_to(wj, (SUBLANE, L)), kh)[0:1, :]
            m_s[hh:hh + 1, :] = jnp.broadcast_to(m_new, (1, LANE))
            ym_ref[rs, sl] = (jax.nn.sigmoid(mo[rs, sl]) * _rms(hcell, mgn_ref[...])).astype(BF16)

    t8 = lax.broadcasted_iota(jnp.int32, (SUBLANE, LRU_W), 0)
    h = h_s[...]
    for g in range(rows // SUBLANE):
        r8 = slice(g * SUBLANE, (g + 1) * SUBLANE)
        a8, u8 = a_s[r8, :], u_s[r8, :]
        for s in (1, 2, 4):
            u8 = jnp.where(t8 >= s, u8 + a8 * pltpu.roll(u8, s, 0), u8)
            a8 = jnp.where(t8 >= s, a8 * pltpu.roll(a8, s, 0), a8)
        h8 = u8 + a8 * h
        a_s[r8, :] = h8
        h = h8[SUBLANE - 1:SUBLANE, :]
    h_s[...] = h
    yl_ref[...] = a_s[...].astype(BF16)
    return h, l_tail, m_tail


def _prompt_mixer(proj3, small3, layer, w, prev, *, L, rows, nseq):
    b, t, _ = proj3.shape
    kw, vw = GLA_H * GLA_DK, GLA_H * GLA_DV
    nblk = ML_W // LANE
    blk3 = lambda width, cb: pl.BlockSpec((nseq, rows, width), lambda bi, c: (bi, c, cb))
    lw3 = lambda s: pl.BlockSpec((None,) + s, lambda bi, c: (layer, 0, 0))
    lw4 = lambda s: pl.BlockSpec((None,) + s, lambda bi, c: (layer, 0, 0, 0))
    creates = prev is None
    lead, at = ((DEPTH,), 0) if creates else ((None,), layer)
    st3 = lambda s: pl.BlockSpec(lead + (nseq,) + s, lambda bi, c: (at, bi, 0, 0))
    st4 = lambda s: pl.BlockSpec(lead + (nseq,) + s, lambda bi, c: (at, bi, 0, 0, 0))
    in_specs = [
        blk3(LRU_W, C_LRU // LRU_W), blk3(kw, C_GQ // kw), blk3(kw, C_GK // kw), blk3(vw, C_GV // vw),
        blk3(vw, C_GGATE // vw), blk3(ML_W, C_MX // ML_W), blk3(ML_W, C_MO // ML_W), blk3(LANE, 0),
        lw3((CONV_W, LRU_W)), lw3((1, LRU_W)), lw4((LRU_BLOCKS, LANE, LANE)), lw3((1, LRU_W)),
        lw4((LRU_BLOCKS, LANE, LANE)), lw3((1, LRU_W)), lw3((1, LRU_W)),
        lw3((LANE, kw)), lw3((1, kw)), lw3((1, GLA_DV)),
        lw3((CONV_W, ML_W)), lw3((1, ML_W)), *([lw4((nblk, LANE, LANE))] * 6), lw3((1, LANE)), lw3((1, ML_DH)),
    ]
    args = [proj3] * 7 + [small3,
                          w["lru_cw"], w["lru_cb"], w["lru_wa"], w["lru_ba"], w["lru_wx"], w["lru_bx"], w["lru_lam"],
                          w["wg2"], w["bg"], w["gla_gn"],
                          w["ml_cw"], w["ml_cb"], *w["ml_wq"], *w["ml_wk"], *w["ml_wv"], w["ml_bif"], w["ml_gn"]]
    aliases = {}
    if prev is not None:
        aliases = {len(in_specs) + j: 3 + j for j in range(len(prev))}
        in_specs += [pl.BlockSpec(memory_space=pl.ANY)] * len(prev)
        args += list(prev)
    sbytes = GLA_H * GLA_DK * GLA_DV * 4
    cbytes = ML_H * ML_DH * ML_DH * 4
    vmem = nseq * (2 * rows * (7 * 1024 * 4 + 3 * 1024 * 2) + 3 * (sbytes + cbytes) + 6 * rows * 1024 * 4)
    vmem += 30 * rows * 1024 * 4 + (12 << 20)
    sc = lambda *s: pltpu.VMEM((nseq,) + s, F32)
    return pl.pallas_call(
        functools.partial(_prompt_mixer_kernel, L=L, layer=layer, creates=creates),
        grid=(b // nseq, t // rows),
        in_specs=in_specs,
        out_specs=[blk3(LRU_W, 0), blk3(vw, 0), blk3(ML_W, 0),
                   st3((1, LRU_W)), st3((CONV_W - 1, LRU_W)), st4((GLA_H, GLA_DK, GLA_DV)),
                   st4((ML_H, ML_DH, ML_DH)), st3((ML_H, ML_DH)), st3((ML_H, LANE)), st3((CONV_W - 1, ML_W))],
        out_shape=[SDS((b, t, LRU_W), BF16), SDS((b, t, vw), BF16), SDS((b, t, ML_W), BF16),
                   SDS((DEPTH, b, 1, LRU_W), F32), SDS((DEPTH, b, CONV_W - 1, LRU_W), F32),
                   SDS((DEPTH, b, GLA_H, GLA_DK, GLA_DV), F32), SDS((DEPTH, b, ML_H, ML_DH, ML_DH), F32),
                   SDS((DEPTH, b, ML_H, ML_DH), F32), SDS((DEPTH, b, ML_H, LANE), F32),
                   SDS((DEPTH, b, CONV_W - 1, ML_W), F32)],
        scratch_shapes=[sc(SUBLANE, LRU_W), sc(SUBLANE, ML_W), sc(rows, LRU_W), sc(rows, LRU_W),
                        sc(1, LRU_W), sc(GLA_H, GLA_DK, GLA_DV), sc(ML_H, ML_DH, ML_DH), sc(ML_H, ML_DH),
                        sc(ML_H, LANE)],
        input_output_aliases=aliases,
        compiler_params=_params(("parallel", "arbitrary"), vmem),
        name="prompt_mixer",
    )(*args)


def _tmask(shape):
    return lax.broadcasted_iota(jnp.int32, shape, 0) % T_STEP


def _down(x, s):
    return x if s == 0 else pltpu.roll(x, s, 0)


def _up(x, s):
    return x if s == 0 else pltpu.roll(x, x.shape[0] - s, 0)


def _seg_cumsum(x):
    t = _tmask(x.shape)
    out = x
    for s in range(1, T_STEP):
        out = out + jnp.where(t >= s, _down(x, s), 0.0)
    return out


def _seg_last(x):
    t = _tmask(x.shape)
    out = x
    for s in range(1, T_STEP):
        out = jnp.where(t == T_STEP - 1 - s, _up(x, s), out)
    return out


def _seg_allreduce(x, op):
    t = _tmask(x.shape)
    y = op(x, jnp.where(t % 2 == 1, _down(x, 1), _up(x, 1)))
    return op(y, jnp.where(t >= 2, _down(y, 2), _up(y, 2)))


def _conv_rows(x, e, w, b):
    t = _tmask(x.shape)
    acc = b + x * w[CONV_W - 1:CONV_W]
    for k in range(1, CONV_W):
        hist = jnp.where(t >= k, _down(x, k), _up(e, CONV_W - 1 - k))
        acc = acc + hist * w[CONV_W - 1 - k:CONV_W - k]
    return acc


def _col(x, lane):
    li = lax.broadcasted_iota(jnp.int32, x.shape, 1)
    return jnp.sum(jnp.where(li == lane, x, 0.0), axis=-1, keepdims=True)


def _sample_mixer_kernel(lx_ref, q_ref, k_ref, v_ref, gate_ref, mx_ref, mo_ref, small_ref,
                         el_ref, em_ref, h0_ref, n0_ref, m0_ref, s0_ref, c0_ref,
                         lcw_ref, lcb_ref, wa_ref, ba_ref, wx_ref, bx_ref, lam_ref,
                         wg2_ref, bg_ref, ggn_ref,
                         mcw_ref, mcb_ref, wqh_ref, wql_ref, wkh_ref, wkl_ref, wvh_ref, wvl_ref, bif_ref, mgn_ref,
                         *rest, layer, creates):
    yl_ref, yg_ref, ym_ref, *state_refs = rest[-8:]
    row_refs, (st_ref, ct_ref) = state_refs[:3], state_refs[3:]
    if creates:
        for ref in row_refs:
            for other in range(DEPTH):
                if other != layer:
                    ref[other] = jnp.zeros(ref.shape[1:], F32)
        row_refs = [ref.at[layer] for ref in row_refs]
    hrow_ref, nrow_ref, mrow_ref = row_refs
    rows = lx_ref.shape[0]
    n_pairs = rows // SUBLANE
    seq_per_tile = SUBLANE // T_STEP
    t_col = _tmask((rows, 1))
    row8 = lax.broadcasted_iota(jnp.int32, (SUBLANE, 1), 0)

    part = pl.program_id(0) % (SUBLANE // (rows // T_STEP))
    r_i = lax.broadcasted_iota(jnp.int32, (rows, SUBLANE), 0)
    c_i = lax.broadcasted_iota(jnp.int32, (rows, SUBLANE), 1)
    mine = c_i == part * (rows // T_STEP) + r_i // T_STEP
    p_seq = jnp.where(mine, 1.0, 0.0).astype(BF16)
    p_time = [jnp.where(jnp.logical_and(mine, r_i % T_STEP == t), 1.0, 0.0).astype(BF16) for t in range(CONV_W - 1)]

    def spread(p, x):
        return sum(jnp.dot(p, piece, preferred_element_type=F32) for piece in _split3(x))

    p_conv = jnp.concatenate(p_time, axis=1)
    el = spread(p_conv, jnp.concatenate([el_ref[t] for t in range(CONV_W - 1)], axis=0))
    u = _conv_rows(lx_ref[...], el, lcw_ref[...], lcb_ref[...])
    a_blocks, g_blocks = [], []
    for blk in range(LRU_BLOCKS):
        sl = slice(blk * LANE, (blk + 1) * LANE)
        a_b, g_b = _lru_gates(u[:, sl], wa_ref[blk], wx_ref[blk], ba_ref[:, sl], bx_ref[:, sl], lam_ref[:, sl])
        a_blocks.append(a_b)
        g_blocks.append(g_b)
    a = jnp.concatenate(a_blocks, axis=1)
    t_w = _tmask(a.shape)
    g = jnp.concatenate(g_blocks, axis=1) + a * spread(p_time[0], h0_ref[...])
    a1 = jnp.where(t_w >= 1, a * _down(a, 1), a)
    g1 = jnp.where(t_w >= 1, g + a * _down(g, 1), g)
    h = jnp.where(t_w >= 2, g1 + a1 * _down(g1, 2), g1)
    yl_ref[...] = h.astype(BF16)
    hrow_ref[...] = h

    q, k, v = q_ref[...], k_ref[...], v_ref[...]
    logdec = _log_sigmoid(_dot(small_ref[...], wg2_ref[...]) + bg_ref[...]) * (1.0 / GLA_TAU)
    bc = _seg_cumsum(logdec)
    bl = _seg_last(bc)
    qs = q * (GLA_DK ** -0.5)
    o_heads = [jnp.zeros((rows, GLA_DV), F32) for _ in range(GLA_H)]
    for s in range(T_STEP):
        prod = qs * _down(k, s) * jnp.exp(bc - _down(bc, s))
        v_s = _down(v, s)
        for hh in range(GLA_H):
            a_sh = jnp.sum(prod[:, hh * GLA_DK:(hh + 1) * GLA_DK], axis=-1, keepdims=True)
            a_sh = jnp.where(t_col >= s, a_sh, 0.0)
            o_heads[hh] = o_heads[hh] + a_sh * v_s[:, hh * GLA_DV:(hh + 1) * GLA_DV]
    qe = qs * jnp.exp(bc)
    kd = k * jnp.exp(bl - bc)
    dec = jnp.exp(bl)
    gate = gate_ref[...]
    for hh in range(GLA_H):
        ks = slice(hh * GLA_DK, (hh + 1) * GLA_DK)
        vs = slice(hh * GLA_DV, (hh + 1) * GLA_DV)
        o_state = []
        for p in range(n_pairs):
            r8 = slice(p * SUBLANE, (p + 1) * SUBLANE)
            kd_t = kd[r8, ks].T
            dec_t = dec[r8, ks].T
            res = None
            for j in range(seq_per_tile):
                b = p * seq_per_tile + j
                s_b = s0_ref[b, hh]
                r_j = _dot(qe[r8, ks], s_b)
                res = r_j if res is None else jnp.where(row8 // T_STEP == j, r_j, res)
                v_j = jnp.where(row8 // T_STEP == j, v[r8, vs], 0.0)
                last = j * T_STEP + T_STEP - 1
                st_ref[b, hh] = dec_t[:, last:last + 1] * s_b + _dot(kd_t, v_j)
            o_state.append(res)
        o = o_heads[hh] + jnp.concatenate(o_state, axis=0)
        yg_ref[:, vs] = (_rms(o, ggn_ref[...]) * _silu(gate[:, vs])).astype(BF16)

    mx = mx_ref[...]
    em = spread(p_conv, jnp.concatenate([em_ref[t] for t in range(CONV_W - 1)], axis=0))
    mc = _silu(_conv_rows(mx, em, mcw_ref[...], mcb_ref[...]))
    mq = _blockdiag3(mc, wqh_ref, wql_ref)
    mk = _blockdiag3(mc, wkh_ref, wkl_ref) * (ML_DH ** -0.5)
    mv = _blockdiag3(mx, wvh_ref, wvl_ref)
    gts = small_ref[...] + bif_ref[...]
    ig = pltpu.roll(gts, S_FG - S_IG, 1)
    fcum = _seg_cumsum(_log_sigmoid(gts))
    flast = _seg_last(fcum)
    m0 = spread(p_seq, m0_ref[...])
    t_g = _tmask(gts.shape)
    inter = fcum + m0
    dms = [jnp.where(t_g >= s, fcum - _down(fcum, s) + _down(ig, s), NEG_BIG) for s in range(T_STEP)]
    mt = inter
    for dm in dms:
        mt = jnp.maximum(mt, dm)
    ci_t = jnp.exp(inter - mt)
    emt_t = jnp.exp(-mt)
    w_t = [jnp.exp(dm - mt) for dm in dms]
    dj = flast - fcum + ig
    m_new = jnp.maximum(flast + m0, _seg_allreduce(dj, jnp.maximum))
    cs_t = jnp.exp(flast + m0 - m_new)
    wj_t = jnp.exp(dj - m_new)
    mrow_ref[...] = m_new
    mo = mo_ref[...]
    n0 = spread(p_seq, n0_ref[...])
    for hh in range(ML_H):
        sl = slice(hh * ML_DH, (hh + 1) * ML_DH)
        lane = S_FG + hh
        qh, kh, vh = mq[:, sl], mk[:, sl], mv[:, sl]
        ci, emt, cs, wj = _col(ci_t, lane), _col(emt_t, lane), _col(cs_t, lane), _col(wj_t, lane)
        num = jnp.zeros((rows, ML_DH), F32)
        den = ci * jnp.sum(qh * n0[:, sl], axis=-1, keepdims=True)
        for s in range(T_STEP):
            sc = jnp.sum(qh * _down(kh, s), axis=-1, keepdims=True) * _col(w_t[s], lane)
            num = num + sc * _down(vh, s)
            den = den + sc
        kw = kh * wj
        nrow_ref[:, sl] = cs * n0[:, sl] + _seg_allreduce(kw, jnp.add)
        qc = []
        for p in range(n_pairs):
            r8 = slice(p * SUBLANE, (p + 1) * SUBLANE)
            kw_t = kw[r8, :].T
            res = None
            for j in range(seq_per_tile):
                b = p * seq_per_tile + j
                c_b = c0_ref[b, hh]
                r_j = _dot(qh[r8, :], c_b)
                res = r_j if res is None else jnp.where(row8 // T_STEP == j, r_j, res)
                v_j = jnp.where(row8 // T_STEP == j, vh[r8, :], 0.0)
                last = p * SUBLANE + j * T_STEP + T_STEP - 1
                ct_ref[b, hh] = cs[last:last + 1, :] * c_b + _dot(kw_t, v_j)
            qc.append(res)
        num = num + ci * jnp.concatenate(qc, axis=0)
        hcell = num / jnp.maximum(jnp.abs(den), emt)
        ym_ref[:, sl] = (jax.nn.sigmoid(mo[:, sl]) * _rms(hcell, mgn_ref[...])).astype(BF16)


def _sample_mixer(proj, small, el, em, h0e, n0e, m0e, s_state, c_state, layer, w, row_prev, big_prev):
    n = proj.shape[0]
    nseq = n // T_STEP
    bb = STEP_BB
    rows = bb * T_STEP
    row = lambda width, blk: pl.BlockSpec((rows, width), lambda i: (i, blk))
    lrow = lambda width: pl.BlockSpec((None, rows, width), lambda i: (layer, i, 0))
    lw3 = lambda s: pl.BlockSpec((None,) + s, lambda i: (layer, 0, 0))
    lw4 = lambda s: pl.BlockSpec((None,) + s, lambda i: (layer, 0, 0, 0))
    sspec = pl.BlockSpec((None, bb, GLA_H, GLA_DK, GLA_DV), lambda i: (layer, i, 0, 0, 0))
    cspec = pl.BlockSpec((None, bb, ML_H, ML_DH, ML_DH), lambda i: (layer, i, 0, 0, 0))
    per_blk = SUBLANE // bb
    seq_blk = lambda width: pl.BlockSpec((None, SUBLANE, width), lambda i: (layer, i // per_blk, 0))
    conv_seq = lambda width: pl.BlockSpec((None, CONV_W - 1, SUBLANE, width), lambda i: (layer, 0, i // per_blk, 0))
    nblk = ML_W // LANE
    kw, vw = GLA_H * GLA_DK, GLA_H * GLA_DV
    in_specs = [
        row(LRU_W, C_LRU // LRU_W), row(kw, C_GQ // kw), row(kw, C_GK // kw), row(vw, C_GV // vw),
        row(vw, C_GGATE // vw), row(ML_W, C_MX // ML_W), row(ML_W, C_MO // ML_W), row(LANE, 0),
        conv_seq(LRU_W), conv_seq(ML_W), seq_blk(LRU_W), seq_blk(ML_W), seq_blk(LANE), sspec, cspec,
        lw3((CONV_W, LRU_W)), lw3((1, LRU_W)), lw4((LRU_BLOCKS, LANE, LANE)), lw3((1, LRU_W)),
        lw4((LRU_BLOCKS, LANE, LANE)), lw3((1, LRU_W)), lw3((1, LRU_W)),
        lw3((LANE, kw)), lw3((1, kw)), lw3((1, GLA_DV)),
        lw3((CONV_W, ML_W)), lw3((1, ML_W)), *([lw4((nblk, LANE, LANE))] * 6), lw3((1, LANE)), lw3((1, ML_DH)),
    ]
    args = [proj] * 7 + [small, el, em, h0e, n0e, m0e, s_state, c_state,
                         w["lru_cw"], w["lru_cb"], w["lru_wa"], w["lru_ba"], w["lru_wx"], w["lru_bx"], w["lru_lam"],
                         w["wg2"], w["bg"], w["gla_gn"],
                         w["ml_cw"], w["ml_cb"], *w["ml_wq"], *w["ml_wk"], *w["ml_wv"], w["ml_bif"], w["ml_gn"]]
    creates = row_prev is None
    prev = list(big_prev) if creates else list(row_prev) + list(big_prev)
    first_out = 6 if creates else 3
    aliases = {len(in_specs) + j: first_out + j for j in range(len(prev))}
    in_specs += [pl.BlockSpec(memory_space=pl.ANY)] * len(prev)
    args += prev
    sbytes = bb * GLA_H * GLA_DK * GLA_DV * 4
    cbytes = bb * ML_H * ML_DH * ML_DH * 4
    vmem = 4 * (sbytes + cbytes) + 2 * rows * (7 * 1024 + 8 * 1024) * 4 + 80 * rows * 1024 * 4 + (12 << 20)
    row_specs = [lrow(LRU_W), lrow(ML_W), lrow(LANE)]
    if creates:
        all_rows = lambda width: pl.BlockSpec((DEPTH, rows, width), lambda i: (0, i, 0))
        row_specs = [all_rows(LRU_W), all_rows(ML_W), all_rows(LANE)]
    state_specs = row_specs + [sspec, cspec]
    return pl.pallas_call(
        functools.partial(_sample_mixer_kernel, layer=layer, creates=creates),
        grid=(nseq // bb,),
        in_specs=in_specs,
        out_specs=[row(LRU_W, 0), row(vw, 0), row(ML_W, 0), *state_specs],
        out_shape=[SDS((n, LRU_W), BF16), SDS((n, vw), BF16), SDS((n, ML_W), BF16),
                   SDS((DEPTH, n, LRU_W), F32), SDS((DEPTH, n, ML_W), F32), SDS((DEPTH, n, LANE), F32),
                   SDS((DEPTH, nseq, GLA_H, GLA_DK, GLA_DV), F32), SDS((DEPTH, nseq, ML_H, ML_DH, ML_DH), F32)],
        input_output_aliases=aliases,
        compiler_params=_params(("arbitrary",), vmem),
        name="sample_mixer",
    )(*args)


def _prep_weights(p):
    w_in = p["w_in"]
    c1 = C_GGATE
    c2 = C_MG + W_SHIFT1
    w_small = jnp.concatenate(
        [w_in[:, :, c1:c1 + GLA_RANK], w_in[:, :, c2:c2 + 2 * ML_H],
         jnp.zeros((DEPTH, D, LANE - GLA_RANK - 2 * ML_H), F32)], axis=-1)
    wg2 = jnp.concatenate(
        [p["gla_w_g2"], jnp.zeros((DEPTH, LANE - GLA_RANK, GLA_H * GLA_DK), F32)], axis=1)
    blk_of = jnp.arange(LANE) // ML_BS
    on_diag = blk_of[:, None] == blk_of[None, :]

    def dense_bd(wb):
        rows = wb.reshape(DEPTH, ML_W // LANE, LANE, ML_BS)
        dense = jnp.where(on_diag, jnp.tile(rows, (1, 1, 1, LANE // ML_BS)), 0.0)
        hi = dense.astype(BF16)
        return hi, (dense - hi.astype(F32)).astype(BF16)

    bif = jnp.zeros((DEPTH, 1, LANE), F32).at[:, 0, S_IG:S_IG + 2 * ML_H].set(p["ml_b_if"])
    r3 = lambda a: a.reshape(DEPTH, 1, a.shape[-1])
    return dict(
        w_small=w_small, wg2=wg2, bg=r3(p["gla_b_g"]), gla_gn=r3(p["gla_g_norm"]),
        g1=r3(p["g_norm1"]), g2=r3(p["g_norm2"]), gf=p["g_final"].reshape(1, D),
        lru_cw=p["lru_conv_w"], lru_cb=r3(p["lru_conv_b"]), lru_wa=p["lru_w_a"], lru_ba=r3(p["lru_b_a"]),
        lru_wx=p["lru_w_x"], lru_bx=r3(p["lru_b_x"]), lru_lam=r3(p["lru_lam"]),
        ml_cw=p["ml_conv_w"], ml_cb=r3(p["ml_conv_b"]), ml_wq=dense_bd(p["ml_w_q"]), ml_wk=dense_bd(p["ml_w_k"]),
        ml_wv=dense_bd(p["ml_w_v"]), ml_bif=bif, ml_gn=r3(p["ml_g_norm"]),
        w_in_t=jnp.swapaxes(w_in, 1, 2), w_br=(p["w_br_lru"], p["w_br_gla"], p["w_br_ml"]), w_out=p["w_out"], w_ff1=p["w_ff1"],
        w_ff2=p["w_ff2"],
    )


def _trunk(xp3, xs3, mod_p, mod_s, states, w):
    bp, tp, _ = xp3.shape
    bs, ts, _ = xs3.shape
    n_p, n_s = bp * tp, bs * ts
    tm_p = min(tp, 1024)
    tm_f = min(tp, 512)
    s_h, s_cv, s_gla, s_c, s_n, s_m, s_mcv = states
    el = jnp.swapaxes(s_cv, 1, 2)
    em = jnp.swapaxes(s_mcv, 1, 2)
    h0e = s_h
    n0e = s_n.reshape(DEPTH, bs, ML_W)
    m0e = jnp.pad(s_m, ((0, 0), (0, 0), (S_FG, LANE - S_FG - ML_H)))
    x_p = xp3.reshape(n_p, D)
    x_s = xs3.reshape(n_s, D)
    prev_p = None
    conv_l, conv_m = [], []
    keep = ts - (CONV_W - 1)
    for l in range(DEPTH):
        last = l == DEPTH - 1
        proj_s, small_s, w_main = _in_proj_cast(x_s, mod_s, l, w["g1"], w["w_in_t"], w["w_small"], tn=512, rpg=n_s)
        if l == 0:
            s_shape = (DEPTH, bs, GLA_H, GLA_DK, GLA_DV)
            c_shape = (DEPTH, bs, ML_H, ML_DH, ML_DH)
            flat_rows = lambda shape: shape[0] * shape[1] * shape[2] * shape[3] * shape[4] // GLA_DV
            proj_p, small_p, s_buf, c_buf = _in_proj(x_p, mod_p, l, w["g1"], w_main, w["w_small"], tm=tm_p, tn=1024,
                                                     rpg=tp, r=1, zero_rows=(flat_rows(s_shape), flat_rows(c_shape)))
            big_s = (s_buf.reshape(s_shape), c_buf.reshape(c_shape))
            rows_s = None
        else:
            proj_p, small_p = _in_proj(x_p, mod_p, l, w["g1"], w_main, w["w_small"], tm=tm_p, tn=1024, rpg=tp, r=1)
        outs_s = _sample_mixer(proj_s, small_s, el, em, h0e, n0e, m0e, s_gla, s_c, l, w, rows_s, big_s)
        rows_s, big_s = outs_s[3:6], outs_s[6:]
        outs_p = _prompt_mixer(proj_p.reshape(bp, tp, N_MAIN), small_p.reshape(bp, tp, LANE), l, w, prev_p,
                               L=min(tp, CHUNK), rows=min(tp, PROMPT_ROWS), nseq=PROMPT_SEQS)
        prev_p = outs_p[3:]
        conv_l.append(proj_s[:, C_LRU:C_LRU + LRU_W].reshape(bs, ts, LRU_W)[:, keep:])
        conv_m.append(proj_s[:, C_MX:C_MX + ML_W].reshape(bs, ts, ML_W)[:, keep:])
        merged_s, *w_br = _merge(*outs_s[:3], proj_s, l, *w["w_br"], tm=n_s, tn=512, cast=True)
        (merged_p,) = _merge(*(y.reshape(n_p, y.shape[-1]) for y in outs_p[:3]), proj_p, l, *w_br, tm=tm_p, tn=512,
                             cast=False)
        x_s, w_out = _out_proj(merged_s, x_s, mod_s, l, w["w_out"], tm=n_s, tn=512, rpg=n_s, r=n_s, cast=True)
        (x_p,) = _out_proj(merged_p, x_p, mod_p, l, w_out, tm=tm_p, tn=1024, rpg=tp, r=1, cast=False)
        x_s, w_ff1, w_ff2 = _ffn(x_s, mod_s, l, w["g2"], w["gf"], w["w_ff1"], w["w_ff2"], tm=n_s, tf=512, rpg=n_s,
                                 r=n_s, final_norm=last, cast=True)
        (x_p,) = _ffn(x_p, mod_p, l, w["g2"], w["gf"], w_ff1, w_ff2, tm=tm_f, tf=1024, rpg=tp, r=1, final_norm=last,
                      cast=False)
    h_t, lcv_t, s_t, c_t, n_t, m_t, mcv_t = prev_p
    p_states = [h_t.reshape(DEPTH, bp, LRU_W), lcv_t, s_t, c_t, n_t, m_t[..., 0], mcv_t]
    hrow, nrow, mrow = rows_s
    s_out, c_out = big_s
    t_last = ts - 1
    s_states = [hrow.reshape(DEPTH, bs, ts, LRU_W)[:, :, t_last], jnp.stack(conv_l), s_out, c_out,
                nrow.reshape(DEPTH, bs, ts, ML_H, ML_DH)[:, :, t_last],
                mrow.reshape(DEPTH, bs, ts, LANE)[:, :, t_last, S_FG:S_FG + ML_H], jnp.stack(conv_m)]
    return x_p.reshape(bp, tp, D), x_s.reshape(bs, ts, D), p_states, s_states


def kernel(x_prompt, x_sample, c_prompt, c_sample, state_lru_h, state_lru_conv, state_gla, state_mlstm_C, state_mlstm_n, state_mlstm_m, state_mlstm_conv, w_ada, b_ada, g_norm1, g_norm2, w_in, lru_conv_w, lru_conv_b, lru_w_a, lru_b_a, lru_w_x, lru_b_x, lru_lam, gla_w_g2, gla_b_g, gla_g_norm, ml_conv_w, ml_conv_b, ml_w_q, ml_w_k, ml_w_v, ml_b_if, ml_g_norm, w_br_lru, w_br_gla, w_br_ml, w_out, w_ff1, w_ff2, g_final):
    p = dict(g_norm1=g_norm1, g_norm2=g_norm2, w_in=w_in, lru_conv_w=lru_conv_w, lru_conv_b=lru_conv_b,
             lru_w_a=lru_w_a, lru_b_a=lru_b_a, lru_w_x=lru_w_x, lru_b_x=lru_b_x, lru_lam=lru_lam,
             gla_w_g2=gla_w_g2, gla_b_g=gla_b_g, gla_g_norm=gla_g_norm, ml_conv_w=ml_conv_w, ml_conv_b=ml_conv_b,
             ml_w_q=ml_w_q, ml_w_k=ml_w_k, ml_w_v=ml_w_v, ml_b_if=ml_b_if, ml_g_norm=ml_g_norm,
             w_br_lru=w_br_lru, w_br_gla=w_br_gla, w_br_ml=w_br_ml, w_out=w_out, w_ff1=w_ff1, w_ff2=w_ff2,
             g_final=g_final)
    w = _prep_weights(p)
    bp = x_prompt.shape[0]
    bs, ts, _ = x_sample.shape
    assert ts == T_STEP and bs % STEP_BB == 0
    ns = bs * ts
    pad = (-(ns + bp)) % SUBLANE
    c_all = jnp.concatenate([jnp.repeat(c_sample, ts, axis=0), c_prompt, jnp.zeros((pad, D), F32)], axis=0)
    mod_all = _ada(c_all, w_ada, b_ada)
    mod_s = mod_all.reshape(DEPTH, 1, ns + bp + pad, N_MOD * D)
    mod_p = mod_all[:, ns:ns + bp].reshape(DEPTH, bp, 1, N_MOD * D)
    y_p, y_s, ps, ss = _trunk(x_prompt, x_sample, mod_p, mod_s,
                              (state_lru_h, state_lru_conv, state_gla, state_mlstm_C, state_mlstm_n, state_mlstm_m,
                               state_mlstm_conv), w)
    return (y_p, y_s, *ps, *ss)
```
